```python
import jax, jax.numpy as jnp
from jax import lax
from jax.lax import linalg as lax_linalg
import numpy as np

D_MODEL = 2048
BATCH = 16
SEQ = 2048
DEPTH = 2

MIX_WIDTH = D_MODEL
CONV_CH = MIX_WIDTH // 2
CONV_GROUPS = 8
CONV_K = 31
GDN_HEAD_DIM = 128
GDN_V_HEADS = (MIX_WIDTH // 2) // GDN_HEAD_DIM
GDN_QK_HEADS = GDN_V_HEADS // 2
GDN_K_WIDTH = GDN_QK_HEADS * GDN_HEAD_DIM
GDN_V_WIDTH = GDN_V_HEADS * GDN_HEAD_DIM
SHORT_CONV_K = 4
CHUNK = 64
D_FF = ((8 * D_MODEL // 3 + 255) // 256) * 256
FFN_CONV_K = 3
EPS = 1e-6
IN_WIDTH = 2 * CONV_CH + 2 * GDN_K_WIDTH + 2 * GDN_V_WIDTH + 2 * GDN_V_HEADS

kernel_name = "hymba_style_conformer_conv_gated_deltanet_convffn"


def rms_norm(x, g):
    xf = x.astype(jnp.float32)
    y = xf * lax.rsqrt(jnp.mean(xf * xf, axis=-1, keepdims=True) + EPS)
    return (y * g.astype(jnp.float32)).astype(x.dtype)


def group_layer_norm(x, g, b, groups):
    B, S, C = x.shape
    xf = x.astype(jnp.float32).reshape(B, S, groups, C // groups)
    mu = jnp.mean(xf, axis=-1, keepdims=True)
    xc = xf - mu
    var = jnp.mean(xc * xc, axis=-1, keepdims=True)
    y = (xc * lax.rsqrt(var + EPS)).reshape(B, S, C)
    return (y * g.astype(jnp.float32) + b.astype(jnp.float32)).astype(x.dtype)


def causal_depthwise_conv(x, w):
    K, C = w.shape
    return lax.conv_general_dilated(
        x, w[:, None, :].astype(x.dtype), window_strides=(1,), padding=[(K - 1, 0)],
        dimension_numbers=("NWC", "WIO", "NWC"), feature_group_count=C)


def l2_normalize(t):
    return t * lax.rsqrt(jnp.sum(t * t, axis=-1, keepdims=True) + EPS)


def chunk_gated_delta_rule(q, k, v, g, beta):
    B, S, H, dk = q.shape
    dv = v.shape[-1]
    N = S // CHUNK
    q = q * (dk ** -0.5)

    def to_chunks(t):
        return t.reshape(B, N, CHUNK, H, t.shape[-1]).transpose(0, 3, 1, 2, 4)

    q, k, v = to_chunks(q), to_chunks(k), to_chunks(v)
    g = g.reshape(B, N, CHUNK, H).transpose(0, 3, 1, 2)
    beta = beta.reshape(B, N, CHUNK, H).transpose(0, 3, 1, 2)
    g = jnp.cumsum(g, axis=-1)

    causal = jnp.tril(jnp.ones((CHUNK, CHUNK), dtype=bool))
    strict = jnp.tril(jnp.ones((CHUNK, CHUNK), dtype=bool), -1)
    decay = jnp.exp(jnp.where(causal, g[..., :, None] - g[..., None, :], -jnp.inf))

    k_beta = k * beta[..., None]
    v_beta = v * beta[..., None]
    L = jnp.where(strict, jnp.einsum("bhnid,bhnjd->bhnij", k_beta, k) * decay, 0.0)
    A = L + jnp.eye(CHUNK, dtype=L.dtype)
    rhs = jnp.concatenate([v_beta, k_beta * jnp.exp(g)[..., None]], axis=-1)
    sol = lax_linalg.triangular_solve(A, rhs, left_side=True, lower=True, unit_diagonal=True)
    u, w = sol[..., :dv], sol[..., dv:]
    qk = jnp.einsum("bhnid,bhnjd->bhnij", q, k) * decay

    def step(state, inp):
        q_c, k_c, u_c, w_c, g_c, qk_c = inp
        v_new = u_c - jnp.einsum("bhck,bhkv->bhcv", w_c, state)
        o_c = jnp.einsum("bhck,bhkv->bhcv", q_c * jnp.exp(g_c)[..., None], state) + \
            jnp.einsum("bhij,bhjv->bhiv", qk_c, v_new)
        g_last = g_c[..., -1]
        k_dec = k_c * jnp.exp(g_last[..., None] - g_c)[..., None]
        state = state * jnp.exp(g_last)[..., None, None] + jnp.einsum("bhck,bhcv->bhkv", k_dec, v_new)
        return state, o_c

    xs = tuple(jnp.moveaxis(t, 2, 0) for t in (q, k, u, w, g, qk))
    state0 = jnp.zeros((B, H, dk, dv), jnp.float32)
    _, o = lax.scan(step, state0, xs)
    return o.transpose(1, 0, 3, 2, 4).reshape(B, S, H, dv)


def conformer_conv_group(a_val, a_gate, dw_w, dw_b, ln_g, ln_b, pw_w, pw_b):
    u = a_val * jax.nn.sigmoid(a_gate)
    u = causal_depthwise_conv(u, dw_w) + dw_b
    u = group_layer_norm(u, ln_g, ln_b, CONV_GROUPS)
    u = jax.nn.silu(u)
    return u @ pw_w + pw_b


def gated_deltanet_group(q, k, v, z, b_raw, a_raw, conv_w, a_log, dt_bias, norm_g):
    B, S, _ = q.shape
    dtype = q.dtype
    qkv = jax.nn.silu(causal_depthwise_conv(jnp.concatenate([q, k, v], axis=-1), conv_w))
    q, k, v = jnp.split(qkv, [GDN_K_WIDTH, 2 * GDN_K_WIDTH], axis=-1)
    rep = GDN_V_HEADS // GDN_QK_HEADS
    q = jnp.repeat(q.reshape(B, S, GDN_QK_HEADS, GDN_HEAD_DIM), rep, axis=2).astype(jnp.float32)
    k = jnp.repeat(k.reshape(B, S, GDN_QK_HEADS, GDN_HEAD_DIM), rep, axis=2).astype(jnp.float32)
    v = v.reshape(B, S, GDN_V_HEADS, GDN_HEAD_DIM).astype(jnp.float32)
    q, k = l2_normalize(q), l2_normalize(k)
    beta = jax.nn.sigmoid(b_raw.astype(jnp.float32))
    g = -jnp.exp(a_log.astype(jnp.float32)) * jax.nn.softplus(
        a_raw.astype(jnp.float32) + dt_bias.astype(jnp.float32))
    o = chunk_gated_delta_rule(q, k, v, g, beta)
    o = o * lax.rsqrt(jnp.mean(o * o, axis=-1, keepdims=True) + EPS) * norm_g.astype(jnp.float32)
    o = o * jax.nn.silu(z.reshape(B, S, GDN_V_HEADS, GDN_HEAD_DIM).astype(jnp.float32))
    return o.reshape(B, S, GDN_V_WIDTH).astype(dtype)


def _fwd_setup_inputs(seed: int = 0) -> dict:
    key = jax.random.key(seed)
    ks = jax.random.split(key, 24)
    L = DEPTH

    def nrm(k, shape, scale):
        return jax.random.normal(k, shape, jnp.float32) * scale

    dt = jnp.exp(jax.random.uniform(ks[10], (L, GDN_V_HEADS), jnp.float32,
                                    np.log(1e-3).astype(np.float32), np.log(1e-1).astype(np.float32)))
    return {
        "x": nrm(ks[0], (BATCH, SEQ, D_MODEL), 1.0),
        "mix_norm_g": 1.0 + nrm(ks[1], (L, D_MODEL), 0.02),
        "w_in": nrm(ks[2], (L, D_MODEL, IN_WIDTH), D_MODEL ** -0.5),
        "conv_dw_w": nrm(ks[3], (L, CONV_K, CONV_CH), CONV_K ** -0.5),
        "conv_dw_b": nrm(ks[4], (L, CONV_CH), 0.02),
        "conv_ln_g": 1.0 + nrm(ks[5], (L, CONV_CH), 0.02),
        "conv_ln_b": nrm(ks[6], (L, CONV_CH), 0.02),
        "conv_pw_w": nrm(ks[7], (L, CONV_CH, CONV_CH), CONV_CH ** -0.5),
        "conv_pw_b": nrm(ks[8], (L, CONV_CH), 0.02),
        "gdn_conv_w": nrm(ks[9], (L, SHORT_CONV_K, 2 * GDN_K_WIDTH + GDN_V_WIDTH), SHORT_CONV_K ** -0.5),
        "gdn_a_log": jnp.log(jax.random.uniform(ks[11], (L, GDN_V_HEADS), jnp.float32, 1.0, 16.0)),
        "gdn_dt_bias": dt + jnp.log(-jnp.expm1(-dt)),
        "gdn_norm_g": 1.0 + nrm(ks[12], (L, GDN_HEAD_DIM), 0.02),
        "w_out": nrm(ks[13], (L, MIX_WIDTH, D_MODEL), MIX_WIDTH ** -0.5),
        "ffn_norm_g": 1.0 + nrm(ks[14], (L, D_MODEL), 0.02),
        "w_up": nrm(ks[15], (L, D_MODEL, 2 * D_FF), D_MODEL ** -0.5),
        "ffn_conv_w": nrm(ks[16], (L, FFN_CONV_K, D_FF), FFN_CONV_K ** -0.5),
        "ffn_conv_b": nrm(ks[17], (L, D_FF), 0.02),
        "w_down": nrm(ks[18], (L, D_FF, D_MODEL), D_FF ** -0.5),
        "final_norm_g": 1.0 + nrm(ks[19], (D_MODEL,), 0.02),
    }


def _fwd_reference(x, mix_norm_g, w_in, conv_dw_w, conv_dw_b, conv_ln_g, conv_ln_b, conv_pw_w, conv_pw_b,
              gdn_conv_w, gdn_a_log, gdn_dt_bias, gdn_norm_g, w_out, ffn_norm_g, w_up,
              ffn_conv_w, ffn_conv_b, w_down, final_norm_g):
    splits = np.cumsum([CONV_CH, CONV_CH, GDN_K_WIDTH, GDN_K_WIDTH, GDN_V_WIDTH, GDN_V_WIDTH,
                        GDN_V_HEADS]).tolist()
    for l in range(DEPTH):
        h = rms_norm(x, mix_norm_g[l])
        p = h @ w_in[l]
        a_val, a_gate, q, k, v, z, b_raw, a_raw = jnp.split(p, splits, axis=-1)
        out_a = conformer_conv_group(a_val, a_gate, conv_dw_w[l], conv_dw_b[l], conv_ln_g[l],
                                     conv_ln_b[l], conv_pw_w[l], conv_pw_b[l])
        out_b = gated_deltanet_group(q, k, v, z, b_raw, a_raw, gdn_conv_w[l], gdn_a_log[l],
                                     gdn_dt_bias[l], gdn_norm_g[l])
        x = x + jnp.concatenate([out_a, out_b], axis=-1) @ w_out[l]
        h = rms_norm(x, ffn_norm_g[l])
        gate, up = jnp.split(h @ w_up[l], 2, axis=-1)
        gate = causal_depthwise_conv(gate, ffn_conv_w[l]) + ffn_conv_b[l]
        x = x + (jax.nn.silu(gate) * up) @ w_down[l]
    return rms_norm(x, final_norm_g)


import jax as _jax
import jax.numpy as _jnp

TWIN_FORMAT = 'train_step'
FWD_PARAMS = ['x', 'mix_norm_g', 'w_in', 'conv_dw_w', 'conv_dw_b', 'conv_ln_g', 'conv_ln_b', 'conv_pw_w', 'conv_pw_b', 'gdn_conv_w', 'gdn_a_log', 'gdn_dt_bias', 'gdn_norm_g', 'w_out', 'ffn_norm_g', 'w_up', 'ffn_conv_w', 'ffn_conv_b', 'w_down', 'final_norm_g']
TWIN_WEIGHTS = ['mix_norm_g', 'w_in', 'conv_dw_w', 'conv_dw_b', 'conv_ln_g', 'conv_ln_b', 'conv_pw_w', 'conv_pw_b', 'gdn_conv_w', 'gdn_a_log', 'gdn_dt_bias', 'gdn_norm_g', 'w_out', 'ffn_norm_g', 'w_up', 'ffn_conv_w', 'ffn_conv_b', 'w_down', 'final_norm_g']
TWIN_DIFF_INPUT = 'x'
TWIN_INPUTS = ['x', 'mix_norm_g', 'w_in', 'conv_dw_w', 'conv_dw_b', 'conv_ln_g', 'conv_ln_b', 'conv_pw_w', 'conv_pw_b', 'gdn_conv_w', 'gdn_a_log', 'gdn_dt_bias', 'gdn_norm_g', 'w_out', 'ffn_norm_g', 'w_up', 'ffn_conv_w', 'ffn_conv_b', 'w_down', 'final_norm_g', 'loss_target', 'm_mix_norm_g', 'm_w_in', 'm_conv_dw_w', 'm_conv_dw_b', 'm_conv_ln_g', 'm_conv_ln_b', 'm_conv_pw_w', 'm_conv_pw_b', 'm_gdn_conv_w', 'm_gdn_a_log', 'm_gdn_dt_bias', 'm_gdn_norm_g', 'm_w_out', 'm_ffn_norm_g', 'm_w_up', 'm_ffn_conv_w', 'm_ffn_conv_b', 'm_w_down', 'm_final_norm_g', 'v_mix_norm_g', 'v_w_in', 'v_conv_dw_w', 'v_conv_dw_b', 'v_conv_ln_g', 'v_conv_ln_b', 'v_conv_pw_w', 'v_conv_pw_b', 'v_gdn_conv_w', 'v_gdn_a_log', 'v_gdn_dt_bias', 'v_gdn_norm_g', 'v_w_out', 'v_ffn_norm_g', 'v_w_up', 'v_ffn_conv_w', 'v_ffn_conv_b', 'v_w_down', 'v_final_norm_g']
TWIN_OUTPUTS = ['loss', 'grad_x', 'grad_mix_norm_g', 'grad_w_in', 'grad_conv_dw_w', 'grad_conv_dw_b', 'grad_conv_ln_g', 'grad_conv_ln_b', 'grad_conv_pw_w', 'grad_conv_pw_b', 'grad_gdn_conv_w', 'grad_gdn_a_log', 'grad_gdn_dt_bias', 'grad_gdn_norm_g', 'grad_w_out', 'grad_ffn_norm_g', 'grad_w_up', 'grad_ffn_conv_w', 'grad_ffn_conv_b', 'grad_w_down', 'grad_final_norm_g', 'delta_mix_norm_g', 'delta_w_in', 'delta_conv_dw_w', 'delta_conv_dw_b', 'delta_conv_ln_g', 'delta_conv_ln_b', 'delta_conv_pw_w', 'delta_conv_pw_b', 'delta_gdn_conv_w', 'delta_gdn_a_log', 'delta_gdn_dt_bias', 'delta_gdn_norm_g', 'delta_w_out', 'delta_ffn_norm_g', 'delta_w_up', 'delta_ffn_conv_w', 'delta_ffn_conv_b', 'delta_w_down', 'delta_final_norm_g', 'new_m_mix_norm_g', 'new_m_w_in', 'new_m_conv_dw_w', 'new_m_conv_dw_b', 'new_m_conv_ln_g', 'new_m_conv_ln_b', 'new_m_conv_pw_w', 'new_m_conv_pw_b', 'new_m_gdn_conv_w', 'new_m_gdn_a_log', 'new_m_gdn_dt_bias', 'new_m_gdn_norm_g', 'new_m_w_out', 'new_m_ffn_norm_g', 'new_m_w_up', 'new_m_ffn_conv_w', 'new_m_ffn_conv_b', 'new_m_w_down', 'new_m_final_norm_g', 'new_v_mix_norm_g', 'new_v_w_in', 'new_v_conv_dw_w', 'new_v_conv_dw_b', 'new_v_conv_ln_g', 'new_v_conv_ln_b', 'new_v_conv_pw_w', 'new_v_conv_pw_b', 'new_v_gdn_conv_w', 'new_v_gdn_a_log', 'new_v_gdn_dt_bias', 'new_v_gdn_norm_g', 'new_v_w_out', 'new_v_ffn_norm_g', 'new_v_w_up', 'new_v_ffn_conv_w', 'new_v_ffn_conv_b', 'new_v_w_down', 'new_v_final_norm_g']
TWIN_LEAF_KINDS = {'loss': 'loss', 'grad_x': 'grad_x', 'grad_mix_norm_g': 'grad_w', 'grad_w_in': 'grad_w', 'grad_conv_dw_w': 'grad_w', 'grad_conv_dw_b': 'grad_w', 'grad_conv_ln_g': 'grad_w', 'grad_conv_ln_b': 'grad_w', 'grad_conv_pw_w': 'grad_w', 'grad_conv_pw_b': 'grad_w', 'grad_gdn_conv_w': 'grad_w', 'grad_gdn_a_log': 'grad_w', 'grad_gdn_dt_bias': 'grad_w', 'grad_gdn_norm_g': 'grad_w', 'grad_w_out': 'grad_w', 'grad_ffn_norm_g': 'grad_w', 'grad_w_up': 'grad_w', 'grad_ffn_conv_w': 'grad_w', 'grad_ffn_conv_b': 'grad_w', 'grad_w_down': 'grad_w', 'grad_final_norm_g': 'grad_w', 'delta_mix_norm_g': 'delta_w', 'delta_w_in': 'delta_w', 'delta_conv_dw_w': 'delta_w', 'delta_conv_dw_b': 'delta_w', 'delta_conv_ln_g': 'delta_w', 'delta_conv_ln_b': 'delta_w', 'delta_conv_pw_w': 'delta_w', 'delta_conv_pw_b': 'delta_w', 'delta_gdn_conv_w': 'delta_w', 'delta_gdn_a_log': 'delta_w', 'delta_gdn_dt_bias': 'delta_w', 'delta_gdn_norm_g': 'delta_w', 'delta_w_out': 'delta_w', 'delta_ffn_norm_g': 'delta_w', 'delta_w_up': 'delta_w', 'delta_ffn_conv_w': 'delta_w', 'delta_ffn_conv_b': 'delta_w', 'delta_w_down': 'delta_w', 'delta_final_norm_g': 'delta_w', 'new_m_mix_norm_g': 'new_m', 'new_m_w_in': 'new_m', 'new_m_conv_dw_w': 'new_m', 'new_m_conv_dw_b': 'new_m', 'new_m_conv_ln_g': 'new_m', 'new_m_conv_ln_b': 'new_m', 'new_m_conv_pw_w': 'new_m', 'new_m_conv_pw_b': 'new_m', 'new_m_gdn_conv_w': 'new_m', 'new_m_gdn_a_log': 'new_m', 'new_m_gdn_dt_bias': 'new_m', 'new_m_gdn_norm_g': 'new_m', 'new_m_w_out': 'new_m', 'new_m_ffn_norm_g': 'new_m', 'new_m_w_up': 'new_m', 'new_m_ffn_conv_w': 'new_m', 'new_m_ffn_conv_b': 'new_m', 'new_m_w_down': 'new_m', 'new_m_final_norm_g': 'new_m', 'new_v_mix_norm_g': 'new_v', 'new_v_w_in': 'new_v', 'new_v_conv_dw_w': 'new_v', 'new_v_conv_dw_b': 'new_v', 'new_v_conv_ln_g': 'new_v', 'new_v_conv_ln_b': 'new_v', 'new_v_conv_pw_w': 'new_v', 'new_v_conv_pw_b': 'new_v', 'new_v_gdn_conv_w': 'new_v', 'new_v_gdn_a_log': 'new_v', 'new_v_gdn_dt_bias': 'new_v', 'new_v_gdn_norm_g': 'new_v', 'new_v_w_out': 'new_v', 'new_v_ffn_norm_g': 'new_v', 'new_v_w_up': 'new_v', 'new_v_ffn_conv_w': 'new_v', 'new_v_ffn_conv_b': 'new_v', 'new_v_w_down': 'new_v', 'new_v_final_norm_g': 'new_v'}


def _forward(args):
    return _fwd_reference(*[args[k] for k in FWD_PARAMS])


def _output_shape():
    out = _jax.eval_shape(lambda: _forward(_fwd_setup_inputs(0)))
    return out.shape, out.dtype

N_MICROBATCH = 1
ADAM_LR = 0.001
ADAM_B1 = 0.9
ADAM_B2 = 0.999
ADAM_EPS = 1e-08
ADAM_WD = 0.01
ADAM_STEP = 10
PER_EXAMPLE_BATCH_AXIS = {'x': 0, 'loss_target': 0}
SHARED_INPUTS = []
_WEIGHT_DTYPES = {'mix_norm_g': _jnp.float32, 'w_in': _jnp.float32, 'conv_dw_w': _jnp.float32, 'conv_dw_b': _jnp.float32, 'conv_ln_g': _jnp.float32, 'conv_ln_b': _jnp.float32, 'conv_pw_w': _jnp.float32, 'conv_pw_b': _jnp.float32, 'gdn_conv_w': _jnp.float32, 'gdn_a_log': _jnp.float32, 'gdn_dt_bias': _jnp.float32, 'gdn_norm_g': _jnp.float32, 'w_out': _jnp.float32, 'ffn_norm_g': _jnp.float32, 'w_up': _jnp.float32, 'ffn_conv_w': _jnp.float32, 'ffn_conv_b': _jnp.float32, 'w_down': _jnp.float32, 'final_norm_g': _jnp.float32}
MOMENT_SCALE = {'mix_norm_g': 7.098349e-02, 'w_in': 4.458806e-02, 'conv_dw_w': 5.267172e-02, 'conv_dw_b': 1.170029e-01, 'conv_ln_g': 6.114939e-02, 'conv_ln_b': 5.591200e-02, 'conv_pw_w': 5.093796e-02, 'conv_pw_b': 9.354805e-02, 'gdn_conv_w': 4.649406e-02, 'gdn_a_log': 3.571071e-01, 'gdn_dt_bias': 3.639508e-01, 'gdn_norm_g': 1.644744e-01, 'w_out': 5.029514e-02, 'ffn_norm_g': 6.082978e-02, 'w_up': 2.563263e-02, 'ffn_conv_w': 2.614480e-02, 'ffn_conv_b': 2.499340e-02, 'w_down': 4.203304e-02, 'final_norm_g': 1.599892e+01}


def _to_microbatches(a, axis):
    t = _jnp.moveaxis(a, axis, 0)
    t = t.reshape((N_MICROBATCH, t.shape[0] // N_MICROBATCH) + t.shape[1:])
    return _jnp.moveaxis(t, 1, axis + 1)


def setup_inputs(seed: int = 0) -> dict:
    inp = _fwd_setup_inputs(seed)
    key = _jax.random.fold_in(_jax.random.key(seed), 7919)
    shape, _ = _output_shape()
    out = dict(inp)
    out["loss_target"] = _jax.random.normal(_jax.random.fold_in(key, 0), shape, _jnp.float32)
    for i, name in enumerate(TWIN_WEIGHTS):
        w = inp[name].astype(_jnp.float32)
        if MOMENT_SCALE is None:
            s = _jnp.sqrt(_jnp.mean(_jnp.square(w)) + 1e-30)
        else:
            s = MOMENT_SCALE[name]
        km, kv = _jax.random.split(_jax.random.fold_in(key, i + 1))
        out[name] = w
        out["m_" + name] = s * _jax.random.normal(km, w.shape, _jnp.float32)
        out["v_" + name] = (s * s) * _jax.random.uniform(kv, w.shape, _jnp.float32, 0.5, 1.5)
    if N_MICROBATCH > 1:
        for name, axis in PER_EXAMPLE_BATCH_AXIS.items():
            out[name] = _to_microbatches(out[name], axis)
    return {'x': out['x'], 'mix_norm_g': out['mix_norm_g'], 'w_in': out['w_in'], 'conv_dw_w': out['conv_dw_w'], 'conv_dw_b': out['conv_dw_b'], 'conv_ln_g': out['conv_ln_g'], 'conv_ln_b': out['conv_ln_b'], 'conv_pw_w': out['conv_pw_w'], 'conv_pw_b': out['conv_pw_b'], 'gdn_conv_w': out['gdn_conv_w'], 'gdn_a_log': out['gdn_a_log'], 'gdn_dt_bias': out['gdn_dt_bias'], 'gdn_norm_g': out['gdn_norm_g'], 'w_out': out['w_out'], 'ffn_norm_g': out['ffn_norm_g'], 'w_up': out['w_up'], 'ffn_conv_w': out['ffn_conv_w'], 'ffn_conv_b': out['ffn_conv_b'], 'w_down': out['w_down'], 'final_norm_g': out['final_norm_g'], 'loss_target': out['loss_target'], 'm_mix_norm_g': out['m_mix_norm_g'], 'm_w_in': out['m_w_in'], 'm_conv_dw_w': out['m_conv_dw_w'], 'm_conv_dw_b': out['m_conv_dw_b'], 'm_conv_ln_g': out['m_conv_ln_g'], 'm_conv_ln_b': out['m_conv_ln_b'], 'm_conv_pw_w': out['m_conv_pw_w'], 'm_conv_pw_b': out['m_conv_pw_b'], 'm_gdn_conv_w': out['m_gdn_conv_w'], 'm_gdn_a_log': out['m_gdn_a_log'], 'm_gdn_dt_bias': out['m_gdn_dt_bias'], 'm_gdn_norm_g': out['m_gdn_norm_g'], 'm_w_out': out['m_w_out'], 'm_ffn_norm_g': out['m_ffn_norm_g'], 'm_w_up': out['m_w_up'], 'm_ffn_conv_w': out['m_ffn_conv_w'], 'm_ffn_conv_b': out['m_ffn_conv_b'], 'm_w_down': out['m_w_down'], 'm_final_norm_g': out['m_final_norm_g'], 'v_mix_norm_g': out['v_mix_norm_g'], 'v_w_in': out['v_w_in'], 'v_conv_dw_w': out['v_conv_dw_w'], 'v_conv_dw_b': out['v_conv_dw_b'], 'v_conv_ln_g': out['v_conv_ln_g'], 'v_conv_ln_b': out['v_conv_ln_b'], 'v_conv_pw_w': out['v_conv_pw_w'], 'v_conv_pw_b': out['v_conv_pw_b'], 'v_gdn_conv_w': out['v_gdn_conv_w'], 'v_gdn_a_log': out['v_gdn_a_log'], 'v_gdn_dt_bias': out['v_gdn_dt_bias'], 'v_gdn_norm_g': out['v_gdn_norm_g'], 'v_w_out': out['v_w_out'], 'v_ffn_norm_g': out['v_ffn_norm_g'], 'v_w_up': out['v_w_up'], 'v_ffn_conv_w': out['v_ffn_conv_w'], 'v_ffn_conv_b': out['v_ffn_conv_b'], 'v_w_down': out['v_w_down'], 'v_final_norm_g': out['v_final_norm_g']}


def _loss(weights, diff, rest, loss_target):
    with _jax.named_scope("forward"):
        args = {**rest, TWIN_DIFF_INPUT: diff, **{k: w.astype(_WEIGHT_DTYPES[k]) for k, w in weights.items()}}
        y = _forward(args)
    with _jax.named_scope("loss_head"):
        err = _jnp.square(y.astype(_jnp.float32) - loss_target)
        return 0.5 * _jnp.sum(_jnp.mean(err, axis=-1)) if err.ndim else 0.5 * err


def _adamw(w, g, m, v):
    m = ADAM_B1 * m + (1.0 - ADAM_B1) * g
    v = ADAM_B2 * v + (1.0 - ADAM_B2) * _jnp.square(g)
    m_hat = m / (1.0 - ADAM_B1 ** ADAM_STEP)
    v_hat = v / (1.0 - ADAM_B2 ** ADAM_STEP)
    delta = -ADAM_LR * (m_hat / (_jnp.sqrt(v_hat) + ADAM_EPS) + ADAM_WD * w)
    return delta, m, v


def reference(x, mix_norm_g, w_in, conv_dw_w, conv_dw_b, conv_ln_g, conv_ln_b, conv_pw_w, conv_pw_b, gdn_conv_w, gdn_a_log, gdn_dt_bias, gdn_norm_g, w_out, ffn_norm_g, w_up, ffn_conv_w, ffn_conv_b, w_down, final_norm_g, loss_target, m_mix_norm_g, m_w_in, m_conv_dw_w, m_conv_dw_b, m_conv_ln_g, m_conv_ln_b, m_conv_pw_w, m_conv_pw_b, m_gdn_conv_w, m_gdn_a_log, m_gdn_dt_bias, m_gdn_norm_g, m_w_out, m_ffn_norm_g, m_w_up, m_ffn_conv_w, m_ffn_conv_b, m_w_down, m_final_norm_g, v_mix_norm_g, v_w_in, v_conv_dw_w, v_conv_dw_b, v_conv_ln_g, v_conv_ln_b, v_conv_pw_w, v_conv_pw_b, v_gdn_conv_w, v_gdn_a_log, v_gdn_dt_bias, v_gdn_norm_g, v_w_out, v_ffn_norm_g, v_w_up, v_ffn_conv_w, v_ffn_conv_b, v_w_down, v_final_norm_g):
    given = dict(x=x, mix_norm_g=mix_norm_g, w_in=w_in, conv_dw_w=conv_dw_w, conv_dw_b=conv_dw_b, conv_ln_g=conv_ln_g, conv_ln_b=conv_ln_b, conv_pw_w=conv_pw_w, conv_pw_b=conv_pw_b, gdn_conv_w=gdn_conv_w, gdn_a_log=gdn_a_log, gdn_dt_bias=gdn_dt_bias, gdn_norm_g=gdn_norm_g, w_out=w_out, ffn_norm_g=ffn_norm_g, w_up=w_up, ffn_conv_w=ffn_conv_w, ffn_conv_b=ffn_conv_b, w_down=w_down, final_norm_g=final_norm_g, loss_target=loss_target, m_mix_norm_g=m_mix_norm_g, m_w_in=m_w_in, m_conv_dw_w=m_conv_dw_w, m_conv_dw_b=m_conv_dw_b, m_conv_ln_g=m_conv_ln_g, m_conv_ln_b=m_conv_ln_b, m_conv_pw_w=m_conv_pw_w, m_conv_pw_b=m_conv_pw_b, m_gdn_conv_w=m_gdn_conv_w, m_gdn_a_log=m_gdn_a_log, m_gdn_dt_bias=m_gdn_dt_bias, m_gdn_norm_g=m_gdn_norm_g, m_w_out=m_w_out, m_ffn_norm_g=m_ffn_norm_g, m_w_up=m_w_up, m_ffn_conv_w=m_ffn_conv_w, m_ffn_conv_b=m_ffn_conv_b, m_w_down=m_w_down, m_final_norm_g=m_final_norm_g, v_mix_norm_g=v_mix_norm_g, v_w_in=v_w_in, v_conv_dw_w=v_conv_dw_w, v_conv_dw_b=v_conv_dw_b, v_conv_ln_g=v_conv_ln_g, v_conv_ln_b=v_conv_ln_b, v_conv_pw_w=v_conv_pw_w, v_conv_pw_b=v_conv_pw_b, v_gdn_conv_w=v_gdn_conv_w, v_gdn_a_log=v_gdn_a_log, v_gdn_dt_bias=v_gdn_dt_bias, v_gdn_norm_g=v_gdn_norm_g, v_w_out=v_w_out, v_ffn_norm_g=v_ffn_norm_g, v_w_up=v_w_up, v_ffn_conv_w=v_ffn_conv_w, v_ffn_conv_b=v_ffn_conv_b, v_w_down=v_w_down, v_final_norm_g=v_final_norm_g)
    weights = {n: given[n] for n in TWIN_WEIGHTS}
    shared = {n: given[n] for n in SHARED_INPUTS}
    per_example = {n: given[n] for n in ['x']}
    grad_fn = _jax.value_and_grad(_loss, argnums=(0, 1))

    def one_microbatch(ex, loss_target):
        ex = dict(ex)
        diff = ex.pop(TWIN_DIFF_INPUT)
        return grad_fn(weights, diff, {**shared, **ex}, loss_target)

    if N_MICROBATCH == 1:
        loss, (grad_w, grad_x) = one_microbatch(per_example, given["loss_target"])
    else:
        def body(carry, xs):
            loss_sum, grad_sum = carry
            l_k, (gw_k, gx_k) = one_microbatch(xs[0], xs[1])
            with _jax.named_scope("update"):
                return (loss_sum + l_k, _jax.tree.map(_jnp.add, grad_sum, gw_k)), gx_k

        init = (_jnp.zeros((), _jnp.float32), _jax.tree.map(_jnp.zeros_like, weights))
        (loss, grad_w), grad_x = _jax.lax.scan(body, init, (per_example, given["loss_target"]))
    with _jax.named_scope("update"):
        delta_w, new_m, new_v = {}, {}, {}
        for n in TWIN_WEIGHTS:
            delta_w[n], new_m[n], new_v[n] = _adamw(weights[n], grad_w[n], given["m_" + n], given["v_" + n])
    return (loss, grad_x, *[grad_w[n] for n in TWIN_WEIGHTS], *[delta_w[n] for n in TWIN_WEIGHTS],
            *[new_m[n] for n in TWIN_WEIGHTS], *[new_v[n] for n in TWIN_WEIGHTS])
```

```python
import functools

import jax
import jax.numpy as jnp
from jax import lax
from jax.experimental import pallas as pl
from jax.experimental.pallas import tpu as pltpu

F32 = jnp.float32
BF16 = jnp.bfloat16
EPS = 1e-6
CHUNK = 64
HEAD = 128
HIGHEST = lax.Precision.HIGHEST
VMEM_LIMIT = 56 * 1024 * 1024


def _pick(dim, cands):
    for c in cands:
        if dim % c == 0:
            return c
    return dim


def _cparams(sem):
    return pltpu.CompilerParams(dimension_semantics=sem, vmem_limit_bytes=VMEM_LIMIT)


def matmul(a, b, *, ta=False, tb=False, out_dtype=F32, res=None, bias=None, b_koff=0, name):
    if ta:
        kdim, m = a.shape
    else:
        m, kdim = a.shape
    if tb:
        n, kb = b.shape
    else:
        kb, n = b.shape
    assert kb >= kdim + b_koff, (a.shape, b.shape, ta, tb)
    tm = _pick(m, (1024, 512, 256, 128))
    tn = _pick(n, (1024, 512, 256, 128))
    tk = _pick(kdim, (512, 256, 128))
    nk = kdim // tk
    assert b_koff % tk == 0
    ko = b_koff // tk
    dims = (((0 if ta else 1,), (1 if tb else 0,)), ((), ()))

    def body(*refs):
        a_ref, b_ref = refs[0], refs[1]
        pos = 2
        bias_ref = res_ref = None
        if bias is not None:
            bias_ref = refs[pos]
            pos += 1
        if res is not None:
            res_ref = refs[pos]
            pos += 1
        o_ref, acc_ref = refs[pos], refs[pos + 1]
        k = pl.program_id(2)

        @pl.when(k == 0)
        def _():
            acc_ref[...] = jnp.zeros_like(acc_ref)

        acc_ref[...] += lax.dot_general(a_ref[...].astype(BF16), b_ref[...].astype(BF16), dims,
                                        preferred_element_type=F32)

        @pl.when(k == nk - 1)
        def _():
            r = acc_ref[...]
            if bias_ref is not None:
                r = r + bias_ref[...]
            if res_ref is not None:
                r = r + res_ref[...]
            o_ref[...] = r.astype(o_ref.dtype)

    a_spec = pl.BlockSpec((tk, tm), lambda i, j, k: (k, i)) if ta else pl.BlockSpec((tm, tk), lambda i, j, k: (i, k))
    b_spec = (pl.BlockSpec((tn, tk), lambda i, j, k: (j, k + ko)) if tb
              else pl.BlockSpec((tk, tn), lambda i, j, k: (k + ko, j)))
    in_specs = [a_spec, b_spec]
    args = [a, b]
    if bias is not None:
        in_specs.append(pl.BlockSpec((1, tn), lambda i, j, k: (0, j)))
        args.append(bias.reshape(1, n).astype(F32))
    if res is not None:
        in_specs.append(pl.BlockSpec((tm, tn), lambda i, j, k: (i, j)))
        args.append(res)
    return pl.pallas_call(
        body,
        name=name,
        grid=(m // tm, n // tn, nk),
        in_specs=in_specs,
        out_specs=pl.BlockSpec((tm, tn), lambda i, j, k: (i, j)),
        out_shape=jax.ShapeDtypeStruct((m, n), out_dtype),
        scratch_shapes=[pltpu.VMEM((tm, tn), F32)],
        compiler_params=_cparams(("parallel", "parallel", "arbitrary")),
    )(*args)


def _sigmoid(x):
    return 1.0 / (1.0 + jnp.exp(-x))


def _softplus(x):
    return jnp.maximum(x, 0.0) + jnp.log(1.0 + jnp.exp(-jnp.abs(x)))


def _shift_back(x, s, row):
    if s == 0:
        return x
    return jnp.where(row >= s, pltpu.roll(x, s, 0), 0.0)


def _shift_fwd(x, s, row):
    if s == 0:
        return x
    n = x.shape[0]
    return jnp.where(row < n - s, pltpu.roll(x, n - s, 0), 0.0)


def _conv_fwd(x, w_ref, kw, row):
    acc = x * w_ref[pl.ds(kw - 1, 1), :]
    for s in range(1, kw):
        acc = acc + _shift_back(x, s, row) * w_ref[pl.ds(kw - 1 - s, 1), :]
    return acc


def _conv_bwd(x, dy, w_ref, kw, row):
    dx = dy * w_ref[pl.ds(kw - 1, 1), :]
    dw = [None] * kw
    dw[kw - 1] = jnp.sum(dy * x, axis=0, keepdims=True)
    for s in range(1, kw):
        dx = dx + _shift_fwd(dy, s, row) * w_ref[pl.ds(kw - 1 - s, 1), :]
        dw[kw - 1 - s] = jnp.sum(dy * _shift_back(x, s, row), axis=0, keepdims=True)
    return dx, dw


def _rows_to_block(rows, nrows, width):
    rid = lax.broadcasted_iota(jnp.int32, (nrows, width), 0)
    out = jnp.zeros((nrows, width), F32)
    for i, r in enumerate(rows):
        out = jnp.where(rid == i, r, out)
    return out


def rmsnorm_fwd(x, g, *, name):
    t, d = x.shape
    tm = _pick(t, (256, 128))

    def body(x_ref, g_ref, o_ref):
        xv = x_ref[...]
        r = lax.rsqrt(jnp.mean(xv * xv, axis=-1, keepdims=True) + EPS)
        o_ref[...] = (xv * r * g_ref[...]).astype(o_ref.dtype)

    return pl.pallas_call(
        body, name=name, grid=(t // tm,),
        in_specs=[pl.BlockSpec((tm, d), lambda i: (i, 0)), pl.BlockSpec((1, d), lambda i: (0, 0))],
        out_specs=pl.BlockSpec((tm, d), lambda i: (i, 0)),
        out_shape=jax.ShapeDtypeStruct((t, d), BF16),
        compiler_params=_cparams(("parallel",)),
    )(x, g.reshape(1, d))


def rmsnorm_bwd(x, g, dh, dres, *, name):
    t, d = x.shape
    tm = _pick(t, (256, 128))

    def body(x_ref, g_ref, dh_ref, dres_ref, dx_ref, dxb_ref, dg_ref):
        xv = x_ref[...]
        r = lax.rsqrt(jnp.mean(xv * xv, axis=-1, keepdims=True) + EPS)
        xh = xv * r
        dy = dh_ref[...]
        dxh = dy * g_ref[...]
        dx = dres_ref[...] + r * (dxh - xh * jnp.mean(dxh * xh, axis=-1, keepdims=True))
        dx_ref[...] = dx
        dxb_ref[...] = dx.astype(BF16)

        @pl.when(pl.program_id(0) == 0)
        def _():
            dg_ref[...] = jnp.zeros_like(dg_ref)

        dg_ref[...] += jnp.sum(dy * xh, axis=0, keepdims=True)

    row = pl.BlockSpec((tm, d), lambda i: (i, 0))
    vec = pl.BlockSpec((1, d), lambda i: (0, 0))
    return pl.pallas_call(
        body, name=name, grid=(t // tm,),
        in_specs=[row, vec, row, row],
        out_specs=[row, row, vec],
        out_shape=[jax.ShapeDtypeStruct((t, d), F32), jax.ShapeDtypeStruct((t, d), BF16),
                   jax.ShapeDtypeStruct((1, d), F32)],
        compiler_params=_cparams(("arbitrary",)),
    )(x, g.reshape(1, d), dh, dres)


def loss_head(x, g, target, *, name):
    t, d = x.shape
    tm = _pick(t, (256, 128))

    def body(x_ref, g_ref, tg_ref, loss_ref, dx_ref, dxb_ref, dg_ref):
        xv = x_ref[...]
        r = lax.rsqrt(jnp.mean(xv * xv, axis=-1, keepdims=True) + EPS)
        xh = xv * r
        err = xh * g_ref[...] - tg_ref[...]
        dy = err * (1.0 / d)
        dxh = dy * g_ref[...]
        dx = r * (dxh - xh * jnp.mean(dxh * xh, axis=-1, keepdims=True))
        dx_ref[...] = dx
        dxb_ref[...] = dx.astype(BF16)

        @pl.when(pl.program_id(0) == 0)
        def _():
            dg_ref[...] = jnp.zeros_like(dg_ref)
            loss_ref[...] = jnp.zeros_like(loss_ref)

        dg_ref[...] += jnp.sum(dy * xh, axis=0, keepdims=True)
        part = jnp.sum(jnp.sum(err * err, axis=-1, keepdims=True), axis=0, keepdims=True) * (0.5 / d)
        loss_ref[...] += jnp.broadcast_to(part, loss_ref.shape)

    row = pl.BlockSpec((tm, d), lambda i: (i, 0))
    vec = pl.BlockSpec((1, d), lambda i: (0, 0))
    return pl.pallas_call(
        body, name=name, grid=(t // tm,),
        in_specs=[row, vec, row],
        out_specs=[pl.BlockSpec((8, 128), lambda i: (0, 0)), row, row, vec],
        out_shape=[jax.ShapeDtypeStruct((8, 128), F32), jax.ShapeDtypeStruct((t, d), F32),
                   jax.ShapeDtypeStruct((t, d), BF16), jax.ShapeDtypeStruct((1, d), F32)],
        compiler_params=_cparams(("arbitrary",)),
    )(x, g.reshape(1, d), target)


def _conf_forward_parts(val, gate, w_ref, b, lg, lb, kw, row):
    sg = _sigmoid(gate)
    u0 = val * sg
    u1 = _conv_fwd(u0, w_ref, kw, row) + b
    mu = jnp.mean(u1, axis=-1, keepdims=True)
    xc = u1 - mu
    rs = lax.rsqrt(jnp.mean(xc * xc, axis=-1, keepdims=True) + EPS)
    xh = xc * rs
    u2 = xh * lg + lb
    s2 = _sigmoid(u2)
    return sg, u0, rs, xh, u2, s2


def conf_fwd(p, dw_w, dw_b, ln_g, ln_b, *, nb, s, name):
    kw, ch = dw_w.shape
    ng = ch // HEAD

    def body(val_ref, gate_ref, w_ref, b_ref, lg_ref, lb_ref, o_ref):
        row = lax.broadcasted_iota(jnp.int32, (s, HEAD), 0)
        _, _, _, _, u2, s2 = _conf_forward_parts(val_ref[...], gate_ref[...], w_ref, b_ref[...], lg_ref[...],
                                                 lb_ref[...], kw, row)
        o_ref[...] = (u2 * s2).astype(o_ref.dtype)

    vec = pl.BlockSpec((1, HEAD), lambda b, g: (0, g))
    return pl.pallas_call(
        body, name=name, grid=(nb, ng),
        in_specs=[pl.BlockSpec((s, HEAD), lambda b, g: (b, g)), pl.BlockSpec((s, HEAD), lambda b, g: (b, ng + g)),
                  pl.BlockSpec((kw, HEAD), lambda b, g: (0, g)), vec, vec, vec],
        out_specs=pl.BlockSpec((s, HEAD), lambda b, g: (b, g)),
        out_shape=jax.ShapeDtypeStruct((nb * s, ch), BF16),
        compiler_params=_cparams(("parallel", "parallel")),
    )(p, p, dw_w, dw_b.reshape(1, ch), ln_g.reshape(1, ch), ln_b.reshape(1, ch))


def conf_bwd(p, dw_w, dw_b, ln_g, ln_b, du3, dout_a, *, nb, s, name):
    kw, ch = dw_w.shape
    ng = ch // HEAD

    def body(val_ref, gate_ref, w_ref, b_ref, lg_ref, lb_ref, du3_ref, doa_ref, dval_ref, dgate_ref, dw_out, sm_out):
        row = lax.broadcasted_iota(jnp.int32, (s, HEAD), 0)
        val = val_ref[...]
        sg, u0, rs, xh, u2, s2 = _conf_forward_parts(val, gate_ref[...], w_ref, b_ref[...], lg_ref[...],
                                                     lb_ref[...], kw, row)
        du2 = du3_ref[...] * (s2 * (1.0 + u2 * (1.0 - s2)))
        dlg = jnp.sum(du2 * xh, axis=0, keepdims=True)
        dlb = jnp.sum(du2, axis=0, keepdims=True)
        dxh = du2 * lg_ref[...]
        du1 = rs * (dxh - jnp.mean(dxh, axis=-1, keepdims=True) - xh * jnp.mean(dxh * xh, axis=-1, keepdims=True))
        ddb = jnp.sum(du1, axis=0, keepdims=True)
        du0, dw = _conv_bwd(u0, du1, w_ref, kw, row)
        dval_ref[...] = (du0 * sg).astype(dval_ref.dtype)
        dgate_ref[...] = (du0 * val * sg * (1.0 - sg)).astype(dgate_ref.dtype)
        for k in range(kw):
            dw_out[0, pl.ds(k, 1), :] = dw[k]
        dpb = jnp.sum(doa_ref[...].astype(F32), axis=0, keepdims=True)
        sm_out[0] = _rows_to_block([ddb, dlg, dlb, dpb], 8, HEAD)

    vec = pl.BlockSpec((1, HEAD), lambda b, g: (0, g))
    blk = pl.BlockSpec((s, HEAD), lambda b, g: (b, g))
    return pl.pallas_call(
        body, name=name, grid=(nb, ng),
        in_specs=[blk, pl.BlockSpec((s, HEAD), lambda b, g: (b, ng + g)),
                  pl.BlockSpec((kw, HEAD), lambda b, g: (0, g)), vec, vec, vec, blk, blk],
        out_specs=[blk, blk, pl.BlockSpec((1, kw, HEAD), lambda b, g: (b, 0, g)),
                   pl.BlockSpec((1, 8, HEAD), lambda b, g: (b, 0, g))],
        out_shape=[jax.ShapeDtypeStruct((nb * s, ch), BF16), jax.ShapeDtypeStruct((nb * s, ch), BF16),
                   jax.ShapeDtypeStruct((nb, kw, ch), F32), jax.ShapeDtypeStruct((nb, 8, ch), F32)],
        compiler_params=_cparams(("parallel", "parallel")),
    )(p, p, dw_w, dw_b.reshape(1, ch), ln_g.reshape(1, ch), ln_b.reshape(1, ch), du3, dout_a)


def qkvconv_fwd(p, w, *, col0, nb, s, name):
    kw, ch = w.shape
    nblk = ch // HEAD
    c0 = col0 // HEAD

    def body(x_ref, w_ref, o_ref):
        row = lax.broadcasted_iota(jnp.int32, (s, HEAD), 0)
        c = _conv_fwd(x_ref[...], w_ref, kw, row)
        o_ref[...] = c * _sigmoid(c)

    return pl.pallas_call(
        body, name=name, grid=(nb, nblk),
        in_specs=[pl.BlockSpec((s, HEAD), lambda b, j: (b, c0 + j)), pl.BlockSpec((kw, HEAD), lambda b, j: (0, j))],
        out_specs=pl.BlockSpec((s, HEAD), lambda b, j: (b, j)),
        out_shape=jax.ShapeDtypeStruct((nb * s, ch), F32),
        compiler_params=_cparams(("parallel", "parallel")),
    )(p, w)


def qkvconv_bwd(p, w, dy, *, col0, nb, s, name):
    kw, ch = w.shape
    nblk = ch // HEAD
    c0 = col0 // HEAD

    def body(x_ref, w_ref, dy_ref, dx_ref, dw_out):
        row = lax.broadcasted_iota(jnp.int32, (s, HEAD), 0)
        xv = x_ref[...]
        c = _conv_fwd(xv, w_ref, kw, row)
        sc = _sigmoid(c)
        dc = dy_ref[...] * (sc * (1.0 + c * (1.0 - sc)))
        dx, dw = _conv_bwd(xv, dc, w_ref, kw, row)
        dx_ref[...] = dx.astype(dx_ref.dtype)
        dw_out[0] = _rows_to_block(dw, 8, HEAD)

    blk = pl.BlockSpec((s, HEAD), lambda b, j: (b, j))
    return pl.pallas_call(
        body, name=name, grid=(nb, nblk),
        in_specs=[pl.BlockSpec((s, HEAD), lambda b, j: (b, c0 + j)), pl.BlockSpec((kw, HEAD), lambda b, j: (0, j)), blk],
        out_specs=[blk, pl.BlockSpec((1, 8, HEAD), lambda b, j: (b, 0, j))],
        out_shape=[jax.ShapeDtypeStruct((nb * s, ch), BF16), jax.ShapeDtypeStruct((nb, 8, ch), F32)],
        compiler_params=_cparams(("parallel", "parallel")),
    )(p, w, dy)


def ffn_act_fwd(up, w, b, *, nb, s, name):
    kw, dff = w.shape
    cb = _pick(dff, (256, 128))
    nblk = dff // cb

    def body(g_ref, u_ref, w_ref, b_ref, o_ref):
        row = lax.broadcasted_iota(jnp.int32, (s, cb), 0)
        gc = _conv_fwd(g_ref[...], w_ref, kw, row) + b_ref[...]
        o_ref[...] = (gc * _sigmoid(gc) * u_ref[...]).astype(o_ref.dtype)

    return pl.pallas_call(
        body, name=name, grid=(nb, nblk),
        in_specs=[pl.BlockSpec((s, cb), lambda i, j: (i, j)), pl.BlockSpec((s, cb), lambda i, j: (i, nblk + j)),
                  pl.BlockSpec((kw, cb), lambda i, j: (0, j)), pl.BlockSpec((1, cb), lambda i, j: (0, j))],
        out_specs=pl.BlockSpec((s, cb), lambda i, j: (i, j)),
        out_shape=jax.ShapeDtypeStruct((nb * s, dff), BF16),
        compiler_params=_cparams(("parallel", "parallel")),
    )(up, up, w, b.reshape(1, dff))


def ffn_act_bwd(up, w, b, dact, *, nb, s, name):
    kw, dff = w.shape
    cb = _pick(dff, (256, 128))
    nblk = dff // cb

    def body(g_ref, u_ref, w_ref, b_ref, da_ref, dg_ref, du_ref, sm_out):
        row = lax.broadcasted_iota(jnp.int32, (s, cb), 0)
        gv = g_ref[...]
        gc = _conv_fwd(gv, w_ref, kw, row) + b_ref[...]
        sc = _sigmoid(gc)
        da = da_ref[...].astype(F32)
        du_ref[...] = (da * gc * sc).astype(du_ref.dtype)
        dgc = da * u_ref[...] * (sc * (1.0 + gc * (1.0 - sc)))
        dgate, dw = _conv_bwd(gv, dgc, w_ref, kw, row)
        dg_ref[...] = dgate.astype(dg_ref.dtype)
        sm_out[0] = _rows_to_block(dw + [jnp.sum(dgc, axis=0, keepdims=True)], 8, cb)

    blk = pl.BlockSpec((s, cb), lambda i, j: (i, j))
    return pl.pallas_call(
        body, name=name, grid=(nb, nblk),
        in_specs=[blk, pl.BlockSpec((s, cb), lambda i, j: (i, nblk + j)),
                  pl.BlockSpec((kw, cb), lambda i, j: (0, j)), pl.BlockSpec((1, cb), lambda i, j: (0, j)), blk],
        out_specs=[blk, blk, pl.BlockSpec((1, 8, cb), lambda i, j: (i, 0, j))],
        out_shape=[jax.ShapeDtypeStruct((nb * s, dff), BF16), jax.ShapeDtypeStruct((nb * s, dff), BF16),
                   jax.ShapeDtypeStruct((nb, 8, dff), F32)],
        compiler_params=_cparams(("parallel", "parallel")),
    )(up, up, w, b.reshape(1, dff), dact)


def _dot(a, b, dims, prec=None):
    return lax.dot_general(a, b, (dims, ((), ())), precision=prec, preferred_element_type=F32)


def _mm_nn(a, b):
    return _dot(a.astype(BF16), b.astype(BF16), ((1,), (0,)))


def _mm_nt(a, b):
    return _dot(a.astype(BF16), b.astype(BF16), ((1,), (1,)))


def _mm_tn(a, b):
    return _dot(a.astype(BF16), b.astype(BF16), ((0,), (0,)))


def _mm_hi(a, b):
    return _dot(a, b, ((1,), (0,)), HIGHEST)


def _gdn_chunk(qc, kc, vc, zc, braw, araw, alog, dtb, ng, state):
    c = qc.shape[0]
    ri = lax.broadcasted_iota(jnp.int32, (c, c), 0)
    ci = lax.broadcasted_iota(jnp.int32, (c, c), 1)
    causal = ri >= ci
    strict = ri > ci
    eye = (ri == ci).astype(F32)
    tri = causal.astype(F32)
    ones = jnp.ones((c, c), F32)

    q = qc * lax.rsqrt(jnp.sum(qc * qc, axis=-1, keepdims=True) + EPS) * (HEAD ** -0.5)
    k = kc * lax.rsqrt(jnp.sum(kc * kc, axis=-1, keepdims=True) + EPS)
    beta = _sigmoid(braw)
    g = -jnp.exp(alog) * _softplus(araw + dtb)

    g_c = jnp.broadcast_to(g, (c, c))
    g_w = jnp.broadcast_to(g, (c, HEAD))
    gi = _mm_hi(tri, g_c)
    gj = _dot(ones, gi * eye, ((1,), (1,)), HIGHEST)
    gw = _mm_hi(tri, g_w)
    glast = jnp.sum(g_w, axis=0, keepdims=True)
    decay = jnp.where(causal, jnp.exp(jnp.where(causal, gi - gj, 0.0)), 0.0)

    kb = k * beta
    vb = vc * beta
    lmat = jnp.where(strict, _mm_nt(kb, k) * decay, 0.0)
    x = -lmat
    ainv = eye + x
    p = 1
    while 2 * p < c:
        x = _mm_hi(x, x)
        ainv = _mm_hi(ainv, eye + x)
        p *= 2
    u = _mm_hi(ainv, vb)
    w = _mm_hi(ainv, kb * jnp.exp(gw))
    qk = jnp.where(causal, _mm_nt(q, k) * decay, 0.0)

    v_new = u - _mm_nn(w, state)
    o = _mm_nn(q * jnp.exp(gw), state) + _mm_nn(qk, v_new)
    k_dec = k * jnp.exp(glast - gw)
    new_state = state * jnp.exp(glast) + _mm_tn(k_dec, v_new)

    o = o * lax.rsqrt(jnp.mean(o * o, axis=-1, keepdims=True) + EPS) * ng
    o = o * (zc * _sigmoid(zc))
    return o, new_state


def _gdn_specs(nb, s, nh, nqk, zc0, n_chunks):
    rep = nh // nqk
    return dict(
        q=pl.BlockSpec((s, HEAD), lambda b, h: (b, h // rep)),
        k=pl.BlockSpec((s, HEAD), lambda b, h: (b, nqk + h // rep)),
        v=pl.BlockSpec((s, HEAD), lambda b, h: (b, 2 * nqk + h)),
        z=pl.BlockSpec((s, HEAD), lambda b, h: (b, zc0 + h)),
        ba=pl.BlockSpec((s, HEAD), lambda b, h: (b, 0)),
        gp=pl.BlockSpec((8, HEAD), lambda b, h: (0, 0)),
        head=pl.BlockSpec((s, HEAD), lambda b, h: (b, h)),
        st=pl.BlockSpec((1, 1, n_chunks, HEAD, HEAD), lambda b, h: (b, h, 0, 0, 0)),
    )


def _gdn_scalars(gp_ref, h):
    lane = lax.broadcasted_iota(jnp.int32, (1, HEAD), 1)
    sel = (lane == h).astype(F32)
    alog = jnp.sum(gp_ref[pl.ds(0, 1), :] * sel, axis=-1, keepdims=True)
    dtb = jnp.sum(gp_ref[pl.ds(1, 1), :] * sel, axis=-1, keepdims=True)
    return alog, dtb, gp_ref[pl.ds(2, 1), :], sel


def gdn_fwd(qkvc, p, pba, gp, *, nb, s, nh, nqk, zcol, name):
    n_chunks = s // CHUNK
    sp = _gdn_specs(nb, s, nh, nqk, zcol // HEAD, n_chunks)

    def body(q_ref, k_ref, v_ref, z_ref, ba_ref, gp_ref, o_ref, st_ref):
        h = pl.program_id(1)
        alog, dtb, ng, _ = _gdn_scalars(gp_ref, h)
        lane = lax.broadcasted_iota(jnp.int32, (CHUNK, HEAD), 1)
        sel_b = (lane == h).astype(F32)
        sel_a = (lane == nh + h).astype(F32)

        def step(n, state):
            rows = pl.ds(pl.multiple_of(n * CHUNK, CHUNK), CHUNK)
            ba = ba_ref[rows, :]
            braw = jnp.sum(ba * sel_b, axis=-1, keepdims=True)
            araw = jnp.sum(ba * sel_a, axis=-1, keepdims=True)
            st_ref[0, 0, n] = state
            o, new_state = _gdn_chunk(q_ref[rows, :], k_ref[rows, :], v_ref[rows, :], z_ref[rows, :],
                                      braw, araw, alog, dtb, ng, state)
            o_ref[rows, :] = o.astype(o_ref.dtype)
            return new_state

        lax.fori_loop(0, n_chunks, step, jnp.zeros((HEAD, HEAD), F32))

    return pl.pallas_call(
        body, name=name, grid=(nb, nh),
        in_specs=[sp["q"], sp["k"], sp["v"], sp["z"], sp["ba"], sp["gp"]],
        out_specs=[sp["head"], sp["st"]],
        out_shape=[jax.ShapeDtypeStruct((nb * s, nh * HEAD), BF16),
                   jax.ShapeDtypeStruct((nb, nh, n_chunks, HEAD, HEAD), F32)],
        compiler_params=_cparams(("parallel", "parallel")),
    )(qkvc, qkvc, qkvc, p, pba, gp)


def gdn_bwd(qkvc, p, pba, gp, states, dout, *, nb, s, nh, nqk, zcol, name):
    n_chunks = s // CHUNK
    rep = nh // nqk
    sp = _gdn_specs(nb, s, nh, nqk, zcol // HEAD, n_chunks)

    def body(q_ref, k_ref, v_ref, z_ref, ba_ref, gp_ref, st_ref, do_ref,
             dq_ref, dk_ref, dv_ref, dz_ref, dba_ref, dgp_ref):
        h = pl.program_id(1)
        alog, dtb, ng, sel_row = _gdn_scalars(gp_ref, h)
        lane = lax.broadcasted_iota(jnp.int32, (CHUNK, HEAD), 1)
        sel_b = (lane == h).astype(F32)
        sel_a = (lane == nh + h).astype(F32)

        @pl.when(h == 0)
        def _():
            dba_ref[...] = jnp.zeros_like(dba_ref)
            dgp_ref[...] = jnp.zeros_like(dgp_ref)

        def step(i, carry):
            dstate, dalog, ddtb, dng = carry
            n = n_chunks - 1 - i
            rows = pl.ds(pl.multiple_of(n * CHUNK, CHUNK), CHUNK)
            ba = ba_ref[rows, :]
            braw = jnp.sum(ba * sel_b, axis=-1, keepdims=True)
            araw = jnp.sum(ba * sel_a, axis=-1, keepdims=True)
            _, vjp = jax.vjp(_gdn_chunk, q_ref[rows, :], k_ref[rows, :], v_ref[rows, :], z_ref[rows, :],
                             braw, araw, alog, dtb, ng, st_ref[0, 0, n])
            gq, gk, gv, gz, gb, ga, galog, gdtb, gng, gstate = vjp((do_ref[rows, :], dstate))

            @pl.when(h % rep == 0)
            def _():
                dq_ref[rows, :] = gq
                dk_ref[rows, :] = gk

            @pl.when(h % rep != 0)
            def _():
                dq_ref[rows, :] += gq
                dk_ref[rows, :] += gk

            dv_ref[rows, :] = gv
            dz_ref[rows, :] = gz.astype(dz_ref.dtype)
            dba_ref[rows, :] += gb * sel_b + ga * sel_a
            return gstate, dalog + galog, ddtb + gdtb, dng + gng

        zero11 = jnp.zeros((1, 1), F32)
        _, dalog, ddtb, dng = lax.fori_loop(
            0, n_chunks, step, (jnp.zeros((HEAD, HEAD), F32), zero11, zero11, jnp.zeros((1, HEAD), F32)))
        dgp_ref[0] += _rows_to_block([dalog * sel_row, ddtb * sel_row, dng], 8, HEAD)

    qk_out = pl.BlockSpec((s, HEAD), lambda b, h: (b, h // rep))
    return pl.pallas_call(
        body, name=name, grid=(nb, nh),
        in_specs=[sp["q"], sp["k"], sp["v"], sp["z"], sp["ba"], sp["gp"], sp["st"], sp["head"]],
        out_specs=[qk_out, qk_out, sp["head"], sp["head"], sp["ba"], pl.BlockSpec((1, 8, HEAD), lambda b, h: (b, 0, 0))],
        out_shape=[jax.ShapeDtypeStruct((nb * s, nqk * HEAD), F32), jax.ShapeDtypeStruct((nb * s, nqk * HEAD), F32),
                   jax.ShapeDtypeStruct((nb * s, nh * HEAD), F32), jax.ShapeDtypeStruct((nb * s, nh * HEAD), BF16),
                   jax.ShapeDtypeStruct((nb * s, HEAD), F32), jax.ShapeDtypeStruct((nb, 8, HEAD), F32)],
        compiler_params=_cparams(("parallel", "arbitrary")),
    )(qkvc, qkvc, qkvc, p, pba, gp, states, dout)


ADAM_LR = 0.001
ADAM_B1 = 0.9
ADAM_B2 = 0.999
ADAM_EPS = 1e-08
ADAM_WD = 0.01
ADAM_STEP = 10
EW_BLOCK_BYTES = 1 << 20


def _row_tile(rows, cols):
    for tr in (1024, 512, 256, 128, 64, 32, 16, 8):
        if rows % tr == 0 and tr * cols * 4 <= EW_BLOCK_BYTES:
            return tr
    return rows


def sum_slots(r, *, name):
    n = r.shape[0]
    shape = r.shape[1:]
    cols = shape[-1]
    r3 = r.reshape(n, -1, cols)
    rows = r3.shape[1]
    tr = _row_tile(rows, cols)

    def body(r_ref, o_ref):
        acc = r_ref[0].astype(F32)
        for i in range(1, n):
            acc = acc + r_ref[i].astype(F32)
        o_ref[...] = acc

    out = pl.pallas_call(
        body, name=name, grid=(rows // tr,),
        in_specs=[pl.BlockSpec((n, tr, cols), lambda i: (0, i, 0))],
        out_specs=pl.BlockSpec((tr, cols), lambda i: (i, 0)),
        out_shape=jax.ShapeDtypeStruct((rows, cols), F32),
        compiler_params=_cparams(("parallel",)),
    )(r3)
    return out.reshape(shape)


def adamw(g_parts, w, m, v, *, name):
    shape = w.shape
    cols = shape[-1]
    to2d = lambda a: a.reshape(-1, cols)
    rows = to2d(w).shape[0]
    tr = _row_tile(rows, cols)
    npart = len(g_parts)
    c1 = 1.0 - ADAM_B1 ** ADAM_STEP
    c2 = 1.0 - ADAM_B2 ** ADAM_STEP

    def body(*refs):
        w_ref, m_ref, v_ref = refs[npart:npart + 3]
        g_ref, d_ref, nm_ref, nv_ref = refs[npart + 3:]
        g = refs[0][...]
        for i in range(1, npart):
            g = g + refs[i][...]
        nm = ADAM_B1 * m_ref[...] + (1.0 - ADAM_B1) * g
        nv = ADAM_B2 * v_ref[...] + (1.0 - ADAM_B2) * (g * g)
        g_ref[...] = g
        nm_ref[...] = nm
        nv_ref[...] = nv
        d_ref[...] = -ADAM_LR * ((nm / c1) / (jnp.sqrt(nv / c2) + ADAM_EPS) + ADAM_WD * w_ref[...])

    blk = pl.BlockSpec((tr, cols), lambda i: (i, 0))
    outs = pl.pallas_call(
        body, name=name, grid=(rows // tr,),
        in_specs=[blk] * (npart + 3),
        out_specs=[blk] * 4,
        out_shape=[jax.ShapeDtypeStruct((rows, cols), F32)] * 4,
        compiler_params=_cparams(("parallel",)),
    )(*[to2d(a) for a in g_parts], to2d(w), to2d(m), to2d(v))
    return tuple(o.reshape(shape) for o in outs)


MESH = pl.DeviceIdType.MESH
ANY = pl.BlockSpec(memory_space=pl.ANY)


def _xy_peers():
    x, y = lax.axis_index("x"), lax.axis_index("y")
    peers = []
    for fx, fy in ((0, 1), (1, 0), (1, 1)):
        px = 1 - x if fx else x
        py = 1 - y if fy else y
        peers.append((2 * px + py, px, py))
    return 2 * x + y, peers


def xy_exchange(arrs, *, gather, name):
    n = len(arrs)

    def body(*refs):
        ins, outs = refs[:n], refs[n:2 * n]
        send_sems, recv_sems, local_sems = refs[2 * n:]
        me, peers = _xy_peers()
        c = lax.axis_index("c")
        copies = []
        for k in range(n):
            src_own = ins[k] if gather else ins[k].at[me]
            local = pltpu.make_async_copy(src_own, outs[k].at[me], local_sems.at[k])
            local.start()
            copies.append(local)
            for j, (slot, px, py) in enumerate(peers):
                rc = pltpu.make_async_remote_copy(
                    src_ref=ins[k] if gather else ins[k].at[slot],
                    dst_ref=outs[k].at[me],
                    send_sem=send_sems.at[3 * k + j], recv_sem=recv_sems.at[3 * k + j],
                    device_id=(px, py, c), device_id_type=MESH)
                rc.start()
                copies.append(rc)
        for cp in copies:
            cp.wait()

    out_shape = [jax.ShapeDtypeStruct((4,) + (a.shape if gather else a.shape[1:]), a.dtype) for a in arrs]
    return pl.pallas_call(
        body, name=name,
        in_specs=[ANY] * n, out_specs=[ANY] * n, out_shape=out_shape,
        scratch_shapes=[pltpu.SemaphoreType.DMA((3 * n,)), pltpu.SemaphoreType.DMA((3 * n,)),
                        pltpu.SemaphoreType.DMA((n,))],
    )(*arrs)


def sibling_swap(arrs, *, name):
    n = len(arrs)

    def body(*refs):
        ins, outs = refs[:n], refs[n:2 * n]
        send_sems, recv_sems = refs[2 * n:]
        peer = (lax.axis_index("x"), lax.axis_index("y"), 1 - lax.axis_index("c"))
        copies = [pltpu.make_async_remote_copy(src_ref=ins[k], dst_ref=outs[k], send_sem=send_sems.at[k],
                                               recv_sem=recv_sems.at[k], device_id=peer, device_id_type=MESH)
                  for k in range(n)]
        for cp in copies:
            cp.start()
        for cp in copies:
            cp.wait()

    return pl.pallas_call(
        body, name=name,
        in_specs=[ANY] * n, out_specs=[ANY] * n,
        out_shape=[jax.ShapeDtypeStruct(a.shape, a.dtype) for a in arrs],
        scratch_shapes=[pltpu.SemaphoreType.DMA((n,)), pltpu.SemaphoreType.DMA((n,))],
    )(*arrs)


def allreduce_small(vec, *, name):
    r = vec.shape[0]

    def body(v_ref, o_ref, slots, send_sems, recv_sems):
        x, y, c = lax.axis_index("x"), lax.axis_index("y"), lax.axis_index("c")
        me = 4 * x + 2 * y + c
        slots[me] = v_ref[...]
        copies = []
        for j in range(1, 8):
            px = 1 - x if j & 4 else x
            py = 1 - y if j & 2 else y
            pc = 1 - c if j & 1 else c
            rc = pltpu.make_async_remote_copy(src_ref=v_ref, dst_ref=slots.at[me], send_sem=send_sems.at[j - 1],
                                              recv_sem=recv_sems.at[j - 1], device_id=(px, py, pc), device_id_type=MESH)
            rc.start()
            copies.append(rc)
        for cp in copies:
            cp.wait()
        acc = slots[0]
        for i in range(1, 8):
            acc = acc + slots[i]
        o_ref[...] = acc

    vm = pl.BlockSpec(memory_space=pltpu.VMEM)
    return pl.pallas_call(
        body, name=name, in_specs=[vm], out_specs=vm,
        out_shape=jax.ShapeDtypeStruct((r, 128), F32),
        scratch_shapes=[pltpu.VMEM((8, r, 128), F32), pltpu.SemaphoreType.DMA((7,)), pltpu.SemaphoreType.DMA((7,))],
        compiler_params=pltpu.CompilerParams(vmem_limit_bytes=VMEM_LIMIT),
    )(vec)


WEIGHTS = ("mix_norm_g", "w_in", "conv_dw_w", "conv_dw_b", "conv_ln_g", "conv_ln_b", "conv_pw_w", "conv_pw_b",
           "gdn_conv_w", "gdn_a_log", "gdn_dt_bias", "gdn_norm_g", "w_out", "ffn_norm_g", "w_up", "ffn_conv_w",
           "ffn_conv_b", "w_down", "final_norm_g")
COL_SHARDED = ("w_in", "w_up", "conv_dw_w", "gdn_conv_w", "ffn_conv_w")
ROW_SHARDED = ("conv_pw_w", "w_out", "w_down")
BIG = ("w_in", "conv_pw_w", "w_out", "w_up", "w_down")
SMALL_CONV = ("conv_dw_w", "gdn_conv_w", "ffn_conv_w")


def _full_from_slots(name, g, layer):
    part = g[:, layer]
    if name in COL_SHARDED:
        r, cs = part.shape[1:]
        return jnp.transpose(part, (1, 0, 2)).reshape(r, 4 * cs)
    rs, c = part.shape[1:]
    return part.reshape(4 * rs, c)


def _slots_from_full(name, full):
    if name in COL_SHARDED:
        r, c = full.shape
        return jnp.transpose(full.reshape(r, 4, c // 4), (1, 0, 2))
    r, c = full.shape
    return full.reshape(4, r // 4, c)


def _pack(parts):
    flat = jnp.concatenate([p.reshape(-1).astype(F32) for p in parts])
    pad = (-flat.shape[0]) % 1024
    return jnp.pad(flat, (0, pad)).reshape(-1, 128)


def _unpack(vec, shapes):
    flat = vec.reshape(-1)
    out, pos = [], 0
    for shp in shapes:
        n = 1
        for d in shp:
            n *= d
        out.append(flat[pos:pos + n].reshape(shp))
        pos += n
    return out


def kernel(x, mix_norm_g, w_in, conv_dw_w, conv_dw_b, conv_ln_g, conv_ln_b, conv_pw_w, conv_pw_b, gdn_conv_w, gdn_a_log, gdn_dt_bias, gdn_norm_g, w_out, ffn_norm_g, w_up, ffn_conv_w, ffn_conv_b, w_down, final_norm_g, loss_target, m_mix_norm_g, m_w_in, m_conv_dw_w, m_conv_dw_b, m_conv_ln_g, m_conv_ln_b, m_conv_pw_w, m_conv_pw_b, m_gdn_conv_w, m_gdn_a_log, m_gdn_dt_bias, m_gdn_norm_g, m_w_out, m_ffn_norm_g, m_w_up, m_ffn_conv_w, m_ffn_conv_b, m_w_down, m_final_norm_g, v_mix_norm_g, v_w_in, v_conv_dw_w, v_conv_dw_b, v_conv_ln_g, v_conv_ln_b, v_conv_pw_w, v_conv_pw_b, v_gdn_conv_w, v_gdn_a_log, v_gdn_dt_bias, v_gdn_norm_g, v_w_out, v_ffn_norm_g, v_w_up, v_ffn_conv_w, v_ffn_conv_b, v_w_down, v_final_norm_g):
    wts = dict(zip(WEIGHTS, (mix_norm_g, w_in, conv_dw_w, conv_dw_b, conv_ln_g, conv_ln_b, conv_pw_w, conv_pw_b,
                             gdn_conv_w, gdn_a_log, gdn_dt_bias, gdn_norm_g, w_out, ffn_norm_g, w_up, ffn_conv_w,
                             ffn_conv_b, w_down, final_norm_g)))
    mom = dict(zip(WEIGHTS, (m_mix_norm_g, m_w_in, m_conv_dw_w, m_conv_dw_b, m_conv_ln_g, m_conv_ln_b, m_conv_pw_w,
                             m_conv_pw_b, m_gdn_conv_w, m_gdn_a_log, m_gdn_dt_bias, m_gdn_norm_g, m_w_out,
                             m_ffn_norm_g, m_w_up, m_ffn_conv_w, m_ffn_conv_b, m_w_down, m_final_norm_g)))
    var = dict(zip(WEIGHTS, (v_mix_norm_g, v_w_in, v_conv_dw_w, v_conv_dw_b, v_conv_ln_g, v_conv_ln_b, v_conv_pw_w,
                             v_conv_pw_b, v_gdn_conv_w, v_gdn_a_log, v_gdn_dt_bias, v_gdn_norm_g, v_w_out,
                             v_ffn_norm_g, v_w_up, v_ffn_conv_w, v_ffn_conv_b, v_w_down, v_final_norm_g)))

    nb, s, d = x.shape
    t = nb * s
    depth = mix_norm_g.shape[0]
    ch = conv_dw_b.shape[1]
    nh = gdn_a_log.shape[1]
    nqk = nh // 2
    kwid, vwid = nqk * HEAD, nh * HEAD
    main = 2 * ch + 2 * kwid + 2 * vwid
    qcol, zcol = 2 * ch, 2 * ch + 2 * kwid + vwid
    dff = ffn_conv_b.shape[1]
    my_xy = 2 * lax.axis_index("x") + lax.axis_index("y")

    gather_names = BIG + SMALL_CONV
    gathered = xy_exchange([wts[n].astype(BF16) if n in BIG else wts[n] for n in gather_names],
                           gather=True, name="gather_weights")
    gathered = dict(zip(gather_names, gathered))

    def layer_weights(l):
        full = {n: _full_from_slots(n, gathered[n], l) for n in gather_names}
        w_in_f = full["w_in"]
        lw = dict(
            w_main=w_in_f[:, :main],
            w_ba=jnp.pad(w_in_f[:, main:], ((0, 0), (0, HEAD - 2 * nh))),
            pw=full["conv_pw_w"], wout_a=full["w_out"][:ch], wout_b=full["w_out"][ch:],
            wup=full["w_up"], wdown=full["w_down"],
            dw_w=full["conv_dw_w"], gconv_w=full["gdn_conv_w"], fconv_w=full["ffn_conv_w"],
            gp=jnp.zeros((8, HEAD), F32).at[0, :nh].set(gdn_a_log[l]).at[1, :nh].set(gdn_dt_bias[l])
                .at[2].set(gdn_norm_g[l]),
        )
        return lw

    lws = [layer_weights(l) for l in range(depth)]

    xc = x.reshape(t, d)
    saved = []
    for l in range(depth):
        lw = lws[l]
        h = rmsnorm_fwd(xc, mix_norm_g[l], name=f"f{l}_norm1")
        p = matmul(h, lw["w_main"], name=f"f{l}_in_main")
        pba = matmul(h, lw["w_ba"], name=f"f{l}_in_ba")
        u3 = conf_fwd(p, lw["dw_w"], conv_dw_b[l], conv_ln_g[l], conv_ln_b[l], nb=nb, s=s, name=f"f{l}_conf")
        out_a = matmul(u3, lw["pw"], bias=conv_pw_b[l], out_dtype=BF16, name=f"f{l}_pw")
        qkvc = qkvconv_fwd(p, lw["gconv_w"], col0=qcol, nb=nb, s=s, name=f"f{l}_qkvconv")
        out_b, states = gdn_fwd(qkvc, p, pba, lw["gp"], nb=nb, s=s, nh=nh, nqk=nqk, zcol=zcol, name=f"f{l}_gdn")
        x1 = matmul(out_a, lw["wout_a"], res=xc, name=f"f{l}_out_a")
        x1 = matmul(out_b, lw["wout_b"], res=x1, name=f"f{l}_out_b")
        h2 = rmsnorm_fwd(x1, ffn_norm_g[l], name=f"f{l}_norm2")
        up = matmul(h2, lw["wup"], name=f"f{l}_up")
        act = ffn_act_fwd(up, lw["fconv_w"], ffn_conv_b[l], nb=nb, s=s, name=f"f{l}_act")
        x2 = matmul(act, lw["wdown"], res=x1, name=f"f{l}_down")
        saved.append(dict(x=xc, h=h, p=p, pba=pba, u3=u3, out_a=out_a, qkvc=qkvc, out_b=out_b, states=states,
                          x1=x1, h2=h2, up=up, act=act))
        xc = x2

    loss_blk, dx, dxb, dgf = loss_head(xc, final_norm_g, loss_target.reshape(t, d), name="loss_head")

    big_grads = {n: [None] * depth for n in BIG}
    small_grads = {n: [None] * depth for n in WEIGHTS if n not in BIG and n != "final_norm_g"}
    for l in reversed(range(depth)):
        lw, sv = lws[l], saved[l]
        dact = matmul(dxb, lw["wdown"], tb=True, name=f"b{l}_dact")
        big_grads["w_down"][l] = matmul(sv["act"], dxb, ta=True, name=f"b{l}_dwdown")
        dgate, dupv, fpart = ffn_act_bwd(sv["up"], lw["fconv_w"], ffn_conv_b[l], dact, nb=nb, s=s, name=f"b{l}_act")
        dh2 = matmul(dgate, lw["wup"], tb=True, name=f"b{l}_dh2_gate")
        dh2 = matmul(dupv, lw["wup"], tb=True, b_koff=dff, res=dh2, name=f"b{l}_dh2_up")
        big_grads["w_up"][l] = jnp.concatenate(
            [matmul(sv["h2"], dgate, ta=True, name=f"b{l}_dwup_gate"),
             matmul(sv["h2"], dupv, ta=True, name=f"b{l}_dwup_up")], axis=1)
        dx1, dx1b, dg2 = rmsnorm_bwd(sv["x1"], ffn_norm_g[l], dh2, dx, name=f"b{l}_norm2")
        fsum = jnp.sum(fpart, axis=0)
        small_grads["ffn_norm_g"][l] = dg2[0]
        small_grads["ffn_conv_w"][l] = fsum[:ffn_conv_w.shape[1]]
        small_grads["ffn_conv_b"][l] = fsum[ffn_conv_w.shape[1]]
        dout_a = matmul(dx1b, lw["wout_a"], tb=True, out_dtype=BF16, name=f"b{l}_dout_a")
        dout_b = matmul(dx1b, lw["wout_b"], tb=True, name=f"b{l}_dout_b")
        big_grads["w_out"][l] = jnp.concatenate(
            [matmul(sv["out_a"], dx1b, ta=True, name=f"b{l}_dwout_a"),
             matmul(sv["out_b"], dx1b, ta=True, name=f"b{l}_dwout_b")], axis=0)
        du3 = matmul(dout_a, lw["pw"], tb=True, name=f"b{l}_du3")
        big_grads["conv_pw_w"][l] = matmul(sv["u3"], dout_a, ta=True, name=f"b{l}_dwpw")
        dval, dagate, cw_part, cs_part = conf_bwd(sv["p"], lw["dw_w"], conv_dw_b[l], conv_ln_g[l], conv_ln_b[l],
                                                  du3, dout_a, nb=nb, s=s, name=f"b{l}_conf")
        csum = jnp.sum(cs_part, axis=0)
        small_grads["conv_dw_w"][l] = jnp.sum(cw_part, axis=0)
        small_grads["conv_dw_b"][l] = csum[0]
        small_grads["conv_ln_g"][l] = csum[1]
        small_grads["conv_ln_b"][l] = csum[2]
        small_grads["conv_pw_b"][l] = csum[3]
        dq, dk, dv, dz, dpba, dgp = gdn_bwd(sv["qkvc"], sv["p"], sv["pba"], lw["gp"], sv["states"], dout_b,
                                            nb=nb, s=s, nh=nh, nqk=nqk, zcol=zcol, name=f"b{l}_gdn")
        dqkv, gw_part = qkvconv_bwd(sv["p"], lw["gconv_w"], jnp.concatenate([dq, dk, dv], axis=1),
                                    col0=qcol, nb=nb, s=s, name=f"b{l}_qkvconv")
        gsum = jnp.sum(dgp, axis=0)
        small_grads["gdn_conv_w"][l] = jnp.sum(gw_part, axis=0)[:gdn_conv_w.shape[1]]
        small_grads["gdn_a_log"][l] = gsum[0, :nh]
        small_grads["gdn_dt_bias"][l] = gsum[1, :nh]
        small_grads["gdn_norm_g"][l] = gsum[2]
        dp = jnp.concatenate([dval, dagate, dqkv, dz], axis=1)
        dh = matmul(dp, lw["w_main"], tb=True, name=f"b{l}_dh_main")
        dh = matmul(dpba, lw["w_ba"], tb=True, res=dh, name=f"b{l}_dh_ba")
        dw_main = matmul(sv["h"], dp, ta=True, name=f"b{l}_dwin_main")
        dw_ba = matmul(sv["h"], dpba, ta=True, name=f"b{l}_dwin_ba")
        big_grads["w_in"][l] = jnp.concatenate([dw_main, dw_ba[:, :2 * nh]], axis=1)
        dx, dxb, dg1 = rmsnorm_bwd(sv["x"], mix_norm_g[l], dh, dx1, name=f"b{l}_norm1")
        small_grads["mix_norm_g"][l] = dg1[0]

    stacks = [jnp.stack([_slots_from_full(n, big_grads[n][l]).astype(BF16) for l in range(depth)], axis=1)
              for n in BIG]
    received = xy_exchange(stacks, gather=False, name="scatter_grads")
    partial = [sum_slots(r, name=f"sum_{n}") for n, r in zip(BIG, received)]
    other = sibling_swap(partial, name="swap_partials")

    grads, deltas, new_m, new_v = {}, {}, {}, {}
    for n, mine, theirs in zip(BIG, partial, other):
        grads[n], deltas[n], new_m[n], new_v[n] = adamw([mine, theirs], wts[n], mom[n], var[n], name=f"adamw_{n}")

    small_names = [n for n in WEIGHTS if n not in BIG]
    small_full = [jnp.stack(small_grads[n]) if n != "final_norm_g" else dgf[0] for n in small_names]
    packed = _pack(small_full + [loss_blk[0, :1]])
    reduced = allreduce_small(packed, name="allreduce_small")
    parts = _unpack(reduced, [a.shape for a in small_full] + [(1,)])
    loss = parts[-1][0]
    for n, g in zip(small_names, parts[:-1]):
        if n in SMALL_CONV:
            wid = wts[n].shape[-1]
            g = lax.dynamic_slice_in_dim(g, my_xy * wid, wid, axis=g.ndim - 1)
        grads[n], deltas[n], new_m[n], new_v[n] = adamw([g], wts[n], mom[n], var[n], name=f"adamw_{n}")

    return (loss, dx.reshape(nb, s, d), *[grads[n] for n in WEIGHTS], *[deltas[n] for n in WEIGHTS],
            *[new_m[n] for n in WEIGHTS], *[new_v[n] for n in WEIGHTS])
```

```python
import functools

import jax
import jax.numpy as jnp
from jax import lax
from jax.experimental import pallas as pl
from jax.experimental.pallas import tpu as pltpu

F32 = jnp.float32
BF16 = jnp.bfloat16
EPS = 1e-6
CHUNK = 64
HEAD = 128
HIGHEST = lax.Precision.HIGHEST
VMEM_LIMIT = 56 * 1024 * 1024


def _pick(dim, cands):
    for c in cands:
        if dim % c == 0:
            return c
    return dim


def _cparams(sem):
    return pltpu.CompilerParams(dimension_semantics=sem, vmem_limit_bytes=VMEM_LIMIT)


def matmul(a, b, *, ta=False, tb=False, out_dtype=F32, res=None, bias=None, b_koff=0, name):
    if ta:
        kdim, m = a.shape
    else:
        m, kdim = a.shape
    if tb:
        n, kb = b.shape
    else:
        kb, n = b.shape
    assert kb >= kdim + b_koff, (a.shape, b.shape, ta, tb)
    tm = _pick(m, (1024, 512, 256, 128))
    tn = _pick(n, (1024, 512, 256, 128))
    tk = _pick(kdim, (512, 256, 128))
    nk = kdim // tk
    assert b_koff % tk == 0
    ko = b_koff // tk
    dims = (((0 if ta else 1,), (1 if tb else 0,)), ((), ()))

    def body(*refs):
        a_ref, b_ref = refs[0], refs[1]
        pos = 2
        bias_ref = res_ref = None
        if bias is not None:
            bias_ref = refs[pos]
            pos += 1
        if res is not None:
            res_ref = refs[pos]
            pos += 1
        o_ref, acc_ref = refs[pos], refs[pos + 1]
        k = pl.program_id(2)

        @pl.when(k == 0)
        def _():
            acc_ref[...] = jnp.zeros_like(acc_ref)

        acc_ref[...] += lax.dot_general(a_ref[...].astype(BF16), b_ref[...].astype(BF16), dims,
                                        preferred_element_type=F32)

        @pl.when(k == nk - 1)
        def _():
            r = acc_ref[...]
            if bias_ref is not None:
                r = r + bias_ref[...]
            if res_ref is not None:
                r = r + res_ref[...]
            o_ref[...] = r.astype(o_ref.dtype)

    a_spec = pl.BlockSpec((tk, tm), lambda i, j, k: (k, i)) if ta else pl.BlockSpec((tm, tk), lambda i, j, k: (i, k))
    b_spec = (pl.BlockSpec((tn, tk), lambda i, j, k: (j, k + ko)) if tb
              else pl.BlockSpec((tk, tn), lambda i, j, k: (k + ko, j)))
    in_specs = [a_spec, b_spec]
    args = [a, b]
    if bias is not None:
        in_specs.append(pl.BlockSpec((1, tn), lambda i, j, k: (0, j)))
        args.append(bias.reshape(1, n).astype(F32))
    if res is not None:
        in_specs.append(pl.BlockSpec((tm, tn), lambda i, j, k: (i, j)))
        args.append(res)
    return pl.pallas_call(
        body,
        name=name,
        grid=(m // tm, n // tn, nk),
        in_specs=in_specs,
        out_specs=pl.BlockSpec((tm, tn), lambda i, j, k: (i, j)),
        out_shape=jax.ShapeDtypeStruct((m, n), out_dtype),
        scratch_shapes=[pltpu.VMEM((tm, tn), F32)],
        compiler_params=_cparams(("parallel", "parallel", "arbitrary")),
    )(*args)


def _sigmoid(x):
    return 1.0 / (1.0 + jnp.exp(-x))


def _softplus(x):
    return jnp.maximum(x, 0.0) + jnp.log(1.0 + jnp.exp(-jnp.abs(x)))


def _shift_back(x, s, row):
    if s == 0:
        return x
    return jnp.where(row >= s, pltpu.roll(x, s, 0), 0.0)


def _shift_fwd(x, s, row):
    if s == 0:
        return x
    n = x.shape[0]
    return jnp.where(row < n - s, pltpu.roll(x, n - s, 0), 0.0)


def _conv_fwd(x, w_ref, kw, row):
    acc = x * w_ref[pl.ds(kw - 1, 1), :]
    for s in range(1, kw):
        acc = acc + _shift_back(x, s, row) * w_ref[pl.ds(kw - 1 - s, 1), :]
    return acc


def _conv_bwd(x, dy, w_ref, kw, row):
    dx = dy * w_ref[pl.ds(kw - 1, 1), :]
    dw = [None] * kw
    dw[kw - 1] = jnp.sum(dy * x, axis=0, keepdims=True)
    for s in range(1, kw):
        dx = dx + _shift_fwd(dy, s, row) * w_ref[pl.ds(kw - 1 - s, 1), :]
        dw[kw - 1 - s] = jnp.sum(dy * _shift_back(x, s, row), axis=0, keepdims=True)
    return dx, dw


def _rows_to_block(rows, nrows, width):
    rid = lax.broadcasted_iota(jnp.int32, (nrows, width), 0)
    out = jnp.zeros((nrows, width), F32)
    for i, r in enumerate(rows):
        out = jnp.where(rid == i, r, out)
    return out


def rmsnorm_fwd(x, g, *, name):
    t, d = x.shape
    tm = _pick(t, (256, 128))

    def body(x_ref, g_ref, o_ref):
        xv = x_ref[...]
        r = lax.rsqrt(jnp.mean(xv * xv, axis=-1, keepdims=True) + EPS)
        o_ref[...] = (xv * r * g_ref[...]).astype(o_ref.dtype)

    return pl.pallas_call(
        body, name=name, grid=(t // tm,),
        in_specs=[pl.BlockSpec((tm, d), lambda i: (i, 0)), pl.BlockSpec((1, d), lambda i: (0, 0))],
        out_specs=pl.BlockSpec((tm, d), lambda i: (i, 0)),
        out_shape=jax.ShapeDtypeStruct((t, d), BF16),
        compiler_params=_cparams(("parallel",)),
    )(x, g.reshape(1, d))


def rmsnorm_bwd(x, g, dh, dres, *, name):
    t, d = x.shape
    tm = _pick(t, (256, 128))

    def body(x_ref, g_ref, dh_ref, dres_ref, dx_ref, dxb_ref, dg_ref):
        xv = x_ref[...]
        r = lax.rsqrt(jnp.mean(xv * xv, axis=-1, keepdims=True) + EPS)
        xh = xv * r
        dy = dh_ref[...]
        dxh = dy * g_ref[...]
        dx = dres_ref[...] + r * (dxh - xh * jnp.mean(dxh * xh, axis=-1, keepdims=True))
        dx_ref[...] = dx
        dxb_ref[...] = dx.astype(BF16)

        @pl.when(pl.program_id(0) == 0)
        def _():
            dg_ref[...] = jnp.zeros_like(dg_ref)

        dg_ref[...] += jnp.sum(dy * xh, axis=0, keepdims=True)

    row = pl.BlockSpec((tm, d), lambda i: (i, 0))
    vec = pl.BlockSpec((1, d), lambda i: (0, 0))
    return pl.pallas_call(
        body, name=name, grid=(t // tm,),
        in_specs=[row, vec, row, row],
        out_specs=[row, row, vec],
        out_shape=[jax.ShapeDtypeStruct((t, d), F32), jax.ShapeDtypeStruct((t, d), BF16),
                   jax.ShapeDtypeStruct((1, d), F32)],
        compiler_params=_cparams(("arbitrary",)),
    )(x, g.reshape(1, d), dh, dres)


def loss_head(x, g, target, *, name):
    t, d = x.shape
    tm = _pick(t, (256, 128))

    def body(x_ref, g_ref, tg_ref, loss_ref, dx_ref, dxb_ref, dg_ref):
        xv = x_ref[...]
        r = lax.rsqrt(jnp.mean(xv * xv, axis=-1, keepdims=True) + EPS)
        xh = xv * r
        err = xh * g_ref[...] - tg_ref[...]
        dy = err * (1.0 / d)
        dxh = dy * g_ref[...]
        dx = r * (dxh - xh * jnp.mean(dxh * xh, axis=-1, keepdims=True))
        dx_ref[...] = dx
        dxb_ref[...] = dx.astype(BF16)

        @pl.when(pl.program_id(0) == 0)
        def _():
            dg_ref[...] = jnp.zeros_like(dg_ref)
            loss_ref[...] = jnp.zeros_like(loss_ref)

        dg_ref[...] += jnp.sum(dy * xh, axis=0, keepdims=True)
        part = jnp.sum(jnp.sum(err * err, axis=-1, keepdims=True), axis=0, keepdims=True) * (0.5 / d)
        loss_ref[...] += jnp.broadcast_to(part, loss_ref.shape)

    row = pl.BlockSpec((tm, d), lambda i: (i, 0))
    vec = pl.BlockSpec((1, d), lambda i: (0, 0))
    return pl.pallas_call(
        body, name=name, grid=(t // tm,),
        in_specs=[row, vec, row],
        out_specs=[pl.BlockSpec((8, 128), lambda i: (0, 0)), row, row, vec],
        out_shape=[jax.ShapeDtypeStruct((8, 128), F32), jax.ShapeDtypeStruct((t, d), F32),
                   jax.ShapeDtypeStruct((t, d), BF16), jax.ShapeDtypeStruct((1, d), F32)],
        compiler_params=_cparams(("arbitrary",)),
    )(x, g.reshape(1, d), target)


def _conf_forward_parts(val, gate, w_ref, b, lg, lb, kw, row):
    sg = _sigmoid(gate)
    u0 = val * sg
    u1 = _conv_fwd(u0, w_ref, kw, row) + b
    mu = jnp.mean(u1, axis=-1, keepdims=True)
    xc = u1 - mu
    rs = lax.rsqrt(jnp.mean(xc * xc, axis=-1, keepdims=True) + EPS)
    xh = xc * rs
    u2 = xh * lg + lb
    s2 = _sigmoid(u2)
    return sg, u0, rs, xh, u2, s2


def conf_fwd(p, dw_w, dw_b, ln_g, ln_b, *, nb, s, name):
    kw, ch = dw_w.shape
    ng = ch // HEAD

    def body(val_ref, gate_ref, w_ref, b_ref, lg_ref, lb_ref, o_ref):
        row = lax.broadcasted_iota(jnp.int32, (s, HEAD), 0)
        _, _, _, _, u2, s2 = _conf_forward_parts(val_ref[...], gate_ref[...], w_ref, b_ref[...], lg_ref[...],
                                                 lb_ref[...], kw, row)
        o_ref[...] = (u2 * s2).astype(o_ref.dtype)

    vec = pl.BlockSpec((1, HEAD), lambda b, g: (0, g))
    return pl.pallas_call(
        body, name=name, grid=(nb, ng),
        in_specs=[pl.BlockSpec((s, HEAD), lambda b, g: (b, g)), pl.BlockSpec((s, HEAD), lambda b, g: (b, ng + g)),
                  pl.BlockSpec((kw, HEAD), lambda b, g: (0, g)), vec, vec, vec],
        out_specs=pl.BlockSpec((s, HEAD), lambda b, g: (b, g)),
        out_shape=jax.ShapeDtypeStruct((nb * s, ch), BF16),
        compiler_params=_cparams(("parallel", "parallel")),
    )(p, p, dw_w, dw_b.reshape(1, ch), ln_g.reshape(1, ch), ln_b.reshape(1, ch))


def conf_bwd(p, dw_w, dw_b, ln_g, ln_b, du3, dout_a, *, nb, s, name):
    kw, ch = dw_w.shape
    ng = ch // HEAD

    def body(val_ref, gate_ref, w_ref, b_ref, lg_ref, lb_ref, du3_ref, doa_ref, dval_ref, dgate_ref, dw_out, sm_out):
        row = lax.broadcasted_iota(jnp.int32, (s, HEAD), 0)
        val = val_ref[...]
        sg, u0, rs, xh, u2, s2 = _conf_forward_parts(val, gate_ref[...], w_ref, b_ref[...], lg_ref[...],
                                                     lb_ref[...], kw, row)
        du2 = du3_ref[...] * (s2 * (1.0 + u2 * (1.0 - s2)))
        dlg = jnp.sum(du2 * xh, axis=0, keepdims=True)
        dlb = jnp.sum(du2, axis=0, keepdims=True)
        dxh = du2 * lg_ref[...]
        du1 = rs * (dxh - jnp.mean(dxh, axis=-1, keepdims=True) - xh * jnp.mean(dxh * xh, axis=-1, keepdims=True))
        ddb = jnp.sum(du1, axis=0, keepdims=True)
        du0, dw = _conv_bwd(u0, du1, w_ref, kw, row)
        dval_ref[...] = (du0 * sg).astype(dval_ref.dtype)
        dgate_ref[...] = (du0 * val * sg * (1.0 - sg)).astype(dgate_ref.dtype)
        for k in range(kw):
            dw_out[0, pl.ds(k, 1), :] = dw[k]
        dpb = jnp.sum(doa_ref[...].astype(F32), axis=0, keepdims=True)
        sm_out[0] = _rows_to_block([ddb, dlg, dlb, dpb], 8, HEAD)

    vec = pl.BlockSpec((1, HEAD), lambda b, g: (0, g))
    blk = pl.BlockSpec((s, HEAD), lambda b, g: (b, g))
    return pl.pallas_call(
        body, name=name, grid=(nb, ng),
        in_specs=[blk, pl.BlockSpec((s, HEAD), lambda b, g: (b, ng + g)),
                  pl.BlockSpec((kw, HEAD), lambda b, g: (0, g)), vec, vec, vec, blk, blk],
        out_specs=[blk, blk, pl.BlockSpec((1, kw, HEAD), lambda b, g: (b, 0, g)),
                   pl.BlockSpec((1, 8, HEAD), lambda b, g: (b, 0, g))],
        out_shape=[jax.ShapeDtypeStruct((nb * s, ch), BF16), jax.ShapeDtypeStruct((nb * s, ch), BF16),
                   jax.ShapeDtypeStruct((nb, kw, ch), F32), jax.ShapeDtypeStruct((nb, 8, ch), F32)],
        compiler_params=_cparams(("parallel", "parallel")),
    )(p, p, dw_w, dw_b.reshape(1, ch), ln_g.reshape(1, ch), ln_b.reshape(1, ch), du3, dout_a)


def qkvconv_fwd(p, w, *, col0, nb, s, name):
    kw, ch = w.shape
    nblk = ch // HEAD
    c0 = col0 // HEAD

    def body(x_ref, w_ref, o_ref):
        row = lax.broadcasted_iota(jnp.int32, (s, HEAD), 0)
        c = _conv_fwd(x_ref[...], w_ref, kw, row)
        o_ref[...] = c * _sigmoid(c)

    return pl.pallas_call(
        body, name=name, grid=(nb, nblk),
        in_specs=[pl.BlockSpec((s, HEAD), lambda b, j: (b, c0 + j)), pl.BlockSpec((kw, HEAD), lambda b, j: (0, j))],
        out_specs=pl.BlockSpec((s, HEAD), lambda b, j: (b, j)),
        out_shape=jax.ShapeDtypeStruct((nb * s, ch), F32),
        compiler_params=_cparams(("parallel", "parallel")),
    )(p, w)


def qkvconv_bwd(p, w, dy, *, col0, nb, s, name):
    kw, ch = w.shape
    nblk = ch // HEAD
    c0 = col0 // HEAD

    def body(x_ref, w_ref, dy_ref, dx_ref, dw_out):
        row = lax.broadcasted_iota(jnp.int32, (s, HEAD), 0)
        xv = x_ref[...]
        c = _conv_fwd(xv, w_ref, kw, row)
        sc = _sigmoid(c)
        dc = dy_ref[...] * (sc * (1.0 + c * (1.0 - sc)))
        dx, dw = _conv_bwd(xv, dc, w_ref, kw, row)
        dx_ref[...] = dx.astype(dx_ref.dtype)
        dw_out[0] = _rows_to_block(dw, 8, HEAD)

    blk = pl.BlockSpec((s, HEAD), lambda b, j: (b, j))
    return pl.pallas_call(
        body, name=name, grid=(nb, nblk),
        in_specs=[pl.BlockSpec((s, HEAD), lambda b, j: (b, c0 + j)), pl.BlockSpec((kw, HEAD), lambda b, j: (0, j)), blk],
        out_specs=[blk, pl.BlockSpec((1, 8, HEAD), lambda b, j: (b, 0, j))],
        out_shape=[jax.ShapeDtypeStruct((nb * s, ch), BF16), jax.ShapeDtypeStruct((nb, 8, ch), F32)],
        compiler_params=_cparams(("parallel", "parallel")),
    )(p, w, dy)


def ffn_act_fwd(up, w, b, *, nb, s, name):
    kw, dff = w.shape
    cb = _pick(dff, (256, 128))
    nblk = dff // cb

    def body(g_ref, u_ref, w_ref, b_ref, o_ref):
        row = lax.broadcasted_iota(jnp.int32, (s, cb), 0)
        gc = _conv_fwd(g_ref[...], w_ref, kw, row) + b_ref[...]
        o_ref[...] = (gc * _sigmoid(gc) * u_ref[...]).astype(o_ref.dtype)

    return pl.pallas_call(
        body, name=name, grid=(nb, nblk),
        in_specs=[pl.BlockSpec((s, cb), lambda i, j: (i, j)), pl.BlockSpec((s, cb), lambda i, j: (i, nblk + j)),
                  pl.BlockSpec((kw, cb), lambda i, j: (0, j)), pl.BlockSpec((1, cb), lambda i, j: (0, j))],
        out_specs=pl.BlockSpec((s, cb), lambda i, j: (i, j)),
        out_shape=jax.ShapeDtypeStruct((nb * s, dff), BF16),
        compiler_params=_cparams(("parallel", "parallel")),
    )(up, up, w, b.reshape(1, dff))


def ffn_act_bwd(up, w, b, dact, *, nb, s, name):
    kw, dff = w.shape
    cb = _pick(dff, (256, 128))
    nblk = dff // cb

    def body(g_ref, u_ref, w_ref, b_ref, da_ref, dg_ref, du_ref, sm_out):
        row = lax.broadcasted_iota(jnp.int32, (s, cb), 0)
        gv = g_ref[...]
        gc = _conv_fwd(gv, w_ref, kw, row) + b_ref[...]
        sc = _sigmoid(gc)
        da = da_ref[...].astype(F32)
        du_ref[...] = (da * gc * sc).astype(du_ref.dtype)
        dgc = da * u_ref[...] * (sc * (1.0 + gc * (1.0 - sc)))
        dgate, dw = _conv_bwd(gv, dgc, w_ref, kw, row)
        dg_ref[...] = dgate.astype(dg_ref.dtype)
        sm_out[0] = _rows_to_block(dw + [jnp.sum(dgc, axis=0, keepdims=True)], 8, cb)

    blk = pl.BlockSpec((s, cb), lambda i, j: (i, j))
    return pl.pallas_call(
        body, name=name, grid=(nb, nblk),
        in_specs=[blk, pl.BlockSpec((s, cb), lambda i, j: (i, nblk + j)),
                  pl.BlockSpec((kw, cb), lambda i, j: (0, j)), pl.BlockSpec((1, cb), lambda i, j: (0, j)), blk],
        out_specs=[blk, blk, pl.BlockSpec((1, 8, cb), lambda i, j: (i, 0, j))],
        out_shape=[jax.ShapeDtypeStruct((nb * s, dff), BF16), jax.ShapeDtypeStruct((nb * s, dff), BF16),
                   jax.ShapeDtypeStruct((nb, 8, dff), F32)],
        compiler_params=_cparams(("parallel", "parallel")),
    )(up, up, w, b.reshape(1, dff), dact)


def _dot(a, b, dims, prec=None):
    return lax.dot_general(a, b, (dims, ((), ())), precision=prec, preferred_element_type=F32)


def _mm_nn(a, b):
    return _dot(a.astype(BF16), b.astype(BF16), ((1,), (0,)))


def _mm_nt(a, b):
    return _dot(a.astype(BF16), b.astype(BF16), ((1,), (1,)))


def _mm_tn(a, b):
    return _dot(a.astype(BF16), b.astype(BF16), ((0,), (0,)))


def _mm_hi(a, b):
    return _dot(a, b, ((1,), (0,)), HIGHEST)


def _mm_inv(a, b):
    return _mm_nn(a, b)


def _split3(x):
    hi = x.astype(BF16)
    rest = x - hi.astype(F32)
    mid = rest.astype(BF16)
    return hi, mid, (rest - mid.astype(F32)).astype(BF16)


def _mask_dot(mask, x, dims, mask_first):
    if mask_first:
        return sum(_dot(mask, p, dims) for p in _split3(x))
    return sum(_dot(p, mask, dims) for p in _split3(x))


@jax.custom_vjp
def _mask_nn(mask, x):
    return _mask_dot(mask, x, ((1,), (0,)), True)


def _mask_nn_fwd(mask, x):
    return _mask_nn(mask, x), mask


def _mask_nn_bwd(mask, ct):
    return jnp.zeros_like(mask), _mask_dot(mask, ct, ((0,), (0,)), True)


_mask_nn.defvjp(_mask_nn_fwd, _mask_nn_bwd)


@jax.custom_vjp
def _mask_tn(x, mask):
    return _mask_dot(mask, x, ((0,), (0,)), False)


def _mask_tn_fwd(x, mask):
    return _mask_tn(x, mask), mask


def _mask_tn_bwd(mask, ct):
    return _mask_dot(mask, ct, ((1,), (1,)), True), jnp.zeros_like(mask)


_mask_tn.defvjp(_mask_tn_fwd, _mask_tn_bwd)


GDN_ROWS = 256


def _gdn_chunk(qc, kc, vc, zc, braw, araw, alog, dtb, ng, state):
    r = qc.shape[0]
    ri = lax.broadcasted_iota(jnp.int32, (r, r), 0)
    ci = lax.broadcasted_iota(jnp.int32, (r, r), 1)
    same = (ri // CHUNK) == (ci // CHUNK)
    causal = same & (ri >= ci)
    strict = same & (ri > ci)
    eye = (ri == ci).astype(F32)
    row_chunk = lax.broadcasted_iota(jnp.int32, (r, HEAD), 0) // CHUNK

    q = qc * lax.rsqrt(jnp.sum(qc * qc, axis=-1, keepdims=True) + EPS) * (HEAD ** -0.5)
    k = kc * lax.rsqrt(jnp.sum(kc * kc, axis=-1, keepdims=True) + EPS)
    beta = _sigmoid(braw)
    g = -jnp.exp(alog) * _softplus(araw + dtb)

    g_w = jnp.broadcast_to(g, (r, HEAD))
    widen = lambda t: jnp.concatenate([t] * (r // HEAD), axis=1)
    gw = _mask_nn(causal.astype(BF16), g_w)
    gi = widen(gw)
    gj = _mask_tn(widen(g_w), (same & (ri <= ci)).astype(BF16))
    decay = jnp.where(causal, jnp.exp(jnp.where(causal, gi - gj, 0.0)), 0.0)

    kb = k * beta
    vb = vc * beta
    lmat = jnp.where(strict, _mm_nt(kb, k) * decay, 0.0)
    x = -lmat
    ainv = eye + x
    p = 1
    while 2 * p < CHUNK:
        x = _mm_inv(x, x)
        ainv = _mm_inv(ainv, eye + x)
        p *= 2
    u = _mm_inv(ainv, vb)
    w = _mm_inv(ainv, kb * jnp.exp(gw))
    qk = jnp.where(causal, _mm_nt(q, k) * decay, 0.0)
    qg = q * jnp.exp(gw)

    o = jnp.zeros((r, HEAD), F32)
    for c in range(r // CHUNK):
        in_c = row_chunk == c
        glast = jnp.sum(jnp.where(in_c, g_w, 0.0), axis=0, keepdims=True)
        v_new = jnp.where(in_c, u - _mm_nn(w, state), 0.0)
        o = o + jnp.where(in_c, _mm_nn(qg, state), 0.0) + _mm_nn(qk, v_new)
        k_dec = jnp.where(in_c, k * jnp.exp(jnp.where(in_c, glast - gw, 0.0)), 0.0)
        state = state * jnp.exp(glast) + _mm_tn(k_dec, v_new)

    o = o * lax.rsqrt(jnp.mean(o * o, axis=-1, keepdims=True) + EPS) * ng
    o = o * (zc * _sigmoid(zc))
    return o, state


GDN_HEADS_PER_STEP = 2


def _gdn_specs(s, nh, nqk, zcol, n_chunks, gh):
    rep = nh // nqk
    qw, vw = (gh // rep) * HEAD, gh * HEAD
    assert gh % rep == 0 and nh % gh == 0 and (nqk * HEAD) % qw == 0 and (2 * nqk * HEAD) % vw == 0 and zcol % vw == 0
    k0, v0, z0 = (nqk * HEAD) // qw, (2 * nqk * HEAD) // vw, zcol // vw
    return dict(
        q=pl.BlockSpec((s, qw), lambda b, j: (b, j)),
        k=pl.BlockSpec((s, qw), lambda b, j: (b, k0 + j)),
        v=pl.BlockSpec((s, vw), lambda b, j: (b, v0 + j)),
        z=pl.BlockSpec((s, vw), lambda b, j: (b, z0 + j)),
        ba=pl.BlockSpec((s, HEAD), lambda b, j: (b, 0)),
        gp=pl.BlockSpec((8, HEAD), lambda b, j: (0, 0)),
        qk_out=pl.BlockSpec((s, qw), lambda b, j: (b, j)),
        head=pl.BlockSpec((s, vw), lambda b, j: (b, j)),
        st=pl.BlockSpec((1, gh, n_chunks, HEAD, HEAD), lambda b, j: (b, j, 0, 0, 0)),
    )


def _gdn_scalars(gp_ref, h):
    lane = lax.broadcasted_iota(jnp.int32, (1, HEAD), 1)
    sel = (lane == h).astype(F32)
    alog = jnp.sum(gp_ref[pl.ds(0, 1), :] * sel, axis=-1, keepdims=True)
    dtb = jnp.sum(gp_ref[pl.ds(1, 1), :] * sel, axis=-1, keepdims=True)
    return alog, dtb, sel


def _lanes(i):
    return pl.ds(i * HEAD, HEAD)


def gdn_fwd(qkvc, p, pba, gp, *, nb, s, nh, nqk, zcol, name):
    rb = min(GDN_ROWS, s)
    n_chunks = s // rb
    gh = GDN_HEADS_PER_STEP
    rep = nh // nqk
    sp = _gdn_specs(s, nh, nqk, zcol, n_chunks, gh)

    def body(q_ref, k_ref, v_ref, z_ref, ba_ref, gp_ref, o_ref, st_ref):
        h0 = pl.program_id(1) * gh
        ng = gp_ref[pl.ds(2, 1), :]
        lane = lax.broadcasted_iota(jnp.int32, (rb, HEAD), 1)
        heads = []
        for i in range(gh):
            alog, dtb, _ = _gdn_scalars(gp_ref, h0 + i)
            heads.append((alog, dtb, (lane == h0 + i).astype(F32), (lane == nh + h0 + i).astype(F32)))

        def step(n, states):
            rows = pl.ds(pl.multiple_of(n * rb, rb), rb)
            ba = ba_ref[rows, :]
            qs = [q_ref[rows, _lanes(j)] for j in range(gh // rep)]
            ks = [k_ref[rows, _lanes(j)] for j in range(gh // rep)]
            vs = [v_ref[rows, _lanes(i)] for i in range(gh)]
            zs = [z_ref[rows, _lanes(i)] for i in range(gh)]
            outs, new_states = [], []
            for i, (alog, dtb, sel_b, sel_a) in enumerate(heads):
                braw = jnp.sum(ba * sel_b, axis=-1, keepdims=True)
                araw = jnp.sum(ba * sel_a, axis=-1, keepdims=True)
                o, new_state = _gdn_chunk(qs[i // rep], ks[i // rep], vs[i], zs[i], braw, araw, alog, dtb, ng,
                                          states[i])
                outs.append(o)
                new_states.append(new_state)
            for i in range(gh):
                st_ref[0, i, n] = states[i]
                o_ref[rows, _lanes(i)] = outs[i].astype(o_ref.dtype)
            return tuple(new_states)

        lax.fori_loop(0, n_chunks, step, tuple(jnp.zeros((HEAD, HEAD), F32) for _ in range(gh)))

    return pl.pallas_call(
        body, name=name, grid=(nb, nh // gh),
        in_specs=[sp["q"], sp["k"], sp["v"], sp["z"], sp["ba"], sp["gp"]],
        out_specs=[sp["head"], sp["st"]],
        out_shape=[jax.ShapeDtypeStruct((nb * s, nh * HEAD), BF16),
                   jax.ShapeDtypeStruct((nb, nh, n_chunks, HEAD, HEAD), F32)],
        compiler_params=_cparams(("parallel", "parallel")),
    )(qkvc, qkvc, qkvc, p, pba, gp)


def gdn_bwd(qkvc, p, pba, gp, states, dout, *, nb, s, nh, nqk, zcol, name):
    rb = min(GDN_ROWS, s)
    n_chunks = s // rb
    gh = GDN_HEADS_PER_STEP
    rep = nh // nqk
    sp = _gdn_specs(s, nh, nqk, zcol, n_chunks, gh)

    def body(q_ref, k_ref, v_ref, z_ref, ba_ref, gp_ref, st_ref, do_ref,
             dq_ref, dk_ref, dv_ref, dz_ref, dba_ref, dgp_ref):
        h0 = pl.program_id(1) * gh
        ng = gp_ref[pl.ds(2, 1), :]
        lane = lax.broadcasted_iota(jnp.int32, (rb, HEAD), 1)
        heads = []
        for i in range(gh):
            alog, dtb, sel_row = _gdn_scalars(gp_ref, h0 + i)
            heads.append((alog, dtb, (lane == h0 + i).astype(F32), (lane == nh + h0 + i).astype(F32), sel_row))

        @pl.when(h0 == 0)
        def _():
            dba_ref[...] = jnp.zeros_like(dba_ref)
            dgp_ref[...] = jnp.zeros_like(dgp_ref)

        def step(it, carry):
            n = n_chunks - 1 - it
            rows = pl.ds(pl.multiple_of(n * rb, rb), rb)
            ba = ba_ref[rows, :]
            dba_old = dba_ref[rows, :]
            qs = [q_ref[rows, _lanes(j)] for j in range(gh // rep)]
            ks = [k_ref[rows, _lanes(j)] for j in range(gh // rep)]
            vs = [v_ref[rows, _lanes(i)] for i in range(gh)]
            zs = [z_ref[rows, _lanes(i)] for i in range(gh)]
            dos = [do_ref[rows, _lanes(i)] for i in range(gh)]
            sts = [st_ref[0, i, n] for i in range(gh)]
            new_carry, dba, gqs, gks, gvs, gzs = [], dba_old, {}, {}, [], []
            for i, (alog, dtb, sel_b, sel_a, _) in enumerate(heads):
                dstate, dalog, ddtb, dng = carry[i]
                braw = jnp.sum(ba * sel_b, axis=-1, keepdims=True)
                araw = jnp.sum(ba * sel_a, axis=-1, keepdims=True)
                _, vjp = jax.vjp(_gdn_chunk, qs[i // rep], ks[i // rep], vs[i], zs[i], braw, araw, alog, dtb, ng,
                                 sts[i])
                gq, gk, gv, gz, gb, ga, galog, gdtb, gng, gstate = vjp((dos[i], dstate))
                gqs[i // rep] = gq if i % rep == 0 else gqs[i // rep] + gq
                gks[i // rep] = gk if i % rep == 0 else gks[i // rep] + gk
                gvs.append(gv)
                gzs.append(gz)
                dba = dba + (gb * sel_b + ga * sel_a)
                new_carry.append((gstate, dalog + galog, ddtb + gdtb, dng + gng))
            for i in range(gh):
                dv_ref[rows, _lanes(i)] = gvs[i]
                dz_ref[rows, _lanes(i)] = gzs[i].astype(dz_ref.dtype)
            for j in gqs:
                dq_ref[rows, _lanes(j)] = gqs[j]
                dk_ref[rows, _lanes(j)] = gks[j]
            dba_ref[rows, :] = dba
            return tuple(new_carry)

        zero11 = jnp.zeros((1, 1), F32)
        init = tuple((jnp.zeros((HEAD, HEAD), F32), zero11, zero11, jnp.zeros((1, HEAD), F32)) for _ in range(gh))
        final = lax.fori_loop(0, n_chunks, step, init)
        rows3 = [sum(final[i][1] * heads[i][4] for i in range(gh)), sum(final[i][2] * heads[i][4] for i in range(gh)),
                 sum(final[i][3] for i in range(gh))]
        dgp_ref[0] += _rows_to_block(rows3, 8, HEAD)

    return pl.pallas_call(
        body, name=name, grid=(nb, nh // gh),
        in_specs=[sp["q"], sp["k"], sp["v"], sp["z"], sp["ba"], sp["gp"], sp["st"], sp["head"]],
        out_specs=[sp["qk_out"], sp["qk_out"], sp["head"], sp["head"], sp["ba"],
                   pl.BlockSpec((1, 8, HEAD), lambda b, j: (b, 0, 0))],
        out_shape=[jax.ShapeDtypeStruct((nb * s, nqk * HEAD), F32), jax.ShapeDtypeStruct((nb * s, nqk * HEAD), F32),
                   jax.ShapeDtypeStruct((nb * s, nh * HEAD), F32), jax.ShapeDtypeStruct((nb * s, nh * HEAD), BF16),
                   jax.ShapeDtypeStruct((nb * s, HEAD), F32), jax.ShapeDtypeStruct((nb, 8, HEAD), F32)],
        compiler_params=_cparams(("parallel", "arbitrary")),
    )(qkvc, qkvc, qkvc, p, pba, gp, states, dout)


ADAM_LR = 0.001
ADAM_B1 = 0.9
ADAM_B2 = 0.999
ADAM_EPS = 1e-08
ADAM_WD = 0.01
ADAM_STEP = 10
EW_BLOCK_BYTES = 1 << 20


def _row_tile(rows, cols):
    for tr in (1024, 512, 256, 128, 64, 32, 16, 8):
        if rows % tr == 0 and tr * cols * 4 <= EW_BLOCK_BYTES:
            return tr
    return rows


def sum_slots(r, *, name):
    n = r.shape[0]
    shape = r.shape[1:]
    cols = shape[-1]
    r3 = r.reshape(n, -1, cols)
    rows = r3.shape[1]
    tr = _row_tile(rows, cols)

    def body(r_ref, o_ref):
        acc = r_ref[0].astype(F32)
        for i in range(1, n):
            acc = acc + r_ref[i].astype(F32)
        o_ref[...] = acc

    out = pl.pallas_call(
        body, name=name, grid=(rows // tr,),
        in_specs=[pl.BlockSpec((n, tr, cols), lambda i: (0, i, 0))],
        out_specs=pl.BlockSpec((tr, cols), lambda i: (i, 0)),
        out_shape=jax.ShapeDtypeStruct((rows, cols), F32),
        compiler_params=_cparams(("parallel",)),
    )(r3)
    return out.reshape(shape)


def adamw(g_parts, w, m, v, *, name):
    shape = w.shape
    cols = shape[-1]
    to2d = lambda a: a.reshape(-1, cols)
    rows = to2d(w).shape[0]
    tr = _row_tile(rows, cols)
    npart = len(g_parts)
    c1 = 1.0 - ADAM_B1 ** ADAM_STEP
    c2 = 1.0 - ADAM_B2 ** ADAM_STEP

    def body(*refs):
        w_ref, m_ref, v_ref = refs[npart:npart + 3]
        g_ref, d_ref, nm_ref, nv_ref = refs[npart + 3:]
        g = refs[0][...]
        for i in range(1, npart):
            g = g + refs[i][...]
        nm = ADAM_B1 * m_ref[...] + (1.0 - ADAM_B1) * g
        nv = ADAM_B2 * v_ref[...] + (1.0 - ADAM_B2) * (g * g)
        g_ref[...] = g
        nm_ref[...] = nm
        nv_ref[...] = nv
        d_ref[...] = -ADAM_LR * ((nm / c1) / (jnp.sqrt(nv / c2) + ADAM_EPS) + ADAM_WD * w_ref[...])

    blk = pl.BlockSpec((tr, cols), lambda i: (i, 0))
    outs = pl.pallas_call(
        body, name=name, grid=(rows // tr,),
        in_specs=[blk] * (npart + 3),
        out_specs=[blk] * 4,
        out_shape=[jax.ShapeDtypeStruct((rows, cols), F32)] * 4,
        compiler_params=_cparams(("parallel",)),
    )(*[to2d(a) for a in g_parts], to2d(w), to2d(m), to2d(v))
    return tuple(o.reshape(shape) for o in outs)


MESH = pl.DeviceIdType.MESH
ANY = pl.BlockSpec(memory_space=pl.ANY)


def _xy_peers():
    x, y = lax.axis_index("x"), lax.axis_index("y")
    peers = []
    for fx, fy in ((0, 1), (1, 0), (1, 1)):
        px = 1 - x if fx else x
        py = 1 - y if fy else y
        peers.append((2 * px + py, px, py))
    return 2 * x + y, peers


def xy_exchange(arrs, *, gather, name):
    n = len(arrs)

    def body(*refs):
        ins, outs = refs[:n], refs[n:2 * n]
        send_sems, recv_sems, local_sems = refs[2 * n:]
        me, peers = _xy_peers()
        c = lax.axis_index("c")
        copies = []
        for k in range(n):
            src_own = ins[k] if gather else ins[k].at[me]
            local = pltpu.make_async_copy(src_own, outs[k].at[me], local_sems.at[k])
            local.start()
            copies.append(local)
            for j, (slot, px, py) in enumerate(peers):
                rc = pltpu.make_async_remote_copy(
                    src_ref=ins[k] if gather else ins[k].at[slot],
                    dst_ref=outs[k].at[me],
                    send_sem=send_sems.at[3 * k + j], recv_sem=recv_sems.at[3 * k + j],
                    device_id=(px, py, c), device_id_type=MESH)
                rc.start()
                copies.append(rc)
        for cp in copies:
            cp.wait()

    out_shape = [jax.ShapeDtypeStruct((4,) + (a.shape if gather else a.shape[1:]), a.dtype) for a in arrs]
    return pl.pallas_call(
        body, name=name,
        in_specs=[ANY] * n, out_specs=[ANY] * n, out_shape=out_shape,
        scratch_shapes=[pltpu.SemaphoreType.DMA((3 * n,)), pltpu.SemaphoreType.DMA((3 * n,)),
                        pltpu.SemaphoreType.DMA((n,))],
    )(*arrs)


def sibling_swap(arrs, *, name):
    n = len(arrs)

    def body(*refs):
        ins, outs = refs[:n], refs[n:2 * n]
        send_sems, recv_sems = refs[2 * n:]
        peer = (lax.axis_index("x"), lax.axis_index("y"), 1 - lax.axis_index("c"))
        copies = [pltpu.make_async_remote_copy(src_ref=ins[k], dst_ref=outs[k], send_sem=send_sems.at[k],
                                               recv_sem=recv_sems.at[k], device_id=peer, device_id_type=MESH)
                  for k in range(n)]
        for cp in copies:
            cp.start()
        for cp in copies:
            cp.wait()

    return pl.pallas_call(
        body, name=name,
        in_specs=[ANY] * n, out_specs=[ANY] * n,
        out_shape=[jax.ShapeDtypeStruct(a.shape, a.dtype) for a in arrs],
        scratch_shapes=[pltpu.SemaphoreType.DMA((n,)), pltpu.SemaphoreType.DMA((n,))],
    )(*arrs)


def allreduce_small(vec, *, name):
    r = vec.shape[0]

    def body(v_ref, o_ref, slots, send_sems, recv_sems):
        x, y, c = lax.axis_index("x"), lax.axis_index("y"), lax.axis_index("c")
        me = 4 * x + 2 * y + c
        slots[me] = v_ref[...]
        copies = []
        for j in range(1, 8):
            px = 1 - x if j & 4 else x
            py = 1 - y if j & 2 else y
            pc = 1 - c if j & 1 else c
            rc = pltpu.make_async_remote_copy(src_ref=v_ref, dst_ref=slots.at[me], send_sem=send_sems.at[j - 1],
                                              recv_sem=recv_sems.at[j - 1], device_id=(px, py, pc), device_id_type=MESH)
            rc.start()
            copies.append(rc)
        for cp in copies:
            cp.wait()
        acc = slots[0]
        for i in range(1, 8):
            acc = acc + slots[i]
        o_ref[...] = acc

    vm = pl.BlockSpec(memory_space=pltpu.VMEM)
    return pl.pallas_call(
        body, name=name, in_specs=[vm], out_specs=vm,
        out_shape=jax.ShapeDtypeStruct((r, 128), F32),
        scratch_shapes=[pltpu.VMEM((8, r, 128), F32), pltpu.SemaphoreType.DMA((7,)), pltpu.SemaphoreType.DMA((7,))],
        compiler_params=pltpu.CompilerParams(vmem_limit_bytes=VMEM_LIMIT),
    )(vec)


WEIGHTS = ("mix_norm_g", "w_in", "conv_dw_w", "conv_dw_b", "conv_ln_g", "conv_ln_b", "conv_pw_w", "conv_pw_b",
           "gdn_conv_w", "gdn_a_log", "gdn_dt_bias", "gdn_norm_g", "w_out", "ffn_norm_g", "w_up", "ffn_conv_w",
           "ffn_conv_b", "w_down", "final_norm_g")
COL_SHARDED = ("w_in", "w_up", "conv_dw_w", "gdn_conv_w", "ffn_conv_w")
ROW_SHARDED = ("conv_pw_w", "w_out", "w_down")
BIG = ("w_in", "conv_pw_w", "w_out", "w_up", "w_down")
SMALL_CONV = ("conv_dw_w", "gdn_conv_w", "ffn_conv_w")


def _full_from_slots(name, g, layer):
    part = g[:, layer]
    if name in COL_SHARDED:
        r, cs = part.shape[1:]
        return jnp.transpose(part, (1, 0, 2)).reshape(r, 4 * cs)
    rs, c = part.shape[1:]
    return part.reshape(4 * rs, c)


def _slots_from_full(name, full):
    if name in COL_SHARDED:
        r, c = full.shape
        return jnp.transpose(full.reshape(r, 4, c // 4), (1, 0, 2))
    r, c = full.shape
    return full.reshape(4, r // 4, c)


def _pack(parts):
    flat = jnp.concatenate([p.reshape(-1).astype(F32) for p in parts])
    pad = (-flat.shape[0]) % 1024
    return jnp.pad(flat, (0, pad)).reshape(-1, 128)


def _unpack(vec, shapes):
    flat = vec.reshape(-1)
    out, pos = [], 0
    for shp in shapes:
        n = 1
        for d in shp:
            n *= d
        out.append(flat[pos:pos + n].reshape(shp))
        pos += n
    return out


def kernel(x, mix_norm_g, w_in, conv_dw_w, conv_dw_b, conv_ln_g, conv_ln_b, conv_pw_w, conv_pw_b, gdn_conv_w, gdn_a_log, gdn_dt_bias, gdn_norm_g, w_out, ffn_norm_g, w_up, ffn_conv_w, ffn_conv_b, w_down, final_norm_g, loss_target, m_mix_norm_g, m_w_in, m_conv_dw_w, m_conv_dw_b, m_conv_ln_g, m_conv_ln_b, m_conv_pw_w, m_conv_pw_b, m_gdn_conv_w, m_gdn_a_log, m_gdn_dt_bias, m_gdn_norm_g, m_w_out, m_ffn_norm_g, m_w_up, m_ffn_conv_w, m_ffn_conv_b, m_w_down, m_final_norm_g, v_mix_norm_g, v_w_in, v_conv_dw_w, v_conv_dw_b, v_conv_ln_g, v_conv_ln_b, v_conv_pw_w, v_conv_pw_b, v_gdn_conv_w, v_gdn_a_log, v_gdn_dt_bias, v_gdn_norm_g, v_w_out, v_ffn_norm_g, v_w_up, v_ffn_conv_w, v_ffn_conv_b, v_w_down, v_final_norm_g):
    wts = dict(zip(WEIGHTS, (mix_norm_g, w_in, conv_dw_w, conv_dw_b, conv_ln_g, conv_ln_b, conv_pw_w, conv_pw_b,
                             gdn_conv_w, gdn_a_log, gdn_dt_bias, gdn_norm_g, w_out, ffn_norm_g, w_up, ffn_conv_w,
                             ffn_conv_b, w_down, final_norm_g)))
    mom = dict(zip(WEIGHTS, (m_mix_norm_g, m_w_in, m_conv_dw_w, m_conv_dw_b, m_conv_ln_g, m_conv_ln_b, m_conv_pw_w,
                             m_conv_pw_b, m_gdn_conv_w, m_gdn_a_log, m_gdn_dt_bias, m_gdn_norm_g, m_w_out,
                             m_ffn_norm_g, m_w_up, m_ffn_conv_w, m_ffn_conv_b, m_w_down, m_final_norm_g)))
    var = dict(zip(WEIGHTS, (v_mix_norm_g, v_w_in, v_conv_dw_w, v_conv_dw_b, v_conv_ln_g, v_conv_ln_b, v_conv_pw_w,
                             v_conv_pw_b, v_gdn_conv_w, v_gdn_a_log, v_gdn_dt_bias, v_gdn_norm_g, v_w_out,
                             v_ffn_norm_g, v_w_up, v_ffn_conv_w, v_ffn_conv_b, v_w_down, v_final_norm_g)))

    nb, s, d = x.shape
    t = nb * s
    depth = mix_norm_g.shape[0]
    ch = conv_dw_b.shape[1]
    nh = gdn_a_log.shape[1]
    nqk = nh // 2
    kwid, vwid = nqk * HEAD, nh * HEAD
    main = 2 * ch + 2 * kwid + 2 * vwid
    qcol, zcol = 2 * ch, 2 * ch + 2 * kwid + vwid
    dff = ffn_conv_b.shape[1]
    my_xy = 2 * lax.axis_index("x") + lax.axis_index("y")

    gather_names = BIG + SMALL_CONV
    gathered = xy_exchange([wts[n].astype(BF16) if n in BIG else wts[n] for n in gather_names],
                           gather=True, name="gather_weights")
    gathered = dict(zip(gather_names, gathered))

    def layer_weights(l):
        full = {n: _full_from_slots(n, gathered[n], l) for n in gather_names}
        w_in_f = full["w_in"]
        lw = dict(
            w_main=w_in_f[:, :main],
            w_ba=jnp.pad(w_in_f[:, main:], ((0, 0), (0, HEAD - 2 * nh))),
            pw=full["conv_pw_w"], wout_a=full["w_out"][:ch], wout_b=full["w_out"][ch:],
            wup=full["w_up"], wdown=full["w_down"],
            dw_w=full["conv_dw_w"], gconv_w=full["gdn_conv_w"], fconv_w=full["ffn_conv_w"],
            gp=jnp.zeros((8, HEAD), F32).at[0, :nh].set(gdn_a_log[l]).at[1, :nh].set(gdn_dt_bias[l])
                .at[2].set(gdn_norm_g[l]),
        )
        return lw

    lws = [layer_weights(l) for l in range(depth)]

    xc = x.reshape(t, d)
    saved = []
    for l in range(depth):
        lw = lws[l]
        h = rmsnorm_fwd(xc, mix_norm_g[l], name=f"f{l}_norm1")
        p = matmul(h, lw["w_main"], name=f"f{l}_in_main")
        pba = matmul(h, lw["w_ba"], name=f"f{l}_in_ba")
        u3 = conf_fwd(p, lw["dw_w"], conv_dw_b[l], conv_ln_g[l], conv_ln_b[l], nb=nb, s=s, name=f"f{l}_conf")
        out_a = matmul(u3, lw["pw"], bias=conv_pw_b[l], out_dtype=BF16, name=f"f{l}_pw")
        qkvc = qkvconv_fwd(p, lw["gconv_w"], col0=qcol, nb=nb, s=s, name=f"f{l}_qkvconv")
        out_b, states = gdn_fwd(qkvc, p, pba, lw["gp"], nb=nb, s=s, nh=nh, nqk=nqk, zcol=zcol, name=f"f{l}_gdn")
        x1 = matmul(out_a, lw["wout_a"], res=xc, name=f"f{l}_out_a")
        x1 = matmul(out_b, lw["wout_b"], res=x1, name=f"f{l}_out_b")
        h2 = rmsnorm_fwd(x1, ffn_norm_g[l], name=f"f{l}_norm2")
        up = matmul(h2, lw["wup"], name=f"f{l}_up")
        act = ffn_act_fwd(up, lw["fconv_w"], ffn_conv_b[l], nb=nb, s=s, name=f"f{l}_act")
        x2 = matmul(act, lw["wdown"], res=x1, name=f"f{l}_down")
        saved.append(dict(x=xc, h=h, p=p, pba=pba, u3=u3, out_a=out_a, qkvc=qkvc, out_b=out_b, states=states,
                          x1=x1, h2=h2, up=up, act=act))
        xc = x2

    loss_blk, dx, dxb, dgf = loss_head(xc, final_norm_g, loss_target.reshape(t, d), name="loss_head")

    big_grads = {n: [None] * depth for n in BIG}
    small_grads = {n: [None] * depth for n in WEIGHTS if n not in BIG and n != "final_norm_g"}
    for l in reversed(range(depth)):
        lw, sv = lws[l], saved[l]
        dact = matmul(dxb, lw["wdown"], tb=True, name=f"b{l}_dact")
        big_grads["w_down"][l] = matmul(sv["act"], dxb, ta=True, name=f"b{l}_dwdown")
        dgate, dupv, fpart = ffn_act_bwd(sv["up"], lw["fconv_w"], ffn_conv_b[l], dact, nb=nb, s=s, name=f"b{l}_act")
        dh2 = matmul(dgate, lw["wup"], tb=True, name=f"b{l}_dh2_gate")
        dh2 = matmul(dupv, lw["wup"], tb=True, b_koff=dff, res=dh2, name=f"b{l}_dh2_up")
        big_grads["w_up"][l] = jnp.concatenate(
            [matmul(sv["h2"], dgate, ta=True, name=f"b{l}_dwup_gate"),
             matmul(sv["h2"], dupv, ta=True, name=f"b{l}_dwup_up")], axis=1)
        dx1, dx1b, dg2 = rmsnorm_bwd(sv["x1"], ffn_norm_g[l], dh2, dx, name=f"b{l}_norm2")
        fsum = jnp.sum(fpart, axis=0)
        small_grads["ffn_norm_g"][l] = dg2[0]
        small_grads["ffn_conv_w"][l] = fsum[:ffn_conv_w.shape[1]]
        small_grads["ffn_conv_b"][l] = fsum[ffn_conv_w.shape[1]]
        dout_a = matmul(dx1b, lw["wout_a"], tb=True, out_dtype=BF16, name=f"b{l}_dout_a")
        dout_b = matmul(dx1b, lw["wout_b"], tb=True, name=f"b{l}_dout_b")
        big_grads["w_out"][l] = jnp.concatenate(
            [matmul(sv["out_a"], dx1b, ta=True, name=f"b{l}_dwout_a"),
             matmul(sv["out_b"], dx1b, ta=True, name=f"b{l}_dwout_b")], axis=0)
        du3 = matmul(dout_a, lw["pw"], tb=True, name=f"b{l}_du3")
        big_grads["conv_pw_w"][l] = matmul(sv["u3"], dout_a, ta=True, name=f"b{l}_dwpw")
        dval, dagate, cw_part, cs_part = conf_bwd(sv["p"], lw["dw_w"], conv_dw_b[l], conv_ln_g[l], conv_ln_b[l],
                                                  du3, dout_a, nb=nb, s=s, name=f"b{l}_conf")
        csum = jnp.sum(cs_part, axis=0)
        small_grads["conv_dw_w"][l] = jnp.sum(cw_part, axis=0)
        small_grads["conv_dw_b"][l] = csum[0]
        small_grads["conv_ln_g"][l] = csum[1]
        small_grads["conv_ln_b"][l] = csum[2]
        small_grads["conv_pw_b"][l] = csum[3]
        dq, dk, dv, dz, dpba, dgp = gdn_bwd(sv["qkvc"], sv["p"], sv["pba"], lw["gp"], sv["states"], dout_b,
                                            nb=nb, s=s, nh=nh, nqk=nqk, zcol=zcol, name=f"b{l}_gdn")
        dqkv, gw_part = qkvconv_bwd(sv["p"], lw["gconv_w"], jnp.concatenate([dq, dk, dv], axis=1),
                                    col0=qcol, nb=nb, s=s, name=f"b{l}_qkvconv")
        gsum = jnp.sum(dgp, axis=0)
        small_grads["gdn_conv_w"][l] = jnp.sum(gw_part, axis=0)[:gdn_conv_w.shape[1]]
        small_grads["gdn_a_log"][l] = gsum[0, :nh]
        small_grads["gdn_dt_bias"][l] = gsum[1, :nh]
        small_grads["gdn_norm_g"][l] = gsum[2]
        dp = jnp.concatenate([dval, dagate, dqkv, dz], axis=1)
        dh = matmul(dp, lw["w_main"], tb=True, name=f"b{l}_dh_main")
        dh = matmul(dpba, lw["w_ba"], tb=True, res=dh, name=f"b{l}_dh_ba")
        dw_main = matmul(sv["h"], dp, ta=True, name=f"b{l}_dwin_main")
        dw_ba = matmul(sv["h"], dpba, ta=True, name=f"b{l}_dwin_ba")
        big_grads["w_in"][l] = jnp.concatenate([dw_main, dw_ba[:, :2 * nh]], axis=1)
        dx, dxb, dg1 = rmsnorm_bwd(sv["x"], mix_norm_g[l], dh, dx1, name=f"b{l}_norm1")
        small_grads["mix_norm_g"][l] = dg1[0]

    stacks = [jnp.stack([_slots_from_full(n, big_grads[n][l]).astype(BF16) for l in range(depth)], axis=1)
              for n in BIG]
    received = xy_exchange(stacks, gather=False, name="scatter_grads")
    partial = [sum_slots(r, name=f"sum_{n}") for n, r in zip(BIG, received)]
    other = sibling_swap(partial, name="swap_partials")

    grads, deltas, new_m, new_v = {}, {}, {}, {}
    for n, mine, theirs in zip(BIG, partial, other):
        grads[n], deltas[n], new_m[n], new_v[n] = adamw([mine, theirs], wts[n], mom[n], var[n], name=f"adamw_{n}")

    small_names = [n for n in WEIGHTS if n not in BIG]
    small_full = [jnp.stack(small_grads[n]) if n != "final_norm_g" else dgf[0] for n in small_names]
    packed = _pack(small_full + [loss_blk[0, :1]])
    reduced = allreduce_small(packed, name="allreduce_small")
    parts = _unpack(reduced, [a.shape for a in small_full] + [(1,)])
    loss = parts[-1][0]
    for n, g in zip(small_names, parts[:-1]):
        if n in SMALL_CONV:
            wid = wts[n].shape[-1]
            g = lax.dynamic_slice_in_dim(g, my_xy * wid, wid, axis=g.ndim - 1)
        grads[n], deltas[n], new_m[n], new_v[n] = adamw([g], wts[n], mom[n], var[n], name=f"adamw_{n}")

    return (loss, dx.reshape(nb, s, d), *[grads[n] for n in WEIGHTS], *[deltas[n] for n in WEIGHTS],
            *[new_m[n] for n in WEIGHTS], *[new_v[n] for n in WEIGHTS])
```

```python
import functools

import jax
import jax.numpy as jnp
from jax import lax
from jax.experimental import pallas as pl
from jax.experimental.pallas import tpu as pltpu

F32 = jnp.float32
BF16 = jnp.bfloat16
EPS = 1e-6
CHUNK = 64
HEAD = 128
HIGHEST = lax.Precision.HIGHEST
VMEM_LIMIT = 56 * 1024 * 1024


def _pick(dim, cands):
    for c in cands:
        if dim % c == 0:
            return c
    return dim


def _cparams(sem):
    return pltpu.CompilerParams(dimension_semantics=sem, vmem_limit_bytes=VMEM_LIMIT)


MESH = pl.DeviceIdType.MESH
ANY = pl.BlockSpec(memory_space=pl.ANY)


def _xy_peers():
    x, y = lax.axis_index("x"), lax.axis_index("y")
    peers = []
    for fx, fy in ((0, 1), (1, 0), (1, 1)):
        px = 1 - x if fx else x
        py = 1 - y if fy else y
        peers.append((2 * px + py, px, py))
    return 2 * x + y, peers


class Exchange:
    def __init__(self, arrs, gather):
        self.arrs, self.gather, self.n = list(arrs), gather, len(arrs)
        self.out_shape = [jax.ShapeDtypeStruct((4,) + (a.shape if gather else a.shape[1:]), a.dtype) for a in arrs]
        self.scratch = [pltpu.SemaphoreType.DMA((3 * self.n,)), pltpu.SemaphoreType.DMA((3 * self.n,)),
                        pltpu.SemaphoreType.DMA((self.n,))]

    def copies(self, ins, outs, send_sems, recv_sems, local_sems):
        me, peers = _xy_peers()
        c = lax.axis_index("c")
        out = []
        for k in range(self.n):
            out.append(pltpu.make_async_copy(ins[k] if self.gather else ins[k].at[me], outs[k].at[me],
                                             local_sems.at[k]))
            for j, (slot, px, py) in enumerate(peers):
                out.append(pltpu.make_async_remote_copy(
                    src_ref=ins[k] if self.gather else ins[k].at[slot], dst_ref=outs[k].at[me],
                    send_sem=send_sems.at[3 * k + j], recv_sem=recv_sems.at[3 * k + j],
                    device_id=(px, py, c), device_id_type=MESH))
        return out


def carried_call(body, *, name, grid, in_specs, out_specs, out_shape, scratch_shapes, semantics, args, carry=None):
    n_in, n_out = len(in_specs), len(out_specs)
    if carry is None:
        outs = pl.pallas_call(body, name=name, grid=grid, in_specs=in_specs, out_specs=out_specs, out_shape=out_shape,
                              scratch_shapes=scratch_shapes, compiler_params=_cparams(semantics))(*args)
        return list(outs), []
    n = carry.n
    n_scr = len(scratch_shapes)

    def wrapped(*refs):
        ins, cin = refs[:n_in], refs[n_in:n_in + n]
        outs, cout = refs[n_in + n:n_in + n + n_out], refs[n_in + n + n_out:n_in + 2 * n + n_out]
        scratch = refs[n_in + 2 * n + n_out:n_in + 2 * n + n_out + n_scr]
        sems = refs[n_in + 2 * n + n_out + n_scr:]
        ids = [pl.program_id(i) for i in range(len(grid))]
        first = functools.reduce(jnp.logical_and, [i == 0 for i in ids])
        last = functools.reduce(jnp.logical_and, [i == g - 1 for i, g in zip(ids, grid)])

        @pl.when(first)
        def _():
            for cp in carry.copies(cin, cout, *sems):
                cp.start()

        body(*ins, *outs, *scratch)

        @pl.when(last)
        def _():
            for cp in carry.copies(cin, cout, *sems):
                cp.wait()

    res = pl.pallas_call(
        wrapped, name=name, grid=grid, in_specs=list(in_specs) + [ANY] * n, out_specs=list(out_specs) + [ANY] * n,
        out_shape=list(out_shape) + carry.out_shape, scratch_shapes=list(scratch_shapes) + carry.scratch,
        compiler_params=_cparams(tuple("arbitrary" for _ in grid)),
    )(*args, *carry.arrs)
    return list(res[:n_out]), list(res[n_out:])


def xy_exchange(arrs, *, gather, name):
    ex = Exchange(arrs, gather)

    def body(*refs):
        cps = ex.copies(refs[:ex.n], refs[ex.n:2 * ex.n], *refs[2 * ex.n:])
        for cp in cps:
            cp.start()
        for cp in cps:
            cp.wait()

    return pl.pallas_call(body, name=name, in_specs=[ANY] * ex.n, out_specs=[ANY] * ex.n, out_shape=ex.out_shape,
                          scratch_shapes=ex.scratch)(*arrs)


def matmul(a, b, *, ta=False, tb=False, out_dtype=F32, res=None, bias=None, b_koff=0, name, carry=None):
    if ta:
        kdim, m = a.shape
    else:
        m, kdim = a.shape
    if tb:
        n, kb = b.shape
    else:
        kb, n = b.shape
    assert kb >= kdim + b_koff, (a.shape, b.shape, ta, tb)
    tm = _pick(m, (1024, 512, 256, 128))
    tn = _pick(n, (1024, 512, 256, 128))
    tk = _pick(kdim, (512, 256, 128))
    nk = kdim // tk
    assert b_koff % tk == 0
    ko = b_koff // tk
    dims = (((0 if ta else 1,), (1 if tb else 0,)), ((), ()))

    def body(*refs):
        a_ref, b_ref = refs[0], refs[1]
        pos = 2
        bias_ref = res_ref = None
        if bias is not None:
            bias_ref = refs[pos]
            pos += 1
        if res is not None:
            res_ref = refs[pos]
            pos += 1
        o_ref, acc_ref = refs[pos], refs[pos + 1]
        k = pl.program_id(2)

        @pl.when(k == 0)
        def _():
            acc_ref[...] = jnp.zeros_like(acc_ref)

        acc_ref[...] += lax.dot_general(a_ref[...].astype(BF16), b_ref[...].astype(BF16), dims,
                                        preferred_element_type=F32)

        @pl.when(k == nk - 1)
        def _():
            r = acc_ref[...]
            if bias_ref is not None:
                r = r + bias_ref[...]
            if res_ref is not None:
                r = r + res_ref[...]
            o_ref[...] = r.astype(o_ref.dtype)

    a_spec = pl.BlockSpec((tk, tm), lambda i, j, k: (k, i)) if ta else pl.BlockSpec((tm, tk), lambda i, j, k: (i, k))
    b_spec = (pl.BlockSpec((tn, tk), lambda i, j, k: (j, k + ko)) if tb
              else pl.BlockSpec((tk, tn), lambda i, j, k: (k + ko, j)))
    in_specs = [a_spec, b_spec]
    args = [a, b]
    if bias is not None:
        in_specs.append(pl.BlockSpec((1, tn), lambda i, j, k: (0, j)))
        args.append(bias.reshape(1, n).astype(F32))
    if res is not None:
        in_specs.append(pl.BlockSpec((tm, tn), lambda i, j, k: (i, j)))
        args.append(res)
    outs, got = carried_call(
        body, name=name, grid=(m // tm, n // tn, nk), in_specs=in_specs,
        out_specs=[pl.BlockSpec((tm, tn), lambda i, j, k: (i, j))],
        out_shape=[jax.ShapeDtypeStruct((m, n), out_dtype)],
        scratch_shapes=[pltpu.VMEM((tm, tn), F32)],
        semantics=("parallel", "parallel", "arbitrary"), args=args, carry=carry)
    return outs[0] if carry is None else (outs[0], got)


def _sigmoid(x):
    return 1.0 / (1.0 + jnp.exp(-x))


def _softplus(x):
    return jnp.maximum(x, 0.0) + jnp.log(1.0 + jnp.exp(-jnp.abs(x)))


def _shift_back(x, s, row):
    if s == 0:
        return x
    return jnp.where(row >= s, pltpu.roll(x, s, 0), 0.0)


def _shift_fwd(x, s, row):
    if s == 0:
        return x
    n = x.shape[0]
    return jnp.where(row < n - s, pltpu.roll(x, n - s, 0), 0.0)


def _conv_fwd(x, w_ref, kw, row):
    acc = x * w_ref[pl.ds(kw - 1, 1), :]
    for s in range(1, kw):
        acc = acc + _shift_back(x, s, row) * w_ref[pl.ds(kw - 1 - s, 1), :]
    return acc


def _conv_bwd(x, dy, w_ref, kw, row):
    dx = dy * w_ref[pl.ds(kw - 1, 1), :]
    dw = [None] * kw
    dw[kw - 1] = jnp.sum(dy * x, axis=0, keepdims=True)
    for s in range(1, kw):
        dx = dx + _shift_fwd(dy, s, row) * w_ref[pl.ds(kw - 1 - s, 1), :]
        dw[kw - 1 - s] = jnp.sum(dy * _shift_back(x, s, row), axis=0, keepdims=True)
    return dx, dw


def _rows_to_block(rows, nrows, width):
    rid = lax.broadcasted_iota(jnp.int32, (nrows, width), 0)
    out = jnp.zeros((nrows, width), F32)
    for i, r in enumerate(rows):
        out = jnp.where(rid == i, r, out)
    return out


def rmsnorm_fwd(x, g, *, name):
    t, d = x.shape
    tm = _pick(t, (256, 128))

    def body(x_ref, g_ref, o_ref):
        xv = x_ref[...]
        r = lax.rsqrt(jnp.mean(xv * xv, axis=-1, keepdims=True) + EPS)
        o_ref[...] = (xv * r * g_ref[...]).astype(o_ref.dtype)

    return pl.pallas_call(
        body, name=name, grid=(t // tm,),
        in_specs=[pl.BlockSpec((tm, d), lambda i: (i, 0)), pl.BlockSpec((1, d), lambda i: (0, 0))],
        out_specs=pl.BlockSpec((tm, d), lambda i: (i, 0)),
        out_shape=jax.ShapeDtypeStruct((t, d), BF16),
        compiler_params=_cparams(("parallel",)),
    )(x, g.reshape(1, d))


def rmsnorm_bwd(x, g, dh, dres, *, name):
    t, d = x.shape
    tm = _pick(t, (256, 128))

    def body(x_ref, g_ref, dh_ref, dres_ref, dx_ref, dxb_ref, dg_ref):
        xv = x_ref[...]
        r = lax.rsqrt(jnp.mean(xv * xv, axis=-1, keepdims=True) + EPS)
        xh = xv * r
        dy = dh_ref[...]
        dxh = dy * g_ref[...]
        dx = dres_ref[...] + r * (dxh - xh * jnp.mean(dxh * xh, axis=-1, keepdims=True))
        dx_ref[...] = dx
        dxb_ref[...] = dx.astype(BF16)

        @pl.when(pl.program_id(0) == 0)
        def _():
            dg_ref[...] = jnp.zeros_like(dg_ref)

        dg_ref[...] += jnp.sum(dy * xh, axis=0, keepdims=True)

    row = pl.BlockSpec((tm, d), lambda i: (i, 0))
    vec = pl.BlockSpec((1, d), lambda i: (0, 0))
    return pl.pallas_call(
        body, name=name, grid=(t // tm,),
        in_specs=[row, vec, row, row],
        out_specs=[row, row, vec],
        out_shape=[jax.ShapeDtypeStruct((t, d), F32), jax.ShapeDtypeStruct((t, d), BF16),
                   jax.ShapeDtypeStruct((1, d), F32)],
        compiler_params=_cparams(("arbitrary",)),
    )(x, g.reshape(1, d), dh, dres)


def loss_head(x, g, target, *, name):
    t, d = x.shape
    tm = _pick(t, (256, 128))

    def body(x_ref, g_ref, tg_ref, loss_ref, dx_ref, dxb_ref, dg_ref):
        xv = x_ref[...]
        r = lax.rsqrt(jnp.mean(xv * xv, axis=-1, keepdims=True) + EPS)
        xh = xv * r
        err = xh * g_ref[...] - tg_ref[...]
        dy = err * (1.0 / d)
        dxh = dy * g_ref[...]
        dx = r * (dxh - xh * jnp.mean(dxh * xh, axis=-1, keepdims=True))
        dx_ref[...] = dx
        dxb_ref[...] = dx.astype(BF16)

        @pl.when(pl.program_id(0) == 0)
        def _():
            dg_ref[...] = jnp.zeros_like(dg_ref)
            loss_ref[...] = jnp.zeros_like(loss_ref)

        dg_ref[...] += jnp.sum(dy * xh, axis=0, keepdims=True)
        part = jnp.sum(jnp.sum(err * err, axis=-1, keepdims=True), axis=0, keepdims=True) * (0.5 / d)
        loss_ref[...] += jnp.broadcast_to(part, loss_ref.shape)

    row = pl.BlockSpec((tm, d), lambda i: (i, 0))
    vec = pl.BlockSpec((1, d), lambda i: (0, 0))
    return pl.pallas_call(
        body, name=name, grid=(t // tm,),
        in_specs=[row, vec, row],
        out_specs=[pl.BlockSpec((8, 128), lambda i: (0, 0)), row, row, vec],
        out_shape=[jax.ShapeDtypeStruct((8, 128), F32), jax.ShapeDtypeStruct((t, d), F32),
                   jax.ShapeDtypeStruct((t, d), BF16), jax.ShapeDtypeStruct((1, d), F32)],
        compiler_params=_cparams(("arbitrary",)),
    )(x, g.reshape(1, d), target)


def _conf_forward_parts(val, gate, w_ref, b, lg, lb, kw, row):
    sg = _sigmoid(gate)
    u0 = val * sg
    u1 = _conv_fwd(u0, w_ref, kw, row) + b
    mu = jnp.mean(u1, axis=-1, keepdims=True)
    xc = u1 - mu
    rs = lax.rsqrt(jnp.mean(xc * xc, axis=-1, keepdims=True) + EPS)
    xh = xc * rs
    u2 = xh * lg + lb
    s2 = _sigmoid(u2)
    return sg, u0, rs, xh, u2, s2


def conf_fwd(p, dw_w, dw_b, ln_g, ln_b, *, nb, s, name):
    kw, ch = dw_w.shape
    ng = ch // HEAD

    def body(val_ref, gate_ref, w_ref, b_ref, lg_ref, lb_ref, o_ref):
        row = lax.broadcasted_iota(jnp.int32, (s, HEAD), 0)
        _, _, _, _, u2, s2 = _conf_forward_parts(val_ref[...], gate_ref[...], w_ref, b_ref[...], lg_ref[...],
                                                 lb_ref[...], kw, row)
        o_ref[...] = (u2 * s2).astype(o_ref.dtype)

    vec = pl.BlockSpec((1, HEAD), lambda b, g: (0, g))
    return pl.pallas_call(
        body, name=name, grid=(nb, ng),
        in_specs=[pl.BlockSpec((s, HEAD), lambda b, g: (b, g)), pl.BlockSpec((s, HEAD), lambda b, g: (b, ng + g)),
                  pl.BlockSpec((kw, HEAD), lambda b, g: (0, g)), vec, vec, vec],
        out_specs=pl.BlockSpec((s, HEAD), lambda b, g: (b, g)),
        out_shape=jax.ShapeDtypeStruct((nb * s, ch), BF16),
        compiler_params=_cparams(("parallel", "parallel")),
    )(p, p, dw_w, dw_b.reshape(1, ch), ln_g.reshape(1, ch), ln_b.reshape(1, ch))


def conf_bwd(p, dw_w, dw_b, ln_g, ln_b, du3, dout_a, *, nb, s, name):
    kw, ch = dw_w.shape
    ng = ch // HEAD

    def body(val_ref, gate_ref, w_ref, b_ref, lg_ref, lb_ref, du3_ref, doa_ref, dval_ref, dgate_ref, dw_out, sm_out):
        row = lax.broadcasted_iota(jnp.int32, (s, HEAD), 0)
        val = val_ref[...]
        sg, u0, rs, xh, u2, s2 = _conf_forward_parts(val, gate_ref[...], w_ref, b_ref[...], lg_ref[...],
                                                     lb_ref[...], kw, row)
        du2 = du3_ref[...] * (s2 * (1.0 + u2 * (1.0 - s2)))
        dlg = jnp.sum(du2 * xh, axis=0, keepdims=True)
        dlb = jnp.sum(du2, axis=0, keepdims=True)
        dxh = du2 * lg_ref[...]
        du1 = rs * (dxh - jnp.mean(dxh, axis=-1, keepdims=True) - xh * jnp.mean(dxh * xh, axis=-1, keepdims=True))
        ddb = jnp.sum(du1, axis=0, keepdims=True)
        du0, dw = _conv_bwd(u0, du1, w_ref, kw, row)
        dval_ref[...] = (du0 * sg).astype(dval_ref.dtype)
        dgate_ref[...] = (du0 * val * sg * (1.0 - sg)).astype(dgate_ref.dtype)
        for k in range(kw):
            dw_out[0, pl.ds(k, 1), :] = dw[k]
        dpb = jnp.sum(doa_ref[...].astype(F32), axis=0, keepdims=True)
        sm_out[0] = _rows_to_block([ddb, dlg, dlb, dpb], 8, HEAD)

    vec = pl.BlockSpec((1, HEAD), lambda b, g: (0, g))
    blk = pl.BlockSpec((s, HEAD), lambda b, g: (b, g))
    return pl.pallas_call(
        body, name=name, grid=(nb, ng),
        in_specs=[blk, pl.BlockSpec((s, HEAD), lambda b, g: (b, ng + g)),
                  pl.BlockSpec((kw, HEAD), lambda b, g: (0, g)), vec, vec, vec, blk, blk],
        out_specs=[blk, blk, pl.BlockSpec((1, kw, HEAD), lambda b, g: (b, 0, g)),
                   pl.BlockSpec((1, 8, HEAD), lambda b, g: (b, 0, g))],
        out_shape=[jax.ShapeDtypeStruct((nb * s, ch), BF16), jax.ShapeDtypeStruct((nb * s, ch), BF16),
                   jax.ShapeDtypeStruct((nb, kw, ch), F32), jax.ShapeDtypeStruct((nb, 8, ch), F32)],
        compiler_params=_cparams(("parallel", "parallel")),
    )(p, p, dw_w, dw_b.reshape(1, ch), ln_g.reshape(1, ch), ln_b.reshape(1, ch), du3, dout_a)


def qkvconv_fwd(p, w, *, col0, nb, s, name):
    kw, ch = w.shape
    nblk = ch // HEAD
    c0 = col0 // HEAD

    def body(x_ref, w_ref, o_ref):
        row = lax.broadcasted_iota(jnp.int32, (s, HEAD), 0)
        c = _conv_fwd(x_ref[...], w_ref, kw, row)
        o_ref[...] = c * _sigmoid(c)

    return pl.pallas_call(
        body, name=name, grid=(nb, nblk),
        in_specs=[pl.BlockSpec((s, HEAD), lambda b, j: (b, c0 + j)), pl.BlockSpec((kw, HEAD), lambda b, j: (0, j))],
        out_specs=pl.BlockSpec((s, HEAD), lambda b, j: (b, j)),
        out_shape=jax.ShapeDtypeStruct((nb * s, ch), F32),
        compiler_params=_cparams(("parallel", "parallel")),
    )(p, w)


def qkvconv_bwd(p, w, dy, *, col0, nb, s, name):
    kw, ch = w.shape
    nblk = ch // HEAD
    c0 = col0 // HEAD

    def body(x_ref, w_ref, dy_ref, dx_ref, dw_out):
        row = lax.broadcasted_iota(jnp.int32, (s, HEAD), 0)
        xv = x_ref[...]
        c = _conv_fwd(xv, w_ref, kw, row)
        sc = _sigmoid(c)
        dc = dy_ref[...] * (sc * (1.0 + c * (1.0 - sc)))
        dx, dw = _conv_bwd(xv, dc, w_ref, kw, row)
        dx_ref[...] = dx.astype(dx_ref.dtype)
        dw_out[0] = _rows_to_block(dw, 8, HEAD)

    blk = pl.BlockSpec((s, HEAD), lambda b, j: (b, j))
    return pl.pallas_call(
        body, name=name, grid=(nb, nblk),
        in_specs=[pl.BlockSpec((s, HEAD), lambda b, j: (b, c0 + j)), pl.BlockSpec((kw, HEAD), lambda b, j: (0, j)), blk],
        out_specs=[blk, pl.BlockSpec((1, 8, HEAD), lambda b, j: (b, 0, j))],
        out_shape=[jax.ShapeDtypeStruct((nb * s, ch), BF16), jax.ShapeDtypeStruct((nb, 8, ch), F32)],
        compiler_params=_cparams(("parallel", "parallel")),
    )(p, w, dy)


def ffn_act_fwd(up, w, b, *, nb, s, name):
    kw, dff = w.shape
    cb = _pick(dff, (256, 128))
    nblk = dff // cb

    def body(g_ref, u_ref, w_ref, b_ref, o_ref):
        row = lax.broadcasted_iota(jnp.int32, (s, cb), 0)
        gc = _conv_fwd(g_ref[...], w_ref, kw, row) + b_ref[...]
        o_ref[...] = (gc * _sigmoid(gc) * u_ref[...]).astype(o_ref.dtype)

    return pl.pallas_call(
        body, name=name, grid=(nb, nblk),
        in_specs=[pl.BlockSpec((s, cb), lambda i, j: (i, j)), pl.BlockSpec((s, cb), lambda i, j: (i, nblk + j)),
                  pl.BlockSpec((kw, cb), lambda i, j: (0, j)), pl.BlockSpec((1, cb), lambda i, j: (0, j))],
        out_specs=pl.BlockSpec((s, cb), lambda i, j: (i, j)),
        out_shape=jax.ShapeDtypeStruct((nb * s, dff), BF16),
        compiler_params=_cparams(("parallel", "parallel")),
    )(up, up, w, b.reshape(1, dff))


def ffn_act_bwd(up, w, b, dact, *, nb, s, name):
    kw, dff = w.shape
    cb = _pick(dff, (256, 128))
    nblk = dff // cb

    def body(g_ref, u_ref, w_ref, b_ref, da_ref, dg_ref, du_ref, sm_out):
        row = lax.broadcasted_iota(jnp.int32, (s, cb), 0)
        gv = g_ref[...]
        gc = _conv_fwd(gv, w_ref, kw, row) + b_ref[...]
        sc = _sigmoid(gc)
        da = da_ref[...].astype(F32)
        du_ref[...] = (da * gc * sc).astype(du_ref.dtype)
        dgc = da * u_ref[...] * (sc * (1.0 + gc * (1.0 - sc)))
        dgate, dw = _conv_bwd(gv, dgc, w_ref, kw, row)
        dg_ref[...] = dgate.astype(dg_ref.dtype)
        sm_out[0] = _rows_to_block(dw + [jnp.sum(dgc, axis=0, keepdims=True)], 8, cb)

    blk = pl.BlockSpec((s, cb), lambda i, j: (i, j))
    return pl.pallas_call(
        body, name=name, grid=(nb, nblk),
        in_specs=[blk, pl.BlockSpec((s, cb), lambda i, j: (i, nblk + j)),
                  pl.BlockSpec((kw, cb), lambda i, j: (0, j)), pl.BlockSpec((1, cb), lambda i, j: (0, j)), blk],
        out_specs=[blk, blk, pl.BlockSpec((1, 8, cb), lambda i, j: (i, 0, j))],
        out_shape=[jax.ShapeDtypeStruct((nb * s, dff), BF16), jax.ShapeDtypeStruct((nb * s, dff), BF16),
                   jax.ShapeDtypeStruct((nb, 8, dff), F32)],
        compiler_params=_cparams(("parallel", "parallel")),
    )(up, up, w, b.reshape(1, dff), dact)


def _dot(a, b, dims, prec=None):
    return lax.dot_general(a, b, (dims, ((), ())), precision=prec, preferred_element_type=F32)


def _mm_nn(a, b):
    return _dot(a.astype(BF16), b.astype(BF16), ((1,), (0,)))


def _mm_nt(a, b):
    return _dot(a.astype(BF16), b.astype(BF16), ((1,), (1,)))


def _mm_tn(a, b):
    return _dot(a.astype(BF16), b.astype(BF16), ((0,), (0,)))


def _mm_hi(a, b):
    return _dot(a, b, ((1,), (0,)), HIGHEST)


def _mm_inv(a, b):
    return _mm_nn(a, b)


def _split3(x):
    hi = x.astype(BF16)
    rest = x - hi.astype(F32)
    mid = rest.astype(BF16)
    return hi, mid, (rest - mid.astype(F32)).astype(BF16)


def _mask_dot(mask, x, dims, mask_first):
    if mask_first:
        return sum(_dot(mask, p, dims) for p in _split3(x))
    return sum(_dot(p, mask, dims) for p in _split3(x))


@jax.custom_vjp
def _mask_nn(mask, x):
    return _mask_dot(mask, x, ((1,), (0,)), True)


def _mask_nn_fwd(mask, x):
    return _mask_nn(mask, x), mask


def _mask_nn_bwd(mask, ct):
    return jnp.zeros_like(mask), _mask_dot(mask, ct, ((0,), (0,)), True)


_mask_nn.defvjp(_mask_nn_fwd, _mask_nn_bwd)


@jax.custom_vjp
def _mask_tn(x, mask):
    return _mask_dot(mask, x, ((0,), (0,)), False)


def _mask_tn_fwd(x, mask):
    return _mask_tn(x, mask), mask


def _mask_tn_bwd(mask, ct):
    return _mask_dot(mask, ct, ((1,), (1,)), True), jnp.zeros_like(mask)


_mask_tn.defvjp(_mask_tn_fwd, _mask_tn_bwd)


GDN_ROWS = 256


def _gdn_chunk(qc, kc, vc, zc, braw, araw, alog, dtb, ng, state):
    r = qc.shape[0]
    ri = lax.broadcasted_iota(jnp.int32, (r, r), 0)
    ci = lax.broadcasted_iota(jnp.int32, (r, r), 1)
    same = (ri // CHUNK) == (ci // CHUNK)
    causal = same & (ri >= ci)
    strict = same & (ri > ci)
    eye = (ri == ci).astype(F32)
    row_chunk = lax.broadcasted_iota(jnp.int32, (r, HEAD), 0) // CHUNK

    q = qc * lax.rsqrt(jnp.sum(qc * qc, axis=-1, keepdims=True) + EPS) * (HEAD ** -0.5)
    k = kc * lax.rsqrt(jnp.sum(kc * kc, axis=-1, keepdims=True) + EPS)
    beta = _sigmoid(braw)
    g = -jnp.exp(alog) * _softplus(araw + dtb)

    g_w = jnp.broadcast_to(g, (r, HEAD))
    widen = lambda t: jnp.concatenate([t] * (r // HEAD), axis=1)
    gw = _mask_nn(causal.astype(BF16), g_w)
    gi = widen(gw)
    gj = _mask_tn(widen(g_w), (same & (ri <= ci)).astype(BF16))
    decay = jnp.where(causal, jnp.exp(jnp.where(causal, gi - gj, 0.0)), 0.0)

    kb = k * beta
    vb = vc * beta
    lmat = jnp.where(strict, _mm_nt(kb, k) * decay, 0.0)
    x = -lmat
    ainv = eye + x
    p = 1
    while 2 * p < CHUNK:
        x = _mm_inv(x, x)
        ainv = _mm_inv(ainv, eye + x)
        p *= 2
    u = _mm_inv(ainv, vb)
    w = _mm_inv(ainv, kb * jnp.exp(gw))
    qk = jnp.where(causal, _mm_nt(q, k) * decay, 0.0)
    qg = q * jnp.exp(gw)

    o = jnp.zeros((r, HEAD), F32)
    for c in range(r // CHUNK):
        in_c = row_chunk == c
        glast = jnp.sum(jnp.where(in_c, g_w, 0.0), axis=0, keepdims=True)
        v_new = jnp.where(in_c, u - _mm_nn(w, state), 0.0)
        o = o + jnp.where(in_c, _mm_nn(qg, state), 0.0) + _mm_nn(qk, v_new)
        k_dec = jnp.where(in_c, k * jnp.exp(jnp.where(in_c, glast - gw, 0.0)), 0.0)
        state = state * jnp.exp(glast) + _mm_tn(k_dec, v_new)

    o = o * lax.rsqrt(jnp.mean(o * o, axis=-1, keepdims=True) + EPS) * ng
    o = o * (zc * _sigmoid(zc))
    return o, state


GDN_HEADS_PER_STEP = 2


def _gdn_specs(s, nh, nqk, zcol, n_chunks, gh):
    rep = nh // nqk
    qw, vw = (gh // rep) * HEAD, gh * HEAD
    assert gh % rep == 0 and nh % gh == 0 and (nqk * HEAD) % qw == 0 and (2 * nqk * HEAD) % vw == 0 and zcol % vw == 0
    k0, v0, z0 = (nqk * HEAD) // qw, (2 * nqk * HEAD) // vw, zcol // vw
    return dict(
        q=pl.BlockSpec((s, qw), lambda b, j: (b, j)),
        k=pl.BlockSpec((s, qw), lambda b, j: (b, k0 + j)),
        v=pl.BlockSpec((s, vw), lambda b, j: (b, v0 + j)),
        z=pl.BlockSpec((s, vw), lambda b, j: (b, z0 + j)),
        ba=pl.BlockSpec((s, HEAD), lambda b, j: (b, 0)),
        gp=pl.BlockSpec((8, HEAD), lambda b, j: (0, 0)),
        qk_out=pl.BlockSpec((s, qw), lambda b, j: (b, j)),
        head=pl.BlockSpec((s, vw), lambda b, j: (b, j)),
        st=pl.BlockSpec((1, gh, n_chunks, HEAD, HEAD), lambda b, j: (b, j, 0, 0, 0)),
    )


def _gdn_scalars(gp_ref, h):
    lane = lax.broadcasted_iota(jnp.int32, (1, HEAD), 1)
    sel = (lane == h).astype(F32)
    alog = jnp.sum(gp_ref[pl.ds(0, 1), :] * sel, axis=-1, keepdims=True)
    dtb = jnp.sum(gp_ref[pl.ds(1, 1), :] * sel, axis=-1, keepdims=True)
    return alog, dtb, sel


def _lanes(i):
    return pl.ds(i * HEAD, HEAD)


def gdn_fwd(qkvc, p, pba, gp, *, nb, s, nh, nqk, zcol, name, carry=None):
    rb = min(GDN_ROWS, s)
    n_chunks = s // rb
    gh = GDN_HEADS_PER_STEP
    rep = nh // nqk
    sp = _gdn_specs(s, nh, nqk, zcol, n_chunks, gh)

    def body(q_ref, k_ref, v_ref, z_ref, ba_ref, gp_ref, o_ref, st_ref):
        h0 = pl.program_id(1) * gh
        ng = gp_ref[pl.ds(2, 1), :]
        lane = lax.broadcasted_iota(jnp.int32, (rb, HEAD), 1)
        heads = []
        for i in range(gh):
            alog, dtb, _ = _gdn_scalars(gp_ref, h0 + i)
            heads.append((alog, dtb, (lane == h0 + i).astype(F32), (lane == nh + h0 + i).astype(F32)))

        def step(n, states):
            rows = pl.ds(pl.multiple_of(n * rb, rb), rb)
            ba = ba_ref[rows, :]
            qs = [q_ref[rows, _lanes(j)] for j in range(gh // rep)]
            ks = [k_ref[rows, _lanes(j)] for j in range(gh // rep)]
            vs = [v_ref[rows, _lanes(i)] for i in range(gh)]
            zs = [z_ref[rows, _lanes(i)] for i in range(gh)]
            outs, new_states = [], []
            for i, (alog, dtb, sel_b, sel_a) in enumerate(heads):
                braw = jnp.sum(ba * sel_b, axis=-1, keepdims=True)
                araw = jnp.sum(ba * sel_a, axis=-1, keepdims=True)
                o, new_state = _gdn_chunk(qs[i // rep], ks[i // rep], vs[i], zs[i], braw, araw, alog, dtb, ng,
                                          states[i])
                outs.append(o)
                new_states.append(new_state)
            for i in range(gh):
                st_ref[0, i, n] = states[i]
                o_ref[rows, _lanes(i)] = outs[i].astype(o_ref.dtype)
            return tuple(new_states)

        lax.fori_loop(0, n_chunks, step, tuple(jnp.zeros((HEAD, HEAD), F32) for _ in range(gh)))

    outs, got = carried_call(
        body, name=name, grid=(nb, nh // gh),
        in_specs=[sp["q"], sp["k"], sp["v"], sp["z"], sp["ba"], sp["gp"]],
        out_specs=[sp["head"], sp["st"]],
        out_shape=[jax.ShapeDtypeStruct((nb * s, nh * HEAD), BF16),
                   jax.ShapeDtypeStruct((nb, nh, n_chunks, HEAD, HEAD), F32)],
        scratch_shapes=[], semantics=("parallel", "parallel"), args=(qkvc, qkvc, qkvc, p, pba, gp), carry=carry)
    return outs + [got]


def gdn_bwd(qkvc, p, pba, gp, states, dout, *, nb, s, nh, nqk, zcol, name, carry=None):
    rb = min(GDN_ROWS, s)
    n_chunks = s // rb
    gh = GDN_HEADS_PER_STEP
    rep = nh // nqk
    sp = _gdn_specs(s, nh, nqk, zcol, n_chunks, gh)

    def body(q_ref, k_ref, v_ref, z_ref, ba_ref, gp_ref, st_ref, do_ref,
             dq_ref, dk_ref, dv_ref, dz_ref, dba_ref, dgp_ref):
        h0 = pl.program_id(1) * gh
        ng = gp_ref[pl.ds(2, 1), :]
        lane = lax.broadcasted_iota(jnp.int32, (rb, HEAD), 1)
        heads = []
        for i in range(gh):
            alog, dtb, sel_row = _gdn_scalars(gp_ref, h0 + i)
            heads.append((alog, dtb, (lane == h0 + i).astype(F32), (lane == nh + h0 + i).astype(F32), sel_row))

        @pl.when(h0 == 0)
        def _():
            dba_ref[...] = jnp.zeros_like(dba_ref)
            dgp_ref[...] = jnp.zeros_like(dgp_ref)

        def step(it, carry):
            n = n_chunks - 1 - it
            rows = pl.ds(pl.multiple_of(n * rb, rb), rb)
            ba = ba_ref[rows, :]
            dba_old = dba_ref[rows, :]
            qs = [q_ref[rows, _lanes(j)] for j in range(gh // rep)]
            ks = [k_ref[rows, _lanes(j)] for j in range(gh // rep)]
            vs = [v_ref[rows, _lanes(i)] for i in range(gh)]
            zs = [z_ref[rows, _lanes(i)] for i in range(gh)]
            dos = [do_ref[rows, _lanes(i)] for i in range(gh)]
            sts = [st_ref[0, i, n] for i in range(gh)]
            new_carry, dba, gqs, gks, gvs, gzs = [], dba_old, {}, {}, [], []
            for i, (alog, dtb, sel_b, sel_a, _) in enumerate(heads):
                dstate, dalog, ddtb, dng = carry[i]
                braw = jnp.sum(ba * sel_b, axis=-1, keepdims=True)
                araw = jnp.sum(ba * sel_a, axis=-1, keepdims=True)
                _, vjp = jax.vjp(_gdn_chunk, qs[i // rep], ks[i // rep], vs[i], zs[i], braw, araw, alog, dtb, ng,
                                 sts[i])
                gq, gk, gv, gz, gb, ga, galog, gdtb, gng, gstate = vjp((dos[i], dstate))
                gqs[i // rep] = gq if i % rep == 0 else gqs[i // rep] + gq
                gks[i // rep] = gk if i % rep == 0 else gks[i // rep] + gk
                gvs.append(gv)
                gzs.append(gz)
                dba = dba + (gb * sel_b + ga * sel_a)
                new_carry.append((gstate, dalog + galog, ddtb + gdtb, dng + gng))
            for i in range(gh):
                dv_ref[rows, _lanes(i)] = gvs[i]
                dz_ref[rows, _lanes(i)] = gzs[i].astype(dz_ref.dtype)
            for j in gqs:
                dq_ref[rows, _lanes(j)] = gqs[j]
                dk_ref[rows, _lanes(j)] = gks[j]
            dba_ref[rows, :] = dba
            return tuple(new_carry)

        zero11 = jnp.zeros((1, 1), F32)
        init = tuple((jnp.zeros((HEAD, HEAD), F32), zero11, zero11, jnp.zeros((1, HEAD), F32)) for _ in range(gh))
        final = lax.fori_loop(0, n_chunks, step, init)
        rows3 = [sum(final[i][1] * heads[i][4] for i in range(gh)), sum(final[i][2] * heads[i][4] for i in range(gh)),
                 sum(final[i][3] for i in range(gh))]
        dgp_ref[0] += _rows_to_block(rows3, 8, HEAD)

    outs, got = carried_call(
        body, name=name, grid=(nb, nh // gh),
        in_specs=[sp["q"], sp["k"], sp["v"], sp["z"], sp["ba"], sp["gp"], sp["st"], sp["head"]],
        out_specs=[sp["qk_out"], sp["qk_out"], sp["head"], sp["head"], sp["ba"],
                   pl.BlockSpec((1, 8, HEAD), lambda b, j: (b, 0, 0))],
        out_shape=[jax.ShapeDtypeStruct((nb * s, nqk * HEAD), F32), jax.ShapeDtypeStruct((nb * s, nqk * HEAD), F32),
                   jax.ShapeDtypeStruct((nb * s, nh * HEAD), F32), jax.ShapeDtypeStruct((nb * s, nh * HEAD), BF16),
                   jax.ShapeDtypeStruct((nb * s, HEAD), F32), jax.ShapeDtypeStruct((nb, 8, HEAD), F32)],
        scratch_shapes=[], semantics=("parallel", "arbitrary"),
        args=(qkvc, qkvc, qkvc, p, pba, gp, states, dout), carry=carry)
    return outs + [got]


ADAM_LR = 0.001
ADAM_B1 = 0.9
ADAM_B2 = 0.999
ADAM_EPS = 1e-08
ADAM_WD = 0.01
ADAM_STEP = 10
EW_BLOCK_BYTES = 1 << 20


def _row_tile(rows, cols):
    for tr in (1024, 512, 256, 128, 64, 32, 16, 8):
        if rows % tr == 0 and tr * cols * 4 <= EW_BLOCK_BYTES:
            return tr
    return rows


def sum_slots(rs, *, name):
    nl = len(rs)
    n, rows, cols = rs[0].shape
    tr = _row_tile(rows, cols)
    nblk = rows // tr

    def body(*refs):
        o_ref = refs[nl]
        for l in range(nl):
            @pl.when(pl.program_id(0) == l)
            def _(l=l):
                acc = refs[l][0].astype(F32)
                for i in range(1, n):
                    acc = acc + refs[l][i].astype(F32)
                o_ref[0] = acc

    def in_map(l):
        return lambda li, i: (0, jnp.where(li == l, i, jnp.where(li < l, 0, nblk - 1)), 0)

    return pl.pallas_call(
        body, name=name, grid=(nl, nblk),
        in_specs=[pl.BlockSpec((n, tr, cols), in_map(l)) for l in range(nl)],
        out_specs=pl.BlockSpec((1, tr, cols), lambda li, i: (li, i, 0)),
        out_shape=jax.ShapeDtypeStruct((nl, rows, cols), F32),
        compiler_params=_cparams(("arbitrary", "arbitrary")),
    )(*rs)


def adamw(g_parts, w, m, v, *, name):
    shape = w.shape
    cols = shape[-1]
    to2d = lambda a: a.reshape(-1, cols)
    rows = to2d(w).shape[0]
    tr = _row_tile(rows, cols)
    npart = len(g_parts)
    c1 = 1.0 - ADAM_B1 ** ADAM_STEP
    c2 = 1.0 - ADAM_B2 ** ADAM_STEP

    def body(*refs):
        w_ref, m_ref, v_ref = refs[npart:npart + 3]
        g_ref, d_ref, nm_ref, nv_ref = refs[npart + 3:]
        g = refs[0][...]
        for i in range(1, npart):
            g = g + refs[i][...]
        nm = ADAM_B1 * m_ref[...] + (1.0 - ADAM_B1) * g
        nv = ADAM_B2 * v_ref[...] + (1.0 - ADAM_B2) * (g * g)
        g_ref[...] = g
        nm_ref[...] = nm
        nv_ref[...] = nv
        d_ref[...] = -ADAM_LR * ((nm / c1) / (jnp.sqrt(nv / c2) + ADAM_EPS) + ADAM_WD * w_ref[...])

    blk = pl.BlockSpec((tr, cols), lambda i: (i, 0))
    outs = pl.pallas_call(
        body, name=name, grid=(rows // tr,),
        in_specs=[blk] * (npart + 3),
        out_specs=[blk] * 4,
        out_shape=[jax.ShapeDtypeStruct((rows, cols), F32)] * 4,
        compiler_params=_cparams(("parallel",)),
    )(*[to2d(a) for a in g_parts], to2d(w), to2d(m), to2d(v))
    return tuple(o.reshape(shape) for o in outs)


def sibling_swap(arrs, *, name):
    n = len(arrs)

    def body(*refs):
        ins, outs = refs[:n], refs[n:2 * n]
        send_sems, recv_sems = refs[2 * n:]
        peer = (lax.axis_index("x"), lax.axis_index("y"), 1 - lax.axis_index("c"))
        copies = [pltpu.make_async_remote_copy(src_ref=ins[k], dst_ref=outs[k], send_sem=send_sems.at[k],
                                               recv_sem=recv_sems.at[k], device_id=peer, device_id_type=MESH)
                  for k in range(n)]
        for cp in copies:
            cp.start()
        for cp in copies:
            cp.wait()

    return pl.pallas_call(
        body, name=name,
        in_specs=[ANY] * n, out_specs=[ANY] * n,
        out_shape=[jax.ShapeDtypeStruct(a.shape, a.dtype) for a in arrs],
        scratch_shapes=[pltpu.SemaphoreType.DMA((n,)), pltpu.SemaphoreType.DMA((n,))],
    )(*arrs)


def allreduce_small(vec, *, name):
    r = vec.shape[0]

    def body(v_ref, o_ref, slots, send_sems, recv_sems):
        x, y, c = lax.axis_index("x"), lax.axis_index("y"), lax.axis_index("c")
        me = 4 * x + 2 * y + c
        slots[me] = v_ref[...]
        copies = []
        for j in range(1, 8):
            px = 1 - x if j & 4 else x
            py = 1 - y if j & 2 else y
            pc = 1 - c if j & 1 else c
            rc = pltpu.make_async_remote_copy(src_ref=v_ref, dst_ref=slots.at[me], send_sem=send_sems.at[j - 1],
                                              recv_sem=recv_sems.at[j - 1], device_id=(px, py, pc), device_id_type=MESH)
            rc.start()
            copies.append(rc)
        for cp in copies:
            cp.wait()
        acc = slots[0]
        for i in range(1, 8):
            acc = acc + slots[i]
        o_ref[...] = acc

    vm = pl.BlockSpec(memory_space=pltpu.VMEM)
    return pl.pallas_call(
        body, name=name, in_specs=[vm], out_specs=vm,
        out_shape=jax.ShapeDtypeStruct((r, 128), F32),
        scratch_shapes=[pltpu.VMEM((8, r, 128), F32), pltpu.SemaphoreType.DMA((7,)), pltpu.SemaphoreType.DMA((7,))],
        compiler_params=pltpu.CompilerParams(vmem_limit_bytes=VMEM_LIMIT),
    )(vec)


WEIGHTS = ("mix_norm_g", "w_in", "conv_dw_w", "conv_dw_b", "conv_ln_g", "conv_ln_b", "conv_pw_w", "conv_pw_b",
           "gdn_conv_w", "gdn_a_log", "gdn_dt_bias", "gdn_norm_g", "w_out", "ffn_norm_g", "w_up", "ffn_conv_w",
           "ffn_conv_b", "w_down", "final_norm_g")
COL_SHARDED = ("w_in", "w_up", "conv_dw_w", "gdn_conv_w", "ffn_conv_w")
ROW_SHARDED = ("conv_pw_w", "w_out", "w_down")
BIG = ("w_in", "conv_pw_w", "w_out", "w_up", "w_down")
SMALL_CONV = ("conv_dw_w", "gdn_conv_w", "ffn_conv_w")


def _full_from_slots(name, part):
    if name in COL_SHARDED:
        r, cs = part.shape[1:]
        return jnp.transpose(part, (1, 0, 2)).reshape(r, 4 * cs)
    rs, c = part.shape[1:]
    return part.reshape(4 * rs, c)


def _slots_from_full(name, full):
    if name in COL_SHARDED:
        r, c = full.shape
        return jnp.transpose(full.reshape(r, 4, c // 4), (1, 0, 2))
    r, c = full.shape
    return full.reshape(4, r // 4, c)


def _pack(parts):
    flat = jnp.concatenate([p.reshape(-1).astype(F32) for p in parts])
    pad = (-flat.shape[0]) % 1024
    return jnp.pad(flat, (0, pad)).reshape(-1, 128)


def _unpack(vec, shapes):
    flat = vec.reshape(-1)
    out, pos = [], 0
    for shp in shapes:
        n = 1
        for d in shp:
            n *= d
        out.append(flat[pos:pos + n].reshape(shp))
        pos += n
    return out


def kernel(x, mix_norm_g, w_in, conv_dw_w, conv_dw_b, conv_ln_g, conv_ln_b, conv_pw_w, conv_pw_b, gdn_conv_w, gdn_a_log, gdn_dt_bias, gdn_norm_g, w_out, ffn_norm_g, w_up, ffn_conv_w, ffn_conv_b, w_down, final_norm_g, loss_target, m_mix_norm_g, m_w_in, m_conv_dw_w, m_conv_dw_b, m_conv_ln_g, m_conv_ln_b, m_conv_pw_w, m_conv_pw_b, m_gdn_conv_w, m_gdn_a_log, m_gdn_dt_bias, m_gdn_norm_g, m_w_out, m_ffn_norm_g, m_w_up, m_ffn_conv_w, m_ffn_conv_b, m_w_down, m_final_norm_g, v_mix_norm_g, v_w_in, v_conv_dw_w, v_conv_dw_b, v_conv_ln_g, v_conv_ln_b, v_conv_pw_w, v_conv_pw_b, v_gdn_conv_w, v_gdn_a_log, v_gdn_dt_bias, v_gdn_norm_g, v_w_out, v_ffn_norm_g, v_w_up, v_ffn_conv_w, v_ffn_conv_b, v_w_down, v_final_norm_g):
    wts = dict(zip(WEIGHTS, (mix_norm_g, w_in, conv_dw_w, conv_dw_b, conv_ln_g, conv_ln_b, conv_pw_w, conv_pw_b,
                             gdn_conv_w, gdn_a_log, gdn_dt_bias, gdn_norm_g, w_out, ffn_norm_g, w_up, ffn_conv_w,
                             ffn_conv_b, w_down, final_norm_g)))
    mom = dict(zip(WEIGHTS, (m_mix_norm_g, m_w_in, m_conv_dw_w, m_conv_dw_b, m_conv_ln_g, m_conv_ln_b, m_conv_pw_w,
                             m_conv_pw_b, m_gdn_conv_w, m_gdn_a_log, m_gdn_dt_bias, m_gdn_norm_g, m_w_out,
                             m_ffn_norm_g, m_w_up, m_ffn_conv_w, m_ffn_conv_b, m_w_down, m_final_norm_g)))
    var = dict(zip(WEIGHTS, (v_mix_norm_g, v_w_in, v_conv_dw_w, v_conv_dw_b, v_conv_ln_g, v_conv_ln_b, v_conv_pw_w,
                             v_conv_pw_b, v_gdn_conv_w, v_gdn_a_log, v_gdn_dt_bias, v_gdn_norm_g, v_w_out,
                             v_ffn_norm_g, v_w_up, v_ffn_conv_w, v_ffn_conv_b, v_w_down, v_final_norm_g)))

    nb, s, d = x.shape
    t = nb * s
    depth = mix_norm_g.shape[0]
    ch = conv_dw_b.shape[1]
    nh = gdn_a_log.shape[1]
    nqk = nh // 2
    kwid, vwid = nqk * HEAD, nh * HEAD
    main = 2 * ch + 2 * kwid + 2 * vwid
    qcol, zcol = 2 * ch, 2 * ch + 2 * kwid + vwid
    dff = ffn_conv_b.shape[1]
    my_xy = 2 * lax.axis_index("x") + lax.axis_index("y")

    shard = {(n, l): wts[n][l].astype(BF16) for n in BIG for l in range(depth)}
    gather = lambda *keys: Exchange([shard[k] for k in keys], gather=True)
    first = xy_exchange([shard["w_in", 0]] + [wts[n] for n in SMALL_CONV], gather=True, name="gather_first")
    full = {("w_in", 0): _full_from_slots("w_in", first[0])}
    for n, g in zip(SMALL_CONV, first[1:]):
        for l in range(depth):
            full[n, l] = _full_from_slots(n, g[:, l])

    def arrived(keys, got):
        for (n, l), g in zip(keys, got):
            full[n, l] = _full_from_slots(n, g)

    xc = x.reshape(t, d)
    saved, lws = [], []
    for l in range(depth):
        w_in_f = full["w_in", l]
        w_main = w_in_f[:, :main]
        w_ba = jnp.pad(w_in_f[:, main:], ((0, 0), (0, HEAD - 2 * nh)))
        gp = (jnp.zeros((8, HEAD), F32).at[0, :nh].set(gdn_a_log[l]).at[1, :nh].set(gdn_dt_bias[l])
              .at[2].set(gdn_norm_g[l]))
        h = rmsnorm_fwd(xc, mix_norm_g[l], name=f"f{l}_norm1")
        keys = [("conv_pw_w", l), ("w_out", l)]
        p, got = matmul(h, w_main, name=f"f{l}_in_main", carry=gather(*keys))
        arrived(keys, got)
        pba = matmul(h, w_ba, name=f"f{l}_in_ba")
        u3 = conf_fwd(p, full["conv_dw_w", l], conv_dw_b[l], conv_ln_g[l], conv_ln_b[l], nb=nb, s=s,
                      name=f"f{l}_conf")
        out_a = matmul(u3, full["conv_pw_w", l], bias=conv_pw_b[l], out_dtype=BF16, name=f"f{l}_pw")
        qkvc = qkvconv_fwd(p, full["gdn_conv_w", l], col0=qcol, nb=nb, s=s, name=f"f{l}_qkvconv")
        keys = [("w_up", l)]
        out_b, states, got = gdn_fwd(qkvc, p, pba, gp, nb=nb, s=s, nh=nh, nqk=nqk, zcol=zcol, name=f"f{l}_gdn",
                                     carry=gather(*keys))
        arrived(keys, got)
        wout_a, wout_b = full["w_out", l][:ch], full["w_out", l][ch:]
        x1 = matmul(out_a, wout_a, res=xc, name=f"f{l}_out_a")
        x1 = matmul(out_b, wout_b, res=x1, name=f"f{l}_out_b")
        h2 = rmsnorm_fwd(x1, ffn_norm_g[l], name=f"f{l}_norm2")
        keys = [("w_down", l)]
        up, got = matmul(h2, full["w_up", l], name=f"f{l}_up", carry=gather(*keys))
        arrived(keys, got)
        act = ffn_act_fwd(up, full["ffn_conv_w", l], ffn_conv_b[l], nb=nb, s=s, name=f"f{l}_act")
        if l + 1 < depth:
            keys = [("w_in", l + 1)]
            x2, got = matmul(act, full["w_down", l], res=x1, name=f"f{l}_down", carry=gather(*keys))
            arrived(keys, got)
        else:
            x2 = matmul(act, full["w_down", l], res=x1, name=f"f{l}_down")
        saved.append(dict(x=xc, h=h, p=p, pba=pba, u3=u3, out_a=out_a, qkvc=qkvc, out_b=out_b, states=states,
                          x1=x1, h2=h2, up=up, act=act))
        lws.append(dict(w_main=w_main, w_ba=w_ba, pw=full["conv_pw_w", l], wout_a=wout_a, wout_b=wout_b,
                        wup=full["w_up", l], wdown=full["w_down", l], dw_w=full["conv_dw_w", l],
                        gconv_w=full["gdn_conv_w", l], fconv_w=full["ffn_conv_w", l], gp=gp))
        xc = x2

    loss_blk, dx, dxb, dgf = loss_head(xc, final_norm_g, loss_target.reshape(t, d), name="loss_head")

    stacks, received = {}, {}
    scatter = lambda *keys: Exchange([stacks[k] for k in keys], gather=False)

    def produced(n, l, grad):
        stacks[n, l] = _slots_from_full(n, grad).astype(BF16)

    def landed(keys, got):
        for k, g in zip(keys, got):
            received[k] = g

    small_grads = {n: [None] * depth for n in WEIGHTS if n not in BIG and n != "final_norm_g"}
    for l in reversed(range(depth)):
        lw, sv = lws[l], saved[l]
        if l + 1 < depth:
            keys = [("w_in", l + 1)]
            dact, got = matmul(dxb, lw["wdown"], tb=True, name=f"b{l}_dact", carry=scatter(*keys))
            landed(keys, got)
        else:
            dact = matmul(dxb, lw["wdown"], tb=True, name=f"b{l}_dact")
        produced("w_down", l, matmul(sv["act"], dxb, ta=True, name=f"b{l}_dwdown"))
        dgate, dupv, fpart = ffn_act_bwd(sv["up"], lw["fconv_w"], ffn_conv_b[l], dact, nb=nb, s=s, name=f"b{l}_act")
        dh2 = matmul(dgate, lw["wup"], tb=True, name=f"b{l}_dh2_gate")
        dh2 = matmul(dupv, lw["wup"], tb=True, b_koff=dff, res=dh2, name=f"b{l}_dh2_up")
        keys = [("w_down", l)]
        dwup_gate, got = matmul(sv["h2"], dgate, ta=True, name=f"b{l}_dwup_gate", carry=scatter(*keys))
        landed(keys, got)
        produced("w_up", l, jnp.concatenate(
            [dwup_gate, matmul(sv["h2"], dupv, ta=True, name=f"b{l}_dwup_up")], axis=1))
        dx1, dx1b, dg2 = rmsnorm_bwd(sv["x1"], ffn_norm_g[l], dh2, dx, name=f"b{l}_norm2")
        fsum = jnp.sum(fpart, axis=0)
        small_grads["ffn_norm_g"][l] = dg2[0]
        small_grads["ffn_conv_w"][l] = fsum[:ffn_conv_w.shape[1]]
        small_grads["ffn_conv_b"][l] = fsum[ffn_conv_w.shape[1]]
        dout_a = matmul(dx1b, lw["wout_a"], tb=True, out_dtype=BF16, name=f"b{l}_dout_a")
        dout_b = matmul(dx1b, lw["wout_b"], tb=True, name=f"b{l}_dout_b")
        produced("w_out", l, jnp.concatenate(
            [matmul(sv["out_a"], dx1b, ta=True, name=f"b{l}_dwout_a"),
             matmul(sv["out_b"], dx1b, ta=True, name=f"b{l}_dwout_b")], axis=0))
        du3 = matmul(dout_a, lw["pw"], tb=True, name=f"b{l}_du3")
        produced("conv_pw_w", l, matmul(sv["u3"], dout_a, ta=True, name=f"b{l}_dwpw"))
        dval, dagate, cw_part, cs_part = conf_bwd(sv["p"], lw["dw_w"], conv_dw_b[l], conv_ln_g[l], conv_ln_b[l],
                                                  du3, dout_a, nb=nb, s=s, name=f"b{l}_conf")
        csum = jnp.sum(cs_part, axis=0)
        small_grads["conv_dw_w"][l] = jnp.sum(cw_part, axis=0)
        small_grads["conv_dw_b"][l] = csum[0]
        small_grads["conv_ln_g"][l] = csum[1]
        small_grads["conv_ln_b"][l] = csum[2]
        small_grads["conv_pw_b"][l] = csum[3]
        keys = [("w_up", l), ("w_out", l), ("conv_pw_w", l)]
        dq, dk, dv, dz, dpba, dgp, got = gdn_bwd(sv["qkvc"], sv["p"], sv["pba"], lw["gp"], sv["states"], dout_b,
                                                 nb=nb, s=s, nh=nh, nqk=nqk, zcol=zcol, name=f"b{l}_gdn",
                                                 carry=scatter(*keys))
        landed(keys, got)
        dqkv, gw_part = qkvconv_bwd(sv["p"], lw["gconv_w"], jnp.concatenate([dq, dk, dv], axis=1),
                                    col0=qcol, nb=nb, s=s, name=f"b{l}_qkvconv")
        gsum = jnp.sum(dgp, axis=0)
        small_grads["gdn_conv_w"][l] = jnp.sum(gw_part, axis=0)[:gdn_conv_w.shape[1]]
        small_grads["gdn_a_log"][l] = gsum[0, :nh]
        small_grads["gdn_dt_bias"][l] = gsum[1, :nh]
        small_grads["gdn_norm_g"][l] = gsum[2]
        dp = jnp.concatenate([dval, dagate, dqkv, dz], axis=1)
        dh = matmul(dp, lw["w_main"], tb=True, name=f"b{l}_dh_main")
        dh = matmul(dpba, lw["w_ba"], tb=True, res=dh, name=f"b{l}_dh_ba")
        dw_main = matmul(sv["h"], dp, ta=True, name=f"b{l}_dwin_main")
        dw_ba = matmul(sv["h"], dpba, ta=True, name=f"b{l}_dwin_ba")
        produced("w_in", l, jnp.concatenate([dw_main, dw_ba[:, :2 * nh]], axis=1))
        dx, dxb, dg1 = rmsnorm_bwd(sv["x"], mix_norm_g[l], dh, dx1, name=f"b{l}_norm1")
        small_grads["mix_norm_g"][l] = dg1[0]
    received["w_in", 0] = xy_exchange([stacks["w_in", 0]], gather=False, name="scatter_last")[0]

    partial = [sum_slots([received[n, l] for l in range(depth)], name=f"sum_{n}").reshape(wts[n].shape) for n in BIG]
    other = sibling_swap(partial, name="swap_partials")

    grads, deltas, new_m, new_v = {}, {}, {}, {}
    for n, mine, theirs in zip(BIG, partial, other):
        grads[n], deltas[n], new_m[n], new_v[n] = adamw([mine, theirs], wts[n], mom[n], var[n], name=f"adamw_{n}")

    small_names = [n for n in WEIGHTS if n not in BIG]
    small_full = [jnp.stack(small_grads[n]) if n != "final_norm_g" else dgf[0] for n in small_names]
    packed = _pack(small_full + [loss_blk[0, :1]])
    reduced = allreduce_small(packed, name="allreduce_small")
    parts = _unpack(reduced, [a.shape for a in small_full] + [(1,)])
    loss = parts[-1][0]
    for n, g in zip(small_names, parts[:-1]):
        if n in SMALL_CONV:
            wid = wts[n].shape[-1]
            g = lax.dynamic_slice_in_dim(g, my_xy * wid, wid, axis=g.ndim - 1)
        grads[n], deltas[n], new_m[n], new_v[n] = adamw([g], wts[n], mom[n], var[n], name=f"adamw_{n}")

    return (loss, dx.reshape(nb, s, d), *[grads[n] for n in WEIGHTS], *[deltas[n] for n in WEIGHTS],
            *[new_m[n] for n in WEIGHTS], *[new_v[n] for n in WEIGHTS])
```

```python
import functools

import jax
import jax.numpy as jnp
from jax import lax
from jax.experimental import pallas as pl
from jax.experimental.pallas import tpu as pltpu

F32 = jnp.float32
BF16 = jnp.bfloat16
EPS = 1e-6
CHUNK = 64
HEAD = 128
HIGHEST = lax.Precision.HIGHEST
VMEM_LIMIT = 56 * 1024 * 1024
MM_VMEM_BUDGET = 48 * 1024 * 1024


def _pick(dim, cands):
    for c in cands:
        if dim % c == 0:
            return c
    return dim


def _cparams(sem):
    return pltpu.CompilerParams(dimension_semantics=sem, vmem_limit_bytes=VMEM_LIMIT)


MESH = pl.DeviceIdType.MESH
ANY = pl.BlockSpec(memory_space=pl.ANY)


def _xy_peers():
    x, y = lax.axis_index("x"), lax.axis_index("y")
    peers = []
    for fx, fy in ((0, 1), (1, 0), (1, 1)):
        px = 1 - x if fx else x
        py = 1 - y if fy else y
        peers.append((2 * px + py, px, py))
    return 2 * x + y, peers


class Exchange:
    def __init__(self, arrs, gather):
        self.arrs, self.gather, self.n = list(arrs), gather, len(arrs)
        self.out_shape = [jax.ShapeDtypeStruct((4,) + (a.shape if gather else a.shape[1:]), a.dtype) for a in arrs]
        self.scratch = [pltpu.SemaphoreType.DMA((3 * self.n,)), pltpu.SemaphoreType.DMA((3 * self.n,)),
                        pltpu.SemaphoreType.DMA((self.n,))]

    def copies(self, ins, outs, send_sems, recv_sems, local_sems):
        me, peers = _xy_peers()
        c = lax.axis_index("c")
        out = []
        for k in range(self.n):
            out.append(pltpu.make_async_copy(ins[k] if self.gather else ins[k].at[me], outs[k].at[me],
                                             local_sems.at[k]))
            for j, (slot, px, py) in enumerate(peers):
                out.append(pltpu.make_async_remote_copy(
                    src_ref=ins[k] if self.gather else ins[k].at[slot], dst_ref=outs[k].at[me],
                    send_sem=send_sems.at[3 * k + j], recv_sem=recv_sems.at[3 * k + j],
                    device_id=(px, py, c), device_id_type=MESH))
        return out


def carried_call(body, *, name, grid, in_specs, out_specs, out_shape, scratch_shapes, semantics, args, carry=None):
    n_in, n_out = len(in_specs), len(out_specs)
    if carry is None:
        outs = pl.pallas_call(body, name=name, grid=grid, in_specs=in_specs, out_specs=out_specs, out_shape=out_shape,
                              scratch_shapes=scratch_shapes, compiler_params=_cparams(semantics))(*args)
        return list(outs), []
    n = carry.n
    n_scr = len(scratch_shapes)

    def wrapped(*refs):
        ins, cin = refs[:n_in], refs[n_in:n_in + n]
        outs, cout = refs[n_in + n:n_in + n + n_out], refs[n_in + n + n_out:n_in + 2 * n + n_out]
        scratch = refs[n_in + 2 * n + n_out:n_in + 2 * n + n_out + n_scr]
        sems = refs[n_in + 2 * n + n_out + n_scr:]
        ids = [pl.program_id(i) for i in range(len(grid))]
        first = functools.reduce(jnp.logical_and, [i == 0 for i in ids])
        last = functools.reduce(jnp.logical_and, [i == g - 1 for i, g in zip(ids, grid)])

        @pl.when(first)
        def _():
            for cp in carry.copies(cin, cout, *sems):
                cp.start()

        body(*ins, *outs, *scratch)

        @pl.when(last)
        def _():
            for cp in carry.copies(cin, cout, *sems):
                cp.wait()

    res = pl.pallas_call(
        wrapped, name=name, grid=grid, in_specs=list(in_specs) + [ANY] * n, out_specs=list(out_specs) + [ANY] * n,
        out_shape=list(out_shape) + carry.out_shape, scratch_shapes=list(scratch_shapes) + carry.scratch,
        compiler_params=_cparams(tuple("arbitrary" for _ in grid)),
    )(*args, *carry.arrs)
    return list(res[:n_out]), list(res[n_out:])


def xy_exchange(arrs, *, gather, name):
    ex = Exchange(arrs, gather)

    def body(*refs):
        cps = ex.copies(refs[:ex.n], refs[ex.n:2 * ex.n], *refs[2 * ex.n:])
        for cp in cps:
            cp.start()
        for cp in cps:
            cp.wait()

    return pl.pallas_call(body, name=name, in_specs=[ANY] * ex.n, out_specs=[ANY] * ex.n, out_shape=ex.out_shape,
                          scratch_shapes=ex.scratch)(*arrs)


def matmul(a, b, *, ta=False, tb=False, out_dtype=F32, res=None, bias=None, b_koff=0, name, carry=None):
    if ta:
        kdim, m = a.shape
    else:
        m, kdim = a.shape
    if tb:
        n, kb = b.shape
    else:
        kb, n = b.shape
    assert kb >= kdim + b_koff, (a.shape, b.shape, ta, tb)
    tn = _pick(n, (1024, 512, 256, 128))
    out_bytes = jnp.dtype(out_dtype).itemsize

    def vmem_bytes(tm_, tk_):
        blocks = a.dtype.itemsize * tm_ * tk_ + b.dtype.itemsize * tk_ * tn + out_bytes * tm_ * tn
        blocks += 4 * tm_ * tn if res is not None else 0
        temps = 4 * tm_ * tn + (2 * tk_ * tn if tb else 0)
        return 2 * blocks + temps + (4 * tm_ * tn if tk_ < kdim else 0)

    def longest_k(tm_):
        return next(c for c in (kdim, 4096, 2816, 2560, 2048, 1024, 512, 256, 128)
                    if kdim % c == 0 and b_koff % c == 0 and c % 128 == 0
                    and (vmem_bytes(tm_, c) <= MM_VMEM_BUDGET or c == 128))

    tall = [c for c in (1024, 512) if m % c == 0] or [_pick(m, (256, 128))]
    tm = max(tall, key=lambda c: (longest_k(c), c))
    tk = longest_k(tm)
    nk = kdim // tk
    ko = b_koff // tk
    dims = (((0 if ta else 1,), (1 if tb else 0,)), ((), ()))

    def body(*refs):
        a_ref, b_ref = refs[0], refs[1]
        pos = 2
        bias_ref = res_ref = None
        if bias is not None:
            bias_ref = refs[pos]
            pos += 1
        if res is not None:
            res_ref = refs[pos]
            pos += 1
        o_ref = refs[pos]
        part = lax.dot_general(a_ref[...].astype(BF16), b_ref[...].astype(BF16), dims, preferred_element_type=F32)

        def finish(r):
            if bias_ref is not None:
                r = r + bias_ref[...]
            if res_ref is not None:
                r = r + res_ref[...]
            o_ref[...] = r.astype(o_ref.dtype)

        if nk == 1:
            finish(part)
            return
        acc_ref = refs[pos + 1]
        k = pl.program_id(2)

        @pl.when(k == 0)
        def _():
            acc_ref[...] = part

        @pl.when((k > 0) & (k < nk - 1))
        def _():
            acc_ref[...] += part

        @pl.when(k == nk - 1)
        def _():
            finish(acc_ref[...] + part)

    a_spec = pl.BlockSpec((tk, tm), lambda i, j, k: (k, i)) if ta else pl.BlockSpec((tm, tk), lambda i, j, k: (i, k))
    b_spec = (pl.BlockSpec((tn, tk), lambda i, j, k: (j, k + ko)) if tb
              else pl.BlockSpec((tk, tn), lambda i, j, k: (k + ko, j)))
    in_specs = [a_spec, b_spec]
    args = [a, b]
    if bias is not None:
        in_specs.append(pl.BlockSpec((1, tn), lambda i, j, k: (0, j)))
        args.append(bias.reshape(1, n).astype(F32))
    if res is not None:
        in_specs.append(pl.BlockSpec((tm, tn), lambda i, j, k: (i, j)))
        args.append(res)
    outs, got = carried_call(
        body, name=name, grid=(m // tm, n // tn, nk), in_specs=in_specs,
        out_specs=[pl.BlockSpec((tm, tn), lambda i, j, k: (i, j))],
        out_shape=[jax.ShapeDtypeStruct((m, n), out_dtype)],
        scratch_shapes=[pltpu.VMEM((tm, tn), F32)] if nk > 1 else [],
        semantics=("parallel", "parallel", "arbitrary"), args=args, carry=carry)
    return outs[0] if carry is None else (outs[0], got)


def _sigmoid(x):
    return 1.0 / (1.0 + jnp.exp(-x))


def _softplus(x):
    return jnp.maximum(x, 0.0) + jnp.log(1.0 + jnp.exp(-jnp.abs(x)))


def _shift_back(x, s, row):
    if s == 0:
        return x
    return jnp.where(row >= s, pltpu.roll(x, s, 0), 0.0)


def _shift_fwd(x, s, row):
    if s == 0:
        return x
    n = x.shape[0]
    return jnp.where(row < n - s, pltpu.roll(x, n - s, 0), 0.0)


def _conv_fwd(x, w_ref, kw, row):
    acc = x * w_ref[pl.ds(kw - 1, 1), :]
    for s in range(1, kw):
        acc = acc + _shift_back(x, s, row) * w_ref[pl.ds(kw - 1 - s, 1), :]
    return acc


def _conv_bwd(x, dy, w_ref, kw, row):
    dx = dy * w_ref[pl.ds(kw - 1, 1), :]
    dw = [None] * kw
    dw[kw - 1] = jnp.sum(dy * x, axis=0, keepdims=True)
    for s in range(1, kw):
        dx = dx + _shift_fwd(dy, s, row) * w_ref[pl.ds(kw - 1 - s, 1), :]
        dw[kw - 1 - s] = jnp.sum(dy * _shift_back(x, s, row), axis=0, keepdims=True)
    return dx, dw


def _rows_to_block(rows, nrows, width):
    rid = lax.broadcasted_iota(jnp.int32, (nrows, width), 0)
    out = jnp.zeros((nrows, width), F32)
    for i, r in enumerate(rows):
        out = jnp.where(rid == i, r, out)
    return out


def rmsnorm_fwd(x, g, *, name):
    t, d = x.shape
    tm = _pick(t, (256, 128))

    def body(x_ref, g_ref, o_ref):
        xv = x_ref[...]
        r = lax.rsqrt(jnp.mean(xv * xv, axis=-1, keepdims=True) + EPS)
        o_ref[...] = (xv * r * g_ref[...]).astype(o_ref.dtype)

    return pl.pallas_call(
        body, name=name, grid=(t // tm,),
        in_specs=[pl.BlockSpec((tm, d), lambda i: (i, 0)), pl.BlockSpec((1, d), lambda i: (0, 0))],
        out_specs=pl.BlockSpec((tm, d), lambda i: (i, 0)),
        out_shape=jax.ShapeDtypeStruct((t, d), BF16),
        compiler_params=_cparams(("parallel",)),
    )(x, g.reshape(1, d))


def rmsnorm_bwd(x, g, dh, dres, *, name):
    t, d = x.shape
    tm = _pick(t, (256, 128))

    def body(x_ref, g_ref, dh_ref, dres_ref, dx_ref, dxb_ref, dg_ref):
        xv = x_ref[...]
        r = lax.rsqrt(jnp.mean(xv * xv, axis=-1, keepdims=True) + EPS)
        xh = xv * r
        dy = dh_ref[...]
        dxh = dy * g_ref[...]
        dx = dres_ref[...] + r * (dxh - xh * jnp.mean(dxh * xh, axis=-1, keepdims=True))
        dx_ref[...] = dx
        dxb_ref[...] = dx.astype(BF16)

        @pl.when(pl.program_id(0) == 0)
        def _():
            dg_ref[...] = jnp.zeros_like(dg_ref)

        dg_ref[...] += jnp.sum(dy * xh, axis=0, keepdims=True)

    row = pl.BlockSpec((tm, d), lambda i: (i, 0))
    vec = pl.BlockSpec((1, d), lambda i: (0, 0))
    return pl.pallas_call(
        body, name=name, grid=(t // tm,),
        in_specs=[row, vec, row, row],
        out_specs=[row, row, vec],
        out_shape=[jax.ShapeDtypeStruct((t, d), F32), jax.ShapeDtypeStruct((t, d), BF16),
                   jax.ShapeDtypeStruct((1, d), F32)],
        compiler_params=_cparams(("arbitrary",)),
    )(x, g.reshape(1, d), dh, dres)


def loss_head(x, g, target, *, name):
    t, d = x.shape
    tm = _pick(t, (256, 128))

    def body(x_ref, g_ref, tg_ref, loss_ref, dx_ref, dxb_ref, dg_ref):
        xv = x_ref[...]
        r = lax.rsqrt(jnp.mean(xv * xv, axis=-1, keepdims=True) + EPS)
        xh = xv * r
        err = xh * g_ref[...] - tg_ref[...]
        dy = err * (1.0 / d)
        dxh = dy * g_ref[...]
        dx = r * (dxh - xh * jnp.mean(dxh * xh, axis=-1, keepdims=True))
        dx_ref[...] = dx
        dxb_ref[...] = dx.astype(BF16)

        @pl.when(pl.program_id(0) == 0)
        def _():
            dg_ref[...] = jnp.zeros_like(dg_ref)
            loss_ref[...] = jnp.zeros_like(loss_ref)

        dg_ref[...] += jnp.sum(dy * xh, axis=0, keepdims=True)
        part = jnp.sum(jnp.sum(err * err, axis=-1, keepdims=True), axis=0, keepdims=True) * (0.5 / d)
        loss_ref[...] += jnp.broadcast_to(part, loss_ref.shape)

    row = pl.BlockSpec((tm, d), lambda i: (i, 0))
    vec = pl.BlockSpec((1, d), lambda i: (0, 0))
    return pl.pallas_call(
        body, name=name, grid=(t // tm,),
        in_specs=[row, vec, row],
        out_specs=[pl.BlockSpec((8, 128), lambda i: (0, 0)), row, row, vec],
        out_shape=[jax.ShapeDtypeStruct((8, 128), F32), jax.ShapeDtypeStruct((t, d), F32),
                   jax.ShapeDtypeStruct((t, d), BF16), jax.ShapeDtypeStruct((1, d), F32)],
        compiler_params=_cparams(("arbitrary",)),
    )(x, g.reshape(1, d), target)


def _conf_forward_parts(val, gate, w_ref, b, lg, lb, kw, row):
    sg = _sigmoid(gate)
    u0 = val * sg
    u1 = _conv_fwd(u0, w_ref, kw, row) + b
    mu = jnp.mean(u1, axis=-1, keepdims=True)
    xc = u1 - mu
    rs = lax.rsqrt(jnp.mean(xc * xc, axis=-1, keepdims=True) + EPS)
    xh = xc * rs
    u2 = xh * lg + lb
    s2 = _sigmoid(u2)
    return sg, u0, rs, xh, u2, s2


def conf_fwd(p, dw_w, dw_b, ln_g, ln_b, *, nb, s, name):
    kw, ch = dw_w.shape
    ng = ch // HEAD

    def body(val_ref, gate_ref, w_ref, b_ref, lg_ref, lb_ref, o_ref):
        row = lax.broadcasted_iota(jnp.int32, (s, HEAD), 0)
        _, _, _, _, u2, s2 = _conf_forward_parts(val_ref[...], gate_ref[...], w_ref, b_ref[...], lg_ref[...],
                                                 lb_ref[...], kw, row)
        o_ref[...] = (u2 * s2).astype(o_ref.dtype)

    vec = pl.BlockSpec((1, HEAD), lambda b, g: (0, g))
    return pl.pallas_call(
        body, name=name, grid=(nb, ng),
        in_specs=[pl.BlockSpec((s, HEAD), lambda b, g: (b, g)), pl.BlockSpec((s, HEAD), lambda b, g: (b, ng + g)),
                  pl.BlockSpec((kw, HEAD), lambda b, g: (0, g)), vec, vec, vec],
        out_specs=pl.BlockSpec((s, HEAD), lambda b, g: (b, g)),
        out_shape=jax.ShapeDtypeStruct((nb * s, ch), BF16),
        compiler_params=_cparams(("parallel", "parallel")),
    )(p, p, dw_w, dw_b.reshape(1, ch), ln_g.reshape(1, ch), ln_b.reshape(1, ch))


def conf_bwd(p, dw_w, dw_b, ln_g, ln_b, du3, dout_a, *, nb, s, name):
    kw, ch = dw_w.shape
    ng = ch // HEAD

    def body(val_ref, gate_ref, w_ref, b_ref, lg_ref, lb_ref, du3_ref, doa_ref, dval_ref, dgate_ref, dw_out, sm_out):
        row = lax.broadcasted_iota(jnp.int32, (s, HEAD), 0)
        val = val_ref[...]
        sg, u0, rs, xh, u2, s2 = _conf_forward_parts(val, gate_ref[...], w_ref, b_ref[...], lg_ref[...],
                                                     lb_ref[...], kw, row)
        du2 = du3_ref[...] * (s2 * (1.0 + u2 * (1.0 - s2)))
        dlg = jnp.sum(du2 * xh, axis=0, keepdims=True)
        dlb = jnp.sum(du2, axis=0, keepdims=True)
        dxh = du2 * lg_ref[...]
        du1 = rs * (dxh - jnp.mean(dxh, axis=-1, keepdims=True) - xh * jnp.mean(dxh * xh, axis=-1, keepdims=True))
        ddb = jnp.sum(du1, axis=0, keepdims=True)
        du0, dw = _conv_bwd(u0, du1, w_ref, kw, row)
        dval_ref[...] = (du0 * sg).astype(dval_ref.dtype)
        dgate_ref[...] = (du0 * val * sg * (1.0 - sg)).astype(dgate_ref.dtype)
        for k in range(kw):
            dw_out[0, pl.ds(k, 1), :] = dw[k]
        dpb = jnp.sum(doa_ref[...].astype(F32), axis=0, keepdims=True)
        sm_out[0] = _rows_to_block([ddb, dlg, dlb, dpb], 8, HEAD)

    vec = pl.BlockSpec((1, HEAD), lambda b, g: (0, g))
    blk = pl.BlockSpec((s, HEAD), lambda b, g: (b, g))
    return pl.pallas_call(
        body, name=name, grid=(nb, ng),
        in_specs=[blk, pl.BlockSpec((s, HEAD), lambda b, g: (b, ng + g)),
                  pl.BlockSpec((kw, HEAD), lambda b, g: (0, g)), vec, vec, vec, blk, blk],
        out_specs=[blk, blk, pl.BlockSpec((1, kw, HEAD), lambda b, g: (b, 0, g)),
                   pl.BlockSpec((1, 8, HEAD), lambda b, g: (b, 0, g))],
        out_shape=[jax.ShapeDtypeStruct((nb * s, ch), BF16), jax.ShapeDtypeStruct((nb * s, ch), BF16),
                   jax.ShapeDtypeStruct((nb, kw, ch), F32), jax.ShapeDtypeStruct((nb, 8, ch), F32)],
        compiler_params=_cparams(("parallel", "parallel")),
    )(p, p, dw_w, dw_b.reshape(1, ch), ln_g.reshape(1, ch), ln_b.reshape(1, ch), du3, dout_a)


def qkvconv_fwd(p, w, *, col0, nb, s, name):
    kw, ch = w.shape
    nblk = ch // HEAD
    c0 = col0 // HEAD

    def body(x_ref, w_ref, o_ref):
        row = lax.broadcasted_iota(jnp.int32, (s, HEAD), 0)
        c = _conv_fwd(x_ref[...], w_ref, kw, row)
        o_ref[...] = c * _sigmoid(c)

    return pl.pallas_call(
        body, name=name, grid=(nb, nblk),
        in_specs=[pl.BlockSpec((s, HEAD), lambda b, j: (b, c0 + j)), pl.BlockSpec((kw, HEAD), lambda b, j: (0, j))],
        out_specs=pl.BlockSpec((s, HEAD), lambda b, j: (b, j)),
        out_shape=jax.ShapeDtypeStruct((nb * s, ch), F32),
        compiler_params=_cparams(("parallel", "parallel")),
    )(p, w)


def qkvconv_bwd(p, w, dy, *, col0, nb, s, name):
    kw, ch = w.shape
    nblk = ch // HEAD
    c0 = col0 // HEAD

    def body(x_ref, w_ref, dy_ref, dx_ref, dw_out):
        row = lax.broadcasted_iota(jnp.int32, (s, HEAD), 0)
        xv = x_ref[...]
        c = _conv_fwd(xv, w_ref, kw, row)
        sc = _sigmoid(c)
        dc = dy_ref[...] * (sc * (1.0 + c * (1.0 - sc)))
        dx, dw = _conv_bwd(xv, dc, w_ref, kw, row)
        dx_ref[...] = dx.astype(dx_ref.dtype)
        dw_out[0] = _rows_to_block(dw, 8, HEAD)

    blk = pl.BlockSpec((s, HEAD), lambda b, j: (b, j))
    return pl.pallas_call(
        body, name=name, grid=(nb, nblk),
        in_specs=[pl.BlockSpec((s, HEAD), lambda b, j: (b, c0 + j)), pl.BlockSpec((kw, HEAD), lambda b, j: (0, j)), blk],
        out_specs=[blk, pl.BlockSpec((1, 8, HEAD), lambda b, j: (b, 0, j))],
        out_shape=[jax.ShapeDtypeStruct((nb * s, ch), BF16), jax.ShapeDtypeStruct((nb, 8, ch), F32)],
        compiler_params=_cparams(("parallel", "parallel")),
    )(p, w, dy)


def ffn_act_fwd(up, w, b, *, nb, s, name):
    kw, dff = w.shape
    cb = _pick(dff, (256, 128))
    nblk = dff // cb

    def body(g_ref, u_ref, w_ref, b_ref, o_ref):
        row = lax.broadcasted_iota(jnp.int32, (s, cb), 0)
        gc = _conv_fwd(g_ref[...], w_ref, kw, row) + b_ref[...]
        o_ref[...] = (gc * _sigmoid(gc) * u_ref[...]).astype(o_ref.dtype)

    return pl.pallas_call(
        body, name=name, grid=(nb, nblk),
        in_specs=[pl.BlockSpec((s, cb), lambda i, j: (i, j)), pl.BlockSpec((s, cb), lambda i, j: (i, nblk + j)),
                  pl.BlockSpec((kw, cb), lambda i, j: (0, j)), pl.BlockSpec((1, cb), lambda i, j: (0, j))],
        out_specs=pl.BlockSpec((s, cb), lambda i, j: (i, j)),
        out_shape=jax.ShapeDtypeStruct((nb * s, dff), BF16),
        compiler_params=_cparams(("parallel", "parallel")),
    )(up, up, w, b.reshape(1, dff))


def ffn_act_bwd(up, w, b, dact, *, nb, s, name):
    kw, dff = w.shape
    cb = _pick(dff, (256, 128))
    nblk = dff // cb

    def body(g_ref, u_ref, w_ref, b_ref, da_ref, dg_ref, du_ref, sm_out):
        row = lax.broadcasted_iota(jnp.int32, (s, cb), 0)
        gv = g_ref[...]
        gc = _conv_fwd(gv, w_ref, kw, row) + b_ref[...]
        sc = _sigmoid(gc)
        da = da_ref[...].astype(F32)
        du_ref[...] = (da * gc * sc).astype(du_ref.dtype)
        dgc = da * u_ref[...] * (sc * (1.0 + gc * (1.0 - sc)))
        dgate, dw = _conv_bwd(gv, dgc, w_ref, kw, row)
        dg_ref[...] = dgate.astype(dg_ref.dtype)
        sm_out[0] = _rows_to_block(dw + [jnp.sum(dgc, axis=0, keepdims=True)], 8, cb)

    blk = pl.BlockSpec((s, cb), lambda i, j: (i, j))
    return pl.pallas_call(
        body, name=name, grid=(nb, nblk),
        in_specs=[blk, pl.BlockSpec((s, cb), lambda i, j: (i, nblk + j)),
                  pl.BlockSpec((kw, cb), lambda i, j: (0, j)), pl.BlockSpec((1, cb), lambda i, j: (0, j)), blk],
        out_specs=[blk, blk, pl.BlockSpec((1, 8, cb), lambda i, j: (i, 0, j))],
        out_shape=[jax.ShapeDtypeStruct((nb * s, dff), BF16), jax.ShapeDtypeStruct((nb * s, dff), BF16),
                   jax.ShapeDtypeStruct((nb, 8, dff), F32)],
        compiler_params=_cparams(("parallel", "parallel")),
    )(up, up, w, b.reshape(1, dff), dact)


def _dot(a, b, dims, prec=None):
    return lax.dot_general(a, b, (dims, ((), ())), precision=prec, preferred_element_type=F32)


def _mm_nn(a, b):
    return _dot(a.astype(BF16), b.astype(BF16), ((1,), (0,)))


def _mm_nt(a, b):
    return _dot(a.astype(BF16), b.astype(BF16), ((1,), (1,)))


def _mm_tn(a, b):
    return _dot(a.astype(BF16), b.astype(BF16), ((0,), (0,)))


def _mm_hi(a, b):
    return _dot(a, b, ((1,), (0,)), HIGHEST)


def _mm_inv(a, b):
    return _mm_nn(a, b)


def _split3(x):
    hi = x.astype(BF16)
    rest = x - hi.astype(F32)
    mid = rest.astype(BF16)
    return hi, mid, (rest - mid.astype(F32)).astype(BF16)


def _mask_dot(mask, x, dims, mask_first):
    if mask_first:
        return sum(_dot(mask, p, dims) for p in _split3(x))
    return sum(_dot(p, mask, dims) for p in _split3(x))


@jax.custom_vjp
def _mask_nn(mask, x):
    return _mask_dot(mask, x, ((1,), (0,)), True)


def _mask_nn_fwd(mask, x):
    return _mask_nn(mask, x), mask


def _mask_nn_bwd(mask, ct):
    return jnp.zeros_like(mask), _mask_dot(mask, ct, ((0,), (0,)), True)


_mask_nn.defvjp(_mask_nn_fwd, _mask_nn_bwd)


@jax.custom_vjp
def _mask_tn(x, mask):
    return _mask_dot(mask, x, ((0,), (0,)), False)


def _mask_tn_fwd(x, mask):
    return _mask_tn(x, mask), mask


def _mask_tn_bwd(mask, ct):
    return _mask_dot(mask, ct, ((1,), (1,)), True), jnp.zeros_like(mask)


_mask_tn.defvjp(_mask_tn_fwd, _mask_tn_bwd)


GDN_ROWS = 256


def _gdn_chunk(qc, kc, vc, zc, braw, araw, alog, dtb, ng, state):
    r = qc.shape[0]
    ri = lax.broadcasted_iota(jnp.int32, (r, r), 0)
    ci = lax.broadcasted_iota(jnp.int32, (r, r), 1)
    same = (ri // CHUNK) == (ci // CHUNK)
    causal = same & (ri >= ci)
    strict = same & (ri > ci)
    eye = (ri == ci).astype(F32)
    row_chunk = lax.broadcasted_iota(jnp.int32, (r, HEAD), 0) // CHUNK

    q = qc * lax.rsqrt(jnp.sum(qc * qc, axis=-1, keepdims=True) + EPS) * (HEAD ** -0.5)
    k = kc * lax.rsqrt(jnp.sum(kc * kc, axis=-1, keepdims=True) + EPS)
    beta = _sigmoid(braw)
    g = -jnp.exp(alog) * _softplus(araw + dtb)

    g_w = jnp.broadcast_to(g, (r, HEAD))
    widen = lambda t: jnp.concatenate([t] * (r // HEAD), axis=1)
    gw = _mask_nn(causal.astype(BF16), g_w)
    gi = widen(gw)
    gj = _mask_tn(widen(g_w), (same & (ri <= ci)).astype(BF16))
    decay = jnp.where(causal, jnp.exp(jnp.where(causal, gi - gj, 0.0)), 0.0)

    kb = k * beta
    vb = vc * beta
    lmat = jnp.where(strict, _mm_nt(kb, k) * decay, 0.0)
    x = -lmat
    ainv = eye + x
    p = 1
    while 2 * p < CHUNK:
        x = _mm_inv(x, x)
        ainv = _mm_inv(ainv, eye + x)
        p *= 2
    u = _mm_inv(ainv, vb)
    w = _mm_inv(ainv, kb * jnp.exp(gw))
    qk = jnp.where(causal, _mm_nt(q, k) * decay, 0.0)
    qg = q * jnp.exp(gw)

    o = jnp.zeros((r, HEAD), F32)
    for c in range(r // CHUNK):
        in_c = row_chunk == c
        glast = jnp.sum(jnp.where(in_c, g_w, 0.0), axis=0, keepdims=True)
        v_new = jnp.where(in_c, u - _mm_nn(w, state), 0.0)
        o = o + jnp.where(in_c, _mm_nn(qg, state), 0.0) + _mm_nn(qk, v_new)
        k_dec = jnp.where(in_c, k * jnp.exp(jnp.where(in_c, glast - gw, 0.0)), 0.0)
        state = state * jnp.exp(glast) + _mm_tn(k_dec, v_new)

    o = o * lax.rsqrt(jnp.mean(o * o, axis=-1, keepdims=True) + EPS) * ng
    o = o * (zc * _sigmoid(zc))
    return o, state


GDN_HEADS_PER_STEP = 2


def _gdn_specs(s, nh, nqk, zcol, n_chunks, gh):
    rep = nh // nqk
    qw, vw = (gh // rep) * HEAD, gh * HEAD
    assert gh % rep == 0 and nh % gh == 0 and (nqk * HEAD) % qw == 0 and (2 * nqk * HEAD) % vw == 0 and zcol % vw == 0
    k0, v0, z0 = (nqk * HEAD) // qw, (2 * nqk * HEAD) // vw, zcol // vw
    return dict(
        q=pl.BlockSpec((s, qw), lambda b, j: (b, j)),
        k=pl.BlockSpec((s, qw), lambda b, j: (b, k0 + j)),
        v=pl.BlockSpec((s, vw), lambda b, j: (b, v0 + j)),
        z=pl.BlockSpec((s, vw), lambda b, j: (b, z0 + j)),
        ba=pl.BlockSpec((s, HEAD), lambda b, j: (b, 0)),
        gp=pl.BlockSpec((8, HEAD), lambda b, j: (0, 0)),
        qk_out=pl.BlockSpec((s, qw), lambda b, j: (b, j)),
        head=pl.BlockSpec((s, vw), lambda b, j: (b, j)),
        st=pl.BlockSpec((1, gh, n_chunks, HEAD, HEAD), lambda b, j: (b, j, 0, 0, 0)),
    )


def _gdn_scalars(gp_ref, h):
    lane = lax.broadcasted_iota(jnp.int32, (1, HEAD), 1)
    sel = (lane == h).astype(F32)
    alog = jnp.sum(gp_ref[pl.ds(0, 1), :] * sel, axis=-1, keepdims=True)
    dtb = jnp.sum(gp_ref[pl.ds(1, 1), :] * sel, axis=-1, keepdims=True)
    return alog, dtb, sel


def _lanes(i):
    return pl.ds(i * HEAD, HEAD)


def gdn_fwd(qkvc, p, pba, gp, *, nb, s, nh, nqk, zcol, name, carry=None):
    rb = min(GDN_ROWS, s)
    n_chunks = s // rb
    gh = GDN_HEADS_PER_STEP
    rep = nh // nqk
    sp = _gdn_specs(s, nh, nqk, zcol, n_chunks, gh)

    def body(q_ref, k_ref, v_ref, z_ref, ba_ref, gp_ref, o_ref, st_ref):
        h0 = pl.program_id(1) * gh
        ng = gp_ref[pl.ds(2, 1), :]
        lane = lax.broadcasted_iota(jnp.int32, (rb, HEAD), 1)
        heads = []
        for i in range(gh):
            alog, dtb, _ = _gdn_scalars(gp_ref, h0 + i)
            heads.append((alog, dtb, (lane == h0 + i).astype(F32), (lane == nh + h0 + i).astype(F32)))

        def step(n, states):
            rows = pl.ds(pl.multiple_of(n * rb, rb), rb)
            ba = ba_ref[rows, :]
            qs = [q_ref[rows, _lanes(j)] for j in range(gh // rep)]
            ks = [k_ref[rows, _lanes(j)] for j in range(gh // rep)]
            vs = [v_ref[rows, _lanes(i)] for i in range(gh)]
            zs = [z_ref[rows, _lanes(i)] for i in range(gh)]
            outs, new_states = [], []
            for i, (alog, dtb, sel_b, sel_a) in enumerate(heads):
                braw = jnp.sum(ba * sel_b, axis=-1, keepdims=True)
                araw = jnp.sum(ba * sel_a, axis=-1, keepdims=True)
                o, new_state = _gdn_chunk(qs[i // rep], ks[i // rep], vs[i], zs[i], braw, araw, alog, dtb, ng,
                                          states[i])
                outs.append(o)
                new_states.append(new_state)
            for i in range(gh):
                st_ref[0, i, n] = states[i]
                o_ref[rows, _lanes(i)] = outs[i].astype(o_ref.dtype)
            return tuple(new_states)

        lax.fori_loop(0, n_chunks, step, tuple(jnp.zeros((HEAD, HEAD), F32) for _ in range(gh)))

    outs, got = carried_call(
        body, name=name, grid=(nb, nh // gh),
        in_specs=[sp["q"], sp["k"], sp["v"], sp["z"], sp["ba"], sp["gp"]],
        out_specs=[sp["head"], sp["st"]],
        out_shape=[jax.ShapeDtypeStruct((nb * s, nh * HEAD), BF16),
                   jax.ShapeDtypeStruct((nb, nh, n_chunks, HEAD, HEAD), F32)],
        scratch_shapes=[], semantics=("parallel", "parallel"), args=(qkvc, qkvc, qkvc, p, pba, gp), carry=carry)
    return outs + [got]


def gdn_bwd(qkvc, p, pba, gp, states, dout, *, nb, s, nh, nqk, zcol, name, carry=None):
    rb = min(GDN_ROWS, s)
    n_chunks = s // rb
    gh = GDN_HEADS_PER_STEP
    rep = nh // nqk
    sp = _gdn_specs(s, nh, nqk, zcol, n_chunks, gh)

    def body(q_ref, k_ref, v_ref, z_ref, ba_ref, gp_ref, st_ref, do_ref,
             dq_ref, dk_ref, dv_ref, dz_ref, dba_ref, dgp_ref):
        h0 = pl.program_id(1) * gh
        ng = gp_ref[pl.ds(2, 1), :]
        lane = lax.broadcasted_iota(jnp.int32, (rb, HEAD), 1)
        heads = []
        for i in range(gh):
            alog, dtb, sel_row = _gdn_scalars(gp_ref, h0 + i)
            heads.append((alog, dtb, (lane == h0 + i).astype(F32), (lane == nh + h0 + i).astype(F32), sel_row))

        @pl.when(h0 == 0)
        def _():
            dba_ref[...] = jnp.zeros_like(dba_ref)
            dgp_ref[...] = jnp.zeros_like(dgp_ref)

        def step(it, carry):
            n = n_chunks - 1 - it
            rows = pl.ds(pl.multiple_of(n * rb, rb), rb)
            ba = ba_ref[rows, :]
            dba_old = dba_ref[rows, :]
            qs = [q_ref[rows, _lanes(j)] for j in range(gh // rep)]
            ks = [k_ref[rows, _lanes(j)] for j in range(gh // rep)]
            vs = [v_ref[rows, _lanes(i)] for i in range(gh)]
            zs = [z_ref[rows, _lanes(i)] for i in range(gh)]
            dos = [do_ref[rows, _lanes(i)] for i in range(gh)]
            sts = [st_ref[0, i, n] for i in range(gh)]
            new_carry, dba, gqs, gks, gvs, gzs = [], dba_old, {}, {}, [], []
            for i, (alog, dtb, sel_b, sel_a, _) in enumerate(heads):
                dstate, dalog, ddtb, dng = carry[i]
                braw = jnp.sum(ba * sel_b, axis=-1, keepdims=True)
                araw = jnp.sum(ba * sel_a, axis=-1, keepdims=True)
                _, vjp = jax.vjp(_gdn_chunk, qs[i // rep], ks[i // rep], vs[i], zs[i], braw, araw, alog, dtb, ng,
                                 sts[i])
                gq, gk, gv, gz, gb, ga, galog, gdtb, gng, gstate = vjp((dos[i], dstate))
                gqs[i // rep] = gq if i % rep == 0 else gqs[i // rep] + gq
                gks[i // rep] = gk if i % rep == 0 else gks[i // rep] + gk
                gvs.append(gv)
                gzs.append(gz)
                dba = dba + (gb * sel_b + ga * sel_a)
                new_carry.append((gstate, dalog + galog, ddtb + gdtb, dng + gng))
            for i in range(gh):
                dv_ref[rows, _lanes(i)] = gvs[i]
                dz_ref[rows, _lanes(i)] = gzs[i].astype(dz_ref.dtype)
            for j in gqs:
                dq_ref[rows, _lanes(j)] = gqs[j]
                dk_ref[rows, _lanes(j)] = gks[j]
            dba_ref[rows, :] = dba
            return tuple(new_carry)

        zero11 = jnp.zeros((1, 1), F32)
        init = tuple((jnp.zeros((HEAD, HEAD), F32), zero11, zero11, jnp.zeros((1, HEAD), F32)) for _ in range(gh))
        final = lax.fori_loop(0, n_chunks, step, init)
        rows3 = [sum(final[i][1] * heads[i][4] for i in range(gh)), sum(final[i][2] * heads[i][4] for i in range(gh)),
                 sum(final[i][3] for i in range(gh))]
        dgp_ref[0] += _rows_to_block(rows3, 8, HEAD)

    outs, got = carried_call(
        body, name=name, grid=(nb, nh // gh),
        in_specs=[sp["q"], sp["k"], sp["v"], sp["z"], sp["ba"], sp["gp"], sp["st"], sp["head"]],
        out_specs=[sp["qk_out"], sp["qk_out"], sp["head"], sp["head"], sp["ba"],
                   pl.BlockSpec((1, 8, HEAD), lambda b, j: (b, 0, 0))],
        out_shape=[jax.ShapeDtypeStruct((nb * s, nqk * HEAD), F32), jax.ShapeDtypeStruct((nb * s, nqk * HEAD), F32),
                   jax.ShapeDtypeStruct((nb * s, nh * HEAD), F32), jax.ShapeDtypeStruct((nb * s, nh * HEAD), BF16),
                   jax.ShapeDtypeStruct((nb * s, HEAD), F32), jax.ShapeDtypeStruct((nb, 8, HEAD), F32)],
        scratch_shapes=[], semantics=("parallel", "arbitrary"),
        args=(qkvc, qkvc, qkvc, p, pba, gp, states, dout), carry=carry)
    return outs + [got]


ADAM_LR = 0.001
ADAM_B1 = 0.9
ADAM_B2 = 0.999
ADAM_EPS = 1e-08
ADAM_WD = 0.01
ADAM_STEP = 10
EW_BLOCK_BYTES = 1 << 20


def _row_tile(rows, cols):
    for tr in (1024, 512, 256, 128, 64, 32, 16, 8):
        if rows % tr == 0 and tr * cols * 4 <= EW_BLOCK_BYTES:
            return tr
    return rows


def sum_slots(rs, *, name):
    nl = len(rs)
    n, rows, cols = rs[0].shape
    tr = _row_tile(rows, cols)
    nblk = rows // tr

    def body(*refs):
        o_ref = refs[nl]
        for l in range(nl):
            @pl.when(pl.program_id(0) == l)
            def _(l=l):
                acc = refs[l][0].astype(F32)
                for i in range(1, n):
                    acc = acc + refs[l][i].astype(F32)
                o_ref[0] = acc

    def in_map(l):
        return lambda li, i: (0, jnp.where(li == l, i, jnp.where(li < l, 0, nblk - 1)), 0)

    return pl.pallas_call(
        body, name=name, grid=(nl, nblk),
        in_specs=[pl.BlockSpec((n, tr, cols), in_map(l)) for l in range(nl)],
        out_specs=pl.BlockSpec((1, tr, cols), lambda li, i: (li, i, 0)),
        out_shape=jax.ShapeDtypeStruct((nl, rows, cols), F32),
        compiler_params=_cparams(("arbitrary", "arbitrary")),
    )(*rs)


def adamw(g_parts, w, m, v, *, name):
    shape = w.shape
    cols = shape[-1]
    to2d = lambda a: a.reshape(-1, cols)
    rows = to2d(w).shape[0]
    tr = _row_tile(rows, cols)
    npart = len(g_parts)
    c1 = 1.0 - ADAM_B1 ** ADAM_STEP
    c2 = 1.0 - ADAM_B2 ** ADAM_STEP

    def body(*refs):
        w_ref, m_ref, v_ref = refs[npart:npart + 3]
        g_ref, d_ref, nm_ref, nv_ref = refs[npart + 3:]
        g = refs[0][...]
        for i in range(1, npart):
            g = g + refs[i][...]
        nm = ADAM_B1 * m_ref[...] + (1.0 - ADAM_B1) * g
        nv = ADAM_B2 * v_ref[...] + (1.0 - ADAM_B2) * (g * g)
        g_ref[...] = g
        nm_ref[...] = nm
        nv_ref[...] = nv
        d_ref[...] = -ADAM_LR * ((nm / c1) / (jnp.sqrt(nv / c2) + ADAM_EPS) + ADAM_WD * w_ref[...])

    blk = pl.BlockSpec((tr, cols), lambda i: (i, 0))
    outs = pl.pallas_call(
        body, name=name, grid=(rows // tr,),
        in_specs=[blk] * (npart + 3),
        out_specs=[blk] * 4,
        out_shape=[jax.ShapeDtypeStruct((rows, cols), F32)] * 4,
        compiler_params=_cparams(("parallel",)),
    )(*[to2d(a) for a in g_parts], to2d(w), to2d(m), to2d(v))
    return tuple(o.reshape(shape) for o in outs)


def sibling_swap(arrs, *, name):
    n = len(arrs)

    def body(*refs):
        ins, outs = refs[:n], refs[n:2 * n]
        send_sems, recv_sems = refs[2 * n:]
        peer = (lax.axis_index("x"), lax.axis_index("y"), 1 - lax.axis_index("c"))
        copies = [pltpu.make_async_remote_copy(src_ref=ins[k], dst_ref=outs[k], send_sem=send_sems.at[k],
                                               recv_sem=recv_sems.at[k], device_id=peer, device_id_type=MESH)
                  for k in range(n)]
        for cp in copies:
            cp.start()
        for cp in copies:
            cp.wait()

    return pl.pallas_call(
        body, name=name,
        in_specs=[ANY] * n, out_specs=[ANY] * n,
        out_shape=[jax.ShapeDtypeStruct(a.shape, a.dtype) for a in arrs],
        scratch_shapes=[pltpu.SemaphoreType.DMA((n,)), pltpu.SemaphoreType.DMA((n,))],
    )(*arrs)


def allreduce_small(vec, *, name):
    r = vec.shape[0]

    def body(v_ref, o_ref, slots, send_sems, recv_sems):
        x, y, c = lax.axis_index("x"), lax.axis_index("y"), lax.axis_index("c")
        me = 4 * x + 2 * y + c
        slots[me] = v_ref[...]
        copies = []
        for j in range(1, 8):
            px = 1 - x if j & 4 else x
            py = 1 - y if j & 2 else y
            pc = 1 - c if j & 1 else c
            rc = pltpu.make_async_remote_copy(src_ref=v_ref, dst_ref=slots.at[me], send_sem=send_sems.at[j - 1],
                                              recv_sem=recv_sems.at[j - 1], device_id=(px, py, pc), device_id_type=MESH)
            rc.start()
            copies.append(rc)
        for cp in copies:
            cp.wait()
        acc = slots[0]
        for i in range(1, 8):
            acc = acc + slots[i]
        o_ref[...] = acc

    vm = pl.BlockSpec(memory_space=pltpu.VMEM)
    return pl.pallas_call(
        body, name=name, in_specs=[vm], out_specs=vm,
        out_shape=jax.ShapeDtypeStruct((r, 128), F32),
        scratch_shapes=[pltpu.VMEM((8, r, 128), F32), pltpu.SemaphoreType.DMA((7,)), pltpu.SemaphoreType.DMA((7,))],
        compiler_params=pltpu.CompilerParams(vmem_limit_bytes=VMEM_LIMIT),
    )(vec)


WEIGHTS = ("mix_norm_g", "w_in", "conv_dw_w", "conv_dw_b", "conv_ln_g", "conv_ln_b", "conv_pw_w", "conv_pw_b",
           "gdn_conv_w", "gdn_a_log", "gdn_dt_bias", "gdn_norm_g", "w_out", "ffn_norm_g", "w_up", "ffn_conv_w",
           "ffn_conv_b", "w_down", "final_norm_g")
COL_SHARDED = ("w_in", "w_up", "conv_dw_w", "gdn_conv_w", "ffn_conv_w")
ROW_SHARDED = ("conv_pw_w", "w_out", "w_down")
BIG = ("w_in", "conv_pw_w", "w_out", "w_up", "w_down")
SMALL_CONV = ("conv_dw_w", "gdn_conv_w", "ffn_conv_w")


def _full_from_slots(name, part):
    if name in COL_SHARDED:
        r, cs = part.shape[1:]
        return jnp.transpose(part, (1, 0, 2)).reshape(r, 4 * cs)
    rs, c = part.shape[1:]
    return part.reshape(4 * rs, c)


def _slots_from_full(name, full):
    if name in COL_SHARDED:
        r, c = full.shape
        return jnp.transpose(full.reshape(r, 4, c // 4), (1, 0, 2))
    r, c = full.shape
    return full.reshape(4, r // 4, c)


def _pack(parts):
    flat = jnp.concatenate([p.reshape(-1).astype(F32) for p in parts])
    pad = (-flat.shape[0]) % 1024
    return jnp.pad(flat, (0, pad)).reshape(-1, 128)


def _unpack(vec, shapes):
    flat = vec.reshape(-1)
    out, pos = [], 0
    for shp in shapes:
        n = 1
        for d in shp:
            n *= d
        out.append(flat[pos:pos + n].reshape(shp))
        pos += n
    return out


def kernel(x, mix_norm_g, w_in, conv_dw_w, conv_dw_b, conv_ln_g, conv_ln_b, conv_pw_w, conv_pw_b, gdn_conv_w, gdn_a_log, gdn_dt_bias, gdn_norm_g, w_out, ffn_norm_g, w_up, ffn_conv_w, ffn_conv_b, w_down, final_norm_g, loss_target, m_mix_norm_g, m_w_in, m_conv_dw_w, m_conv_dw_b, m_conv_ln_g, m_conv_ln_b, m_conv_pw_w, m_conv_pw_b, m_gdn_conv_w, m_gdn_a_log, m_gdn_dt_bias, m_gdn_norm_g, m_w_out, m_ffn_norm_g, m_w_up, m_ffn_conv_w, m_ffn_conv_b, m_w_down, m_final_norm_g, v_mix_norm_g, v_w_in, v_conv_dw_w, v_conv_dw_b, v_conv_ln_g, v_conv_ln_b, v_conv_pw_w, v_conv_pw_b, v_gdn_conv_w, v_gdn_a_log, v_gdn_dt_bias, v_gdn_norm_g, v_w_out, v_ffn_norm_g, v_w_up, v_ffn_conv_w, v_ffn_conv_b, v_w_down, v_final_norm_g):
    wts = dict(zip(WEIGHTS, (mix_norm_g, w_in, conv_dw_w, conv_dw_b, conv_ln_g, conv_ln_b, conv_pw_w, conv_pw_b,
                             gdn_conv_w, gdn_a_log, gdn_dt_bias, gdn_norm_g, w_out, ffn_norm_g, w_up, ffn_conv_w,
                             ffn_conv_b, w_down, final_norm_g)))
    mom = dict(zip(WEIGHTS, (m_mix_norm_g, m_w_in, m_conv_dw_w, m_conv_dw_b, m_conv_ln_g, m_conv_ln_b, m_conv_pw_w,
                             m_conv_pw_b, m_gdn_conv_w, m_gdn_a_log, m_gdn_dt_bias, m_gdn_norm_g, m_w_out,
                             m_ffn_norm_g, m_w_up, m_ffn_conv_w, m_ffn_conv_b, m_w_down, m_final_norm_g)))
    var = dict(zip(WEIGHTS, (v_mix_norm_g, v_w_in, v_conv_dw_w, v_conv_dw_b, v_conv_ln_g, v_conv_ln_b, v_conv_pw_w,
                             v_conv_pw_b, v_gdn_conv_w, v_gdn_a_log, v_gdn_dt_bias, v_gdn_norm_g, v_w_out,
                             v_ffn_norm_g, v_w_up, v_ffn_conv_w, v_ffn_conv_b, v_w_down, v_final_norm_g)))

    nb, s, d = x.shape
    t = nb * s
    depth = mix_norm_g.shape[0]
    ch = conv_dw_b.shape[1]
    nh = gdn_a_log.shape[1]
    nqk = nh // 2
    kwid, vwid = nqk * HEAD, nh * HEAD
    main = 2 * ch + 2 * kwid + 2 * vwid
    qcol, zcol = 2 * ch, 2 * ch + 2 * kwid + vwid
    dff = ffn_conv_b.shape[1]
    my_xy = 2 * lax.axis_index("x") + lax.axis_index("y")

    shard = {(n, l): wts[n][l].astype(BF16) for n in BIG for l in range(depth)}
    gather = lambda *keys: Exchange([shard[k] for k in keys], gather=True)
    first = xy_exchange([shard["w_in", 0]] + [wts[n] for n in SMALL_CONV], gather=True, name="gather_first")
    full = {("w_in", 0): _full_from_slots("w_in", first[0])}
    for n, g in zip(SMALL_CONV, first[1:]):
        for l in range(depth):
            full[n, l] = _full_from_slots(n, g[:, l])

    def arrived(keys, got):
        for (n, l), g in zip(keys, got):
            full[n, l] = _full_from_slots(n, g)

    xc = x.reshape(t, d)
    saved, lws = [], []
    for l in range(depth):
        w_in_f = full["w_in", l]
        w_main = w_in_f[:, :main]
        w_ba = jnp.pad(w_in_f[:, main:], ((0, 0), (0, HEAD - 2 * nh)))
        gp = (jnp.zeros((8, HEAD), F32).at[0, :nh].set(gdn_a_log[l]).at[1, :nh].set(gdn_dt_bias[l])
              .at[2].set(gdn_norm_g[l]))
        h = rmsnorm_fwd(xc, mix_norm_g[l], name=f"f{l}_norm1")
        keys = [("conv_pw_w", l), ("w_out", l)]
        p, got = matmul(h, w_main, name=f"f{l}_in_main", carry=gather(*keys))
        arrived(keys, got)
        pba = matmul(h, w_ba, name=f"f{l}_in_ba")
        u3 = conf_fwd(p, full["conv_dw_w", l], conv_dw_b[l], conv_ln_g[l], conv_ln_b[l], nb=nb, s=s,
                      name=f"f{l}_conf")
        out_a = matmul(u3, full["conv_pw_w", l], bias=conv_pw_b[l], out_dtype=BF16, name=f"f{l}_pw")
        qkvc = qkvconv_fwd(p, full["gdn_conv_w", l], col0=qcol, nb=nb, s=s, name=f"f{l}_qkvconv")
        keys = [("w_up", l)]
        out_b, states, got = gdn_fwd(qkvc, p, pba, gp, nb=nb, s=s, nh=nh, nqk=nqk, zcol=zcol, name=f"f{l}_gdn",
                                     carry=gather(*keys))
        arrived(keys, got)
        wout_a, wout_b = full["w_out", l][:ch], full["w_out", l][ch:]
        x1 = matmul(out_a, wout_a, res=xc, name=f"f{l}_out_a")
        x1 = matmul(out_b, wout_b, res=x1, name=f"f{l}_out_b")
        h2 = rmsnorm_fwd(x1, ffn_norm_g[l], name=f"f{l}_norm2")
        keys = [("w_down", l)]
        up, got = matmul(h2, full["w_up", l], name=f"f{l}_up", carry=gather(*keys))
        arrived(keys, got)
        act = ffn_act_fwd(up, full["ffn_conv_w", l], ffn_conv_b[l], nb=nb, s=s, name=f"f{l}_act")
        if l + 1 < depth:
            keys = [("w_in", l + 1)]
            x2, got = matmul(act, full["w_down", l], res=x1, name=f"f{l}_down", carry=gather(*keys))
            arrived(keys, got)
        else:
            x2 = matmul(act, full["w_down", l], res=x1, name=f"f{l}_down")
        saved.append(dict(x=xc, h=h, p=p, pba=pba, u3=u3, out_a=out_a, qkvc=qkvc, out_b=out_b, states=states,
                          x1=x1, h2=h2, up=up, act=act))
        lws.append(dict(w_main=w_main, w_ba=w_ba, pw=full["conv_pw_w", l], wout_a=wout_a, wout_b=wout_b,
                        wup=full["w_up", l], wdown=full["w_down", l], dw_w=full["conv_dw_w", l],
                        gconv_w=full["gdn_conv_w", l], fconv_w=full["ffn_conv_w", l], gp=gp))
        xc = x2

    loss_blk, dx, dxb, dgf = loss_head(xc, final_norm_g, loss_target.reshape(t, d), name="loss_head")

    stacks, received = {}, {}
    scatter = lambda *keys: Exchange([stacks[k] for k in keys], gather=False)

    def produced(n, l, grad):
        stacks[n, l] = _slots_from_full(n, grad).astype(BF16)

    def landed(keys, got):
        for k, g in zip(keys, got):
            received[k] = g

    small_grads = {n: [None] * depth for n in WEIGHTS if n not in BIG and n != "final_norm_g"}
    for l in reversed(range(depth)):
        lw, sv = lws[l], saved[l]
        dact = matmul(dxb, lw["wdown"], tb=True, name=f"b{l}_dact")
        produced("w_down", l, matmul(sv["act"], dxb, ta=True, name=f"b{l}_dwdown"))
        dgate, dupv, fpart = ffn_act_bwd(sv["up"], lw["fconv_w"], ffn_conv_b[l], dact, nb=nb, s=s, name=f"b{l}_act")
        dh2 = matmul(dgate, lw["wup"], tb=True, name=f"b{l}_dh2_gate")
        dh2 = matmul(dupv, lw["wup"], tb=True, b_koff=dff, res=dh2, name=f"b{l}_dh2_up")
        produced("w_up", l, jnp.concatenate(
            [matmul(sv["h2"], dgate, ta=True, name=f"b{l}_dwup_gate"),
             matmul(sv["h2"], dupv, ta=True, name=f"b{l}_dwup_up")], axis=1))
        dx1, dx1b, dg2 = rmsnorm_bwd(sv["x1"], ffn_norm_g[l], dh2, dx, name=f"b{l}_norm2")
        fsum = jnp.sum(fpart, axis=0)
        small_grads["ffn_norm_g"][l] = dg2[0]
        small_grads["ffn_conv_w"][l] = fsum[:ffn_conv_w.shape[1]]
        small_grads["ffn_conv_b"][l] = fsum[ffn_conv_w.shape[1]]
        dout_a = matmul(dx1b, lw["wout_a"], tb=True, out_dtype=BF16, name=f"b{l}_dout_a")
        dout_b = matmul(dx1b, lw["wout_b"], tb=True, name=f"b{l}_dout_b")
        produced("w_out", l, jnp.concatenate(
            [matmul(sv["out_a"], dx1b, ta=True, name=f"b{l}_dwout_a"),
             matmul(sv["out_b"], dx1b, ta=True, name=f"b{l}_dwout_b")], axis=0))
        du3 = matmul(dout_a, lw["pw"], tb=True, name=f"b{l}_du3")
        produced("conv_pw_w", l, matmul(sv["u3"], dout_a, ta=True, name=f"b{l}_dwpw"))
        dval, dagate, cw_part, cs_part = conf_bwd(sv["p"], lw["dw_w"], conv_dw_b[l], conv_ln_g[l], conv_ln_b[l],
                                                  du3, dout_a, nb=nb, s=s, name=f"b{l}_conf")
        csum = jnp.sum(cs_part, axis=0)
        small_grads["conv_dw_w"][l] = jnp.sum(cw_part, axis=0)
        small_grads["conv_dw_b"][l] = csum[0]
        small_grads["conv_ln_g"][l] = csum[1]
        small_grads["conv_ln_b"][l] = csum[2]
        small_grads["conv_pw_b"][l] = csum[3]
        keys = ([("w_in", l + 1)] if l + 1 < depth else []) + [("w_down", l), ("w_up", l), ("w_out", l),
                                                              ("conv_pw_w", l)]
        dq, dk, dv, dz, dpba, dgp, got = gdn_bwd(sv["qkvc"], sv["p"], sv["pba"], lw["gp"], sv["states"], dout_b,
                                                 nb=nb, s=s, nh=nh, nqk=nqk, zcol=zcol, name=f"b{l}_gdn",
                                                 carry=scatter(*keys))
        landed(keys, got)
        dqkv, gw_part = qkvconv_bwd(sv["p"], lw["gconv_w"], jnp.concatenate([dq, dk, dv], axis=1),
                                    col0=qcol, nb=nb, s=s, name=f"b{l}_qkvconv")
        gsum = jnp.sum(dgp, axis=0)
        small_grads["gdn_conv_w"][l] = jnp.sum(gw_part, axis=0)[:gdn_conv_w.shape[1]]
        small_grads["gdn_a_log"][l] = gsum[0, :nh]
        small_grads["gdn_dt_bias"][l] = gsum[1, :nh]
        small_grads["gdn_norm_g"][l] = gsum[2]
        dp = jnp.concatenate([dval, dagate, dqkv, dz], axis=1)
        dh = matmul(dp, lw["w_main"], tb=True, name=f"b{l}_dh_main")
        dh = matmul(dpba, lw["w_ba"], tb=True, res=dh, name=f"b{l}_dh_ba")
        dw_main = matmul(sv["h"], dp, ta=True, name=f"b{l}_dwin_main")
        dw_ba = matmul(sv["h"], dpba, ta=True, name=f"b{l}_dwin_ba")
        produced("w_in", l, jnp.concatenate([dw_main, dw_ba[:, :2 * nh]], axis=1))
        dx, dxb, dg1 = rmsnorm_bwd(sv["x"], mix_norm_g[l], dh, dx1, name=f"b{l}_norm1")
        small_grads["mix_norm_g"][l] = dg1[0]
    received["w_in", 0] = xy_exchange([stacks["w_in", 0]], gather=False, name="scatter_last")[0]

    partial = [sum_slots([received[n, l] for l in range(depth)], name=f"sum_{n}").reshape(wts[n].shape) for n in BIG]
    other = sibling_swap(partial, name="swap_partials")

    grads, deltas, new_m, new_v = {}, {}, {}, {}
    for n, mine, theirs in zip(BIG, partial, other):
        grads[n], deltas[n], new_m[n], new_v[n] = adamw([mine, theirs], wts[n], mom[n], var[n], name=f"adamw_{n}")

    small_names = [n for n in WEIGHTS if n not in BIG]
    small_full = [jnp.stack(small_grads[n]) if n != "final_norm_g" else dgf[0] for n in small_names]
    packed = _pack(small_full + [loss_blk[0, :1]])
    reduced = allreduce_small(packed, name="allreduce_small")
    parts = _unpack(reduced, [a.shape for a in small_full] + [(1,)])
    loss = parts[-1][0]
    for n, g in zip(small_names, parts[:-1]):
        if n in SMALL_CONV:
            wid = wts[n].shape[-1]
            g = lax.dynamic_slice_in_dim(g, my_xy * wid, wid, axis=g.ndim - 1)
        grads[n], deltas[n], new_m[n], new_v[n] = adamw([g], wts[n], mom[n], var[n], name=f"adamw_{n}")

    return (loss, dx.reshape(nb, s, d), *[grads[n] for n in WEIGHTS], *[deltas[n] for n in WEIGHTS],
            *[new_m[n] for n in WEIGHTS], *[new_v[n] for n in WEIGHTS])
```

```python
import functools

import jax
import jax.numpy as jnp
from jax import lax
from jax.experimental import pallas as pl
from jax.experimental.pallas import tpu as pltpu

F32 = jnp.float32
BF16 = jnp.bfloat16
EPS = 1e-6
CHUNK = 64
HEAD = 128
HIGHEST = lax.Precision.HIGHEST
VMEM_LIMIT = 56 * 1024 * 1024
MM_VMEM_BUDGET = 48 * 1024 * 1024


def _pick(dim, cands):
    for c in cands:
        if dim % c == 0:
            return c
    return dim


def _cparams(sem):
    return pltpu.CompilerParams(dimension_semantics=sem, vmem_limit_bytes=VMEM_LIMIT)


MESH = pl.DeviceIdType.MESH
ANY = pl.BlockSpec(memory_space=pl.ANY)


def _xy_peers():
    x, y = lax.axis_index("x"), lax.axis_index("y")
    peers = []
    for fx, fy in ((0, 1), (1, 0), (1, 1)):
        px = 1 - x if fx else x
        py = 1 - y if fy else y
        peers.append((2 * px + py, px, py))
    return 2 * x + y, peers


class Exchange:
    def __init__(self, arrs, gather):
        self.arrs, self.gather, self.n = list(arrs), gather, len(arrs)
        self.out_shape = [jax.ShapeDtypeStruct((4,) + (a.shape if gather else a.shape[1:]), a.dtype) for a in arrs]
        self.scratch = [pltpu.SemaphoreType.DMA((3 * self.n,)), pltpu.SemaphoreType.DMA((3 * self.n,)),
                        pltpu.SemaphoreType.DMA((self.n,))]

    def copies(self, ins, outs, send_sems, recv_sems, local_sems):
        me, peers = _xy_peers()
        c = lax.axis_index("c")
        out = []
        for k in range(self.n):
            out.append(pltpu.make_async_copy(ins[k] if self.gather else ins[k].at[me], outs[k].at[me],
                                             local_sems.at[k]))
            for j, (slot, px, py) in enumerate(peers):
                out.append(pltpu.make_async_remote_copy(
                    src_ref=ins[k] if self.gather else ins[k].at[slot], dst_ref=outs[k].at[me],
                    send_sem=send_sems.at[3 * k + j], recv_sem=recv_sems.at[3 * k + j],
                    device_id=(px, py, c), device_id_type=MESH))
        return out

    def start(self, ins, outs, sems):
        for cp in self.copies(ins, outs, *sems):
            cp.start()

    def finish(self, ins, outs, sems):
        for cp in self.copies(ins, outs, *sems):
            cp.wait()


class SplitGather:
    def __init__(self, arrs):
        self.arrs, self.n = list(arrs), len(arrs)
        assert all(a.ndim == 2 and a.shape[0] % 32 == 0 for a in arrs)
        self.out_shape = [jax.ShapeDtypeStruct((4,) + a.shape, a.dtype) for a in arrs]
        dma = pltpu.SemaphoreType.DMA
        self.scratch = [dma((3 * self.n,)), dma((3 * self.n,)), dma((3 * self.n,)), dma((3 * self.n,)), dma((self.n,))]

    def _half(self, k, c):
        half = self.arrs[k].shape[0] // 2
        return pl.ds(pl.multiple_of(c * half, 16), half)

    def over_ici(self, ins, outs, sems):
        me, peers = _xy_peers()
        c = lax.axis_index("c")
        out = []
        for k in range(self.n):
            out.append(pltpu.make_async_copy(ins[k], outs[k].at[me], sems[4].at[k]))
            for j, (slot, px, py) in enumerate(peers):
                out.append(pltpu.make_async_remote_copy(
                    src_ref=ins[k].at[self._half(k, c)], dst_ref=outs[k].at[me, self._half(k, c)],
                    send_sem=sems[0].at[3 * k + j], recv_sem=sems[1].at[3 * k + j],
                    device_id=(px, py, c), device_id_type=MESH))
        return out

    def over_d2d(self, outs, sems):
        _, peers = _xy_peers()
        x, y, c = lax.axis_index("x"), lax.axis_index("y"), lax.axis_index("c")
        out = []
        for k in range(self.n):
            for j, (slot, _, _) in enumerate(peers):
                rows = outs[k].at[slot, self._half(k, c)]
                out.append(pltpu.make_async_remote_copy(
                    src_ref=rows, dst_ref=rows, send_sem=sems[2].at[3 * k + j], recv_sem=sems[3].at[3 * k + j],
                    device_id=(x, y, 1 - c), device_id_type=MESH))
        return out

    def start(self, ins, outs, sems):
        for cp in self.over_ici(ins, outs, sems):
            cp.start()

    def finish(self, ins, outs, sems):
        for cp in self.over_ici(ins, outs, sems):
            cp.wait()
        passed = self.over_d2d(outs, sems)
        for cp in passed:
            cp.start()
        for cp in passed:
            cp.wait()


class SiblingSwap:
    def __init__(self, arrs):
        self.arrs, self.n = list(arrs), len(arrs)
        self.out_shape = [jax.ShapeDtypeStruct(a.shape, a.dtype) for a in arrs]
        self.scratch = [pltpu.SemaphoreType.DMA((self.n,)), pltpu.SemaphoreType.DMA((self.n,))]

    def copies(self, ins, outs, sems):
        peer = (lax.axis_index("x"), lax.axis_index("y"), 1 - lax.axis_index("c"))
        return [pltpu.make_async_remote_copy(src_ref=ins[k], dst_ref=outs[k], send_sem=sems[0].at[k],
                                             recv_sem=sems[1].at[k], device_id=peer, device_id_type=MESH)
                for k in range(self.n)]

    def start(self, ins, outs, sems):
        for cp in self.copies(ins, outs, sems):
            cp.start()

    def finish(self, ins, outs, sems):
        for cp in self.copies(ins, outs, sems):
            cp.wait()


def _split_refs(refs, carries, attr):
    groups, pos = [], 0
    for cr in carries:
        n = len(getattr(cr, attr))
        groups.append(refs[pos:pos + n])
        pos += n
    return groups


def carried_call(body, *, name, grid, in_specs, out_specs, out_shape, scratch_shapes, semantics, args, carry=None):
    n_in, n_out, n_scr = len(in_specs), len(out_specs), len(scratch_shapes)
    if carry is None:
        outs = pl.pallas_call(body, name=name, grid=grid, in_specs=in_specs, out_specs=out_specs, out_shape=out_shape,
                              scratch_shapes=scratch_shapes, compiler_params=_cparams(semantics))(*args)
        return list(outs), []
    carries = list(carry) if isinstance(carry, (list, tuple)) else [carry]
    n = sum(cr.n for cr in carries)

    def wrapped(*refs):
        ins, cin = refs[:n_in], _split_refs(refs[n_in:n_in + n], carries, "arrs")
        outs = refs[n_in + n:n_in + n + n_out]
        cout = _split_refs(refs[n_in + n + n_out:n_in + 2 * n + n_out], carries, "arrs")
        scratch = refs[n_in + 2 * n + n_out:n_in + 2 * n + n_out + n_scr]
        sems = _split_refs(refs[n_in + 2 * n + n_out + n_scr:], carries, "scratch")
        ids = [pl.program_id(i) for i in range(len(grid))]
        first = functools.reduce(jnp.logical_and, [i == 0 for i in ids])
        last = functools.reduce(jnp.logical_and, [i == g - 1 for i, g in zip(ids, grid)])

        @pl.when(first)
        def _():
            for cr, i, o, s in zip(carries, cin, cout, sems):
                cr.start(i, o, s)

        body(*ins, *outs, *scratch)

        @pl.when(last)
        def _():
            for cr, i, o, s in zip(carries, cin, cout, sems):
                cr.finish(i, o, s)

    res = pl.pallas_call(
        wrapped, name=name, grid=grid, in_specs=list(in_specs) + [ANY] * n, out_specs=list(out_specs) + [ANY] * n,
        out_shape=list(out_shape) + [s for cr in carries for s in cr.out_shape],
        scratch_shapes=list(scratch_shapes) + [s for cr in carries for s in cr.scratch],
        compiler_params=_cparams(tuple("arbitrary" for _ in grid)),
    )(*args, *[a for cr in carries for a in cr.arrs])
    got = _split_refs(list(res[n_out:]), carries, "arrs")
    return list(res[:n_out]), (got if isinstance(carry, (list, tuple)) else got[0])


def run_exchanges(carries, *, name):
    n = sum(cr.n for cr in carries)

    def body(*refs):
        cin = _split_refs(refs[:n], carries, "arrs")
        cout = _split_refs(refs[n:2 * n], carries, "arrs")
        sems = _split_refs(refs[2 * n:], carries, "scratch")
        for cr, i, o, s in zip(carries, cin, cout, sems):
            cr.start(i, o, s)
        for cr, i, o, s in zip(carries, cin, cout, sems):
            cr.finish(i, o, s)

    res = pl.pallas_call(body, name=name, in_specs=[ANY] * n, out_specs=[ANY] * n,
                         out_shape=[s for cr in carries for s in cr.out_shape],
                         scratch_shapes=[s for cr in carries for s in cr.scratch],
                         )(*[a for cr in carries for a in cr.arrs])
    return _split_refs(list(res), carries, "arrs")


def matmul(a, b, *, ta=False, tb=False, out_dtype=F32, res=None, bias=None, b_koff=0, name, carry=None):
    if ta:
        kdim, m = a.shape
    else:
        m, kdim = a.shape
    if tb:
        n, kb = b.shape
    else:
        kb, n = b.shape
    assert kb >= kdim + b_koff, (a.shape, b.shape, ta, tb)
    tn = _pick(n, (1024, 512, 256, 128))
    out_bytes = jnp.dtype(out_dtype).itemsize

    def vmem_bytes(tm_, tk_):
        blocks = a.dtype.itemsize * tm_ * tk_ + b.dtype.itemsize * tk_ * tn + out_bytes * tm_ * tn
        blocks += 4 * tm_ * tn if res is not None else 0
        temps = 4 * tm_ * tn + (2 * tk_ * tn if tb else 0)
        return 2 * blocks + temps + (4 * tm_ * tn if tk_ < kdim else 0)

    def longest_k(tm_):
        return next(c for c in (kdim, 4096, 2816, 2560, 2048, 1024, 512, 256, 128)
                    if kdim % c == 0 and b_koff % c == 0 and c % 128 == 0
                    and (vmem_bytes(tm_, c) <= MM_VMEM_BUDGET or c == 128))

    tall = [c for c in (1024, 512) if m % c == 0] or [_pick(m, (256, 128))]
    tm = max(tall, key=lambda c: (longest_k(c), c))
    tk = longest_k(tm)
    nk = kdim // tk
    ko = b_koff // tk
    dims = (((0 if ta else 1,), (1 if tb else 0,)), ((), ()))

    def body(*refs):
        a_ref, b_ref = refs[0], refs[1]
        pos = 2
        bias_ref = res_ref = None
        if bias is not None:
            bias_ref = refs[pos]
            pos += 1
        if res is not None:
            res_ref = refs[pos]
            pos += 1
        o_ref = refs[pos]
        part = lax.dot_general(a_ref[...].astype(BF16), b_ref[...].astype(BF16), dims, preferred_element_type=F32)

        def finish(r):
            if bias_ref is not None:
                r = r + bias_ref[...]
            if res_ref is not None:
                r = r + res_ref[...]
            o_ref[...] = r.astype(o_ref.dtype)

        if nk == 1:
            finish(part)
            return
        acc_ref = refs[pos + 1]
        k = pl.program_id(2)

        @pl.when(k == 0)
        def _():
            acc_ref[...] = part

        @pl.when((k > 0) & (k < nk - 1))
        def _():
            acc_ref[...] += part

        @pl.when(k == nk - 1)
        def _():
            finish(acc_ref[...] + part)

    a_spec = pl.BlockSpec((tk, tm), lambda i, j, k: (k, i)) if ta else pl.BlockSpec((tm, tk), lambda i, j, k: (i, k))
    b_spec = (pl.BlockSpec((tn, tk), lambda i, j, k: (j, k + ko)) if tb
              else pl.BlockSpec((tk, tn), lambda i, j, k: (k + ko, j)))
    in_specs = [a_spec, b_spec]
    args = [a, b]
    if bias is not None:
        in_specs.append(pl.BlockSpec((1, tn), lambda i, j, k: (0, j)))
        args.append(bias.reshape(1, n).astype(F32))
    if res is not None:
        in_specs.append(pl.BlockSpec((tm, tn), lambda i, j, k: (i, j)))
        args.append(res)
    outs, got = carried_call(
        body, name=name, grid=(m // tm, n // tn, nk), in_specs=in_specs,
        out_specs=[pl.BlockSpec((tm, tn), lambda i, j, k: (i, j))],
        out_shape=[jax.ShapeDtypeStruct((m, n), out_dtype)],
        scratch_shapes=[pltpu.VMEM((tm, tn), F32)] if nk > 1 else [],
        semantics=("parallel", "parallel", "arbitrary"), args=args, carry=carry)
    return outs[0] if carry is None else (outs[0], got)


def _sigmoid(x):
    return 1.0 / (1.0 + jnp.exp(-x))


def _softplus(x):
    return jnp.maximum(x, 0.0) + jnp.log(1.0 + jnp.exp(-jnp.abs(x)))


def _shift_back(x, s, row):
    if s == 0:
        return x
    return jnp.where(row >= s, pltpu.roll(x, s, 0), 0.0)


def _shift_fwd(x, s, row):
    if s == 0:
        return x
    n = x.shape[0]
    return jnp.where(row < n - s, pltpu.roll(x, n - s, 0), 0.0)


def _conv_fwd(x, w_ref, kw, row):
    acc = x * w_ref[pl.ds(kw - 1, 1), :]
    for s in range(1, kw):
        acc = acc + _shift_back(x, s, row) * w_ref[pl.ds(kw - 1 - s, 1), :]
    return acc


def _conv_bwd(x, dy, w_ref, kw, row):
    dx = dy * w_ref[pl.ds(kw - 1, 1), :]
    dw = [None] * kw
    dw[kw - 1] = jnp.sum(dy * x, axis=0, keepdims=True)
    for s in range(1, kw):
        dx = dx + _shift_fwd(dy, s, row) * w_ref[pl.ds(kw - 1 - s, 1), :]
        dw[kw - 1 - s] = jnp.sum(dy * _shift_back(x, s, row), axis=0, keepdims=True)
    return dx, dw


def _rows_to_block(rows, nrows, width):
    rid = lax.broadcasted_iota(jnp.int32, (nrows, width), 0)
    out = jnp.zeros((nrows, width), F32)
    for i, r in enumerate(rows):
        out = jnp.where(rid == i, r, out)
    return out


def rmsnorm_fwd(x, g, *, name):
    t, d = x.shape
    tm = _pick(t, (256, 128))

    def body(x_ref, g_ref, o_ref):
        xv = x_ref[...]
        r = lax.rsqrt(jnp.mean(xv * xv, axis=-1, keepdims=True) + EPS)
        o_ref[...] = (xv * r * g_ref[...]).astype(o_ref.dtype)

    return pl.pallas_call(
        body, name=name, grid=(t // tm,),
        in_specs=[pl.BlockSpec((tm, d), lambda i: (i, 0)), pl.BlockSpec((1, d), lambda i: (0, 0))],
        out_specs=pl.BlockSpec((tm, d), lambda i: (i, 0)),
        out_shape=jax.ShapeDtypeStruct((t, d), BF16),
        compiler_params=_cparams(("parallel",)),
    )(x, g.reshape(1, d))


def rmsnorm_bwd(x, g, dh, dres, *, name):
    t, d = x.shape
    tm = _pick(t, (256, 128))

    def body(x_ref, g_ref, dh_ref, dres_ref, dx_ref, dxb_ref, dg_ref):
        xv = x_ref[...]
        r = lax.rsqrt(jnp.mean(xv * xv, axis=-1, keepdims=True) + EPS)
        xh = xv * r
        dy = dh_ref[...]
        dxh = dy * g_ref[...]
        dx = dres_ref[...] + r * (dxh - xh * jnp.mean(dxh * xh, axis=-1, keepdims=True))
        dx_ref[...] = dx
        dxb_ref[...] = dx.astype(BF16)

        @pl.when(pl.program_id(0) == 0)
        def _():
            dg_ref[...] = jnp.zeros_like(dg_ref)

        dg_ref[...] += jnp.sum(dy * xh, axis=0, keepdims=True)

    row = pl.BlockSpec((tm, d), lambda i: (i, 0))
    vec = pl.BlockSpec((1, d), lambda i: (0, 0))
    return pl.pallas_call(
        body, name=name, grid=(t // tm,),
        in_specs=[row, vec, row, row],
        out_specs=[row, row, vec],
        out_shape=[jax.ShapeDtypeStruct((t, d), F32), jax.ShapeDtypeStruct((t, d), BF16),
                   jax.ShapeDtypeStruct((1, d), F32)],
        compiler_params=_cparams(("arbitrary",)),
    )(x, g.reshape(1, d), dh, dres)


def loss_head(x, g, target, *, name):
    t, d = x.shape
    tm = _pick(t, (256, 128))

    def body(x_ref, g_ref, tg_ref, loss_ref, dx_ref, dxb_ref, dg_ref):
        xv = x_ref[...]
        r = lax.rsqrt(jnp.mean(xv * xv, axis=-1, keepdims=True) + EPS)
        xh = xv * r
        err = xh * g_ref[...] - tg_ref[...]
        dy = err * (1.0 / d)
        dxh = dy * g_ref[...]
        dx = r * (dxh - xh * jnp.mean(dxh * xh, axis=-1, keepdims=True))
        dx_ref[...] = dx
        dxb_ref[...] = dx.astype(BF16)

        @pl.when(pl.program_id(0) == 0)
        def _():
            dg_ref[...] = jnp.zeros_like(dg_ref)
            loss_ref[...] = jnp.zeros_like(loss_ref)

        dg_ref[...] += jnp.sum(dy * xh, axis=0, keepdims=True)
        part = jnp.sum(jnp.sum(err * err, axis=-1, keepdims=True), axis=0, keepdims=True) * (0.5 / d)
        loss_ref[...] += jnp.broadcast_to(part, loss_ref.shape)

    row = pl.BlockSpec((tm, d), lambda i: (i, 0))
    vec = pl.BlockSpec((1, d), lambda i: (0, 0))
    return pl.pallas_call(
        body, name=name, grid=(t // tm,),
        in_specs=[row, vec, row],
        out_specs=[pl.BlockSpec((8, 128), lambda i: (0, 0)), row, row, vec],
        out_shape=[jax.ShapeDtypeStruct((8, 128), F32), jax.ShapeDtypeStruct((t, d), F32),
                   jax.ShapeDtypeStruct((t, d), BF16), jax.ShapeDtypeStruct((1, d), F32)],
        compiler_params=_cparams(("arbitrary",)),
    )(x, g.reshape(1, d), target)


def _conf_forward_parts(val, gate, w_ref, b, lg, lb, kw, row):
    sg = _sigmoid(gate)
    u0 = val * sg
    u1 = _conv_fwd(u0, w_ref, kw, row) + b
    mu = jnp.mean(u1, axis=-1, keepdims=True)
    xc = u1 - mu
    rs = lax.rsqrt(jnp.mean(xc * xc, axis=-1, keepdims=True) + EPS)
    xh = xc * rs
    u2 = xh * lg + lb
    s2 = _sigmoid(u2)
    return sg, u0, rs, xh, u2, s2


def conf_fwd(p, dw_w, dw_b, ln_g, ln_b, *, nb, s, name):
    kw, ch = dw_w.shape
    ng = ch // HEAD

    def body(val_ref, gate_ref, w_ref, b_ref, lg_ref, lb_ref, o_ref):
        row = lax.broadcasted_iota(jnp.int32, (s, HEAD), 0)
        _, _, _, _, u2, s2 = _conf_forward_parts(val_ref[...], gate_ref[...], w_ref, b_ref[...], lg_ref[...],
                                                 lb_ref[...], kw, row)
        o_ref[...] = (u2 * s2).astype(o_ref.dtype)

    vec = pl.BlockSpec((1, HEAD), lambda b, g: (0, g))
    return pl.pallas_call(
        body, name=name, grid=(nb, ng),
        in_specs=[pl.BlockSpec((s, HEAD), lambda b, g: (b, g)), pl.BlockSpec((s, HEAD), lambda b, g: (b, ng + g)),
                  pl.BlockSpec((kw, HEAD), lambda b, g: (0, g)), vec, vec, vec],
        out_specs=pl.BlockSpec((s, HEAD), lambda b, g: (b, g)),
        out_shape=jax.ShapeDtypeStruct((nb * s, ch), BF16),
        compiler_params=_cparams(("parallel", "parallel")),
    )(p, p, dw_w, dw_b.reshape(1, ch), ln_g.reshape(1, ch), ln_b.reshape(1, ch))


def conf_bwd(p, dw_w, dw_b, ln_g, ln_b, du3, dout_a, *, nb, s, name):
    kw, ch = dw_w.shape
    ng = ch // HEAD

    def body(val_ref, gate_ref, w_ref, b_ref, lg_ref, lb_ref, du3_ref, doa_ref, dval_ref, dgate_ref, dw_out, sm_out):
        row = lax.broadcasted_iota(jnp.int32, (s, HEAD), 0)
        val = val_ref[...]
        sg, u0, rs, xh, u2, s2 = _conf_forward_parts(val, gate_ref[...], w_ref, b_ref[...], lg_ref[...],
                                                     lb_ref[...], kw, row)
        du2 = du3_ref[...] * (s2 * (1.0 + u2 * (1.0 - s2)))
        dlg = jnp.sum(du2 * xh, axis=0, keepdims=True)
        dlb = jnp.sum(du2, axis=0, keepdims=True)
        dxh = du2 * lg_ref[...]
        du1 = rs * (dxh - jnp.mean(dxh, axis=-1, keepdims=True) - xh * jnp.mean(dxh * xh, axis=-1, keepdims=True))
        ddb = jnp.sum(du1, axis=0, keepdims=True)
        du0, dw = _conv_bwd(u0, du1, w_ref, kw, row)
        dval_ref[...] = (du0 * sg).astype(dval_ref.dtype)
        dgate_ref[...] = (du0 * val * sg * (1.0 - sg)).astype(dgate_ref.dtype)
        for k in range(kw):
            dw_out[0, pl.ds(k, 1), :] = dw[k]
        dpb = jnp.sum(doa_ref[...].astype(F32), axis=0, keepdims=True)
        sm_out[0] = _rows_to_block([ddb, dlg, dlb, dpb], 8, HEAD)

    vec = pl.BlockSpec((1, HEAD), lambda b, g: (0, g))
    blk = pl.BlockSpec((s, HEAD), lambda b, g: (b, g))
    return pl.pallas_call(
        body, name=name, grid=(nb, ng),
        in_specs=[blk, pl.BlockSpec((s, HEAD), lambda b, g: (b, ng + g)),
                  pl.BlockSpec((kw, HEAD), lambda b, g: (0, g)), vec, vec, vec, blk, blk],
        out_specs=[blk, blk, pl.BlockSpec((1, kw, HEAD), lambda b, g: (b, 0, g)),
                   pl.BlockSpec((1, 8, HEAD), lambda b, g: (b, 0, g))],
        out_shape=[jax.ShapeDtypeStruct((nb * s, ch), BF16), jax.ShapeDtypeStruct((nb * s, ch), BF16),
                   jax.ShapeDtypeStruct((nb, kw, ch), F32), jax.ShapeDtypeStruct((nb, 8, ch), F32)],
        compiler_params=_cparams(("parallel", "parallel")),
    )(p, p, dw_w, dw_b.reshape(1, ch), ln_g.reshape(1, ch), ln_b.reshape(1, ch), du3, dout_a)


def qkvconv_fwd(p, w, *, col0, nb, s, name):
    kw, ch = w.shape
    nblk = ch // HEAD
    c0 = col0 // HEAD

    def body(x_ref, w_ref, o_ref):
        row = lax.broadcasted_iota(jnp.int32, (s, HEAD), 0)
        c = _conv_fwd(x_ref[...], w_ref, kw, row)
        o_ref[...] = c * _sigmoid(c)

    return pl.pallas_call(
        body, name=name, grid=(nb, nblk),
        in_specs=[pl.BlockSpec((s, HEAD), lambda b, j: (b, c0 + j)), pl.BlockSpec((kw, HEAD), lambda b, j: (0, j))],
        out_specs=pl.BlockSpec((s, HEAD), lambda b, j: (b, j)),
        out_shape=jax.ShapeDtypeStruct((nb * s, ch), F32),
        compiler_params=_cparams(("parallel", "parallel")),
    )(p, w)


def qkvconv_bwd(p, w, dy, *, col0, nb, s, name):
    kw, ch = w.shape
    nblk = ch // HEAD
    c0 = col0 // HEAD

    def body(x_ref, w_ref, dy_ref, dx_ref, dw_out):
        row = lax.broadcasted_iota(jnp.int32, (s, HEAD), 0)
        xv = x_ref[...]
        c = _conv_fwd(xv, w_ref, kw, row)
        sc = _sigmoid(c)
        dc = dy_ref[...] * (sc * (1.0 + c * (1.0 - sc)))
        dx, dw = _conv_bwd(xv, dc, w_ref, kw, row)
        dx_ref[...] = dx.astype(dx_ref.dtype)
        dw_out[0] = _rows_to_block(dw, 8, HEAD)

    blk = pl.BlockSpec((s, HEAD), lambda b, j: (b, j))
    return pl.pallas_call(
        body, name=name, grid=(nb, nblk),
        in_specs=[pl.BlockSpec((s, HEAD), lambda b, j: (b, c0 + j)), pl.BlockSpec((kw, HEAD), lambda b, j: (0, j)), blk],
        out_specs=[blk, pl.BlockSpec((1, 8, HEAD), lambda b, j: (b, 0, j))],
        out_shape=[jax.ShapeDtypeStruct((nb * s, ch), BF16), jax.ShapeDtypeStruct((nb, 8, ch), F32)],
        compiler_params=_cparams(("parallel", "parallel")),
    )(p, w, dy)


def ffn_act_fwd(up, w, b, *, nb, s, name):
    kw, dff = w.shape
    cb = _pick(dff, (256, 128))
    nblk = dff // cb

    def body(g_ref, u_ref, w_ref, b_ref, o_ref):
        row = lax.broadcasted_iota(jnp.int32, (s, cb), 0)
        gc = _conv_fwd(g_ref[...].astype(F32), w_ref, kw, row) + b_ref[...]
        o_ref[...] = (gc * _sigmoid(gc) * u_ref[...].astype(F32)).astype(o_ref.dtype)

    return pl.pallas_call(
        body, name=name, grid=(nb, nblk),
        in_specs=[pl.BlockSpec((s, cb), lambda i, j: (i, j)), pl.BlockSpec((s, cb), lambda i, j: (i, nblk + j)),
                  pl.BlockSpec((kw, cb), lambda i, j: (0, j)), pl.BlockSpec((1, cb), lambda i, j: (0, j))],
        out_specs=pl.BlockSpec((s, cb), lambda i, j: (i, j)),
        out_shape=jax.ShapeDtypeStruct((nb * s, dff), BF16),
        compiler_params=_cparams(("parallel", "parallel")),
    )(up, up, w, b.reshape(1, dff))


def ffn_act_bwd(up, w, b, dact, *, nb, s, name):
    kw, dff = w.shape
    cb = _pick(dff, (256, 128))
    nblk = dff // cb

    def body(g_ref, u_ref, w_ref, b_ref, da_ref, dg_ref, du_ref, sm_out):
        row = lax.broadcasted_iota(jnp.int32, (s, cb), 0)
        gv = g_ref[...].astype(F32)
        gc = _conv_fwd(gv, w_ref, kw, row) + b_ref[...]
        sc = _sigmoid(gc)
        da = da_ref[...].astype(F32)
        du_ref[...] = (da * gc * sc).astype(du_ref.dtype)
        dgc = da * u_ref[...].astype(F32) * (sc * (1.0 + gc * (1.0 - sc)))
        dgate, dw = _conv_bwd(gv, dgc, w_ref, kw, row)
        dg_ref[...] = dgate.astype(dg_ref.dtype)
        sm_out[0] = _rows_to_block(dw + [jnp.sum(dgc, axis=0, keepdims=True)], 8, cb)

    blk = pl.BlockSpec((s, cb), lambda i, j: (i, j))
    return pl.pallas_call(
        body, name=name, grid=(nb, nblk),
        in_specs=[blk, pl.BlockSpec((s, cb), lambda i, j: (i, nblk + j)),
                  pl.BlockSpec((kw, cb), lambda i, j: (0, j)), pl.BlockSpec((1, cb), lambda i, j: (0, j)), blk],
        out_specs=[blk, blk, pl.BlockSpec((1, 8, cb), lambda i, j: (i, 0, j))],
        out_shape=[jax.ShapeDtypeStruct((nb * s, dff), BF16), jax.ShapeDtypeStruct((nb * s, dff), BF16),
                   jax.ShapeDtypeStruct((nb, 8, dff), F32)],
        compiler_params=_cparams(("parallel", "parallel")),
    )(up, up, w, b.reshape(1, dff), dact)


def _dot(a, b, dims, prec=None):
    return lax.dot_general(a, b, (dims, ((), ())), precision=prec, preferred_element_type=F32)


def _mm_nn(a, b):
    return _dot(a.astype(BF16), b.astype(BF16), ((1,), (0,)))


def _mm_nt(a, b):
    return _dot(a.astype(BF16), b.astype(BF16), ((1,), (1,)))


def _mm_tn(a, b):
    return _dot(a.astype(BF16), b.astype(BF16), ((0,), (0,)))


def _mm_hi(a, b):
    return _dot(a, b, ((1,), (0,)), HIGHEST)


def _mm_inv(a, b):
    return _mm_nn(a, b)


def _split3(x):
    hi = x.astype(BF16)
    rest = x - hi.astype(F32)
    mid = rest.astype(BF16)
    return hi, mid, (rest - mid.astype(F32)).astype(BF16)


def _mask_dot(mask, x, dims, mask_first):
    if mask_first:
        return sum(_dot(mask, p, dims) for p in _split3(x))
    return sum(_dot(p, mask, dims) for p in _split3(x))


@jax.custom_vjp
def _mask_nn(mask, x):
    return _mask_dot(mask, x, ((1,), (0,)), True)


def _mask_nn_fwd(mask, x):
    return _mask_nn(mask, x), mask


def _mask_nn_bwd(mask, ct):
    return jnp.zeros_like(mask), _mask_dot(mask, ct, ((0,), (0,)), True)


_mask_nn.defvjp(_mask_nn_fwd, _mask_nn_bwd)


@jax.custom_vjp
def _mask_tn(x, mask):
    return _mask_dot(mask, x, ((0,), (0,)), False)


def _mask_tn_fwd(x, mask):
    return _mask_tn(x, mask), mask


def _mask_tn_bwd(mask, ct):
    return _mask_dot(mask, ct, ((1,), (1,)), True), jnp.zeros_like(mask)


_mask_tn.defvjp(_mask_tn_fwd, _mask_tn_bwd)


GDN_ROWS = 256


def _gdn_chunk(qc, kc, vc, zc, braw, araw, alog, dtb, ng, state):
    r = qc.shape[0]
    ri = lax.broadcasted_iota(jnp.int32, (r, r), 0)
    ci = lax.broadcasted_iota(jnp.int32, (r, r), 1)
    same = (ri // CHUNK) == (ci // CHUNK)
    causal = same & (ri >= ci)
    strict = same & (ri > ci)
    eye = (ri == ci).astype(F32)
    row_chunk = lax.broadcasted_iota(jnp.int32, (r, HEAD), 0) // CHUNK

    q = qc * lax.rsqrt(jnp.sum(qc * qc, axis=-1, keepdims=True) + EPS) * (HEAD ** -0.5)
    k = kc * lax.rsqrt(jnp.sum(kc * kc, axis=-1, keepdims=True) + EPS)
    beta = _sigmoid(braw)
    g = -jnp.exp(alog) * _softplus(araw + dtb)

    g_w = jnp.broadcast_to(g, (r, HEAD))
    widen = lambda t: jnp.concatenate([t] * (r // HEAD), axis=1)
    gw = _mask_nn(causal.astype(BF16), g_w)
    gi = widen(gw)
    gj = _mask_tn(widen(g_w), (same & (ri <= ci)).astype(BF16))
    decay = jnp.where(causal, jnp.exp(jnp.where(causal, gi - gj, 0.0)), 0.0)

    kb = k * beta
    vb = vc * beta
    lmat = jnp.where(strict, _mm_nt(kb, k) * decay, 0.0)
    x = -lmat
    ainv = eye + x
    p = 1
    while 2 * p < CHUNK:
        x = _mm_inv(x, x)
        ainv = _mm_inv(ainv, eye + x)
        p *= 2
    u = _mm_inv(ainv, vb)
    w = _mm_inv(ainv, kb * jnp.exp(gw))
    qk = jnp.where(causal, _mm_nt(q, k) * decay, 0.0)
    qg = q * jnp.exp(gw)

    o = jnp.zeros((r, HEAD), F32)
    for c in range(r // CHUNK):
        in_c = row_chunk == c
        glast = jnp.sum(jnp.where(in_c, g_w, 0.0), axis=0, keepdims=True)
        v_new = jnp.where(in_c, u - _mm_nn(w, state), 0.0)
        o = o + jnp.where(in_c, _mm_nn(qg, state), 0.0) + _mm_nn(qk, v_new)
        k_dec = jnp.where(in_c, k * jnp.exp(jnp.where(in_c, glast - gw, 0.0)), 0.0)
        state = state * jnp.exp(glast) + _mm_tn(k_dec, v_new)

    o = o * lax.rsqrt(jnp.mean(o * o, axis=-1, keepdims=True) + EPS) * ng
    o = o * (zc * _sigmoid(zc))
    return o, state


GDN_HEADS_PER_STEP = 2


def _gdn_specs(s, nh, nqk, zcol, n_chunks, gh):
    rep = nh // nqk
    qw, vw = (gh // rep) * HEAD, gh * HEAD
    assert gh % rep == 0 and nh % gh == 0 and (nqk * HEAD) % qw == 0 and (2 * nqk * HEAD) % vw == 0 and zcol % vw == 0
    k0, v0, z0 = (nqk * HEAD) // qw, (2 * nqk * HEAD) // vw, zcol // vw
    return dict(
        q=pl.BlockSpec((s, qw), lambda b, j: (b, j)),
        k=pl.BlockSpec((s, qw), lambda b, j: (b, k0 + j)),
        v=pl.BlockSpec((s, vw), lambda b, j: (b, v0 + j)),
        z=pl.BlockSpec((s, vw), lambda b, j: (b, z0 + j)),
        ba=pl.BlockSpec((s, HEAD), lambda b, j: (b, 0)),
        gp=pl.BlockSpec((8, HEAD), lambda b, j: (0, 0)),
        qk_out=pl.BlockSpec((s, qw), lambda b, j: (b, j)),
        head=pl.BlockSpec((s, vw), lambda b, j: (b, j)),
        st=pl.BlockSpec((1, gh, n_chunks, HEAD, HEAD), lambda b, j: (b, j, 0, 0, 0)),
    )


def _gdn_scalars(gp_ref, h):
    lane = lax.broadcasted_iota(jnp.int32, (1, HEAD), 1)
    sel = (lane == h).astype(F32)
    alog = jnp.sum(gp_ref[pl.ds(0, 1), :] * sel, axis=-1, keepdims=True)
    dtb = jnp.sum(gp_ref[pl.ds(1, 1), :] * sel, axis=-1, keepdims=True)
    return alog, dtb, sel


def _lanes(i):
    return pl.ds(i * HEAD, HEAD)


def gdn_fwd(qkvc, p, pba, gp, *, nb, s, nh, nqk, zcol, name, carry=None):
    rb = min(GDN_ROWS, s)
    n_chunks = s // rb
    gh = GDN_HEADS_PER_STEP
    rep = nh // nqk
    sp = _gdn_specs(s, nh, nqk, zcol, n_chunks, gh)

    def body(q_ref, k_ref, v_ref, z_ref, ba_ref, gp_ref, o_ref, st_ref):
        h0 = pl.program_id(1) * gh
        ng = gp_ref[pl.ds(2, 1), :]
        lane = lax.broadcasted_iota(jnp.int32, (rb, HEAD), 1)
        heads = []
        for i in range(gh):
            alog, dtb, _ = _gdn_scalars(gp_ref, h0 + i)
            heads.append((alog, dtb, (lane == h0 + i).astype(F32), (lane == nh + h0 + i).astype(F32)))

        def step(n, states):
            rows = pl.ds(pl.multiple_of(n * rb, rb), rb)
            ba = ba_ref[rows, :]
            qs = [q_ref[rows, _lanes(j)] for j in range(gh // rep)]
            ks = [k_ref[rows, _lanes(j)] for j in range(gh // rep)]
            vs = [v_ref[rows, _lanes(i)] for i in range(gh)]
            zs = [z_ref[rows, _lanes(i)] for i in range(gh)]
            outs, new_states = [], []
            for i, (alog, dtb, sel_b, sel_a) in enumerate(heads):
                braw = jnp.sum(ba * sel_b, axis=-1, keepdims=True)
                araw = jnp.sum(ba * sel_a, axis=-1, keepdims=True)
                o, new_state = _gdn_chunk(qs[i // rep], ks[i // rep], vs[i], zs[i], braw, araw, alog, dtb, ng,
                                          states[i])
                outs.append(o)
                new_states.append(new_state)
            for i in range(gh):
                st_ref[0, i, n] = states[i]
                o_ref[rows, _lanes(i)] = outs[i].astype(o_ref.dtype)
            return tuple(new_states)

        lax.fori_loop(0, n_chunks, step, tuple(jnp.zeros((HEAD, HEAD), F32) for _ in range(gh)))

    outs, got = carried_call(
        body, name=name, grid=(nb, nh // gh),
        in_specs=[sp["q"], sp["k"], sp["v"], sp["z"], sp["ba"], sp["gp"]],
        out_specs=[sp["head"], sp["st"]],
        out_shape=[jax.ShapeDtypeStruct((nb * s, nh * HEAD), BF16),
                   jax.ShapeDtypeStruct((nb, nh, n_chunks, HEAD, HEAD), F32)],
        scratch_shapes=[], semantics=("parallel", "parallel"), args=(qkvc, qkvc, qkvc, p, pba, gp), carry=carry)
    return outs + [got]


def gdn_bwd(qkvc, p, pba, gp, states, dout, *, nb, s, nh, nqk, zcol, name, carry=None):
    rb = min(GDN_ROWS, s)
    n_chunks = s // rb
    gh = GDN_HEADS_PER_STEP
    rep = nh // nqk
    sp = _gdn_specs(s, nh, nqk, zcol, n_chunks, gh)

    def body(q_ref, k_ref, v_ref, z_ref, ba_ref, gp_ref, st_ref, do_ref,
             dq_ref, dk_ref, dv_ref, dz_ref, dba_ref, dgp_ref):
        h0 = pl.program_id(1) * gh
        ng = gp_ref[pl.ds(2, 1), :]
        lane = lax.broadcasted_iota(jnp.int32, (rb, HEAD), 1)
        heads = []
        for i in range(gh):
            alog, dtb, sel_row = _gdn_scalars(gp_ref, h0 + i)
            heads.append((alog, dtb, (lane == h0 + i).astype(F32), (lane == nh + h0 + i).astype(F32), sel_row))

        @pl.when(h0 == 0)
        def _():
            dba_ref[...] = jnp.zeros_like(dba_ref)
            dgp_ref[...] = jnp.zeros_like(dgp_ref)

        def step(it, carry):
            n = n_chunks - 1 - it
            rows = pl.ds(pl.multiple_of(n * rb, rb), rb)
            ba = ba_ref[rows, :]
            dba_old = dba_ref[rows, :]
            qs = [q_ref[rows, _lanes(j)] for j in range(gh // rep)]
            ks = [k_ref[rows, _lanes(j)] for j in range(gh // rep)]
            vs = [v_ref[rows, _lanes(i)] for i in range(gh)]
            zs = [z_ref[rows, _lanes(i)] for i in range(gh)]
            dos = [do_ref[rows, _lanes(i)] for i in range(gh)]
            sts = [st_ref[0, i, n] for i in range(gh)]
            new_carry, dba, gqs, gks, gvs, gzs = [], dba_old, {}, {}, [], []
            for i, (alog, dtb, sel_b, sel_a, _) in enumerate(heads):
                dstate, dalog, ddtb, dng = carry[i]
                braw = jnp.sum(ba * sel_b, axis=-1, keepdims=True)
                araw = jnp.sum(ba * sel_a, axis=-1, keepdims=True)
                _, vjp = jax.vjp(_gdn_chunk, qs[i // rep], ks[i // rep], vs[i], zs[i], braw, araw, alog, dtb, ng,
                                 sts[i])
                gq, gk, gv, gz, gb, ga, galog, gdtb, gng, gstate = vjp((dos[i], dstate))
                gqs[i // rep] = gq if i % rep == 0 else gqs[i // rep] + gq
                gks[i // rep] = gk if i % rep == 0 else gks[i // rep] + gk
                gvs.append(gv)
                gzs.append(gz)
                dba = dba + (gb * sel_b + ga * sel_a)
                new_carry.append((gstate, dalog + galog, ddtb + gdtb, dng + gng))
            for i in range(gh):
                dv_ref[rows, _lanes(i)] = gvs[i]
                dz_ref[rows, _lanes(i)] = gzs[i].astype(dz_ref.dtype)
            for j in gqs:
                dq_ref[rows, _lanes(j)] = gqs[j]
                dk_ref[rows, _lanes(j)] = gks[j]
            dba_ref[rows, :] = dba
            return tuple(new_carry)

        zero11 = jnp.zeros((1, 1), F32)
        init = tuple((jnp.zeros((HEAD, HEAD), F32), zero11, zero11, jnp.zeros((1, HEAD), F32)) for _ in range(gh))
        final = lax.fori_loop(0, n_chunks, step, init)
        rows3 = [sum(final[i][1] * heads[i][4] for i in range(gh)), sum(final[i][2] * heads[i][4] for i in range(gh)),
                 sum(final[i][3] for i in range(gh))]
        dgp_ref[0] += _rows_to_block(rows3, 8, HEAD)

    outs, got = carried_call(
        body, name=name, grid=(nb, nh // gh),
        in_specs=[sp["q"], sp["k"], sp["v"], sp["z"], sp["ba"], sp["gp"], sp["st"], sp["head"]],
        out_specs=[sp["qk_out"], sp["qk_out"], sp["head"], sp["head"], sp["ba"],
                   pl.BlockSpec((1, 8, HEAD), lambda b, j: (b, 0, 0))],
        out_shape=[jax.ShapeDtypeStruct((nb * s, nqk * HEAD), F32), jax.ShapeDtypeStruct((nb * s, nqk * HEAD), F32),
                   jax.ShapeDtypeStruct((nb * s, nh * HEAD), F32), jax.ShapeDtypeStruct((nb * s, nh * HEAD), BF16),
                   jax.ShapeDtypeStruct((nb * s, HEAD), F32), jax.ShapeDtypeStruct((nb, 8, HEAD), F32)],
        scratch_shapes=[], semantics=("parallel", "arbitrary"),
        args=(qkvc, qkvc, qkvc, p, pba, gp, states, dout), carry=carry)
    return outs + [got]


ADAM_LR = 0.001
ADAM_B1 = 0.9
ADAM_B2 = 0.999
ADAM_EPS = 1e-08
ADAM_WD = 0.01
ADAM_STEP = 10
EW_BLOCK_BYTES = 1 << 20


def _row_tile(rows, cols):
    for tr in (1024, 512, 256, 128, 64, 32, 16, 8):
        if rows % tr == 0 and tr * cols * 4 <= EW_BLOCK_BYTES:
            return tr
    return rows


def sum_slots(rs, *, name):
    nl = len(rs)
    n, rows, cols = rs[0].shape
    tr = _row_tile(rows, cols)
    nblk = rows // tr

    def body(*refs):
        o_ref = refs[nl]
        for l in range(nl):
            @pl.when(pl.program_id(0) == l)
            def _(l=l):
                acc = refs[l][0].astype(F32)
                for i in range(1, n):
                    acc = acc + refs[l][i].astype(F32)
                o_ref[0] = acc

    def in_map(l):
        return lambda li, i: (0, jnp.where(li == l, i, jnp.where(li < l, 0, nblk - 1)), 0)

    return pl.pallas_call(
        body, name=name, grid=(nl, nblk),
        in_specs=[pl.BlockSpec((n, tr, cols), in_map(l)) for l in range(nl)],
        out_specs=pl.BlockSpec((1, tr, cols), lambda li, i: (li, i, 0)),
        out_shape=jax.ShapeDtypeStruct((nl, rows, cols), F32),
        compiler_params=_cparams(("arbitrary", "arbitrary")),
    )(*rs)


def adamw(g_parts, w, m, v, *, name):
    shape = w.shape
    cols = shape[-1]
    to2d = lambda a: a.reshape(-1, cols)
    rows = to2d(w).shape[0]
    tr = _row_tile(rows, cols)
    npart = len(g_parts)
    c1 = 1.0 - ADAM_B1 ** ADAM_STEP
    c2 = 1.0 - ADAM_B2 ** ADAM_STEP

    def body(*refs):
        w_ref, m_ref, v_ref = refs[npart:npart + 3]
        g_ref, d_ref, nm_ref, nv_ref = refs[npart + 3:]
        g = refs[0][...]
        for i in range(1, npart):
            g = g + refs[i][...]
        nm = ADAM_B1 * m_ref[...] + (1.0 - ADAM_B1) * g
        nv = ADAM_B2 * v_ref[...] + (1.0 - ADAM_B2) * (g * g)
        g_ref[...] = g
        nm_ref[...] = nm
        nv_ref[...] = nv
        d_ref[...] = -ADAM_LR * ((nm / c1) / (jnp.sqrt(nv / c2) + ADAM_EPS) + ADAM_WD * w_ref[...])

    blk = pl.BlockSpec((tr, cols), lambda i: (i, 0))
    outs = pl.pallas_call(
        body, name=name, grid=(rows // tr,),
        in_specs=[blk] * (npart + 3),
        out_specs=[blk] * 4,
        out_shape=[jax.ShapeDtypeStruct((rows, cols), F32)] * 4,
        compiler_params=_cparams(("parallel",)),
    )(*[to2d(a) for a in g_parts], to2d(w), to2d(m), to2d(v))
    return tuple(o.reshape(shape) for o in outs)


def allreduce_small(vec, *, name):
    r = vec.shape[0]

    def body(v_ref, o_ref, slots, send_sems, recv_sems):
        x, y, c = lax.axis_index("x"), lax.axis_index("y"), lax.axis_index("c")
        me = 4 * x + 2 * y + c
        slots[me] = v_ref[...]
        copies = []
        for j in range(1, 8):
            px = 1 - x if j & 4 else x
            py = 1 - y if j & 2 else y
            pc = 1 - c if j & 1 else c
            rc = pltpu.make_async_remote_copy(src_ref=v_ref, dst_ref=slots.at[me], send_sem=send_sems.at[j - 1],
                                              recv_sem=recv_sems.at[j - 1], device_id=(px, py, pc), device_id_type=MESH)
            rc.start()
            copies.append(rc)
        for cp in copies:
            cp.wait()
        acc = slots[0]
        for i in range(1, 8):
            acc = acc + slots[i]
        o_ref[...] = acc

    vm = pl.BlockSpec(memory_space=pltpu.VMEM)
    return pl.pallas_call(
        body, name=name, in_specs=[vm], out_specs=vm,
        out_shape=jax.ShapeDtypeStruct((r, 128), F32),
        scratch_shapes=[pltpu.VMEM((8, r, 128), F32), pltpu.SemaphoreType.DMA((7,)), pltpu.SemaphoreType.DMA((7,))],
        compiler_params=pltpu.CompilerParams(vmem_limit_bytes=VMEM_LIMIT),
    )(vec)


WEIGHTS = ("mix_norm_g", "w_in", "conv_dw_w", "conv_dw_b", "conv_ln_g", "conv_ln_b", "conv_pw_w", "conv_pw_b",
           "gdn_conv_w", "gdn_a_log", "gdn_dt_bias", "gdn_norm_g", "w_out", "ffn_norm_g", "w_up", "ffn_conv_w",
           "ffn_conv_b", "w_down", "final_norm_g")
COL_SHARDED = ("w_in", "w_up", "conv_dw_w", "gdn_conv_w", "ffn_conv_w")
ROW_SHARDED = ("conv_pw_w", "w_out", "w_down")
BIG = ("w_in", "conv_pw_w", "w_out", "w_up", "w_down")
SMALL_CONV = ("conv_dw_w", "gdn_conv_w", "ffn_conv_w")


def _full_from_slots(name, part):
    if name in COL_SHARDED:
        r, cs = part.shape[1:]
        return jnp.transpose(part, (1, 0, 2)).reshape(r, 4 * cs)
    rs, c = part.shape[1:]
    return part.reshape(4 * rs, c)


def _slots_from_full(name, full):
    if name in COL_SHARDED:
        r, c = full.shape
        return jnp.transpose(full.reshape(r, 4, c // 4), (1, 0, 2))
    r, c = full.shape
    return full.reshape(4, r // 4, c)


def _pack(parts):
    flat = jnp.concatenate([p.reshape(-1).astype(F32) for p in parts])
    pad = (-flat.shape[0]) % 1024
    return jnp.pad(flat, (0, pad)).reshape(-1, 128)


def _unpack(vec, shapes):
    flat = vec.reshape(-1)
    out, pos = [], 0
    for shp in shapes:
        n = 1
        for d in shp:
            n *= d
        out.append(flat[pos:pos + n].reshape(shp))
        pos += n
    return out


def kernel(x, mix_norm_g, w_in, conv_dw_w, conv_dw_b, conv_ln_g, conv_ln_b, conv_pw_w, conv_pw_b, gdn_conv_w, gdn_a_log, gdn_dt_bias, gdn_norm_g, w_out, ffn_norm_g, w_up, ffn_conv_w, ffn_conv_b, w_down, final_norm_g, loss_target, m_mix_norm_g, m_w_in, m_conv_dw_w, m_conv_dw_b, m_conv_ln_g, m_conv_ln_b, m_conv_pw_w, m_conv_pw_b, m_gdn_conv_w, m_gdn_a_log, m_gdn_dt_bias, m_gdn_norm_g, m_w_out, m_ffn_norm_g, m_w_up, m_ffn_conv_w, m_ffn_conv_b, m_w_down, m_final_norm_g, v_mix_norm_g, v_w_in, v_conv_dw_w, v_conv_dw_b, v_conv_ln_g, v_conv_ln_b, v_conv_pw_w, v_conv_pw_b, v_gdn_conv_w, v_gdn_a_log, v_gdn_dt_bias, v_gdn_norm_g, v_w_out, v_ffn_norm_g, v_w_up, v_ffn_conv_w, v_ffn_conv_b, v_w_down, v_final_norm_g):
    wts = dict(zip(WEIGHTS, (mix_norm_g, w_in, conv_dw_w, conv_dw_b, conv_ln_g, conv_ln_b, conv_pw_w, conv_pw_b,
                             gdn_conv_w, gdn_a_log, gdn_dt_bias, gdn_norm_g, w_out, ffn_norm_g, w_up, ffn_conv_w,
                             ffn_conv_b, w_down, final_norm_g)))
    mom = dict(zip(WEIGHTS, (m_mix_norm_g, m_w_in, m_conv_dw_w, m_conv_dw_b, m_conv_ln_g, m_conv_ln_b, m_conv_pw_w,
                             m_conv_pw_b, m_gdn_conv_w, m_gdn_a_log, m_gdn_dt_bias, m_gdn_norm_g, m_w_out,
                             m_ffn_norm_g, m_w_up, m_ffn_conv_w, m_ffn_conv_b, m_w_down, m_final_norm_g)))
    var = dict(zip(WEIGHTS, (v_mix_norm_g, v_w_in, v_conv_dw_w, v_conv_dw_b, v_conv_ln_g, v_conv_ln_b, v_conv_pw_w,
                             v_conv_pw_b, v_gdn_conv_w, v_gdn_a_log, v_gdn_dt_bias, v_gdn_norm_g, v_w_out,
                             v_ffn_norm_g, v_w_up, v_ffn_conv_w, v_ffn_conv_b, v_w_down, v_final_norm_g)))

    nb, s, d = x.shape
    t = nb * s
    depth = mix_norm_g.shape[0]
    ch = conv_dw_b.shape[1]
    nh = gdn_a_log.shape[1]
    nqk = nh // 2
    kwid, vwid = nqk * HEAD, nh * HEAD
    main = 2 * ch + 2 * kwid + 2 * vwid
    qcol, zcol = 2 * ch, 2 * ch + 2 * kwid + vwid
    dff = ffn_conv_b.shape[1]
    my_xy = 2 * lax.axis_index("x") + lax.axis_index("y")

    shard = {(n, l): wts[n][l].astype(BF16) for n in BIG for l in range(depth)}
    gather = lambda *keys: SplitGather([shard[k] for k in keys])
    first, small = run_exchanges([gather(("w_in", 0)), Exchange([wts[n] for n in SMALL_CONV], gather=True)],
                                 name="gather_first")
    full = {("w_in", 0): _full_from_slots("w_in", first[0])}
    for n, g in zip(SMALL_CONV, small):
        for l in range(depth):
            full[n, l] = _full_from_slots(n, g[:, l])

    def arrived(keys, got):
        for (n, l), g in zip(keys, got):
            full[n, l] = _full_from_slots(n, g)

    xc = x.reshape(t, d)
    saved, lws = [], []
    for l in range(depth):
        w_in_f = full["w_in", l]
        w_main = w_in_f[:, :main]
        w_ba = jnp.pad(w_in_f[:, main:], ((0, 0), (0, HEAD - 2 * nh)))
        gp = (jnp.zeros((8, HEAD), F32).at[0, :nh].set(gdn_a_log[l]).at[1, :nh].set(gdn_dt_bias[l])
              .at[2].set(gdn_norm_g[l]))
        h = rmsnorm_fwd(xc, mix_norm_g[l], name=f"f{l}_norm1")
        keys = [("conv_pw_w", l), ("w_out", l)]
        p, got = matmul(h, w_main, name=f"f{l}_in_main", carry=gather(*keys))
        arrived(keys, got)
        pba = matmul(h, w_ba, name=f"f{l}_in_ba")
        u3 = conf_fwd(p, full["conv_dw_w", l], conv_dw_b[l], conv_ln_g[l], conv_ln_b[l], nb=nb, s=s,
                      name=f"f{l}_conf")
        out_a = matmul(u3, full["conv_pw_w", l], bias=conv_pw_b[l], out_dtype=BF16, name=f"f{l}_pw")
        qkvc = qkvconv_fwd(p, full["gdn_conv_w", l], col0=qcol, nb=nb, s=s, name=f"f{l}_qkvconv")
        keys = [("w_up", l)]
        out_b, states, got = gdn_fwd(qkvc, p, pba, gp, nb=nb, s=s, nh=nh, nqk=nqk, zcol=zcol, name=f"f{l}_gdn",
                                     carry=gather(*keys))
        arrived(keys, got)
        wout_a, wout_b = full["w_out", l][:ch], full["w_out", l][ch:]
        x1 = matmul(out_a, wout_a, res=xc, name=f"f{l}_out_a")
        x1 = matmul(out_b, wout_b, res=x1, name=f"f{l}_out_b")
        h2 = rmsnorm_fwd(x1, ffn_norm_g[l], name=f"f{l}_norm2")
        keys = [("w_down", l)]
        up, got = matmul(h2, full["w_up", l], out_dtype=BF16, name=f"f{l}_up", carry=gather(*keys))
        arrived(keys, got)
        act = ffn_act_fwd(up, full["ffn_conv_w", l], ffn_conv_b[l], nb=nb, s=s, name=f"f{l}_act")
        if l + 1 < depth:
            keys = [("w_in", l + 1)]
            x2, got = matmul(act, full["w_down", l], res=x1, name=f"f{l}_down", carry=gather(*keys))
            arrived(keys, got)
        else:
            x2 = matmul(act, full["w_down", l], res=x1, name=f"f{l}_down")
        saved.append(dict(x=xc, h=h, p=p, pba=pba, u3=u3, out_a=out_a, qkvc=qkvc, out_b=out_b, states=states,
                          x1=x1, h2=h2, up=up, act=act))
        lws.append(dict(w_main=w_main, w_ba=w_ba, pw=full["conv_pw_w", l], wout_a=wout_a, wout_b=wout_b,
                        wup=full["w_up", l], wdown=full["w_down", l], dw_w=full["conv_dw_w", l],
                        gconv_w=full["gdn_conv_w", l], fconv_w=full["ffn_conv_w", l], gp=gp))
        xc = x2

    loss_blk, dx, dxb, dgf = loss_head(xc, final_norm_g, loss_target.reshape(t, d), name="loss_head")

    stacks, received = {}, {}
    scatter = lambda *keys: Exchange([stacks[k] for k in keys], gather=False)

    def produced(n, l, grad):
        stacks[n, l] = _slots_from_full(n, grad).astype(BF16)

    def landed(keys, got):
        for k, g in zip(keys, got):
            received[k] = g

    small_grads = {n: [None] * depth for n in WEIGHTS if n not in BIG and n != "final_norm_g"}
    for l in reversed(range(depth)):
        lw, sv = lws[l], saved[l]
        dact = matmul(dxb, lw["wdown"], tb=True, out_dtype=BF16, name=f"b{l}_dact")
        produced("w_down", l, matmul(sv["act"], dxb, ta=True, out_dtype=BF16, name=f"b{l}_dwdown"))
        dgate, dupv, fpart = ffn_act_bwd(sv["up"], lw["fconv_w"], ffn_conv_b[l], dact, nb=nb, s=s, name=f"b{l}_act")
        dh2 = matmul(dgate, lw["wup"], tb=True, name=f"b{l}_dh2_gate")
        dh2 = matmul(dupv, lw["wup"], tb=True, b_koff=dff, res=dh2, name=f"b{l}_dh2_up")
        produced("w_up", l, jnp.concatenate(
            [matmul(sv["h2"], dgate, ta=True, out_dtype=BF16, name=f"b{l}_dwup_gate"),
             matmul(sv["h2"], dupv, ta=True, out_dtype=BF16, name=f"b{l}_dwup_up")], axis=1))
        dx1, dx1b, dg2 = rmsnorm_bwd(sv["x1"], ffn_norm_g[l], dh2, dx, name=f"b{l}_norm2")
        fsum = jnp.sum(fpart, axis=0)
        small_grads["ffn_norm_g"][l] = dg2[0]
        small_grads["ffn_conv_w"][l] = fsum[:ffn_conv_w.shape[1]]
        small_grads["ffn_conv_b"][l] = fsum[ffn_conv_w.shape[1]]
        dout_a = matmul(dx1b, lw["wout_a"], tb=True, out_dtype=BF16, name=f"b{l}_dout_a")
        dout_b = matmul(dx1b, lw["wout_b"], tb=True, name=f"b{l}_dout_b")
        produced("w_out", l, jnp.concatenate(
            [matmul(sv["out_a"], dx1b, ta=True, out_dtype=BF16, name=f"b{l}_dwout_a"),
             matmul(sv["out_b"], dx1b, ta=True, out_dtype=BF16, name=f"b{l}_dwout_b")], axis=0))
        du3 = matmul(dout_a, lw["pw"], tb=True, name=f"b{l}_du3")
        produced("conv_pw_w", l, matmul(sv["u3"], dout_a, ta=True, out_dtype=BF16, name=f"b{l}_dwpw"))
        dval, dagate, cw_part, cs_part = conf_bwd(sv["p"], lw["dw_w"], conv_dw_b[l], conv_ln_g[l], conv_ln_b[l],
                                                  du3, dout_a, nb=nb, s=s, name=f"b{l}_conf")
        csum = jnp.sum(cs_part, axis=0)
        small_grads["conv_dw_w"][l] = jnp.sum(cw_part, axis=0)
        small_grads["conv_dw_b"][l] = csum[0]
        small_grads["conv_ln_g"][l] = csum[1]
        small_grads["conv_ln_b"][l] = csum[2]
        small_grads["conv_pw_b"][l] = csum[3]
        keys = ([("w_in", l + 1)] if l + 1 < depth else []) + [("w_down", l), ("w_up", l), ("w_out", l),
                                                              ("conv_pw_w", l)]
        dq, dk, dv, dz, dpba, dgp, got = gdn_bwd(sv["qkvc"], sv["p"], sv["pba"], lw["gp"], sv["states"], dout_b,
                                                 nb=nb, s=s, nh=nh, nqk=nqk, zcol=zcol, name=f"b{l}_gdn",
                                                 carry=scatter(*keys))
        landed(keys, got)
        dqkv, gw_part = qkvconv_bwd(sv["p"], lw["gconv_w"], jnp.concatenate([dq, dk, dv], axis=1),
                                    col0=qcol, nb=nb, s=s, name=f"b{l}_qkvconv")
        gsum = jnp.sum(dgp, axis=0)
        small_grads["gdn_conv_w"][l] = jnp.sum(gw_part, axis=0)[:gdn_conv_w.shape[1]]
        small_grads["gdn_a_log"][l] = gsum[0, :nh]
        small_grads["gdn_dt_bias"][l] = gsum[1, :nh]
        small_grads["gdn_norm_g"][l] = gsum[2]
        dp = jnp.concatenate([dval, dagate, dqkv, dz], axis=1)
        dh = matmul(dp, lw["w_main"], tb=True, name=f"b{l}_dh_main")
        dh = matmul(dpba, lw["w_ba"], tb=True, res=dh, name=f"b{l}_dh_ba")
        dw_main = matmul(sv["h"], dp, ta=True, out_dtype=BF16, name=f"b{l}_dwin_main")
        dw_ba = matmul(sv["h"], dpba, ta=True, out_dtype=BF16, name=f"b{l}_dwin_ba")
        produced("w_in", l, jnp.concatenate([dw_main, dw_ba[:, :2 * nh]], axis=1))
        dx, dxb, dg1 = rmsnorm_bwd(sv["x"], mix_norm_g[l], dh, dx1, name=f"b{l}_norm1")
        small_grads["mix_norm_g"][l] = dg1[0]

    def summed(n):
        return sum_slots([received[n, l] for l in range(depth)], name=f"sum_{n}").reshape(wts[n].shape)

    early = [n for n in BIG if n != "w_in"]
    partial = {n: summed(n) for n in early}
    swapped, last = run_exchanges([SiblingSwap([partial[n] for n in early]),
                                   Exchange([stacks["w_in", 0]], gather=False)], name="swap_and_scatter_last")
    other = dict(zip(early, swapped))
    received["w_in", 0] = last[0]
    partial["w_in"] = summed("w_in")
    other["w_in"] = run_exchanges([SiblingSwap([partial["w_in"]])], name="swap_w_in")[0][0]

    grads, deltas, new_m, new_v = {}, {}, {}, {}
    for n in BIG:
        grads[n], deltas[n], new_m[n], new_v[n] = adamw([partial[n], other[n]], wts[n], mom[n], var[n],
                                                        name=f"adamw_{n}")

    small_names = [n for n in WEIGHTS if n not in BIG]
    small_full = [jnp.stack(small_grads[n]) if n != "final_norm_g" else dgf[0] for n in small_names]
    packed = _pack(small_full + [loss_blk[0, :1]])
    reduced = allreduce_small(packed, name="allreduce_small")
    parts = _unpack(reduced, [a.shape for a in small_full] + [(1,)])
    loss = parts[-1][0]
    for n, g in zip(small_names, parts[:-1]):
        if n in SMALL_CONV:
            wid = wts[n].shape[-1]
            g = lax.dynamic_slice_in_dim(g, my_xy * wid, wid, axis=g.ndim - 1)
        grads[n], deltas[n], new_m[n], new_v[n] = adamw([g], wts[n], mom[n], var[n], name=f"adamw_{n}")

    return (loss, dx.reshape(nb, s, d), *[grads[n] for n in WEIGHTS], *[deltas[n] for n in WEIGHTS],
            *[new_m[n] for n in WEIGHTS], *[new_v[n] for n in WEIGHTS])
```

```python
import functools

import jax
import jax.numpy as jnp
from jax import lax
from jax.experimental import pallas as pl
from jax.experimental.pallas import tpu as pltpu

F32 = jnp.float32
BF16 = jnp.bfloat16
EPS = 1e-6
CHUNK = 64
HEAD = 128
HIGHEST = lax.Precision.HIGHEST
VMEM_LIMIT = 56 * 1024 * 1024
MM_VMEM_BUDGET = 48 * 1024 * 1024


def _pick(dim, cands):
    for c in cands:
        if dim % c == 0:
            return c
    return dim


def _cparams(sem):
    return pltpu.CompilerParams(dimension_semantics=sem, vmem_limit_bytes=VMEM_LIMIT)


MESH = pl.DeviceIdType.MESH
ANY = pl.BlockSpec(memory_space=pl.ANY)


def _xy_peers():
    x, y = lax.axis_index("x"), lax.axis_index("y")
    peers = []
    for fx, fy in ((0, 1), (1, 0), (1, 1)):
        px = 1 - x if fx else x
        py = 1 - y if fy else y
        peers.append((2 * px + py, px, py))
    return 2 * x + y, peers


class Exchange:
    def __init__(self, arrs, gather):
        self.arrs, self.gather, self.n = list(arrs), gather, len(arrs)
        self.out_shape = [jax.ShapeDtypeStruct((4,) + (a.shape if gather else a.shape[1:]), a.dtype) for a in arrs]
        self.scratch = [pltpu.SemaphoreType.DMA((3 * self.n,)), pltpu.SemaphoreType.DMA((3 * self.n,)),
                        pltpu.SemaphoreType.DMA((self.n,))]

    def copies(self, ins, outs, send_sems, recv_sems, local_sems):
        me, peers = _xy_peers()
        c = lax.axis_index("c")
        out = []
        for k in range(self.n):
            out.append(pltpu.make_async_copy(ins[k] if self.gather else ins[k].at[me], outs[k].at[me],
                                             local_sems.at[k]))
            for j, (slot, px, py) in enumerate(peers):
                out.append(pltpu.make_async_remote_copy(
                    src_ref=ins[k] if self.gather else ins[k].at[slot], dst_ref=outs[k].at[me],
                    send_sem=send_sems.at[3 * k + j], recv_sem=recv_sems.at[3 * k + j],
                    device_id=(px, py, c), device_id_type=MESH))
        return out

    def start(self, ins, outs, sems):
        for cp in self.copies(ins, outs, *sems):
            cp.start()

    def finish(self, ins, outs, sems):
        for cp in self.copies(ins, outs, *sems):
            cp.wait()


class SplitGather:
    def __init__(self, arrs):
        self.arrs, self.n = list(arrs), len(arrs)
        assert all(a.ndim == 2 and a.shape[0] % 32 == 0 for a in arrs)
        self.out_shape = [jax.ShapeDtypeStruct((4,) + a.shape, a.dtype) for a in arrs]
        dma = pltpu.SemaphoreType.DMA
        self.scratch = [dma((3 * self.n,)), dma((3 * self.n,)), dma((3 * self.n,)), dma((3 * self.n,)), dma((self.n,))]

    def _half(self, k, c):
        half = self.arrs[k].shape[0] // 2
        return pl.ds(pl.multiple_of(c * half, 16), half)

    def over_ici(self, ins, outs, sems):
        me, peers = _xy_peers()
        c = lax.axis_index("c")
        out = []
        for k in range(self.n):
            out.append(pltpu.make_async_copy(ins[k], outs[k].at[me], sems[4].at[k]))
            for j, (slot, px, py) in enumerate(peers):
                out.append(pltpu.make_async_remote_copy(
                    src_ref=ins[k].at[self._half(k, c)], dst_ref=outs[k].at[me, self._half(k, c)],
                    send_sem=sems[0].at[3 * k + j], recv_sem=sems[1].at[3 * k + j],
                    device_id=(px, py, c), device_id_type=MESH))
        return out

    def over_d2d(self, outs, sems):
        _, peers = _xy_peers()
        x, y, c = lax.axis_index("x"), lax.axis_index("y"), lax.axis_index("c")
        out = []
        for k in range(self.n):
            for j, (slot, _, _) in enumerate(peers):
                rows = outs[k].at[slot, self._half(k, c)]
                out.append(pltpu.make_async_remote_copy(
                    src_ref=rows, dst_ref=rows, send_sem=sems[2].at[3 * k + j], recv_sem=sems[3].at[3 * k + j],
                    device_id=(x, y, 1 - c), device_id_type=MESH))
        return out

    def start(self, ins, outs, sems):
        for cp in self.over_ici(ins, outs, sems):
            cp.start()

    def finish(self, ins, outs, sems):
        for cp in self.over_ici(ins, outs, sems):
            cp.wait()
        passed = self.over_d2d(outs, sems)
        for cp in passed:
            cp.start()
        for cp in passed:
            cp.wait()


class SiblingSwap:
    def __init__(self, arrs):
        self.arrs, self.n = list(arrs), len(arrs)
        self.out_shape = [jax.ShapeDtypeStruct(a.shape, a.dtype) for a in arrs]
        self.scratch = [pltpu.SemaphoreType.DMA((self.n,)), pltpu.SemaphoreType.DMA((self.n,))]

    def copies(self, ins, outs, sems):
        peer = (lax.axis_index("x"), lax.axis_index("y"), 1 - lax.axis_index("c"))
        return [pltpu.make_async_remote_copy(src_ref=ins[k], dst_ref=outs[k], send_sem=sems[0].at[k],
                                             recv_sem=sems[1].at[k], device_id=peer, device_id_type=MESH)
                for k in range(self.n)]

    def start(self, ins, outs, sems):
        for cp in self.copies(ins, outs, sems):
            cp.start()

    def finish(self, ins, outs, sems):
        for cp in self.copies(ins, outs, sems):
            cp.wait()


def _split_refs(refs, carries, attr):
    groups, pos = [], 0
    for cr in carries:
        n = len(getattr(cr, attr))
        groups.append(refs[pos:pos + n])
        pos += n
    return groups


def carried_call(body, *, name, grid, in_specs, out_specs, out_shape, scratch_shapes, semantics, args, carry=None):
    n_in, n_out, n_scr = len(in_specs), len(out_specs), len(scratch_shapes)
    if carry is None:
        outs = pl.pallas_call(body, name=name, grid=grid, in_specs=in_specs, out_specs=out_specs, out_shape=out_shape,
                              scratch_shapes=scratch_shapes, compiler_params=_cparams(semantics))(*args)
        return list(outs), []
    carries = list(carry) if isinstance(carry, (list, tuple)) else [carry]
    n = sum(cr.n for cr in carries)

    def wrapped(*refs):
        ins, cin = refs[:n_in], _split_refs(refs[n_in:n_in + n], carries, "arrs")
        outs = refs[n_in + n:n_in + n + n_out]
        cout = _split_refs(refs[n_in + n + n_out:n_in + 2 * n + n_out], carries, "arrs")
        scratch = refs[n_in + 2 * n + n_out:n_in + 2 * n + n_out + n_scr]
        sems = _split_refs(refs[n_in + 2 * n + n_out + n_scr:], carries, "scratch")
        ids = [pl.program_id(i) for i in range(len(grid))]
        first = functools.reduce(jnp.logical_and, [i == 0 for i in ids])
        last = functools.reduce(jnp.logical_and, [i == g - 1 for i, g in zip(ids, grid)])

        @pl.when(first)
        def _():
            for cr, i, o, s in zip(carries, cin, cout, sems):
                cr.start(i, o, s)

        body(*ins, *outs, *scratch)

        @pl.when(last)
        def _():
            for cr, i, o, s in zip(carries, cin, cout, sems):
                cr.finish(i, o, s)

    res = pl.pallas_call(
        wrapped, name=name, grid=grid, in_specs=list(in_specs) + [ANY] * n, out_specs=list(out_specs) + [ANY] * n,
        out_shape=list(out_shape) + [s for cr in carries for s in cr.out_shape],
        scratch_shapes=list(scratch_shapes) + [s for cr in carries for s in cr.scratch],
        compiler_params=_cparams(tuple("arbitrary" for _ in grid)),
    )(*args, *[a for cr in carries for a in cr.arrs])
    got = _split_refs(list(res[n_out:]), carries, "arrs")
    return list(res[:n_out]), (got if isinstance(carry, (list, tuple)) else got[0])


def run_exchanges(carries, *, name):
    n = sum(cr.n for cr in carries)

    def body(*refs):
        cin = _split_refs(refs[:n], carries, "arrs")
        cout = _split_refs(refs[n:2 * n], carries, "arrs")
        sems = _split_refs(refs[2 * n:], carries, "scratch")
        for cr, i, o, s in zip(carries, cin, cout, sems):
            cr.start(i, o, s)
        for cr, i, o, s in zip(carries, cin, cout, sems):
            cr.finish(i, o, s)

    res = pl.pallas_call(body, name=name, in_specs=[ANY] * n, out_specs=[ANY] * n,
                         out_shape=[s for cr in carries for s in cr.out_shape],
                         scratch_shapes=[s for cr in carries for s in cr.scratch],
                         )(*[a for cr in carries for a in cr.arrs])
    return _split_refs(list(res), carries, "arrs")


def matmul(a, b, *, ta=False, tb=False, out_dtype=F32, res=None, bias=None, b_koff=0, name, carry=None):
    if ta:
        kdim, m = a.shape
    else:
        m, kdim = a.shape
    if tb:
        n, kb = b.shape
    else:
        kb, n = b.shape
    assert kb >= kdim + b_koff, (a.shape, b.shape, ta, tb)
    tn = _pick(n, (1024, 512, 256, 128))
    out_bytes = jnp.dtype(out_dtype).itemsize

    def vmem_bytes(tm_, tk_):
        blocks = a.dtype.itemsize * tm_ * tk_ + b.dtype.itemsize * tk_ * tn + out_bytes * tm_ * tn
        blocks += 4 * tm_ * tn if res is not None else 0
        temps = 4 * tm_ * tn + (2 * tk_ * tn if tb else 0)
        return 2 * blocks + temps + (4 * tm_ * tn if tk_ < kdim else 0)

    def longest_k(tm_):
        return next(c for c in (kdim, 4096, 2816, 2560, 2048, 1024, 512, 256, 128)
                    if kdim % c == 0 and b_koff % c == 0 and c % 128 == 0
                    and (vmem_bytes(tm_, c) <= MM_VMEM_BUDGET or c == 128))

    tall = [c for c in (1024, 512) if m % c == 0] or [_pick(m, (256, 128))]
    tm = max(tall, key=lambda c: (longest_k(c), c))
    tk = longest_k(tm)
    nk = kdim // tk
    ko = b_koff // tk
    dims = (((0 if ta else 1,), (1 if tb else 0,)), ((), ()))

    def body(*refs):
        a_ref, b_ref = refs[0], refs[1]
        pos = 2
        bias_ref = res_ref = None
        if bias is not None:
            bias_ref = refs[pos]
            pos += 1
        if res is not None:
            res_ref = refs[pos]
            pos += 1
        o_ref = refs[pos]
        part = lax.dot_general(a_ref[...].astype(BF16), b_ref[...].astype(BF16), dims, preferred_element_type=F32)

        def finish(r):
            if bias_ref is not None:
                r = r + bias_ref[...]
            if res_ref is not None:
                r = r + res_ref[...]
            o_ref[...] = r.astype(o_ref.dtype)

        if nk == 1:
            finish(part)
            return
        acc_ref = refs[pos + 1]
        k = pl.program_id(2)

        @pl.when(k == 0)
        def _():
            acc_ref[...] = part

        @pl.when((k > 0) & (k < nk - 1))
        def _():
            acc_ref[...] += part

        @pl.when(k == nk - 1)
        def _():
            finish(acc_ref[...] + part)

    a_spec = pl.BlockSpec((tk, tm), lambda i, j, k: (k, i)) if ta else pl.BlockSpec((tm, tk), lambda i, j, k: (i, k))
    b_spec = (pl.BlockSpec((tn, tk), lambda i, j, k: (j, k + ko)) if tb
              else pl.BlockSpec((tk, tn), lambda i, j, k: (k + ko, j)))
    in_specs = [a_spec, b_spec]
    args = [a, b]
    if bias is not None:
        in_specs.append(pl.BlockSpec((1, tn), lambda i, j, k: (0, j)))
        args.append(bias.reshape(1, n).astype(F32))
    if res is not None:
        in_specs.append(pl.BlockSpec((tm, tn), lambda i, j, k: (i, j)))
        args.append(res)
    outs, got = carried_call(
        body, name=name, grid=(m // tm, n // tn, nk), in_specs=in_specs,
        out_specs=[pl.BlockSpec((tm, tn), lambda i, j, k: (i, j))],
        out_shape=[jax.ShapeDtypeStruct((m, n), out_dtype)],
        scratch_shapes=[pltpu.VMEM((tm, tn), F32)] if nk > 1 else [],
        semantics=("parallel", "parallel", "arbitrary"), args=args, carry=carry)
    return outs[0] if carry is None else (outs[0], got)


def _sigmoid(x):
    return 1.0 / (1.0 + jnp.exp(-x))


def _softplus(x):
    return jnp.maximum(x, 0.0) + jnp.log(1.0 + jnp.exp(-jnp.abs(x)))


def _shift_back(x, s, row):
    if s == 0:
        return x
    return jnp.where(row >= s, pltpu.roll(x, s, 0), 0.0)


def _shift_fwd(x, s, row):
    if s == 0:
        return x
    n = x.shape[0]
    return jnp.where(row < n - s, pltpu.roll(x, n - s, 0), 0.0)


def _conv_fwd(x, w_ref, kw, row):
    acc = x * w_ref[pl.ds(kw - 1, 1), :]
    for s in range(1, kw):
        acc = acc + _shift_back(x, s, row) * w_ref[pl.ds(kw - 1 - s, 1), :]
    return acc


def _conv_bwd(x, dy, w_ref, kw, row):
    dx = dy * w_ref[pl.ds(kw - 1, 1), :]
    dw = [None] * kw
    dw[kw - 1] = jnp.sum(dy * x, axis=0, keepdims=True)
    for s in range(1, kw):
        dx = dx + _shift_fwd(dy, s, row) * w_ref[pl.ds(kw - 1 - s, 1), :]
        dw[kw - 1 - s] = jnp.sum(dy * _shift_back(x, s, row), axis=0, keepdims=True)
    return dx, dw


def _rows_to_block(rows, nrows, width):
    rid = lax.broadcasted_iota(jnp.int32, (nrows, width), 0)
    out = jnp.zeros((nrows, width), F32)
    for i, r in enumerate(rows):
        out = jnp.where(rid == i, r, out)
    return out


def rmsnorm_fwd(x, g, *, name):
    t, d = x.shape
    tm = _pick(t, (256, 128))

    def body(x_ref, g_ref, o_ref):
        xv = x_ref[...]
        r = lax.rsqrt(jnp.mean(xv * xv, axis=-1, keepdims=True) + EPS)
        o_ref[...] = (xv * r * g_ref[...]).astype(o_ref.dtype)

    return pl.pallas_call(
        body, name=name, grid=(t // tm,),
        in_specs=[pl.BlockSpec((tm, d), lambda i: (i, 0)), pl.BlockSpec((1, d), lambda i: (0, 0))],
        out_specs=pl.BlockSpec((tm, d), lambda i: (i, 0)),
        out_shape=jax.ShapeDtypeStruct((t, d), BF16),
        compiler_params=_cparams(("parallel",)),
    )(x, g.reshape(1, d))


def rmsnorm_bwd(x, g, dh, dres, *, name):
    t, d = x.shape
    tm = _pick(t, (256, 128))

    def body(x_ref, g_ref, dh_ref, dres_ref, dx_ref, dxb_ref, dg_ref):
        xv = x_ref[...]
        r = lax.rsqrt(jnp.mean(xv * xv, axis=-1, keepdims=True) + EPS)
        xh = xv * r
        dy = dh_ref[...]
        dxh = dy * g_ref[...]
        dx = dres_ref[...] + r * (dxh - xh * jnp.mean(dxh * xh, axis=-1, keepdims=True))
        dx_ref[...] = dx
        dxb_ref[...] = dx.astype(BF16)

        @pl.when(pl.program_id(0) == 0)
        def _():
            dg_ref[...] = jnp.zeros_like(dg_ref)

        dg_ref[...] += jnp.sum(dy * xh, axis=0, keepdims=True)

    row = pl.BlockSpec((tm, d), lambda i: (i, 0))
    vec = pl.BlockSpec((1, d), lambda i: (0, 0))
    return pl.pallas_call(
        body, name=name, grid=(t // tm,),
        in_specs=[row, vec, row, row],
        out_specs=[row, row, vec],
        out_shape=[jax.ShapeDtypeStruct((t, d), F32), jax.ShapeDtypeStruct((t, d), BF16),
                   jax.ShapeDtypeStruct((1, d), F32)],
        compiler_params=_cparams(("arbitrary",)),
    )(x, g.reshape(1, d), dh, dres)


def loss_head(x, g, target, *, name):
    t, d = x.shape
    tm = _pick(t, (256, 128))

    def body(x_ref, g_ref, tg_ref, loss_ref, dx_ref, dxb_ref, dg_ref):
        xv = x_ref[...]
        r = lax.rsqrt(jnp.mean(xv * xv, axis=-1, keepdims=True) + EPS)
        xh = xv * r
        err = xh * g_ref[...] - tg_ref[...]
        dy = err * (1.0 / d)
        dxh = dy * g_ref[...]
        dx = r * (dxh - xh * jnp.mean(dxh * xh, axis=-1, keepdims=True))
        dx_ref[...] = dx
        dxb_ref[...] = dx.astype(BF16)

        @pl.when(pl.program_id(0) == 0)
        def _():
            dg_ref[...] = jnp.zeros_like(dg_ref)
            loss_ref[...] = jnp.zeros_like(loss_ref)

        dg_ref[...] += jnp.sum(dy * xh, axis=0, keepdims=True)
        part = jnp.sum(jnp.sum(err * err, axis=-1, keepdims=True), axis=0, keepdims=True) * (0.5 / d)
        loss_ref[...] += jnp.broadcast_to(part, loss_ref.shape)

    row = pl.BlockSpec((tm, d), lambda i: (i, 0))
    vec = pl.BlockSpec((1, d), lambda i: (0, 0))
    return pl.pallas_call(
        body, name=name, grid=(t // tm,),
        in_specs=[row, vec, row],
        out_specs=[pl.BlockSpec((8, 128), lambda i: (0, 0)), row, row, vec],
        out_shape=[jax.ShapeDtypeStruct((8, 128), F32), jax.ShapeDtypeStruct((t, d), F32),
                   jax.ShapeDtypeStruct((t, d), BF16), jax.ShapeDtypeStruct((1, d), F32)],
        compiler_params=_cparams(("arbitrary",)),
    )(x, g.reshape(1, d), target)


def _conf_forward_parts(val, gate, w_ref, b, lg, lb, kw, row):
    sg = _sigmoid(gate)
    u0 = val * sg
    u1 = _conv_fwd(u0, w_ref, kw, row) + b
    mu = jnp.mean(u1, axis=-1, keepdims=True)
    xc = u1 - mu
    rs = lax.rsqrt(jnp.mean(xc * xc, axis=-1, keepdims=True) + EPS)
    xh = xc * rs
    u2 = xh * lg + lb
    s2 = _sigmoid(u2)
    return sg, u0, rs, xh, u2, s2


def conf_fwd(p, dw_w, dw_b, ln_g, ln_b, *, nb, s, name):
    kw, ch = dw_w.shape
    ng = ch // HEAD

    def body(val_ref, gate_ref, w_ref, b_ref, lg_ref, lb_ref, o_ref):
        row = lax.broadcasted_iota(jnp.int32, (s, HEAD), 0)
        _, _, _, _, u2, s2 = _conf_forward_parts(val_ref[...], gate_ref[...], w_ref, b_ref[...], lg_ref[...],
                                                 lb_ref[...], kw, row)
        o_ref[...] = (u2 * s2).astype(o_ref.dtype)

    vec = pl.BlockSpec((1, HEAD), lambda b, g: (0, g))
    return pl.pallas_call(
        body, name=name, grid=(nb, ng),
        in_specs=[pl.BlockSpec((s, HEAD), lambda b, g: (b, g)), pl.BlockSpec((s, HEAD), lambda b, g: (b, ng + g)),
                  pl.BlockSpec((kw, HEAD), lambda b, g: (0, g)), vec, vec, vec],
        out_specs=pl.BlockSpec((s, HEAD), lambda b, g: (b, g)),
        out_shape=jax.ShapeDtypeStruct((nb * s, ch), BF16),
        compiler_params=_cparams(("parallel", "parallel")),
    )(p, p, dw_w, dw_b.reshape(1, ch), ln_g.reshape(1, ch), ln_b.reshape(1, ch))


def conf_bwd(p, dw_w, dw_b, ln_g, ln_b, du3, dout_a, *, nb, s, name, carry=None):
    kw, ch = dw_w.shape
    ng = ch // HEAD

    def body(val_ref, gate_ref, w_ref, b_ref, lg_ref, lb_ref, du3_ref, doa_ref, dval_ref, dgate_ref, dw_out, sm_out):
        row = lax.broadcasted_iota(jnp.int32, (s, HEAD), 0)
        val = val_ref[...]
        sg, u0, rs, xh, u2, s2 = _conf_forward_parts(val, gate_ref[...], w_ref, b_ref[...], lg_ref[...],
                                                     lb_ref[...], kw, row)
        du2 = du3_ref[...] * (s2 * (1.0 + u2 * (1.0 - s2)))
        dlg = jnp.sum(du2 * xh, axis=0, keepdims=True)
        dlb = jnp.sum(du2, axis=0, keepdims=True)
        dxh = du2 * lg_ref[...]
        du1 = rs * (dxh - jnp.mean(dxh, axis=-1, keepdims=True) - xh * jnp.mean(dxh * xh, axis=-1, keepdims=True))
        ddb = jnp.sum(du1, axis=0, keepdims=True)
        du0, dw = _conv_bwd(u0, du1, w_ref, kw, row)
        dval_ref[...] = (du0 * sg).astype(dval_ref.dtype)
        dgate_ref[...] = (du0 * val * sg * (1.0 - sg)).astype(dgate_ref.dtype)
        for k in range(kw):
            dw_out[0, pl.ds(k, 1), :] = dw[k]
        dpb = jnp.sum(doa_ref[...].astype(F32), axis=0, keepdims=True)
        sm_out[0] = _rows_to_block([ddb, dlg, dlb, dpb], 8, HEAD)

    vec = pl.BlockSpec((1, HEAD), lambda b, g: (0, g))
    blk = pl.BlockSpec((s, HEAD), lambda b, g: (b, g))
    outs, got = carried_call(
        body, name=name, grid=(nb, ng),
        in_specs=[blk, pl.BlockSpec((s, HEAD), lambda b, g: (b, ng + g)),
                  pl.BlockSpec((kw, HEAD), lambda b, g: (0, g)), vec, vec, vec, blk, blk],
        out_specs=[blk, blk, pl.BlockSpec((1, kw, HEAD), lambda b, g: (b, 0, g)),
                   pl.BlockSpec((1, 8, HEAD), lambda b, g: (b, 0, g))],
        out_shape=[jax.ShapeDtypeStruct((nb * s, ch), BF16), jax.ShapeDtypeStruct((nb * s, ch), BF16),
                   jax.ShapeDtypeStruct((nb, kw, ch), F32), jax.ShapeDtypeStruct((nb, 8, ch), F32)],
        scratch_shapes=[], semantics=("parallel", "parallel"),
        args=(p, p, dw_w, dw_b.reshape(1, ch), ln_g.reshape(1, ch), ln_b.reshape(1, ch), du3, dout_a), carry=carry)
    return outs + [got]


def qkvconv_fwd(p, w, *, col0, nb, s, name):
    kw, ch = w.shape
    nblk = ch // HEAD
    c0 = col0 // HEAD

    def body(x_ref, w_ref, o_ref):
        row = lax.broadcasted_iota(jnp.int32, (s, HEAD), 0)
        c = _conv_fwd(x_ref[...], w_ref, kw, row)
        o_ref[...] = c * _sigmoid(c)

    return pl.pallas_call(
        body, name=name, grid=(nb, nblk),
        in_specs=[pl.BlockSpec((s, HEAD), lambda b, j: (b, c0 + j)), pl.BlockSpec((kw, HEAD), lambda b, j: (0, j))],
        out_specs=pl.BlockSpec((s, HEAD), lambda b, j: (b, j)),
        out_shape=jax.ShapeDtypeStruct((nb * s, ch), F32),
        compiler_params=_cparams(("parallel", "parallel")),
    )(p, w)


def qkvconv_bwd(p, w, dy, *, col0, nb, s, name):
    kw, ch = w.shape
    nblk = ch // HEAD
    c0 = col0 // HEAD

    def body(x_ref, w_ref, dy_ref, dx_ref, dw_out):
        row = lax.broadcasted_iota(jnp.int32, (s, HEAD), 0)
        xv = x_ref[...]
        c = _conv_fwd(xv, w_ref, kw, row)
        sc = _sigmoid(c)
        dc = dy_ref[...] * (sc * (1.0 + c * (1.0 - sc)))
        dx, dw = _conv_bwd(xv, dc, w_ref, kw, row)
        dx_ref[...] = dx.astype(dx_ref.dtype)
        dw_out[0] = _rows_to_block(dw, 8, HEAD)

    blk = pl.BlockSpec((s, HEAD), lambda b, j: (b, j))
    return pl.pallas_call(
        body, name=name, grid=(nb, nblk),
        in_specs=[pl.BlockSpec((s, HEAD), lambda b, j: (b, c0 + j)), pl.BlockSpec((kw, HEAD), lambda b, j: (0, j)), blk],
        out_specs=[blk, pl.BlockSpec((1, 8, HEAD), lambda b, j: (b, 0, j))],
        out_shape=[jax.ShapeDtypeStruct((nb * s, ch), BF16), jax.ShapeDtypeStruct((nb, 8, ch), F32)],
        compiler_params=_cparams(("parallel", "parallel")),
    )(p, w, dy)


def ffn_act_fwd(up, w, b, *, nb, s, name):
    kw, dff = w.shape
    cb = _pick(dff, (256, 128))
    nblk = dff // cb

    def body(g_ref, u_ref, w_ref, b_ref, o_ref):
        row = lax.broadcasted_iota(jnp.int32, (s, cb), 0)
        gc = _conv_fwd(g_ref[...].astype(F32), w_ref, kw, row) + b_ref[...]
        o_ref[...] = (gc * _sigmoid(gc) * u_ref[...].astype(F32)).astype(o_ref.dtype)

    return pl.pallas_call(
        body, name=name, grid=(nb, nblk),
        in_specs=[pl.BlockSpec((s, cb), lambda i, j: (i, j)), pl.BlockSpec((s, cb), lambda i, j: (i, nblk + j)),
                  pl.BlockSpec((kw, cb), lambda i, j: (0, j)), pl.BlockSpec((1, cb), lambda i, j: (0, j))],
        out_specs=pl.BlockSpec((s, cb), lambda i, j: (i, j)),
        out_shape=jax.ShapeDtypeStruct((nb * s, dff), BF16),
        compiler_params=_cparams(("parallel", "parallel")),
    )(up, up, w, b.reshape(1, dff))


def ffn_act_bwd(up, w, b, dact, *, nb, s, name):
    kw, dff = w.shape
    cb = _pick(dff, (256, 128))
    nblk = dff // cb

    def body(g_ref, u_ref, w_ref, b_ref, da_ref, dg_ref, du_ref, sm_out):
        row = lax.broadcasted_iota(jnp.int32, (s, cb), 0)
        gv = g_ref[...].astype(F32)
        gc = _conv_fwd(gv, w_ref, kw, row) + b_ref[...]
        sc = _sigmoid(gc)
        da = da_ref[...].astype(F32)
        du_ref[...] = (da * gc * sc).astype(du_ref.dtype)
        dgc = da * u_ref[...].astype(F32) * (sc * (1.0 + gc * (1.0 - sc)))
        dgate, dw = _conv_bwd(gv, dgc, w_ref, kw, row)
        dg_ref[...] = dgate.astype(dg_ref.dtype)
        sm_out[0] = _rows_to_block(dw + [jnp.sum(dgc, axis=0, keepdims=True)], 8, cb)

    blk = pl.BlockSpec((s, cb), lambda i, j: (i, j))
    return pl.pallas_call(
        body, name=name, grid=(nb, nblk),
        in_specs=[blk, pl.BlockSpec((s, cb), lambda i, j: (i, nblk + j)),
                  pl.BlockSpec((kw, cb), lambda i, j: (0, j)), pl.BlockSpec((1, cb), lambda i, j: (0, j)), blk],
        out_specs=[blk, blk, pl.BlockSpec((1, 8, cb), lambda i, j: (i, 0, j))],
        out_shape=[jax.ShapeDtypeStruct((nb * s, dff), BF16), jax.ShapeDtypeStruct((nb * s, dff), BF16),
                   jax.ShapeDtypeStruct((nb, 8, dff), F32)],
        compiler_params=_cparams(("parallel", "parallel")),
    )(up, up, w, b.reshape(1, dff), dact)


def _dot(a, b, dims):
    return lax.dot_general(a, b, (dims, ((0,), (0,))), preferred_element_type=F32)


def _mm_nn(a, b):
    return _dot(a.astype(BF16), b.astype(BF16), ((2,), (1,)))


def _mm_nt(a, b):
    return _dot(a.astype(BF16), b.astype(BF16), ((2,), (2,)))


def _mm_tn(a, b):
    return _dot(a.astype(BF16), b.astype(BF16), ((1,), (1,)))


def _split3(x):
    hi = x.astype(BF16)
    rest = x - hi.astype(F32)
    mid = rest.astype(BF16)
    return hi, mid, (rest - mid.astype(F32)).astype(BF16)


def _mask_dot(mask, x, dims, mask_first):
    if mask_first:
        return sum(_dot(mask, p, dims) for p in _split3(x))
    return sum(_dot(p, mask, dims) for p in _split3(x))


@jax.custom_vjp
def _mask_nn(mask, x):
    return _mask_dot(mask, x, ((2,), (1,)), True)


def _mask_nn_fwd(mask, x):
    return _mask_nn(mask, x), mask


def _mask_nn_bwd(mask, ct):
    return jnp.zeros_like(mask), _mask_dot(mask, ct, ((1,), (1,)), True)


_mask_nn.defvjp(_mask_nn_fwd, _mask_nn_bwd)


@jax.custom_vjp
def _mask_tn(x, mask):
    return _mask_dot(mask, x, ((1,), (1,)), False)


def _mask_tn_fwd(x, mask):
    return _mask_tn(x, mask), mask


def _mask_tn_bwd(mask, ct):
    return _mask_dot(mask, ct, ((2,), (2,)), True), jnp.zeros_like(mask)


_mask_tn.defvjp(_mask_tn_fwd, _mask_tn_bwd)


GDN_ROWS = 256


def _gdn_chunk(qc, kc, vc, zc, braw, araw, alog, dtb, ng, state):
    nhead, r = qc.shape[0], qc.shape[1]
    ri = lax.broadcasted_iota(jnp.int32, (r, r), 0)
    ci = lax.broadcasted_iota(jnp.int32, (r, r), 1)
    same = (ri // CHUNK) == (ci // CHUNK)
    causal = same & (ri >= ci)
    strict = same & (ri > ci)
    eye = (ri == ci).astype(F32)
    row_chunk = lax.broadcasted_iota(jnp.int32, (r, HEAD), 0) // CHUNK
    per_head = lambda m: jnp.broadcast_to(m.astype(BF16), (nhead, r, r))

    q = qc * lax.rsqrt(jnp.sum(qc * qc, axis=-1, keepdims=True) + EPS) * (HEAD ** -0.5)
    k = kc * lax.rsqrt(jnp.sum(kc * kc, axis=-1, keepdims=True) + EPS)
    beta = _sigmoid(braw)
    g = -jnp.exp(alog) * _softplus(araw + dtb)

    g_w = jnp.broadcast_to(g, (nhead, r, HEAD))
    widen = lambda t: jnp.concatenate([t] * (r // HEAD), axis=2)
    gw = _mask_nn(per_head(causal), g_w)
    gi = widen(gw)
    gj = _mask_tn(widen(g_w), per_head(same & (ri <= ci)))
    decay = jnp.where(causal, jnp.exp(jnp.where(causal, gi - gj, 0.0)), 0.0)

    kb = k * beta
    vb = vc * beta
    lmat = jnp.where(strict, _mm_nt(kb, k) * decay, 0.0)
    x = -lmat
    ainv = eye + x
    p = 1
    while 2 * p < CHUNK:
        x = _mm_nn(x, x)
        ainv = _mm_nn(ainv, eye + x)
        p *= 2
    u = _mm_nn(ainv, vb)
    w = _mm_nn(ainv, kb * jnp.exp(gw))
    qk = jnp.where(causal, _mm_nt(q, k) * decay, 0.0)
    qg = q * jnp.exp(gw)

    o = jnp.zeros((nhead, r, HEAD), F32)
    for c in range(r // CHUNK):
        in_c = row_chunk == c
        glast = jnp.sum(jnp.where(in_c, g_w, 0.0), axis=1, keepdims=True)
        v_new = jnp.where(in_c, u - _mm_nn(w, state), 0.0)
        o = o + jnp.where(in_c, _mm_nn(qg, state), 0.0) + _mm_nn(qk, v_new)
        k_dec = jnp.where(in_c, k * jnp.exp(jnp.where(in_c, glast - gw, 0.0)), 0.0)
        state = state * jnp.exp(glast) + _mm_tn(k_dec, v_new)

    o = o * lax.rsqrt(jnp.mean(o * o, axis=-1, keepdims=True) + EPS) * ng
    o = o * (zc * _sigmoid(zc))
    return o, state


GDN_HEADS_FWD = 4
GDN_HEADS_BWD = 4


def _gdn_specs(s, nh, nqk, zcol, n_chunks, gh, single=False):
    rep = nh // nqk
    qw, vw = (gh // rep) * HEAD, gh * HEAD
    assert gh % rep == 0 and nh % gh == 0 and (nqk * HEAD) % qw == 0 and (2 * nqk * HEAD) % vw == 0 and zcol % vw == 0
    k0, v0, z0 = (nqk * HEAD) // qw, (2 * nqk * HEAD) // vw, zcol // vw
    mode = dict(pipeline_mode=pl.Buffered(1)) if single else {}
    return dict(
        q=pl.BlockSpec((s, qw), lambda b, j: (b, j), **mode),
        k=pl.BlockSpec((s, qw), lambda b, j: (b, k0 + j), **mode),
        v=pl.BlockSpec((s, vw), lambda b, j: (b, v0 + j), **mode),
        z=pl.BlockSpec((s, vw), lambda b, j: (b, z0 + j), **mode),
        ba=pl.BlockSpec((s, HEAD), lambda b, j: (b, 0)),
        gp=pl.BlockSpec((8, HEAD), lambda b, j: (0, 0)),
        qk_out=pl.BlockSpec((s, qw), lambda b, j: (b, j), **mode),
        head=pl.BlockSpec((s, vw), lambda b, j: (b, j), **mode),
        head_in=pl.BlockSpec((s, vw), lambda b, j: (b, j), **mode),
        st=pl.BlockSpec((1, gh, n_chunks, HEAD, HEAD), lambda b, j: (b, j, 0, 0, 0)),
        st_in=pl.BlockSpec((1, gh, n_chunks, HEAD, HEAD), lambda b, j: (b, j, 0, 0, 0), **mode),
    )


def _gdn_scalars(gp_ref, h):
    lane = lax.broadcasted_iota(jnp.int32, (1, HEAD), 1)
    sel = (lane == h).astype(F32)
    alog = jnp.sum(gp_ref[pl.ds(0, 1), :] * sel, axis=-1, keepdims=True)
    dtb = jnp.sum(gp_ref[pl.ds(1, 1), :] * sel, axis=-1, keepdims=True)
    return alog, dtb, sel


def _lanes(i):
    return pl.ds(i * HEAD, HEAD)


def gdn_fwd(qkvc, p, pba, gp, *, nb, s, nh, nqk, zcol, name, carry=None):
    rb = min(GDN_ROWS, s)
    n_chunks = s // rb
    gh = GDN_HEADS_FWD
    rep = nh // nqk
    sp = _gdn_specs(s, nh, nqk, zcol, n_chunks, gh)

    def body(q_ref, k_ref, v_ref, z_ref, ba_ref, gp_ref, o_ref, st_ref):
        h0 = pl.program_id(1) * gh
        ng = gp_ref[pl.ds(2, 1), :]
        lane = lax.broadcasted_iota(jnp.int32, (rb, HEAD), 1)
        heads = []
        for i in range(gh):
            alog, dtb, _ = _gdn_scalars(gp_ref, h0 + i)
            heads.append((alog, dtb, (lane == h0 + i).astype(F32), (lane == nh + h0 + i).astype(F32)))

        alogs = jnp.stack([hd[0] for hd in heads])
        dtbs = jnp.stack([hd[1] for hd in heads])

        def step(n, state):
            rows = pl.ds(pl.multiple_of(n * rb, rb), rb)
            ba = ba_ref[rows, :]
            qs = jnp.stack([q_ref[rows, _lanes(i // rep)] for i in range(gh)])
            ks = jnp.stack([k_ref[rows, _lanes(i // rep)] for i in range(gh)])
            vs = jnp.stack([v_ref[rows, _lanes(i)] for i in range(gh)])
            zs = jnp.stack([z_ref[rows, _lanes(i)] for i in range(gh)])
            braw = jnp.stack([jnp.sum(ba * hd[2], axis=-1, keepdims=True) for hd in heads])
            araw = jnp.stack([jnp.sum(ba * hd[3], axis=-1, keepdims=True) for hd in heads])
            o, new_state = _gdn_chunk(qs, ks, vs, zs, braw, araw, alogs, dtbs, ng, state)
            for i in range(gh):
                st_ref[0, i, n] = state[i]
                o_ref[rows, _lanes(i)] = o[i].astype(o_ref.dtype)
            return new_state

        lax.fori_loop(0, n_chunks, step, jnp.zeros((gh, HEAD, HEAD), F32))

    outs, got = carried_call(
        body, name=name, grid=(nb, nh // gh),
        in_specs=[sp["q"], sp["k"], sp["v"], sp["z"], sp["ba"], sp["gp"]],
        out_specs=[sp["head"], sp["st"]],
        out_shape=[jax.ShapeDtypeStruct((nb * s, nh * HEAD), BF16),
                   jax.ShapeDtypeStruct((nb, nh, n_chunks, HEAD, HEAD), F32)],
        scratch_shapes=[], semantics=("parallel", "parallel"), args=(qkvc, qkvc, qkvc, p, pba, gp), carry=carry)
    return outs + [got]


def gdn_bwd(qkvc, p, pba, gp, states, dout, *, nb, s, nh, nqk, zcol, name, carry=None):
    rb = min(GDN_ROWS, s)
    n_chunks = s // rb
    gh = GDN_HEADS_BWD
    rep = nh // nqk
    sp = _gdn_specs(s, nh, nqk, zcol, n_chunks, gh, single=True)

    def body(q_ref, k_ref, v_ref, z_ref, ba_ref, gp_ref, st_ref, do_ref,
             dq_ref, dk_ref, dv_ref, dz_ref, dba_ref, dgp_ref):
        h0 = pl.program_id(1) * gh
        ng = gp_ref[pl.ds(2, 1), :]
        lane = lax.broadcasted_iota(jnp.int32, (rb, HEAD), 1)
        heads = []
        for i in range(gh):
            alog, dtb, sel_row = _gdn_scalars(gp_ref, h0 + i)
            heads.append((alog, dtb, (lane == h0 + i).astype(F32), (lane == nh + h0 + i).astype(F32), sel_row))

        @pl.when(h0 == 0)
        def _():
            dba_ref[...] = jnp.zeros_like(dba_ref)
            dgp_ref[...] = jnp.zeros_like(dgp_ref)

        alogs = jnp.stack([hd[0] for hd in heads])
        dtbs = jnp.stack([hd[1] for hd in heads])

        def step(it, carry):
            dstate, dalog, ddtb, dng = carry
            n = n_chunks - 1 - it
            rows = pl.ds(pl.multiple_of(n * rb, rb), rb)
            ba = ba_ref[rows, :]
            qs = jnp.stack([q_ref[rows, _lanes(i // rep)] for i in range(gh)])
            ks = jnp.stack([k_ref[rows, _lanes(i // rep)] for i in range(gh)])
            vs = jnp.stack([v_ref[rows, _lanes(i)] for i in range(gh)])
            zs = jnp.stack([z_ref[rows, _lanes(i)] for i in range(gh)])
            dos = jnp.stack([do_ref[rows, _lanes(i)] for i in range(gh)])
            braw = jnp.stack([jnp.sum(ba * hd[2], axis=-1, keepdims=True) for hd in heads])
            araw = jnp.stack([jnp.sum(ba * hd[3], axis=-1, keepdims=True) for hd in heads])
            _, vjp = jax.vjp(_gdn_chunk, qs, ks, vs, zs, braw, araw, alogs, dtbs, ng, st_ref[0, :, n])
            gq, gk, gv, gz, gb, ga, galog, gdtb, gng, gstate = vjp((dos, dstate))
            dba = dba_ref[rows, :]
            for i in range(gh):
                dv_ref[rows, _lanes(i)] = gv[i]
                dz_ref[rows, _lanes(i)] = gz[i].astype(dz_ref.dtype)
                dba = dba + (gb[i] * heads[i][2] + ga[i] * heads[i][3])
            for j in range(gh // rep):
                dq_ref[rows, _lanes(j)] = sum(gq[i] for i in range(j * rep, (j + 1) * rep))
                dk_ref[rows, _lanes(j)] = sum(gk[i] for i in range(j * rep, (j + 1) * rep))
            dba_ref[rows, :] = dba
            return gstate, dalog + galog, ddtb + gdtb, dng + gng

        init = (jnp.zeros((gh, HEAD, HEAD), F32), jnp.zeros((gh, 1, 1), F32), jnp.zeros((gh, 1, 1), F32),
                jnp.zeros((1, HEAD), F32))
        _, dalog, ddtb, dng = lax.fori_loop(0, n_chunks, step, init)
        rows3 = [sum(dalog[i] * heads[i][4] for i in range(gh)), sum(ddtb[i] * heads[i][4] for i in range(gh)), dng]
        dgp_ref[0] += _rows_to_block(rows3, 8, HEAD)

    outs, got = carried_call(
        body, name=name, grid=(nb, nh // gh),
        in_specs=[sp["q"], sp["k"], sp["v"], sp["z"], sp["ba"], sp["gp"], sp["st_in"], sp["head_in"]],
        out_specs=[sp["qk_out"], sp["qk_out"], sp["head"], sp["head"], sp["ba"],
                   pl.BlockSpec((1, 8, HEAD), lambda b, j: (b, 0, 0))],
        out_shape=[jax.ShapeDtypeStruct((nb * s, nqk * HEAD), F32), jax.ShapeDtypeStruct((nb * s, nqk * HEAD), F32),
                   jax.ShapeDtypeStruct((nb * s, nh * HEAD), F32), jax.ShapeDtypeStruct((nb * s, nh * HEAD), BF16),
                   jax.ShapeDtypeStruct((nb * s, HEAD), F32), jax.ShapeDtypeStruct((nb, 8, HEAD), F32)],
        scratch_shapes=[], semantics=("parallel", "arbitrary"),
        args=(qkvc, qkvc, qkvc, p, pba, gp, states, dout), carry=carry)
    return outs + [got]


ADAM_LR = 0.001
ADAM_B1 = 0.9
ADAM_B2 = 0.999
ADAM_EPS = 1e-08
ADAM_WD = 0.01
ADAM_STEP = 10
EW_BLOCK_BYTES = 1 << 20


def _row_tile(rows, cols):
    for tr in (1024, 512, 256, 128, 64, 32, 16, 8):
        if rows % tr == 0 and tr * cols * 4 <= EW_BLOCK_BYTES:
            return tr
    return rows


def sum_slots(rs, *, name):
    nl = len(rs)
    n, rows, cols = rs[0].shape
    tr = _row_tile(rows, cols)
    nblk = rows // tr

    def body(*refs):
        o_ref = refs[nl]
        for l in range(nl):
            @pl.when(pl.program_id(0) == l)
            def _(l=l):
                acc = refs[l][0].astype(F32)
                for i in range(1, n):
                    acc = acc + refs[l][i].astype(F32)
                o_ref[0] = acc

    def in_map(l):
        return lambda li, i: (0, jnp.where(li == l, i, jnp.where(li < l, 0, nblk - 1)), 0)

    return pl.pallas_call(
        body, name=name, grid=(nl, nblk),
        in_specs=[pl.BlockSpec((n, tr, cols), in_map(l)) for l in range(nl)],
        out_specs=pl.BlockSpec((1, tr, cols), lambda li, i: (li, i, 0)),
        out_shape=jax.ShapeDtypeStruct((nl, rows, cols), F32),
        compiler_params=_cparams(("arbitrary", "arbitrary")),
    )(*rs)


def adamw(g_parts, w, m, v, *, name):
    shape = w.shape
    cols = shape[-1]
    to2d = lambda a: a.reshape(-1, cols)
    rows = to2d(w).shape[0]
    tr = _row_tile(rows, cols)
    npart = len(g_parts)
    c1 = 1.0 - ADAM_B1 ** ADAM_STEP
    c2 = 1.0 - ADAM_B2 ** ADAM_STEP

    def body(*refs):
        w_ref, m_ref, v_ref = refs[npart:npart + 3]
        g_ref, d_ref, nm_ref, nv_ref = refs[npart + 3:]
        g = refs[0][...]
        for i in range(1, npart):
            g = g + refs[i][...]
        nm = ADAM_B1 * m_ref[...] + (1.0 - ADAM_B1) * g
        nv = ADAM_B2 * v_ref[...] + (1.0 - ADAM_B2) * (g * g)
        g_ref[...] = g
        nm_ref[...] = nm
        nv_ref[...] = nv
        d_ref[...] = -ADAM_LR * ((nm / c1) / (jnp.sqrt(nv / c2) + ADAM_EPS) + ADAM_WD * w_ref[...])

    blk = pl.BlockSpec((tr, cols), lambda i: (i, 0))
    outs = pl.pallas_call(
        body, name=name, grid=(rows // tr,),
        in_specs=[blk] * (npart + 3),
        out_specs=[blk] * 4,
        out_shape=[jax.ShapeDtypeStruct((rows, cols), F32)] * 4,
        compiler_params=_cparams(("parallel",)),
    )(*[to2d(a) for a in g_parts], to2d(w), to2d(m), to2d(v))
    return tuple(o.reshape(shape) for o in outs)


def allreduce_small(vec, *, name):
    r = vec.shape[0]

    def body(v_ref, o_ref, slots, send_sems, recv_sems):
        x, y, c = lax.axis_index("x"), lax.axis_index("y"), lax.axis_index("c")
        me = 4 * x + 2 * y + c
        slots[me] = v_ref[...]
        copies = []
        for j in range(1, 8):
            px = 1 - x if j & 4 else x
            py = 1 - y if j & 2 else y
            pc = 1 - c if j & 1 else c
            rc = pltpu.make_async_remote_copy(src_ref=v_ref, dst_ref=slots.at[me], send_sem=send_sems.at[j - 1],
                                              recv_sem=recv_sems.at[j - 1], device_id=(px, py, pc), device_id_type=MESH)
            rc.start()
            copies.append(rc)
        for cp in copies:
            cp.wait()
        acc = slots[0]
        for i in range(1, 8):
            acc = acc + slots[i]
        o_ref[...] = acc

    vm = pl.BlockSpec(memory_space=pltpu.VMEM)
    return pl.pallas_call(
        body, name=name, in_specs=[vm], out_specs=vm,
        out_shape=jax.ShapeDtypeStruct((r, 128), F32),
        scratch_shapes=[pltpu.VMEM((8, r, 128), F32), pltpu.SemaphoreType.DMA((7,)), pltpu.SemaphoreType.DMA((7,))],
        compiler_params=pltpu.CompilerParams(vmem_limit_bytes=VMEM_LIMIT),
    )(vec)


WEIGHTS = ("mix_norm_g", "w_in", "conv_dw_w", "conv_dw_b", "conv_ln_g", "conv_ln_b", "conv_pw_w", "conv_pw_b",
           "gdn_conv_w", "gdn_a_log", "gdn_dt_bias", "gdn_norm_g", "w_out", "ffn_norm_g", "w_up", "ffn_conv_w",
           "ffn_conv_b", "w_down", "final_norm_g")
COL_SHARDED = ("w_in", "w_up", "conv_dw_w", "gdn_conv_w", "ffn_conv_w")
ROW_SHARDED = ("conv_pw_w", "w_out", "w_down")
BIG = ("w_in", "conv_pw_w", "w_out", "w_up", "w_down")
SMALL_CONV = ("conv_dw_w", "gdn_conv_w", "ffn_conv_w")


def _full_from_slots(name, part):
    if name in COL_SHARDED:
        r, cs = part.shape[1:]
        return jnp.transpose(part, (1, 0, 2)).reshape(r, 4 * cs)
    rs, c = part.shape[1:]
    return part.reshape(4 * rs, c)


def _slots_from_full(name, full):
    if name in COL_SHARDED:
        r, c = full.shape
        return jnp.transpose(full.reshape(r, 4, c // 4), (1, 0, 2))
    r, c = full.shape
    return full.reshape(4, r // 4, c)


def _pack(parts):
    flat = jnp.concatenate([p.reshape(-1).astype(F32) for p in parts])
    pad = (-flat.shape[0]) % 1024
    return jnp.pad(flat, (0, pad)).reshape(-1, 128)


def _unpack(vec, shapes):
    flat = vec.reshape(-1)
    out, pos = [], 0
    for shp in shapes:
        n = 1
        for d in shp:
            n *= d
        out.append(flat[pos:pos + n].reshape(shp))
        pos += n
    return out


def kernel(x, mix_norm_g, w_in, conv_dw_w, conv_dw_b, conv_ln_g, conv_ln_b, conv_pw_w, conv_pw_b, gdn_conv_w, gdn_a_log, gdn_dt_bias, gdn_norm_g, w_out, ffn_norm_g, w_up, ffn_conv_w, ffn_conv_b, w_down, final_norm_g, loss_target, m_mix_norm_g, m_w_in, m_conv_dw_w, m_conv_dw_b, m_conv_ln_g, m_conv_ln_b, m_conv_pw_w, m_conv_pw_b, m_gdn_conv_w, m_gdn_a_log, m_gdn_dt_bias, m_gdn_norm_g, m_w_out, m_ffn_norm_g, m_w_up, m_ffn_conv_w, m_ffn_conv_b, m_w_down, m_final_norm_g, v_mix_norm_g, v_w_in, v_conv_dw_w, v_conv_dw_b, v_conv_ln_g, v_conv_ln_b, v_conv_pw_w, v_conv_pw_b, v_gdn_conv_w, v_gdn_a_log, v_gdn_dt_bias, v_gdn_norm_g, v_w_out, v_ffn_norm_g, v_w_up, v_ffn_conv_w, v_ffn_conv_b, v_w_down, v_final_norm_g):
    wts = dict(zip(WEIGHTS, (mix_norm_g, w_in, conv_dw_w, conv_dw_b, conv_ln_g, conv_ln_b, conv_pw_w, conv_pw_b,
                             gdn_conv_w, gdn_a_log, gdn_dt_bias, gdn_norm_g, w_out, ffn_norm_g, w_up, ffn_conv_w,
                             ffn_conv_b, w_down, final_norm_g)))
    mom = dict(zip(WEIGHTS, (m_mix_norm_g, m_w_in, m_conv_dw_w, m_conv_dw_b, m_conv_ln_g, m_conv_ln_b, m_conv_pw_w,
                             m_conv_pw_b, m_gdn_conv_w, m_gdn_a_log, m_gdn_dt_bias, m_gdn_norm_g, m_w_out,
                             m_ffn_norm_g, m_w_up, m_ffn_conv_w, m_ffn_conv_b, m_w_down, m_final_norm_g)))
    var = dict(zip(WEIGHTS, (v_mix_norm_g, v_w_in, v_conv_dw_w, v_conv_dw_b, v_conv_ln_g, v_conv_ln_b, v_conv_pw_w,
                             v_conv_pw_b, v_gdn_conv_w, v_gdn_a_log, v_gdn_dt_bias, v_gdn_norm_g, v_w_out,
                             v_ffn_norm_g, v_w_up, v_ffn_conv_w, v_ffn_conv_b, v_w_down, v_final_norm_g)))

    nb, s, d = x.shape
    t = nb * s
    depth = mix_norm_g.shape[0]
    ch = conv_dw_b.shape[1]
    nh = gdn_a_log.shape[1]
    nqk = nh // 2
    kwid, vwid = nqk * HEAD, nh * HEAD
    main = 2 * ch + 2 * kwid + 2 * vwid
    qcol, zcol = 2 * ch, 2 * ch + 2 * kwid + vwid
    dff = ffn_conv_b.shape[1]
    my_xy = 2 * lax.axis_index("x") + lax.axis_index("y")

    shard = {(n, l): wts[n][l].astype(BF16) for n in BIG for l in range(depth)}
    gather = lambda *keys: SplitGather([shard[k] for k in keys])
    first, small = run_exchanges([gather(("w_in", 0)), Exchange([wts[n] for n in SMALL_CONV], gather=True)],
                                 name="gather_first")
    full = {("w_in", 0): _full_from_slots("w_in", first[0])}
    for n, g in zip(SMALL_CONV, small):
        for l in range(depth):
            full[n, l] = _full_from_slots(n, g[:, l])

    def arrived(keys, got):
        for (n, l), g in zip(keys, got):
            full[n, l] = _full_from_slots(n, g)

    xc = x.reshape(t, d)
    saved, lws = [], []
    for l in range(depth):
        w_in_f = full["w_in", l]
        w_main = w_in_f[:, :main]
        w_ba = jnp.pad(w_in_f[:, main:], ((0, 0), (0, HEAD - 2 * nh)))
        gp = (jnp.zeros((8, HEAD), F32).at[0, :nh].set(gdn_a_log[l]).at[1, :nh].set(gdn_dt_bias[l])
              .at[2].set(gdn_norm_g[l]))
        h = rmsnorm_fwd(xc, mix_norm_g[l], name=f"f{l}_norm1")
        keys = [("conv_pw_w", l), ("w_out", l)]
        p, got = matmul(h, w_main, name=f"f{l}_in_main", carry=gather(*keys))
        arrived(keys, got)
        pba = matmul(h, w_ba, name=f"f{l}_in_ba")
        u3 = conf_fwd(p, full["conv_dw_w", l], conv_dw_b[l], conv_ln_g[l], conv_ln_b[l], nb=nb, s=s,
                      name=f"f{l}_conf")
        out_a = matmul(u3, full["conv_pw_w", l], bias=conv_pw_b[l], out_dtype=BF16, name=f"f{l}_pw")
        qkvc = qkvconv_fwd(p, full["gdn_conv_w", l], col0=qcol, nb=nb, s=s, name=f"f{l}_qkvconv")
        keys = [("w_up", l)]
        out_b, states, got = gdn_fwd(qkvc, p, pba, gp, nb=nb, s=s, nh=nh, nqk=nqk, zcol=zcol, name=f"f{l}_gdn",
                                     carry=gather(*keys))
        arrived(keys, got)
        wout_a, wout_b = full["w_out", l][:ch], full["w_out", l][ch:]
        x1 = matmul(out_a, wout_a, res=xc, name=f"f{l}_out_a")
        x1 = matmul(out_b, wout_b, res=x1, name=f"f{l}_out_b")
        h2 = rmsnorm_fwd(x1, ffn_norm_g[l], name=f"f{l}_norm2")
        keys = [("w_down", l)]
        up, got = matmul(h2, full["w_up", l], out_dtype=BF16, name=f"f{l}_up", carry=gather(*keys))
        arrived(keys, got)
        act = ffn_act_fwd(up, full["ffn_conv_w", l], ffn_conv_b[l], nb=nb, s=s, name=f"f{l}_act")
        if l + 1 < depth:
            keys = [("w_in", l + 1)]
            x2, got = matmul(act, full["w_down", l], res=x1, name=f"f{l}_down", carry=gather(*keys))
            arrived(keys, got)
        else:
            x2 = matmul(act, full["w_down", l], res=x1, name=f"f{l}_down")
        saved.append(dict(x=xc, h=h, p=p, pba=pba, u3=u3, out_a=out_a, qkvc=qkvc, out_b=out_b, states=states,
                          x1=x1, h2=h2, up=up, act=act))
        lws.append(dict(w_main=w_main, w_ba=w_ba, pw=full["conv_pw_w", l], wout_a=wout_a, wout_b=wout_b,
                        wup=full["w_up", l], wdown=full["w_down", l], dw_w=full["conv_dw_w", l],
                        gconv_w=full["gdn_conv_w", l], fconv_w=full["ffn_conv_w", l], gp=gp))
        xc = x2

    loss_blk, dx, dxb, dgf = loss_head(xc, final_norm_g, loss_target.reshape(t, d), name="loss_head")

    stacks, received = {}, {}
    scatter = lambda *keys: Exchange([stacks[k] for k in keys], gather=False)

    def produced(n, l, grad, halves=False):
        slots = _slots_from_full(n, grad).astype(BF16)
        if halves:
            half = slots.shape[1] // 2
            stacks[n, l, 0], stacks[n, l, 1] = slots[:, :half], slots[:, half:]
        else:
            stacks[n, l] = slots

    def landed(keys, got):
        for k, g in zip(keys, got):
            received[k] = g

    def carrying(keys, call, **kw):
        res = call(carry=scatter(*keys), **kw)
        landed(keys, res[-1])
        return res[0] if len(res) == 2 else res[:-1]

    small_grads = {n: [None] * depth for n in WEIGHTS if n not in BIG and n != "final_norm_g"}
    for l in reversed(range(depth)):
        lw, sv = lws[l], saved[l]
        dact_call = functools.partial(matmul, dxb, lw["wdown"], tb=True, out_dtype=BF16, name=f"b{l}_dact")
        dact = carrying([("w_in", l + 1, 1)], dact_call) if l + 1 < depth else dact_call()
        produced("w_down", l, matmul(sv["act"], dxb, ta=True, out_dtype=BF16, name=f"b{l}_dwdown"), halves=True)
        dgate, dupv, fpart = ffn_act_bwd(sv["up"], lw["fconv_w"], ffn_conv_b[l], dact, nb=nb, s=s, name=f"b{l}_act")
        dh2 = carrying([("w_down", l, 0)], functools.partial(matmul, dgate, lw["wup"], tb=True, name=f"b{l}_dh2_gate"))
        dh2 = carrying([("w_down", l, 1)], functools.partial(matmul, dupv, lw["wup"], tb=True, b_koff=dff, res=dh2,
                                                             name=f"b{l}_dh2_up"))
        produced("w_up", l, jnp.concatenate(
            [matmul(sv["h2"], dgate, ta=True, out_dtype=BF16, name=f"b{l}_dwup_gate"),
             matmul(sv["h2"], dupv, ta=True, out_dtype=BF16, name=f"b{l}_dwup_up")], axis=1))
        dx1, dx1b, dg2 = rmsnorm_bwd(sv["x1"], ffn_norm_g[l], dh2, dx, name=f"b{l}_norm2")
        fsum = jnp.sum(fpart, axis=0)
        small_grads["ffn_norm_g"][l] = dg2[0]
        small_grads["ffn_conv_w"][l] = fsum[:ffn_conv_w.shape[1]]
        small_grads["ffn_conv_b"][l] = fsum[ffn_conv_w.shape[1]]
        dout_a = matmul(dx1b, lw["wout_a"], tb=True, out_dtype=BF16, name=f"b{l}_dout_a")
        dout_b = matmul(dx1b, lw["wout_b"], tb=True, name=f"b{l}_dout_b")
        produced("w_out", l, jnp.concatenate(
            [matmul(sv["out_a"], dx1b, ta=True, out_dtype=BF16, name=f"b{l}_dwout_a"),
             matmul(sv["out_b"], dx1b, ta=True, out_dtype=BF16, name=f"b{l}_dwout_b")], axis=0))
        du3 = matmul(dout_a, lw["pw"], tb=True, name=f"b{l}_du3")
        produced("conv_pw_w", l, matmul(sv["u3"], dout_a, ta=True, out_dtype=BF16, name=f"b{l}_dwpw"))
        dval, dagate, cw_part, cs_part = carrying(
            [("w_out", l), ("conv_pw_w", l)],
            functools.partial(conf_bwd, sv["p"], lw["dw_w"], conv_dw_b[l], conv_ln_g[l], conv_ln_b[l], du3, dout_a,
                              nb=nb, s=s, name=f"b{l}_conf"))
        csum = jnp.sum(cs_part, axis=0)
        small_grads["conv_dw_w"][l] = jnp.sum(cw_part, axis=0)
        small_grads["conv_dw_b"][l] = csum[0]
        small_grads["conv_ln_g"][l] = csum[1]
        small_grads["conv_ln_b"][l] = csum[2]
        small_grads["conv_pw_b"][l] = csum[3]
        dq, dk, dv, dz, dpba, dgp = carrying(
            [("w_up", l)],
            functools.partial(gdn_bwd, sv["qkvc"], sv["p"], sv["pba"], lw["gp"], sv["states"], dout_b,
                              nb=nb, s=s, nh=nh, nqk=nqk, zcol=zcol, name=f"b{l}_gdn"))
        dqkv, gw_part = qkvconv_bwd(sv["p"], lw["gconv_w"], jnp.concatenate([dq, dk, dv], axis=1),
                                    col0=qcol, nb=nb, s=s, name=f"b{l}_qkvconv")
        gsum = jnp.sum(dgp, axis=0)
        small_grads["gdn_conv_w"][l] = jnp.sum(gw_part, axis=0)[:gdn_conv_w.shape[1]]
        small_grads["gdn_a_log"][l] = gsum[0, :nh]
        small_grads["gdn_dt_bias"][l] = gsum[1, :nh]
        small_grads["gdn_norm_g"][l] = gsum[2]
        dp = jnp.concatenate([dval, dagate, dqkv, dz], axis=1)
        dw_main = matmul(sv["h"], dp, ta=True, out_dtype=BF16, name=f"b{l}_dwin_main")
        dw_ba = matmul(sv["h"], dpba, ta=True, out_dtype=BF16, name=f"b{l}_dwin_ba")
        produced("w_in", l, jnp.concatenate([dw_main, dw_ba[:, :2 * nh]], axis=1), halves=True)
        dh = carrying([("w_in", l, 0)], functools.partial(matmul, dp, lw["w_main"], tb=True, name=f"b{l}_dh_main"))
        dh = matmul(dpba, lw["w_ba"], tb=True, res=dh, name=f"b{l}_dh_ba")
        dx, dxb, dg1 = rmsnorm_bwd(sv["x"], mix_norm_g[l], dh, dx1, name=f"b{l}_norm1")
        small_grads["mix_norm_g"][l] = dg1[0]

    def summed(n):
        parts = [received[k] for l in range(depth) for k in ([(n, l)] if (n, l) in received else [(n, l, 0), (n, l, 1)])]
        return sum_slots(parts, name=f"sum_{n}").reshape(wts[n].shape)

    early = [n for n in BIG if n != "w_in"]
    partial = {n: summed(n) for n in early}
    swapped, last = run_exchanges([SiblingSwap([partial[n] for n in early]),
                                   Exchange([stacks["w_in", 0, 1]], gather=False)], name="swap_and_scatter_last")
    other = dict(zip(early, swapped))
    received["w_in", 0, 1] = last[0]
    partial["w_in"] = summed("w_in")
    other["w_in"] = run_exchanges([SiblingSwap([partial["w_in"]])], name="swap_w_in")[0][0]

    grads, deltas, new_m, new_v = {}, {}, {}, {}
    for n in BIG:
        grads[n], deltas[n], new_m[n], new_v[n] = adamw([partial[n], other[n]], wts[n], mom[n], var[n],
                                                        name=f"adamw_{n}")

    small_names = [n for n in WEIGHTS if n not in BIG]
    small_full = [jnp.stack(small_grads[n]) if n != "final_norm_g" else dgf[0] for n in small_names]
    packed = _pack(small_full + [loss_blk[0, :1]])
    reduced = allreduce_small(packed, name="allreduce_small")
    parts = _unpack(reduced, [a.shape for a in small_full] + [(1,)])
    loss = parts[-1][0]
    for n, g in zip(small_names, parts[:-1]):
        if n in SMALL_CONV:
            wid = wts[n].shape[-1]
            g = lax.dynamic_slice_in_dim(g, my_xy * wid, wid, axis=g.ndim - 1)
        grads[n], deltas[n], new_m[n], new_v[n] = adamw([g], wts[n], mom[n], var[n], name=f"adamw_{n}")

    return (loss, dx.reshape(nb, s, d), *[grads[n] for n in WEIGHTS], *[deltas[n] for n in WEIGHTS],
            *[new_m[n] for n in WEIGHTS], *[new_v[n] for n in WEIGHTS])
```

```python
import functools

import jax
import jax.numpy as jnp
from jax import lax
from jax.experimental import pallas as pl
from jax.experimental.pallas import tpu as pltpu

F32 = jnp.float32
BF16 = jnp.bfloat16
EPS = 1e-6
CHUNK = 64
HEAD = 128
HIGHEST = lax.Precision.HIGHEST
VMEM_LIMIT = 56 * 1024 * 1024
MM_VMEM_BUDGET = 48 * 1024 * 1024


def _pick(dim, cands):
    for c in cands:
        if dim % c == 0:
            return c
    return dim


def _cparams(sem):
    return pltpu.CompilerParams(dimension_semantics=sem, vmem_limit_bytes=VMEM_LIMIT)


MESH = pl.DeviceIdType.MESH
ANY = pl.BlockSpec(memory_space=pl.ANY)


def _xy_peers():
    x, y = lax.axis_index("x"), lax.axis_index("y")
    peers = []
    for fx, fy in ((0, 1), (1, 0), (1, 1)):
        px = 1 - x if fx else x
        py = 1 - y if fy else y
        peers.append((2 * px + py, px, py))
    return 2 * x + y, peers


class Exchange:
    def __init__(self, arrs, gather):
        self.arrs, self.gather, self.n = list(arrs), gather, len(arrs)
        self.out_shape = [jax.ShapeDtypeStruct((4,) + (a.shape if gather else a.shape[1:]), a.dtype) for a in arrs]
        self.scratch = [pltpu.SemaphoreType.DMA((3 * self.n,)), pltpu.SemaphoreType.DMA((3 * self.n,)),
                        pltpu.SemaphoreType.DMA((self.n,))]

    def copies(self, ins, outs, send_sems, recv_sems, local_sems):
        me, peers = _xy_peers()
        c = lax.axis_index("c")
        out = []
        for k in range(self.n):
            out.append(pltpu.make_async_copy(ins[k] if self.gather else ins[k].at[me], outs[k].at[me],
                                             local_sems.at[k]))
            for j, (slot, px, py) in enumerate(peers):
                out.append(pltpu.make_async_remote_copy(
                    src_ref=ins[k] if self.gather else ins[k].at[slot], dst_ref=outs[k].at[me],
                    send_sem=send_sems.at[3 * k + j], recv_sem=recv_sems.at[3 * k + j],
                    device_id=(px, py, c), device_id_type=MESH))
        return out

    def start(self, ins, outs, sems):
        for cp in self.copies(ins, outs, *sems):
            cp.start()

    def finish(self, ins, outs, sems):
        for cp in self.copies(ins, outs, *sems):
            cp.wait()


class SplitGather:
    def __init__(self, arrs):
        self.arrs, self.n = list(arrs), len(arrs)
        assert all(a.ndim == 2 and a.shape[0] % 32 == 0 for a in arrs)
        self.out_shape = [jax.ShapeDtypeStruct((4,) + a.shape, a.dtype) for a in arrs]
        dma = pltpu.SemaphoreType.DMA
        self.scratch = [dma((3 * self.n,)), dma((3 * self.n,)), dma((3 * self.n,)), dma((3 * self.n,)), dma((self.n,))]

    def _half(self, k, c):
        half = self.arrs[k].shape[0] // 2
        return pl.ds(pl.multiple_of(c * half, 16), half)

    def over_ici(self, ins, outs, sems):
        me, peers = _xy_peers()
        c = lax.axis_index("c")
        out = []
        for k in range(self.n):
            out.append(pltpu.make_async_copy(ins[k], outs[k].at[me], sems[4].at[k]))
            for j, (slot, px, py) in enumerate(peers):
                out.append(pltpu.make_async_remote_copy(
                    src_ref=ins[k].at[self._half(k, c)], dst_ref=outs[k].at[me, self._half(k, c)],
                    send_sem=sems[0].at[3 * k + j], recv_sem=sems[1].at[3 * k + j],
                    device_id=(px, py, c), device_id_type=MESH))
        return out

    def over_d2d(self, outs, sems):
        _, peers = _xy_peers()
        x, y, c = lax.axis_index("x"), lax.axis_index("y"), lax.axis_index("c")
        out = []
        for k in range(self.n):
            for j, (slot, _, _) in enumerate(peers):
                rows = outs[k].at[slot, self._half(k, c)]
                out.append(pltpu.make_async_remote_copy(
                    src_ref=rows, dst_ref=rows, send_sem=sems[2].at[3 * k + j], recv_sem=sems[3].at[3 * k + j],
                    device_id=(x, y, 1 - c), device_id_type=MESH))
        return out

    def start(self, ins, outs, sems):
        for cp in self.over_ici(ins, outs, sems):
            cp.start()

    def finish(self, ins, outs, sems):
        for cp in self.over_ici(ins, outs, sems):
            cp.wait()
        passed = self.over_d2d(outs, sems)
        for cp in passed:
            cp.start()
        for cp in passed:
            cp.wait()


class SiblingSwap:
    def __init__(self, arrs):
        self.arrs, self.n = list(arrs), len(arrs)
        self.out_shape = [jax.ShapeDtypeStruct(a.shape, a.dtype) for a in arrs]
        self.scratch = [pltpu.SemaphoreType.DMA((self.n,)), pltpu.SemaphoreType.DMA((self.n,))]

    def copies(self, ins, outs, sems):
        peer = (lax.axis_index("x"), lax.axis_index("y"), 1 - lax.axis_index("c"))
        return [pltpu.make_async_remote_copy(src_ref=ins[k], dst_ref=outs[k], send_sem=sems[0].at[k],
                                             recv_sem=sems[1].at[k], device_id=peer, device_id_type=MESH)
                for k in range(self.n)]

    def start(self, ins, outs, sems):
        for cp in self.copies(ins, outs, sems):
            cp.start()

    def finish(self, ins, outs, sems):
        for cp in self.copies(ins, outs, sems):
            cp.wait()


def _split_refs(refs, carries, attr):
    groups, pos = [], 0
    for cr in carries:
        n = len(getattr(cr, attr))
        groups.append(refs[pos:pos + n])
        pos += n
    return groups


def carried_call(body, *, name, grid, in_specs, out_specs, out_shape, scratch_shapes, semantics, args, carry=None):
    n_in, n_out, n_scr = len(in_specs), len(out_specs), len(scratch_shapes)
    if carry is None:
        outs = pl.pallas_call(body, name=name, grid=grid, in_specs=in_specs, out_specs=out_specs, out_shape=out_shape,
                              scratch_shapes=scratch_shapes, compiler_params=_cparams(semantics))(*args)
        return list(outs), []
    carries = list(carry) if isinstance(carry, (list, tuple)) else [carry]
    n = sum(cr.n for cr in carries)

    def wrapped(*refs):
        ins, cin = refs[:n_in], _split_refs(refs[n_in:n_in + n], carries, "arrs")
        outs = refs[n_in + n:n_in + n + n_out]
        cout = _split_refs(refs[n_in + n + n_out:n_in + 2 * n + n_out], carries, "arrs")
        scratch = refs[n_in + 2 * n + n_out:n_in + 2 * n + n_out + n_scr]
        sems = _split_refs(refs[n_in + 2 * n + n_out + n_scr:], carries, "scratch")
        ids = [pl.program_id(i) for i in range(len(grid))]
        first = functools.reduce(jnp.logical_and, [i == 0 for i in ids])
        last = functools.reduce(jnp.logical_and, [i == g - 1 for i, g in zip(ids, grid)])

        @pl.when(first)
        def _():
            for cr, i, o, s in zip(carries, cin, cout, sems):
                cr.start(i, o, s)

        body(*ins, *outs, *scratch)

        @pl.when(last)
        def _():
            for cr, i, o, s in zip(carries, cin, cout, sems):
                cr.finish(i, o, s)

    res = pl.pallas_call(
        wrapped, name=name, grid=grid, in_specs=list(in_specs) + [ANY] * n, out_specs=list(out_specs) + [ANY] * n,
        out_shape=list(out_shape) + [s for cr in carries for s in cr.out_shape],
        scratch_shapes=list(scratch_shapes) + [s for cr in carries for s in cr.scratch],
        compiler_params=_cparams(tuple("arbitrary" for _ in grid)),
    )(*args, *[a for cr in carries for a in cr.arrs])
    got = _split_refs(list(res[n_out:]), carries, "arrs")
    return list(res[:n_out]), (got if isinstance(carry, (list, tuple)) else got[0])


def run_exchanges(carries, *, name):
    n = sum(cr.n for cr in carries)

    def body(*refs):
        cin = _split_refs(refs[:n], carries, "arrs")
        cout = _split_refs(refs[n:2 * n], carries, "arrs")
        sems = _split_refs(refs[2 * n:], carries, "scratch")
        for cr, i, o, s in zip(carries, cin, cout, sems):
            cr.start(i, o, s)
        for cr, i, o, s in zip(carries, cin, cout, sems):
            cr.finish(i, o, s)

    res = pl.pallas_call(body, name=name, in_specs=[ANY] * n, out_specs=[ANY] * n,
                         out_shape=[s for cr in carries for s in cr.out_shape],
                         scratch_shapes=[s for cr in carries for s in cr.scratch],
                         )(*[a for cr in carries for a in cr.arrs])
    return _split_refs(list(res), carries, "arrs")


def matmul(a, b, *, ta=False, tb=False, out_dtype=F32, res=None, bias=None, b_koff=0, name, carry=None):
    if ta:
        kdim, m = a.shape
    else:
        m, kdim = a.shape
    if tb:
        n, kb = b.shape
    else:
        kb, n = b.shape
    assert kb >= kdim + b_koff, (a.shape, b.shape, ta, tb)
    tn = _pick(n, (1024, 512, 256, 128))
    out_bytes = jnp.dtype(out_dtype).itemsize

    def vmem_bytes(tm_, tk_):
        blocks = a.dtype.itemsize * tm_ * tk_ + b.dtype.itemsize * tk_ * tn + out_bytes * tm_ * tn
        blocks += 4 * tm_ * tn if res is not None else 0
        temps = 4 * tm_ * tn + (2 * tk_ * tn if tb else 0)
        return 2 * blocks + temps + (4 * tm_ * tn if tk_ < kdim else 0)

    def longest_k(tm_):
        return next(c for c in (kdim, 4096, 2816, 2560, 2048, 1024, 512, 256, 128)
                    if kdim % c == 0 and b_koff % c == 0 and c % 128 == 0
                    and (vmem_bytes(tm_, c) <= MM_VMEM_BUDGET or c == 128))

    tall = [c for c in (1024, 512) if m % c == 0] or [_pick(m, (256, 128))]
    tm = max(tall, key=lambda c: (longest_k(c), c))
    tk = longest_k(tm)
    nk = kdim // tk
    ko = b_koff // tk
    dims = (((0 if ta else 1,), (1 if tb else 0,)), ((), ()))

    def body(*refs):
        a_ref, b_ref = refs[0], refs[1]
        pos = 2
        bias_ref = res_ref = None
        if bias is not None:
            bias_ref = refs[pos]
            pos += 1
        if res is not None:
            res_ref = refs[pos]
            pos += 1
        o_ref = refs[pos]
        part = lax.dot_general(a_ref[...].astype(BF16), b_ref[...].astype(BF16), dims, preferred_element_type=F32)

        def finish(r):
            if bias_ref is not None:
                r = r + bias_ref[...]
            if res_ref is not None:
                r = r + res_ref[...]
            o_ref[...] = r.astype(o_ref.dtype)

        if nk == 1:
            finish(part)
            return
        acc_ref = refs[pos + 1]
        k = pl.program_id(2)

        @pl.when(k == 0)
        def _():
            acc_ref[...] = part

        @pl.when((k > 0) & (k < nk - 1))
        def _():
            acc_ref[...] += part

        @pl.when(k == nk - 1)
        def _():
            finish(acc_ref[...] + part)

    a_spec = pl.BlockSpec((tk, tm), lambda i, j, k: (k, i)) if ta else pl.BlockSpec((tm, tk), lambda i, j, k: (i, k))
    b_spec = (pl.BlockSpec((tn, tk), lambda i, j, k: (j, k + ko)) if tb
              else pl.BlockSpec((tk, tn), lambda i, j, k: (k + ko, j)))
    in_specs = [a_spec, b_spec]
    args = [a, b]
    if bias is not None:
        in_specs.append(pl.BlockSpec((1, tn), lambda i, j, k: (0, j)))
        args.append(bias.reshape(1, n).astype(F32))
    if res is not None:
        in_specs.append(pl.BlockSpec((tm, tn), lambda i, j, k: (i, j)))
        args.append(res)
    outs, got = carried_call(
        body, name=name, grid=(m // tm, n // tn, nk), in_specs=in_specs,
        out_specs=[pl.BlockSpec((tm, tn), lambda i, j, k: (i, j))],
        out_shape=[jax.ShapeDtypeStruct((m, n), out_dtype)],
        scratch_shapes=[pltpu.VMEM((tm, tn), F32)] if nk > 1 else [],
        semantics=("parallel", "parallel", "arbitrary"), args=args, carry=carry)
    return outs[0] if carry is None else (outs[0], got)


def _sigmoid(x):
    return 1.0 / (1.0 + jnp.exp(-x))


def _softplus(x):
    return jnp.maximum(x, 0.0) + jnp.log(1.0 + jnp.exp(-jnp.abs(x)))


def _shift_back(x, s, row):
    if s == 0:
        return x
    return jnp.where(row >= s, pltpu.roll(x, s, 0), 0.0)


def _shift_fwd(x, s, row):
    if s == 0:
        return x
    n = x.shape[0]
    return jnp.where(row < n - s, pltpu.roll(x, n - s, 0), 0.0)


def _conv_fwd(x, w_ref, kw, row):
    acc = x * w_ref[pl.ds(kw - 1, 1), :]
    for s in range(1, kw):
        acc = acc + _shift_back(x, s, row) * w_ref[pl.ds(kw - 1 - s, 1), :]
    return acc


def _conv_bwd(x, dy, w_ref, kw, row):
    dx = dy * w_ref[pl.ds(kw - 1, 1), :]
    dw = [None] * kw
    dw[kw - 1] = jnp.sum(dy * x, axis=0, keepdims=True)
    for s in range(1, kw):
        dx = dx + _shift_fwd(dy, s, row) * w_ref[pl.ds(kw - 1 - s, 1), :]
        dw[kw - 1 - s] = jnp.sum(dy * _shift_back(x, s, row), axis=0, keepdims=True)
    return dx, dw


def _rows_to_block(rows, nrows, width):
    rid = lax.broadcasted_iota(jnp.int32, (nrows, width), 0)
    out = jnp.zeros((nrows, width), F32)
    for i, r in enumerate(rows):
        out = jnp.where(rid == i, r, out)
    return out


def rmsnorm_fwd(x, g, *, name):
    t, d = x.shape
    tm = _pick(t, (256, 128))

    def body(x_ref, g_ref, o_ref):
        xv = x_ref[...]
        r = lax.rsqrt(jnp.mean(xv * xv, axis=-1, keepdims=True) + EPS)
        o_ref[...] = (xv * r * g_ref[...]).astype(o_ref.dtype)

    return pl.pallas_call(
        body, name=name, grid=(t // tm,),
        in_specs=[pl.BlockSpec((tm, d), lambda i: (i, 0)), pl.BlockSpec((1, d), lambda i: (0, 0))],
        out_specs=pl.BlockSpec((tm, d), lambda i: (i, 0)),
        out_shape=jax.ShapeDtypeStruct((t, d), BF16),
        compiler_params=_cparams(("parallel",)),
    )(x, g.reshape(1, d))


def rmsnorm_bwd(x, g, dh, dres, *, name):
    t, d = x.shape
    tm = _pick(t, (256, 128))

    def body(x_ref, g_ref, dh_ref, dres_ref, dx_ref, dxb_ref, dg_ref):
        xv = x_ref[...]
        r = lax.rsqrt(jnp.mean(xv * xv, axis=-1, keepdims=True) + EPS)
        xh = xv * r
        dy = dh_ref[...]
        dxh = dy * g_ref[...]
        dx = dres_ref[...] + r * (dxh - xh * jnp.mean(dxh * xh, axis=-1, keepdims=True))
        dx_ref[...] = dx
        dxb_ref[...] = dx.astype(BF16)

        @pl.when(pl.program_id(0) == 0)
        def _():
            dg_ref[...] = jnp.zeros_like(dg_ref)

        dg_ref[...] += jnp.sum(dy * xh, axis=0, keepdims=True)

    row = pl.BlockSpec((tm, d), lambda i: (i, 0))
    vec = pl.BlockSpec((1, d), lambda i: (0, 0))
    return pl.pallas_call(
        body, name=name, grid=(t // tm,),
        in_specs=[row, vec, row, row],
        out_specs=[row, row, vec],
        out_shape=[jax.ShapeDtypeStruct((t, d), F32), jax.ShapeDtypeStruct((t, d), BF16),
                   jax.ShapeDtypeStruct((1, d), F32)],
        compiler_params=_cparams(("arbitrary",)),
    )(x, g.reshape(1, d), dh, dres)


def loss_head(x, g, target, *, name):
    t, d = x.shape
    tm = _pick(t, (256, 128))

    def body(x_ref, g_ref, tg_ref, loss_ref, dx_ref, dxb_ref, dg_ref):
        xv = x_ref[...]
        r = lax.rsqrt(jnp.mean(xv * xv, axis=-1, keepdims=True) + EPS)
        xh = xv * r
        err = xh * g_ref[...] - tg_ref[...]
        dy = err * (1.0 / d)
        dxh = dy * g_ref[...]
        dx = r * (dxh - xh * jnp.mean(dxh * xh, axis=-1, keepdims=True))
        dx_ref[...] = dx
        dxb_ref[...] = dx.astype(BF16)

        @pl.when(pl.program_id(0) == 0)
        def _():
            dg_ref[...] = jnp.zeros_like(dg_ref)
            loss_ref[...] = jnp.zeros_like(loss_ref)

        dg_ref[...] += jnp.sum(dy * xh, axis=0, keepdims=True)
        part = jnp.sum(jnp.sum(err * err, axis=-1, keepdims=True), axis=0, keepdims=True) * (0.5 / d)
        loss_ref[...] += jnp.broadcast_to(part, loss_ref.shape)

    row = pl.BlockSpec((tm, d), lambda i: (i, 0))
    vec = pl.BlockSpec((1, d), lambda i: (0, 0))
    return pl.pallas_call(
        body, name=name, grid=(t // tm,),
        in_specs=[row, vec, row],
        out_specs=[pl.BlockSpec((8, 128), lambda i: (0, 0)), row, row, vec],
        out_shape=[jax.ShapeDtypeStruct((8, 128), F32), jax.ShapeDtypeStruct((t, d), F32),
                   jax.ShapeDtypeStruct((t, d), BF16), jax.ShapeDtypeStruct((1, d), F32)],
        compiler_params=_cparams(("arbitrary",)),
    )(x, g.reshape(1, d), target)


def _conf_forward_parts(val, gate, w_ref, b, lg, lb, kw, row, u1=None):
    sg = _sigmoid(gate)
    u0 = val * sg
    if u1 is None:
        u1 = _conv_fwd(u0, w_ref, kw, row) + b
    mu = jnp.mean(u1, axis=-1, keepdims=True)
    xc = u1 - mu
    rs = lax.rsqrt(jnp.mean(xc * xc, axis=-1, keepdims=True) + EPS)
    xh = xc * rs
    u2 = xh * lg + lb
    s2 = _sigmoid(u2)
    return sg, u0, u1, rs, xh, u2, s2


def conf_fwd(p, dw_w, dw_b, ln_g, ln_b, *, nb, s, name, carry=None):
    kw, ch = dw_w.shape
    ng = ch // HEAD

    def body(val_ref, gate_ref, w_ref, b_ref, lg_ref, lb_ref, o_ref, u1_ref):
        row = lax.broadcasted_iota(jnp.int32, (s, HEAD), 0)
        _, _, u1, _, _, u2, s2 = _conf_forward_parts(val_ref[...], gate_ref[...], w_ref, b_ref[...], lg_ref[...],
                                                     lb_ref[...], kw, row)
        o_ref[...] = (u2 * s2).astype(o_ref.dtype)
        u1_ref[...] = u1

    vec = pl.BlockSpec((1, HEAD), lambda b, g: (0, g))
    blk = pl.BlockSpec((s, HEAD), lambda b, g: (b, g))
    outs, got = carried_call(
        body, name=name, grid=(nb, ng),
        in_specs=[blk, pl.BlockSpec((s, HEAD), lambda b, g: (b, ng + g)),
                  pl.BlockSpec((kw, HEAD), lambda b, g: (0, g)), vec, vec, vec],
        out_specs=[blk, blk],
        out_shape=[jax.ShapeDtypeStruct((nb * s, ch), BF16), jax.ShapeDtypeStruct((nb * s, ch), F32)],
        scratch_shapes=[], semantics=("parallel", "parallel"),
        args=(p, p, dw_w, dw_b.reshape(1, ch), ln_g.reshape(1, ch), ln_b.reshape(1, ch)), carry=carry)
    return outs + [got]


def conf_bwd(p, u1, dw_w, dw_b, ln_g, ln_b, du3, dout_a, *, nb, s, name, carry=None):
    kw, ch = dw_w.shape
    ng = ch // HEAD

    def body(val_ref, gate_ref, u1_ref, w_ref, b_ref, lg_ref, lb_ref, du3_ref, doa_ref, dval_ref, dgate_ref, dw_out,
             sm_out):
        row = lax.broadcasted_iota(jnp.int32, (s, HEAD), 0)
        val = val_ref[...]
        sg, u0, _, rs, xh, u2, s2 = _conf_forward_parts(val, gate_ref[...], w_ref, b_ref[...], lg_ref[...],
                                                        lb_ref[...], kw, row, u1=u1_ref[...])
        du2 = du3_ref[...] * (s2 * (1.0 + u2 * (1.0 - s2)))
        dlg = jnp.sum(du2 * xh, axis=0, keepdims=True)
        dlb = jnp.sum(du2, axis=0, keepdims=True)
        dxh = du2 * lg_ref[...]
        du1 = rs * (dxh - jnp.mean(dxh, axis=-1, keepdims=True) - xh * jnp.mean(dxh * xh, axis=-1, keepdims=True))
        ddb = jnp.sum(du1, axis=0, keepdims=True)
        du0, dw = _conv_bwd(u0, du1, w_ref, kw, row)
        dval_ref[...] = (du0 * sg).astype(dval_ref.dtype)
        dgate_ref[...] = (du0 * val * sg * (1.0 - sg)).astype(dgate_ref.dtype)
        for k in range(kw):
            dw_out[0, pl.ds(k, 1), :] = dw[k]
        dpb = jnp.sum(doa_ref[...].astype(F32), axis=0, keepdims=True)
        sm_out[0] = _rows_to_block([ddb, dlg, dlb, dpb], 8, HEAD)

    vec = pl.BlockSpec((1, HEAD), lambda b, g: (0, g))
    blk = pl.BlockSpec((s, HEAD), lambda b, g: (b, g))
    outs, got = carried_call(
        body, name=name, grid=(nb, ng),
        in_specs=[blk, pl.BlockSpec((s, HEAD), lambda b, g: (b, ng + g)), blk,
                  pl.BlockSpec((kw, HEAD), lambda b, g: (0, g)), vec, vec, vec, blk, blk],
        out_specs=[blk, blk, pl.BlockSpec((1, kw, HEAD), lambda b, g: (b, 0, g)),
                   pl.BlockSpec((1, 8, HEAD), lambda b, g: (b, 0, g))],
        out_shape=[jax.ShapeDtypeStruct((nb * s, ch), BF16), jax.ShapeDtypeStruct((nb * s, ch), BF16),
                   jax.ShapeDtypeStruct((nb, kw, ch), F32), jax.ShapeDtypeStruct((nb, 8, ch), F32)],
        scratch_shapes=[], semantics=("parallel", "parallel"),
        args=(p, p, u1, dw_w, dw_b.reshape(1, ch), ln_g.reshape(1, ch), ln_b.reshape(1, ch), du3, dout_a),
        carry=carry)
    return outs + [got]


def qkvconv_fwd(p, w, *, col0, nb, s, name):
    kw, ch = w.shape
    nblk = ch // HEAD
    c0 = col0 // HEAD

    def body(x_ref, w_ref, o_ref):
        row = lax.broadcasted_iota(jnp.int32, (s, HEAD), 0)
        c = _conv_fwd(x_ref[...], w_ref, kw, row)
        o_ref[...] = c * _sigmoid(c)

    return pl.pallas_call(
        body, name=name, grid=(nb, nblk),
        in_specs=[pl.BlockSpec((s, HEAD), lambda b, j: (b, c0 + j)), pl.BlockSpec((kw, HEAD), lambda b, j: (0, j))],
        out_specs=pl.BlockSpec((s, HEAD), lambda b, j: (b, j)),
        out_shape=jax.ShapeDtypeStruct((nb * s, ch), F32),
        compiler_params=_cparams(("parallel", "parallel")),
    )(p, w)


def qkvconv_bwd(p, w, dy, *, col0, nb, s, name):
    kw, ch = w.shape
    nblk = ch // HEAD
    c0 = col0 // HEAD

    def body(x_ref, w_ref, dy_ref, dx_ref, dw_out):
        row = lax.broadcasted_iota(jnp.int32, (s, HEAD), 0)
        xv = x_ref[...]
        c = _conv_fwd(xv, w_ref, kw, row)
        sc = _sigmoid(c)
        dc = dy_ref[...] * (sc * (1.0 + c * (1.0 - sc)))
        dx, dw = _conv_bwd(xv, dc, w_ref, kw, row)
        dx_ref[...] = dx.astype(dx_ref.dtype)
        dw_out[0] = _rows_to_block(dw, 8, HEAD)

    blk = pl.BlockSpec((s, HEAD), lambda b, j: (b, j))
    return pl.pallas_call(
        body, name=name, grid=(nb, nblk),
        in_specs=[pl.BlockSpec((s, HEAD), lambda b, j: (b, c0 + j)), pl.BlockSpec((kw, HEAD), lambda b, j: (0, j)), blk],
        out_specs=[blk, pl.BlockSpec((1, 8, HEAD), lambda b, j: (b, 0, j))],
        out_shape=[jax.ShapeDtypeStruct((nb * s, ch), BF16), jax.ShapeDtypeStruct((nb, 8, ch), F32)],
        compiler_params=_cparams(("parallel", "parallel")),
    )(p, w, dy)


def ffn_act_fwd(up, w, b, *, nb, s, name, carry=None):
    kw, dff = w.shape
    cb = _pick(dff, (256, 128))
    nblk = dff // cb

    def body(g_ref, u_ref, w_ref, b_ref, o_ref):
        row = lax.broadcasted_iota(jnp.int32, (s, cb), 0)
        gc = _conv_fwd(g_ref[...].astype(F32), w_ref, kw, row) + b_ref[...]
        o_ref[...] = (gc * _sigmoid(gc) * u_ref[...].astype(F32)).astype(o_ref.dtype)

    outs, got = carried_call(
        body, name=name, grid=(nb, nblk),
        in_specs=[pl.BlockSpec((s, cb), lambda i, j: (i, j)), pl.BlockSpec((s, cb), lambda i, j: (i, nblk + j)),
                  pl.BlockSpec((kw, cb), lambda i, j: (0, j)), pl.BlockSpec((1, cb), lambda i, j: (0, j))],
        out_specs=[pl.BlockSpec((s, cb), lambda i, j: (i, j))],
        out_shape=[jax.ShapeDtypeStruct((nb * s, dff), BF16)],
        scratch_shapes=[], semantics=("parallel", "parallel"), args=(up, up, w, b.reshape(1, dff)), carry=carry)
    return outs + [got]


def ffn_act_bwd(up, w, b, dact, *, nb, s, name):
    kw, dff = w.shape
    cb = _pick(dff, (256, 128))
    nblk = dff // cb

    def body(g_ref, u_ref, w_ref, b_ref, da_ref, dg_ref, du_ref, sm_out):
        row = lax.broadcasted_iota(jnp.int32, (s, cb), 0)
        gv = g_ref[...].astype(F32)
        gc = _conv_fwd(gv, w_ref, kw, row) + b_ref[...]
        sc = _sigmoid(gc)
        da = da_ref[...].astype(F32)
        du_ref[...] = (da * gc * sc).astype(du_ref.dtype)
        dgc = da * u_ref[...].astype(F32) * (sc * (1.0 + gc * (1.0 - sc)))
        dgate, dw = _conv_bwd(gv, dgc, w_ref, kw, row)
        dg_ref[...] = dgate.astype(dg_ref.dtype)
        sm_out[0] = _rows_to_block(dw + [jnp.sum(dgc, axis=0, keepdims=True)], 8, cb)

    blk = pl.BlockSpec((s, cb), lambda i, j: (i, j))
    return pl.pallas_call(
        body, name=name, grid=(nb, nblk),
        in_specs=[blk, pl.BlockSpec((s, cb), lambda i, j: (i, nblk + j)),
                  pl.BlockSpec((kw, cb), lambda i, j: (0, j)), pl.BlockSpec((1, cb), lambda i, j: (0, j)), blk],
        out_specs=[blk, blk, pl.BlockSpec((1, 8, cb), lambda i, j: (i, 0, j))],
        out_shape=[jax.ShapeDtypeStruct((nb * s, dff), BF16), jax.ShapeDtypeStruct((nb * s, dff), BF16),
                   jax.ShapeDtypeStruct((nb, 8, dff), F32)],
        compiler_params=_cparams(("parallel", "parallel")),
    )(up, up, w, b.reshape(1, dff), dact)


def _dot(a, b, dims):
    return lax.dot_general(a, b, (dims, ((0,), (0,))), preferred_element_type=F32)


def _mm_nn(a, b):
    return _dot(a.astype(BF16), b.astype(BF16), ((2,), (1,)))


def _mm_nt(a, b):
    return _dot(a.astype(BF16), b.astype(BF16), ((2,), (2,)))


def _mm_tn(a, b):
    return _dot(a.astype(BF16), b.astype(BF16), ((1,), (1,)))


def _split3(x):
    hi = x.astype(BF16)
    rest = x - hi.astype(F32)
    mid = rest.astype(BF16)
    return hi, mid, (rest - mid.astype(F32)).astype(BF16)


def _mask_dot(mask, x, dims, mask_first):
    if mask_first:
        return sum(_dot(mask, p, dims) for p in _split3(x))
    return sum(_dot(p, mask, dims) for p in _split3(x))


@jax.custom_vjp
def _mask_nn(mask, x):
    return _mask_dot(mask, x, ((2,), (1,)), True)


def _mask_nn_fwd(mask, x):
    return _mask_nn(mask, x), mask


def _mask_nn_bwd(mask, ct):
    return jnp.zeros_like(mask), _mask_dot(mask, ct, ((1,), (1,)), True)


_mask_nn.defvjp(_mask_nn_fwd, _mask_nn_bwd)


@jax.custom_vjp
def _mask_tn(x, mask):
    return _mask_dot(mask, x, ((1,), (1,)), False)


def _mask_tn_fwd(x, mask):
    return _mask_tn(x, mask), mask


def _mask_tn_bwd(mask, ct):
    return _mask_dot(mask, ct, ((2,), (2,)), True), jnp.zeros_like(mask)


_mask_tn.defvjp(_mask_tn_fwd, _mask_tn_bwd)


GDN_ROWS = 256


def _gdn_chunk(qc, kc, vc, zc, braw, araw, alog, dtb, ng, state):
    nhead, r = qc.shape[0], qc.shape[1]
    ri = lax.broadcasted_iota(jnp.int32, (r, r), 0)
    ci = lax.broadcasted_iota(jnp.int32, (r, r), 1)
    same = (ri // CHUNK) == (ci // CHUNK)
    causal = same & (ri >= ci)
    strict = same & (ri > ci)
    eye = (ri == ci).astype(F32)
    row_chunk = lax.broadcasted_iota(jnp.int32, (r, HEAD), 0) // CHUNK
    per_head = lambda m: jnp.broadcast_to(m.astype(BF16), (nhead, r, r))

    q = qc * lax.rsqrt(jnp.sum(qc * qc, axis=-1, keepdims=True) + EPS) * (HEAD ** -0.5)
    k = kc * lax.rsqrt(jnp.sum(kc * kc, axis=-1, keepdims=True) + EPS)
    beta = _sigmoid(braw)
    g = -jnp.exp(alog) * _softplus(araw + dtb)

    g_w = jnp.broadcast_to(g, (nhead, r, HEAD))
    widen = lambda t: jnp.concatenate([t] * (r // HEAD), axis=2)
    gw = _mask_nn(per_head(causal), g_w)
    gi = widen(gw)
    gj = _mask_tn(widen(g_w), per_head(same & (ri <= ci)))
    decay = jnp.where(causal, jnp.exp(jnp.where(causal, gi - gj, 0.0)), 0.0)

    kb = k * beta
    vb = vc * beta
    lmat = jnp.where(strict, _mm_nt(kb, k) * decay, 0.0)
    x = -lmat
    ainv = eye + x
    p = 1
    while 2 * p < CHUNK:
        x = _mm_nn(x, x)
        ainv = _mm_nn(ainv, eye + x)
        p *= 2
    u = _mm_nn(ainv, vb)
    w = _mm_nn(ainv, kb * jnp.exp(gw))
    qk = jnp.where(causal, _mm_nt(q, k) * decay, 0.0)
    qg = q * jnp.exp(gw)

    o = jnp.zeros((nhead, r, HEAD), F32)
    for c in range(r // CHUNK):
        in_c = row_chunk == c
        glast = jnp.sum(jnp.where(in_c, g_w, 0.0), axis=1, keepdims=True)
        v_new = jnp.where(in_c, u - _mm_nn(w, state), 0.0)
        o = o + jnp.where(in_c, _mm_nn(qg, state), 0.0) + _mm_nn(qk, v_new)
        k_dec = jnp.where(in_c, k * jnp.exp(jnp.where(in_c, glast - gw, 0.0)), 0.0)
        state = state * jnp.exp(glast) + _mm_tn(k_dec, v_new)

    o = o * lax.rsqrt(jnp.mean(o * o, axis=-1, keepdims=True) + EPS) * ng
    o = o * (zc * _sigmoid(zc))
    return o, state


GDN_HEADS_FWD = 4
GDN_HEADS_BWD = 4


def _gdn_specs(s, nh, nqk, zcol, n_chunks, gh, single=False):
    rep = nh // nqk
    qw, vw = (gh // rep) * HEAD, gh * HEAD
    assert gh % rep == 0 and nh % gh == 0 and (nqk * HEAD) % qw == 0 and (2 * nqk * HEAD) % vw == 0 and zcol % vw == 0
    k0, v0, z0 = (nqk * HEAD) // qw, (2 * nqk * HEAD) // vw, zcol // vw
    mode = dict(pipeline_mode=pl.Buffered(1)) if single else {}
    return dict(
        q=pl.BlockSpec((s, qw), lambda b, j: (b, j), **mode),
        k=pl.BlockSpec((s, qw), lambda b, j: (b, k0 + j), **mode),
        v=pl.BlockSpec((s, vw), lambda b, j: (b, v0 + j), **mode),
        z=pl.BlockSpec((s, vw), lambda b, j: (b, z0 + j), **mode),
        ba=pl.BlockSpec((s, HEAD), lambda b, j: (b, 0)),
        gp=pl.BlockSpec((8, HEAD), lambda b, j: (0, 0)),
        qk_out=pl.BlockSpec((s, qw), lambda b, j: (b, j), **mode),
        head=pl.BlockSpec((s, vw), lambda b, j: (b, j), **mode),
        head_in=pl.BlockSpec((s, vw), lambda b, j: (b, j), **mode),
        st=pl.BlockSpec((1, gh, n_chunks, HEAD, HEAD), lambda b, j: (b, j, 0, 0, 0)),
        st_in=pl.BlockSpec((1, gh, n_chunks, HEAD, HEAD), lambda b, j: (b, j, 0, 0, 0), **mode),
    )


def _gdn_scalars(gp_ref, h):
    lane = lax.broadcasted_iota(jnp.int32, (1, HEAD), 1)
    sel = (lane == h).astype(F32)
    alog = jnp.sum(gp_ref[pl.ds(0, 1), :] * sel, axis=-1, keepdims=True)
    dtb = jnp.sum(gp_ref[pl.ds(1, 1), :] * sel, axis=-1, keepdims=True)
    return alog, dtb, sel


def _lanes(i):
    return pl.ds(i * HEAD, HEAD)


def gdn_fwd(qkvc, p, pba, gp, *, nb, s, nh, nqk, zcol, name, carry=None):
    rb = min(GDN_ROWS, s)
    n_chunks = s // rb
    gh = GDN_HEADS_FWD
    rep = nh // nqk
    sp = _gdn_specs(s, nh, nqk, zcol, n_chunks, gh)

    def body(q_ref, k_ref, v_ref, z_ref, ba_ref, gp_ref, o_ref, st_ref):
        h0 = pl.program_id(1) * gh
        ng = gp_ref[pl.ds(2, 1), :]
        lane = lax.broadcasted_iota(jnp.int32, (rb, HEAD), 1)
        heads = []
        for i in range(gh):
            alog, dtb, _ = _gdn_scalars(gp_ref, h0 + i)
            heads.append((alog, dtb, (lane == h0 + i).astype(F32), (lane == nh + h0 + i).astype(F32)))

        alogs = jnp.stack([hd[0] for hd in heads])
        dtbs = jnp.stack([hd[1] for hd in heads])

        def step(n, state):
            rows = pl.ds(pl.multiple_of(n * rb, rb), rb)
            ba = ba_ref[rows, :]
            qs = jnp.stack([q_ref[rows, _lanes(i // rep)] for i in range(gh)])
            ks = jnp.stack([k_ref[rows, _lanes(i // rep)] for i in range(gh)])
            vs = jnp.stack([v_ref[rows, _lanes(i)] for i in range(gh)])
            zs = jnp.stack([z_ref[rows, _lanes(i)] for i in range(gh)])
            braw = jnp.stack([jnp.sum(ba * hd[2], axis=-1, keepdims=True) for hd in heads])
            araw = jnp.stack([jnp.sum(ba * hd[3], axis=-1, keepdims=True) for hd in heads])
            o, new_state = _gdn_chunk(qs, ks, vs, zs, braw, araw, alogs, dtbs, ng, state)
            for i in range(gh):
                st_ref[0, i, n] = state[i]
                o_ref[rows, _lanes(i)] = o[i].astype(o_ref.dtype)
            return new_state

        lax.fori_loop(0, n_chunks, step, jnp.zeros((gh, HEAD, HEAD), F32))

    outs, got = carried_call(
        body, name=name, grid=(nb, nh // gh),
        in_specs=[sp["q"], sp["k"], sp["v"], sp["z"], sp["ba"], sp["gp"]],
        out_specs=[sp["head"], sp["st"]],
        out_shape=[jax.ShapeDtypeStruct((nb * s, nh * HEAD), BF16),
                   jax.ShapeDtypeStruct((nb, nh, n_chunks, HEAD, HEAD), F32)],
        scratch_shapes=[], semantics=("parallel", "parallel"), args=(qkvc, qkvc, qkvc, p, pba, gp), carry=carry)
    return outs + [got]


def gdn_bwd(qkvc, p, pba, gp, states, dout, *, nb, s, nh, nqk, zcol, name, carry=None):
    rb = min(GDN_ROWS, s)
    n_chunks = s // rb
    gh = GDN_HEADS_BWD
    rep = nh // nqk
    sp = _gdn_specs(s, nh, nqk, zcol, n_chunks, gh, single=True)

    def body(q_ref, k_ref, v_ref, z_ref, ba_ref, gp_ref, st_ref, do_ref,
             dq_ref, dk_ref, dv_ref, dz_ref, dba_ref, dgp_ref):
        h0 = pl.program_id(1) * gh
        ng = gp_ref[pl.ds(2, 1), :]
        lane = lax.broadcasted_iota(jnp.int32, (rb, HEAD), 1)
        heads = []
        for i in range(gh):
            alog, dtb, sel_row = _gdn_scalars(gp_ref, h0 + i)
            heads.append((alog, dtb, (lane == h0 + i).astype(F32), (lane == nh + h0 + i).astype(F32), sel_row))

        @pl.when(h0 == 0)
        def _():
            dba_ref[...] = jnp.zeros_like(dba_ref)
            dgp_ref[...] = jnp.zeros_like(dgp_ref)

        alogs = jnp.stack([hd[0] for hd in heads])
        dtbs = jnp.stack([hd[1] for hd in heads])

        def step(it, carry):
            dstate, dalog, ddtb, dng = carry
            n = n_chunks - 1 - it
            rows = pl.ds(pl.multiple_of(n * rb, rb), rb)
            ba = ba_ref[rows, :]
            qs = jnp.stack([q_ref[rows, _lanes(i // rep)] for i in range(gh)])
            ks = jnp.stack([k_ref[rows, _lanes(i // rep)] for i in range(gh)])
            vs = jnp.stack([v_ref[rows, _lanes(i)] for i in range(gh)])
            zs = jnp.stack([z_ref[rows, _lanes(i)] for i in range(gh)])
            dos = jnp.stack([do_ref[rows, _lanes(i)] for i in range(gh)])
            braw = jnp.stack([jnp.sum(ba * hd[2], axis=-1, keepdims=True) for hd in heads])
            araw = jnp.stack([jnp.sum(ba * hd[3], axis=-1, keepdims=True) for hd in heads])
            _, vjp = jax.vjp(_gdn_chunk, qs, ks, vs, zs, braw, araw, alogs, dtbs, ng, st_ref[0, :, n])
            gq, gk, gv, gz, gb, ga, galog, gdtb, gng, gstate = vjp((dos, dstate))
            dba = dba_ref[rows, :]
            for i in range(gh):
                dv_ref[rows, _lanes(i)] = gv[i]
                dz_ref[rows, _lanes(i)] = gz[i].astype(dz_ref.dtype)
                dba = dba + (gb[i] * heads[i][2] + ga[i] * heads[i][3])
            for j in range(gh // rep):
                dq_ref[rows, _lanes(j)] = sum(gq[i] for i in range(j * rep, (j + 1) * rep))
                dk_ref[rows, _lanes(j)] = sum(gk[i] for i in range(j * rep, (j + 1) * rep))
            dba_ref[rows, :] = dba
            return gstate, dalog + galog, ddtb + gdtb, dng + gng

        init = (jnp.zeros((gh, HEAD, HEAD), F32), jnp.zeros((gh, 1, 1), F32), jnp.zeros((gh, 1, 1), F32),
                jnp.zeros((1, HEAD), F32))
        _, dalog, ddtb, dng = lax.fori_loop(0, n_chunks, step, init)
        rows3 = [sum(dalog[i] * heads[i][4] for i in range(gh)), sum(ddtb[i] * heads[i][4] for i in range(gh)), dng]
        dgp_ref[0] += _rows_to_block(rows3, 8, HEAD)

    outs, got = carried_call(
        body, name=name, grid=(nb, nh // gh),
        in_specs=[sp["q"], sp["k"], sp["v"], sp["z"], sp["ba"], sp["gp"], sp["st_in"], sp["head_in"]],
        out_specs=[sp["qk_out"], sp["qk_out"], sp["head"], sp["head"], sp["ba"],
                   pl.BlockSpec((1, 8, HEAD), lambda b, j: (b, 0, 0))],
        out_shape=[jax.ShapeDtypeStruct((nb * s, nqk * HEAD), F32), jax.ShapeDtypeStruct((nb * s, nqk * HEAD), F32),
                   jax.ShapeDtypeStruct((nb * s, nh * HEAD), F32), jax.ShapeDtypeStruct((nb * s, nh * HEAD), BF16),
                   jax.ShapeDtypeStruct((nb * s, HEAD), F32), jax.ShapeDtypeStruct((nb, 8, HEAD), F32)],
        scratch_shapes=[], semantics=("parallel", "arbitrary"),
        args=(qkvc, qkvc, qkvc, p, pba, gp, states, dout), carry=carry)
    return outs + [got]


ADAM_LR = 0.001
ADAM_B1 = 0.9
ADAM_B2 = 0.999
ADAM_EPS = 1e-08
ADAM_WD = 0.01
ADAM_STEP = 10
EW_BLOCK_BYTES = 1 << 20


def _row_tile(rows, cols):
    for tr in (1024, 512, 256, 128, 64, 32, 16, 8):
        if rows % tr == 0 and tr * cols * 4 <= EW_BLOCK_BYTES:
            return tr
    return rows


def sum_slots(rs, *, name):
    nl = len(rs)
    n, rows, cols = rs[0].shape
    tr = _row_tile(rows, cols)
    nblk = rows // tr

    def body(*refs):
        o_ref = refs[nl]
        for l in range(nl):
            @pl.when(pl.program_id(0) == l)
            def _(l=l):
                acc = refs[l][0].astype(F32)
                for i in range(1, n):
                    acc = acc + refs[l][i].astype(F32)
                o_ref[0] = acc

    def in_map(l):
        return lambda li, i: (0, jnp.where(li == l, i, jnp.where(li < l, 0, nblk - 1)), 0)

    return pl.pallas_call(
        body, name=name, grid=(nl, nblk),
        in_specs=[pl.BlockSpec((n, tr, cols), in_map(l)) for l in range(nl)],
        out_specs=pl.BlockSpec((1, tr, cols), lambda li, i: (li, i, 0)),
        out_shape=jax.ShapeDtypeStruct((nl, rows, cols), F32),
        compiler_params=_cparams(("arbitrary", "arbitrary")),
    )(*rs)


def adamw(g_parts, w, m, v, *, name):
    shape = w.shape
    cols = shape[-1]
    lead = shape[0] if w.ndim == 3 else 1
    view = (lambda a: a.reshape(shape)) if w.ndim == 3 else (lambda a: a.reshape(1, -1, cols))
    rows = view(w).shape[1]
    tr = _row_tile(rows, cols)
    npart = len(g_parts)
    c1 = 1.0 - ADAM_B1 ** ADAM_STEP
    c2 = 1.0 - ADAM_B2 ** ADAM_STEP

    def body(*refs):
        w_ref, m_ref, v_ref = refs[npart:npart + 3]
        g_ref, d_ref, nm_ref, nv_ref = refs[npart + 3:]
        g = refs[0][...]
        for i in range(1, npart):
            g = g + refs[i][...]
        nm = ADAM_B1 * m_ref[...] + (1.0 - ADAM_B1) * g
        nv = ADAM_B2 * v_ref[...] + (1.0 - ADAM_B2) * (g * g)
        g_ref[...] = g
        nm_ref[...] = nm
        nv_ref[...] = nv
        d_ref[...] = -ADAM_LR * ((nm / c1) / (jnp.sqrt(nv / c2) + ADAM_EPS) + ADAM_WD * w_ref[...])

    blk = pl.BlockSpec((1, tr, cols), lambda l, i: (l, i, 0))
    outs = pl.pallas_call(
        body, name=name, grid=(lead, rows // tr),
        in_specs=[blk] * (npart + 3),
        out_specs=[blk] * 4,
        out_shape=[jax.ShapeDtypeStruct((lead, rows, cols), F32)] * 4,
        compiler_params=_cparams(("parallel", "parallel")),
    )(*[view(a) for a in g_parts], view(w), view(m), view(v))
    return tuple(o.reshape(shape) for o in outs)


def allreduce_small(vec, *, name):
    r = vec.shape[0]

    def body(v_ref, o_ref, slots, send_sems, recv_sems):
        x, y, c = lax.axis_index("x"), lax.axis_index("y"), lax.axis_index("c")
        me = 4 * x + 2 * y + c
        slots[me] = v_ref[...]
        copies = []
        for j in range(1, 8):
            px = 1 - x if j & 4 else x
            py = 1 - y if j & 2 else y
            pc = 1 - c if j & 1 else c
            rc = pltpu.make_async_remote_copy(src_ref=v_ref, dst_ref=slots.at[me], send_sem=send_sems.at[j - 1],
                                              recv_sem=recv_sems.at[j - 1], device_id=(px, py, pc), device_id_type=MESH)
            rc.start()
            copies.append(rc)
        for cp in copies:
            cp.wait()
        acc = slots[0]
        for i in range(1, 8):
            acc = acc + slots[i]
        o_ref[...] = acc

    vm = pl.BlockSpec(memory_space=pltpu.VMEM)
    return pl.pallas_call(
        body, name=name, in_specs=[vm], out_specs=vm,
        out_shape=jax.ShapeDtypeStruct((r, 128), F32),
        scratch_shapes=[pltpu.VMEM((8, r, 128), F32), pltpu.SemaphoreType.DMA((7,)), pltpu.SemaphoreType.DMA((7,))],
        compiler_params=pltpu.CompilerParams(vmem_limit_bytes=VMEM_LIMIT),
    )(vec)


WEIGHTS = ("mix_norm_g", "w_in", "conv_dw_w", "conv_dw_b", "conv_ln_g", "conv_ln_b", "conv_pw_w", "conv_pw_b",
           "gdn_conv_w", "gdn_a_log", "gdn_dt_bias", "gdn_norm_g", "w_out", "ffn_norm_g", "w_up", "ffn_conv_w",
           "ffn_conv_b", "w_down", "final_norm_g")
COL_SHARDED = ("w_in", "w_up", "conv_dw_w", "gdn_conv_w", "ffn_conv_w")
ROW_SHARDED = ("conv_pw_w", "w_out", "w_down")
BIG = ("w_in", "conv_pw_w", "w_out", "w_up", "w_down")
SMALL_CONV = ("conv_dw_w", "gdn_conv_w", "ffn_conv_w")


def _full_from_slots(name, part):
    if name in COL_SHARDED:
        r, cs = part.shape[1:]
        return jnp.transpose(part, (1, 0, 2)).reshape(r, 4 * cs)
    rs, c = part.shape[1:]
    return part.reshape(4 * rs, c)


def _slots_from_full(name, full):
    if name in COL_SHARDED:
        r, c = full.shape
        return jnp.transpose(full.reshape(r, 4, c // 4), (1, 0, 2))
    r, c = full.shape
    return full.reshape(4, r // 4, c)


def _pack(parts):
    flat = jnp.concatenate([p.reshape(-1).astype(F32) for p in parts])
    pad = (-flat.shape[0]) % 1024
    return jnp.pad(flat, (0, pad)).reshape(-1, 128)


def _unpack(vec, shapes):
    flat = vec.reshape(-1)
    out, pos = [], 0
    for shp in shapes:
        n = 1
        for d in shp:
            n *= d
        out.append(flat[pos:pos + n].reshape(shp))
        pos += n
    return out


def kernel(x, mix_norm_g, w_in, conv_dw_w, conv_dw_b, conv_ln_g, conv_ln_b, conv_pw_w, conv_pw_b, gdn_conv_w, gdn_a_log, gdn_dt_bias, gdn_norm_g, w_out, ffn_norm_g, w_up, ffn_conv_w, ffn_conv_b, w_down, final_norm_g, loss_target, m_mix_norm_g, m_w_in, m_conv_dw_w, m_conv_dw_b, m_conv_ln_g, m_conv_ln_b, m_conv_pw_w, m_conv_pw_b, m_gdn_conv_w, m_gdn_a_log, m_gdn_dt_bias, m_gdn_norm_g, m_w_out, m_ffn_norm_g, m_w_up, m_ffn_conv_w, m_ffn_conv_b, m_w_down, m_final_norm_g, v_mix_norm_g, v_w_in, v_conv_dw_w, v_conv_dw_b, v_conv_ln_g, v_conv_ln_b, v_conv_pw_w, v_conv_pw_b, v_gdn_conv_w, v_gdn_a_log, v_gdn_dt_bias, v_gdn_norm_g, v_w_out, v_ffn_norm_g, v_w_up, v_ffn_conv_w, v_ffn_conv_b, v_w_down, v_final_norm_g):
    wts = dict(zip(WEIGHTS, (mix_norm_g, w_in, conv_dw_w, conv_dw_b, conv_ln_g, conv_ln_b, conv_pw_w, conv_pw_b,
                             gdn_conv_w, gdn_a_log, gdn_dt_bias, gdn_norm_g, w_out, ffn_norm_g, w_up, ffn_conv_w,
                             ffn_conv_b, w_down, final_norm_g)))
    mom = dict(zip(WEIGHTS, (m_mix_norm_g, m_w_in, m_conv_dw_w, m_conv_dw_b, m_conv_ln_g, m_conv_ln_b, m_conv_pw_w,
                             m_conv_pw_b, m_gdn_conv_w, m_gdn_a_log, m_gdn_dt_bias, m_gdn_norm_g, m_w_out,
                             m_ffn_norm_g, m_w_up, m_ffn_conv_w, m_ffn_conv_b, m_w_down, m_final_norm_g)))
    var = dict(zip(WEIGHTS, (v_mix_norm_g, v_w_in, v_conv_dw_w, v_conv_dw_b, v_conv_ln_g, v_conv_ln_b, v_conv_pw_w,
                             v_conv_pw_b, v_gdn_conv_w, v_gdn_a_log, v_gdn_dt_bias, v_gdn_norm_g, v_w_out,
                             v_ffn_norm_g, v_w_up, v_ffn_conv_w, v_ffn_conv_b, v_w_down, v_final_norm_g)))

    nb, s, d = x.shape
    t = nb * s
    depth = mix_norm_g.shape[0]
    ch = conv_dw_b.shape[1]
    nh = gdn_a_log.shape[1]
    nqk = nh // 2
    kwid, vwid = nqk * HEAD, nh * HEAD
    main = 2 * ch + 2 * kwid + 2 * vwid
    qcol, zcol = 2 * ch, 2 * ch + 2 * kwid + vwid
    dff = ffn_conv_b.shape[1]
    my_xy = 2 * lax.axis_index("x") + lax.axis_index("y")

    def shard(key):
        n, l, i, k = key
        rows = wts[n].shape[1] // k
        return wts[n][l, i * rows:(i + 1) * rows].astype(BF16)

    gather = lambda *keys: SplitGather([shard(k) for k in keys])
    pieces, full = {}, {}

    def arrived(keys, got):
        for (n, l, i, k), g in zip(keys, got):
            pieces.setdefault((n, l), {})[i] = _full_from_slots(n, g)
            if len(pieces[n, l]) == k:
                blocks = [pieces[n, l][j] for j in range(k)]
                full[n, l] = blocks[0] if k == 1 else jnp.concatenate(blocks, axis=0)

    def carrying_gather(keys, call):
        res = call(carry=gather(*keys))
        arrived(keys, res[-1])
        return res[0] if len(res) == 2 else res[:-1]

    first, small = run_exchanges([gather(("w_in", 0, 0, 1)), Exchange([wts[n] for n in SMALL_CONV], gather=True)],
                                 name="gather_first")
    arrived([("w_in", 0, 0, 1)], first)
    for n, g in zip(SMALL_CONV, small):
        for l in range(depth):
            full[n, l] = _full_from_slots(n, g[:, l])

    xc = x.reshape(t, d)
    saved, lws = [], []
    for l in range(depth):
        w_in_f = full["w_in", l]
        w_main = w_in_f[:, :main]
        w_ba = jnp.pad(w_in_f[:, main:], ((0, 0), (0, HEAD - 2 * nh)))
        gp = (jnp.zeros((8, HEAD), F32).at[0, :nh].set(gdn_a_log[l]).at[1, :nh].set(gdn_dt_bias[l])
              .at[2].set(gdn_norm_g[l]))
        h = rmsnorm_fwd(xc, mix_norm_g[l], name=f"f{l}_norm1")
        p = carrying_gather([("conv_pw_w", l, 0, 1), ("w_out", l, 0, 1), ("w_up", l, 0, 4)],
                            functools.partial(matmul, h, w_main, name=f"f{l}_in_main"))
        pba = matmul(h, w_ba, name=f"f{l}_in_ba")
        u3, u1 = carrying_gather([("w_up", l, 1, 4)],
                                 functools.partial(conf_fwd, p, full["conv_dw_w", l], conv_dw_b[l], conv_ln_g[l],
                                                   conv_ln_b[l], nb=nb, s=s, name=f"f{l}_conf"))
        out_a = matmul(u3, full["conv_pw_w", l], bias=conv_pw_b[l], out_dtype=BF16, name=f"f{l}_pw")
        qkvc = qkvconv_fwd(p, full["gdn_conv_w", l], col0=qcol, nb=nb, s=s, name=f"f{l}_qkvconv")
        out_b, states = carrying_gather([("w_up", l, 2, 4), ("w_up", l, 3, 4)],
                                        functools.partial(gdn_fwd, qkvc, p, pba, gp, nb=nb, s=s, nh=nh, nqk=nqk,
                                                          zcol=zcol, name=f"f{l}_gdn"))
        wout_a, wout_b = full["w_out", l][:ch], full["w_out", l][ch:]
        x1 = matmul(out_a, wout_a, res=xc, name=f"f{l}_out_a")
        x1 = matmul(out_b, wout_b, res=x1, name=f"f{l}_out_b")
        h2 = rmsnorm_fwd(x1, ffn_norm_g[l], name=f"f{l}_norm2")
        up = carrying_gather([("w_down", l, 0, 1)],
                             functools.partial(matmul, h2, full["w_up", l], out_dtype=BF16, name=f"f{l}_up"))
        act_call = functools.partial(ffn_act_fwd, up, full["ffn_conv_w", l], ffn_conv_b[l], nb=nb, s=s,
                                     name=f"f{l}_act")
        if l + 1 < depth:
            act = carrying_gather([("w_in", l + 1, 0, 2)], act_call)
            x2 = carrying_gather([("w_in", l + 1, 1, 2)],
                                 functools.partial(matmul, act, full["w_down", l], res=x1, name=f"f{l}_down"))
        else:
            act = act_call()[0]
            x2 = matmul(act, full["w_down", l], res=x1, name=f"f{l}_down")
        saved.append(dict(x=xc, h=h, p=p, pba=pba, u3=u3, u1=u1, out_a=out_a, qkvc=qkvc, out_b=out_b, states=states,
                          x1=x1, h2=h2, up=up, act=act))
        lws.append(dict(w_main=w_main, w_ba=w_ba, pw=full["conv_pw_w", l], wout_a=wout_a, wout_b=wout_b,
                        wup=full["w_up", l], wdown=full["w_down", l], dw_w=full["conv_dw_w", l],
                        gconv_w=full["gdn_conv_w", l], fconv_w=full["ffn_conv_w", l], gp=gp))
        xc = x2

    loss_blk, dx, dxb, dgf = loss_head(xc, final_norm_g, loss_target.reshape(t, d), name="loss_head")

    stacks, received = {}, {}
    scatter = lambda *keys: Exchange([stacks[k] for k in keys], gather=False)

    def produced(n, l, grad, halves=False):
        slots = _slots_from_full(n, grad).astype(BF16)
        if halves:
            half = slots.shape[1] // 2
            stacks[n, l, 0], stacks[n, l, 1] = slots[:, :half], slots[:, half:]
        else:
            stacks[n, l] = slots

    def landed(keys, got):
        for k, g in zip(keys, got):
            received[k] = g

    def carrying(keys, call, **kw):
        res = call(carry=scatter(*keys), **kw)
        landed(keys, res[-1])
        return res[0] if len(res) == 2 else res[:-1]

    small_grads = {n: [None] * depth for n in WEIGHTS if n not in BIG and n != "final_norm_g"}
    for l in reversed(range(depth)):
        lw, sv = lws[l], saved[l]
        dact_call = functools.partial(matmul, dxb, lw["wdown"], tb=True, out_dtype=BF16, name=f"b{l}_dact")
        dact = carrying([("w_in", l + 1, 1)], dact_call) if l + 1 < depth else dact_call()
        produced("w_down", l, matmul(sv["act"], dxb, ta=True, out_dtype=BF16, name=f"b{l}_dwdown"), halves=True)
        dgate, dupv, fpart = ffn_act_bwd(sv["up"], lw["fconv_w"], ffn_conv_b[l], dact, nb=nb, s=s, name=f"b{l}_act")
        dh2 = carrying([("w_down", l, 0)], functools.partial(matmul, dgate, lw["wup"], tb=True, name=f"b{l}_dh2_gate"))
        dh2 = carrying([("w_down", l, 1)], functools.partial(matmul, dupv, lw["wup"], tb=True, b_koff=dff, res=dh2,
                                                             name=f"b{l}_dh2_up"))
        produced("w_up", l, jnp.concatenate(
            [matmul(sv["h2"], dgate, ta=True, out_dtype=BF16, name=f"b{l}_dwup_gate"),
             matmul(sv["h2"], dupv, ta=True, out_dtype=BF16, name=f"b{l}_dwup_up")], axis=1))
        dx1, dx1b, dg2 = rmsnorm_bwd(sv["x1"], ffn_norm_g[l], dh2, dx, name=f"b{l}_norm2")
        fsum = jnp.sum(fpart, axis=0)
        small_grads["ffn_norm_g"][l] = dg2[0]
        small_grads["ffn_conv_w"][l] = fsum[:ffn_conv_w.shape[1]]
        small_grads["ffn_conv_b"][l] = fsum[ffn_conv_w.shape[1]]
        dout_a = matmul(dx1b, lw["wout_a"], tb=True, out_dtype=BF16, name=f"b{l}_dout_a")
        dout_b = matmul(dx1b, lw["wout_b"], tb=True, name=f"b{l}_dout_b")
        produced("w_out", l, jnp.concatenate(
            [matmul(sv["out_a"], dx1b, ta=True, out_dtype=BF16, name=f"b{l}_dwout_a"),
             matmul(sv["out_b"], dx1b, ta=True, out_dtype=BF16, name=f"b{l}_dwout_b")], axis=0))
        du3 = matmul(dout_a, lw["pw"], tb=True, name=f"b{l}_du3")
        produced("conv_pw_w", l, matmul(sv["u3"], dout_a, ta=True, out_dtype=BF16, name=f"b{l}_dwpw"))
        dval, dagate, cw_part, cs_part = carrying(
            [("w_out", l), ("conv_pw_w", l)],
            functools.partial(conf_bwd, sv["p"], sv["u1"], lw["dw_w"], conv_dw_b[l], conv_ln_g[l], conv_ln_b[l], du3,
                              dout_a,
                              nb=nb, s=s, name=f"b{l}_conf"))
        csum = jnp.sum(cs_part, axis=0)
        small_grads["conv_dw_w"][l] = jnp.sum(cw_part, axis=0)
        small_grads["conv_dw_b"][l] = csum[0]
        small_grads["conv_ln_g"][l] = csum[1]
        small_grads["conv_ln_b"][l] = csum[2]
        small_grads["conv_pw_b"][l] = csum[3]
        dq, dk, dv, dz, dpba, dgp = carrying(
            [("w_up", l)],
            functools.partial(gdn_bwd, sv["qkvc"], sv["p"], sv["pba"], lw["gp"], sv["states"], dout_b,
                              nb=nb, s=s, nh=nh, nqk=nqk, zcol=zcol, name=f"b{l}_gdn"))
        dqkv, gw_part = qkvconv_bwd(sv["p"], lw["gconv_w"], jnp.concatenate([dq, dk, dv], axis=1),
                                    col0=qcol, nb=nb, s=s, name=f"b{l}_qkvconv")
        gsum = jnp.sum(dgp, axis=0)
        small_grads["gdn_conv_w"][l] = jnp.sum(gw_part, axis=0)[:gdn_conv_w.shape[1]]
        small_grads["gdn_a_log"][l] = gsum[0, :nh]
        small_grads["gdn_dt_bias"][l] = gsum[1, :nh]
        small_grads["gdn_norm_g"][l] = gsum[2]
        dp = jnp.concatenate([dval, dagate, dqkv, dz], axis=1)
        dw_main = matmul(sv["h"], dp, ta=True, out_dtype=BF16, name=f"b{l}_dwin_main")
        dw_ba = matmul(sv["h"], dpba, ta=True, out_dtype=BF16, name=f"b{l}_dwin_ba")
        produced("w_in", l, jnp.concatenate([dw_main, dw_ba[:, :2 * nh]], axis=1), halves=True)
        dh = carrying([("w_in", l, 0)], functools.partial(matmul, dp, lw["w_main"], tb=True, name=f"b{l}_dh_main"))
        dh = matmul(dpba, lw["w_ba"], tb=True, res=dh, name=f"b{l}_dh_ba")
        dx, dxb, dg1 = rmsnorm_bwd(sv["x"], mix_norm_g[l], dh, dx1, name=f"b{l}_norm1")
        small_grads["mix_norm_g"][l] = dg1[0]

    def summed(n):
        parts = [received[k] for l in range(depth) for k in ([(n, l)] if (n, l) in received else [(n, l, 0), (n, l, 1)])]
        return sum_slots(parts, name=f"sum_{n}").reshape(wts[n].shape)

    early = [n for n in BIG if n != "w_in"]
    partial = {n: summed(n) for n in early}
    swapped, last = run_exchanges([SiblingSwap([partial[n] for n in early]),
                                   Exchange([stacks["w_in", 0, 1]], gather=False)], name="swap_and_scatter_last")
    other = dict(zip(early, swapped))
    received["w_in", 0, 1] = last[0]
    partial["w_in"] = summed("w_in")
    other["w_in"] = run_exchanges([SiblingSwap([partial["w_in"]])], name="swap_w_in")[0][0]

    grads, deltas, new_m, new_v = {}, {}, {}, {}
    for n in BIG:
        grads[n], deltas[n], new_m[n], new_v[n] = adamw([partial[n], other[n]], wts[n], mom[n], var[n],
                                                        name=f"adamw_{n}")

    small_names = [n for n in WEIGHTS if n not in BIG]
    small_full = [jnp.stack(small_grads[n]) if n != "final_norm_g" else dgf[0] for n in small_names]
    packed = _pack(small_full + [loss_blk[0, :1]])
    reduced = allreduce_small(packed, name="allreduce_small")
    parts = _unpack(reduced, [a.shape for a in small_full] + [(1,)])
    loss = parts[-1][0]
    for n, g in zip(small_names, parts[:-1]):
        if n in SMALL_CONV:
            wid = wts[n].shape[-1]
            g = lax.dynamic_slice_in_dim(g, my_xy * wid, wid, axis=g.ndim - 1)
        grads[n], deltas[n], new_m[n], new_v[n] = adamw([g], wts[n], mom[n], var[n], name=f"adamw_{n}")

    return (loss, dx.reshape(nb, s, d), *[grads[n] for n in WEIGHTS], *[deltas[n] for n in WEIGHTS],
            *[new_m[n] for n in WEIGHTS], *[new_v[n] for n in WEIGHTS])
```

```python
import functools

import jax
import jax.numpy as jnp
from jax import lax
from jax.experimental import pallas as pl
from jax.experimental.pallas import tpu as pltpu

F32 = jnp.float32
BF16 = jnp.bfloat16
EPS = 1e-6
CHUNK = 64
HEAD = 128
HIGHEST = lax.Precision.HIGHEST
VMEM_LIMIT = 56 * 1024 * 1024
MM_VMEM_BUDGET = 48 * 1024 * 1024


def _pick(dim, cands):
    for c in cands:
        if dim % c == 0:
            return c
    return dim


def _cparams(sem):
    return pltpu.CompilerParams(dimension_semantics=sem, vmem_limit_bytes=VMEM_LIMIT)


MESH = pl.DeviceIdType.MESH
ANY = pl.BlockSpec(memory_space=pl.ANY)


def _xy_peers():
    x, y = lax.axis_index("x"), lax.axis_index("y")
    peers = []
    for fx, fy in ((0, 1), (1, 0), (1, 1)):
        px = 1 - x if fx else x
        py = 1 - y if fy else y
        peers.append((2 * px + py, px, py))
    return 2 * x + y, peers


class Exchange:
    def __init__(self, arrs, gather):
        self.arrs, self.gather, self.n = list(arrs), gather, len(arrs)
        self.out_shape = [jax.ShapeDtypeStruct((4,) + (a.shape if gather else a.shape[1:]), a.dtype) for a in arrs]
        self.scratch = [pltpu.SemaphoreType.DMA((3 * self.n,)), pltpu.SemaphoreType.DMA((3 * self.n,)),
                        pltpu.SemaphoreType.DMA((self.n,))]

    def copies(self, ins, outs, send_sems, recv_sems, local_sems):
        me, peers = _xy_peers()
        c = lax.axis_index("c")
        out = []
        for k in range(self.n):
            out.append(pltpu.make_async_copy(ins[k] if self.gather else ins[k].at[me], outs[k].at[me],
                                             local_sems.at[k]))
            for j, (slot, px, py) in enumerate(peers):
                out.append(pltpu.make_async_remote_copy(
                    src_ref=ins[k] if self.gather else ins[k].at[slot], dst_ref=outs[k].at[me],
                    send_sem=send_sems.at[3 * k + j], recv_sem=recv_sems.at[3 * k + j],
                    device_id=(px, py, c), device_id_type=MESH))
        return out

    def start(self, ins, outs, sems):
        for cp in self.copies(ins, outs, *sems):
            cp.start()

    def finish(self, ins, outs, sems):
        for cp in self.copies(ins, outs, *sems):
            cp.wait()


class SplitGather:
    def __init__(self, arrs):
        self.arrs, self.n = list(arrs), len(arrs)
        assert all(a.ndim == 2 and a.shape[0] % 32 == 0 for a in arrs)
        self.out_shape = [jax.ShapeDtypeStruct((4,) + a.shape, a.dtype) for a in arrs]
        dma = pltpu.SemaphoreType.DMA
        self.scratch = [dma((3 * self.n,)), dma((3 * self.n,)), dma((3 * self.n,)), dma((3 * self.n,)), dma((self.n,))]

    def _half(self, k, c):
        half = self.arrs[k].shape[0] // 2
        return pl.ds(pl.multiple_of(c * half, 16), half)

    def over_ici(self, ins, outs, sems):
        me, peers = _xy_peers()
        c = lax.axis_index("c")
        out = []
        for k in range(self.n):
            out.append(pltpu.make_async_copy(ins[k], outs[k].at[me], sems[4].at[k]))
            for j, (slot, px, py) in enumerate(peers):
                out.append(pltpu.make_async_remote_copy(
                    src_ref=ins[k].at[self._half(k, c)], dst_ref=outs[k].at[me, self._half(k, c)],
                    send_sem=sems[0].at[3 * k + j], recv_sem=sems[1].at[3 * k + j],
                    device_id=(px, py, c), device_id_type=MESH))
        return out

    def over_d2d(self, outs, sems):
        _, peers = _xy_peers()
        x, y, c = lax.axis_index("x"), lax.axis_index("y"), lax.axis_index("c")
        out = []
        for k in range(self.n):
            for j, (slot, _, _) in enumerate(peers):
                rows = outs[k].at[slot, self._half(k, c)]
                out.append(pltpu.make_async_remote_copy(
                    src_ref=rows, dst_ref=rows, send_sem=sems[2].at[3 * k + j], recv_sem=sems[3].at[3 * k + j],
                    device_id=(x, y, 1 - c), device_id_type=MESH))
        return out

    def start(self, ins, outs, sems):
        for cp in self.over_ici(ins, outs, sems):
            cp.start()

    def finish(self, ins, outs, sems):
        for cp in self.over_ici(ins, outs, sems):
            cp.wait()
        passed = self.over_d2d(outs, sems)
        for cp in passed:
            cp.start()
        for cp in passed:
            cp.wait()


class SiblingSwap:
    def __init__(self, arrs):
        self.arrs, self.n = list(arrs), len(arrs)
        self.out_shape = [jax.ShapeDtypeStruct(a.shape, a.dtype) for a in arrs]
        self.scratch = [pltpu.SemaphoreType.DMA((self.n,)), pltpu.SemaphoreType.DMA((self.n,))]

    def copies(self, ins, outs, sems):
        peer = (lax.axis_index("x"), lax.axis_index("y"), 1 - lax.axis_index("c"))
        return [pltpu.make_async_remote_copy(src_ref=ins[k], dst_ref=outs[k], send_sem=sems[0].at[k],
                                             recv_sem=sems[1].at[k], device_id=peer, device_id_type=MESH)
                for k in range(self.n)]

    def start(self, ins, outs, sems):
        for cp in self.copies(ins, outs, sems):
            cp.start()

    def finish(self, ins, outs, sems):
        for cp in self.copies(ins, outs, sems):
            cp.wait()


def _split_refs(refs, carries, attr):
    groups, pos = [], 0
    for cr in carries:
        n = len(getattr(cr, attr))
        groups.append(refs[pos:pos + n])
        pos += n
    return groups


def carried_call(body, *, name, grid, in_specs, out_specs, out_shape, scratch_shapes, semantics, args, carry=None):
    n_in, n_out, n_scr = len(in_specs), len(out_specs), len(scratch_shapes)
    if carry is None:
        outs = pl.pallas_call(body, name=name, grid=grid, in_specs=in_specs, out_specs=out_specs, out_shape=out_shape,
                              scratch_shapes=scratch_shapes, compiler_params=_cparams(semantics))(*args)
        return list(outs), []
    carries = list(carry) if isinstance(carry, (list, tuple)) else [carry]
    n = sum(cr.n for cr in carries)

    def wrapped(*refs):
        ins, cin = refs[:n_in], _split_refs(refs[n_in:n_in + n], carries, "arrs")
        outs = refs[n_in + n:n_in + n + n_out]
        cout = _split_refs(refs[n_in + n + n_out:n_in + 2 * n + n_out], carries, "arrs")
        scratch = refs[n_in + 2 * n + n_out:n_in + 2 * n + n_out + n_scr]
        sems = _split_refs(refs[n_in + 2 * n + n_out + n_scr:], carries, "scratch")
        ids = [pl.program_id(i) for i in range(len(grid))]
        first = functools.reduce(jnp.logical_and, [i == 0 for i in ids])
        last = functools.reduce(jnp.logical_and, [i == g - 1 for i, g in zip(ids, grid)])

        @pl.when(first)
        def _():
            for cr, i, o, s in zip(carries, cin, cout, sems):
                cr.start(i, o, s)

        body(*ins, *outs, *scratch)

        @pl.when(last)
        def _():
            for cr, i, o, s in zip(carries, cin, cout, sems):
                cr.finish(i, o, s)

    res = pl.pallas_call(
        wrapped, name=name, grid=grid, in_specs=list(in_specs) + [ANY] * n, out_specs=list(out_specs) + [ANY] * n,
        out_shape=list(out_shape) + [s for cr in carries for s in cr.out_shape],
        scratch_shapes=list(scratch_shapes) + [s for cr in carries for s in cr.scratch],
        compiler_params=_cparams(tuple("arbitrary" for _ in grid)),
    )(*args, *[a for cr in carries for a in cr.arrs])
    got = _split_refs(list(res[n_out:]), carries, "arrs")
    return list(res[:n_out]), (got if isinstance(carry, (list, tuple)) else got[0])


def run_exchanges(carries, *, name):
    n = sum(cr.n for cr in carries)

    def body(*refs):
        cin = _split_refs(refs[:n], carries, "arrs")
        cout = _split_refs(refs[n:2 * n], carries, "arrs")
        sems = _split_refs(refs[2 * n:], carries, "scratch")
        for cr, i, o, s in zip(carries, cin, cout, sems):
            cr.start(i, o, s)
        for cr, i, o, s in zip(carries, cin, cout, sems):
            cr.finish(i, o, s)

    res = pl.pallas_call(body, name=name, in_specs=[ANY] * n, out_specs=[ANY] * n,
                         out_shape=[s for cr in carries for s in cr.out_shape],
                         scratch_shapes=[s for cr in carries for s in cr.scratch],
                         )(*[a for cr in carries for a in cr.arrs])
    return _split_refs(list(res), carries, "arrs")


def matmul(a, b, *, ta=False, tb=False, out_dtype=F32, res=None, bias=None, b_koff=0, out_slots=0, name, carry=None):
    if ta:
        kdim, m = a.shape
    else:
        m, kdim = a.shape
    slot_w = 0
    if b.ndim == 3 and tb:
        nslot, n, slot_w = b.shape
        kb = nslot * slot_w
    elif b.ndim == 3:
        nslot, kb, slot_w = b.shape
        n = nslot * slot_w
    elif tb:
        n, kb = b.shape
    else:
        kb, n = b.shape
    assert kb >= kdim + b_koff, (a.shape, b.shape, ta, tb)
    out_w = n // out_slots if out_slots else n
    if (slot_w and not tb) or out_slots:
        tn = _pick(min(slot_w, out_w) if (slot_w and not tb) else out_w, (1408, 1024, 512, 256, 128))
        assert out_w % tn == 0 and (tb or not slot_w or slot_w % tn == 0)
    else:
        tn = _pick(n, (1024, 512, 256, 128))
    out_bytes = jnp.dtype(out_dtype).itemsize

    def vmem_bytes(tm_, tk_):
        blocks = a.dtype.itemsize * tm_ * tk_ + b.dtype.itemsize * tk_ * tn + out_bytes * tm_ * tn
        blocks += 4 * tm_ * tn if res is not None else 0
        temps = 4 * tm_ * tn + (2 * tk_ * tn if tb else 0)
        return 2 * blocks + temps + (4 * tm_ * tn if tk_ < kdim else 0)

    def longest_k(tm_):
        if slot_w and tb:
            return slot_w
        return next(c for c in (kdim, 4096, 2816, 2560, 2048, 1024, 512, 256, 128)
                    if kdim % c == 0 and b_koff % c == 0 and c % 128 == 0
                    and (vmem_bytes(tm_, c) <= MM_VMEM_BUDGET or c == 128))

    tall = [c for c in (1024, 512) if m % c == 0] or [_pick(m, (256, 128))]
    tm = max(tall, key=lambda c: (longest_k(c), c))
    tk = longest_k(tm)
    nk = kdim // tk
    ko = b_koff // tk
    dims = (((0 if ta else 1,), (1 if tb else 0,)), ((), ()))

    def body(*refs):
        a_ref, b_ref = refs[0], refs[1]
        pos = 2
        bias_ref = res_ref = None
        if bias is not None:
            bias_ref = refs[pos]
            pos += 1
        if res is not None:
            res_ref = refs[pos]
            pos += 1
        o_ref = refs[pos]
        part = lax.dot_general(a_ref[...].astype(BF16), b_ref[...].astype(BF16), dims, preferred_element_type=F32)

        def finish(r):
            if bias_ref is not None:
                r = r + bias_ref[...]
            if res_ref is not None:
                r = r + res_ref[...]
            o_ref[...] = r.astype(o_ref.dtype)

        if nk == 1:
            finish(part)
            return
        acc_ref = refs[pos + 1]
        k = pl.program_id(2)

        @pl.when(k == 0)
        def _():
            acc_ref[...] = part

        @pl.when((k > 0) & (k < nk - 1))
        def _():
            acc_ref[...] += part

        @pl.when(k == nk - 1)
        def _():
            finish(acc_ref[...] + part)

    assert kdim % tk == 0 and b_koff % tk == 0
    a_spec = pl.BlockSpec((tk, tm), lambda i, j, k: (k, i)) if ta else pl.BlockSpec((tm, tk), lambda i, j, k: (i, k))
    if slot_w and tb:
        b_spec = pl.BlockSpec((None, tn, tk), lambda i, j, k: (k + ko, j, 0))
    elif slot_w:
        per = slot_w // tn
        b_spec = pl.BlockSpec((None, tk, tn), lambda i, j, k: (j // per, k + ko, j % per))
    elif tb:
        b_spec = pl.BlockSpec((tn, tk), lambda i, j, k: (j, k + ko))
    else:
        b_spec = pl.BlockSpec((tk, tn), lambda i, j, k: (k + ko, j))
    if out_slots:
        oper = out_w // tn
        out_spec = pl.BlockSpec((None, tm, tn), lambda i, j, k: (j // oper, i, j % oper))
        out_struct = jax.ShapeDtypeStruct((out_slots, m, out_w), out_dtype)
    else:
        out_spec = pl.BlockSpec((tm, tn), lambda i, j, k: (i, j))
        out_struct = jax.ShapeDtypeStruct((m, n), out_dtype)
    in_specs = [a_spec, b_spec]
    args = [a, b]
    if bias is not None:
        in_specs.append(pl.BlockSpec((1, tn), lambda i, j, k: (0, j)))
        args.append(bias.reshape(1, n).astype(F32))
    if res is not None:
        in_specs.append(pl.BlockSpec((tm, tn), lambda i, j, k: (i, j)))
        args.append(res)
    outs, got = carried_call(
        body, name=name, grid=(m // tm, n // tn, nk), in_specs=in_specs,
        out_specs=[out_spec], out_shape=[out_struct],
        scratch_shapes=[pltpu.VMEM((tm, tn), F32)] if nk > 1 else [],
        semantics=("parallel", "parallel", "arbitrary"), args=args, carry=carry)
    return outs[0] if carry is None else (outs[0], got)


def _sigmoid(x):
    return 1.0 / (1.0 + jnp.exp(-x))


def _softplus(x):
    return jnp.maximum(x, 0.0) + jnp.log(1.0 + jnp.exp(-jnp.abs(x)))


def _shift_back(x, s, row):
    if s == 0:
        return x
    return jnp.where(row >= s, pltpu.roll(x, s, 0), 0.0)


def _shift_fwd(x, s, row):
    if s == 0:
        return x
    n = x.shape[0]
    return jnp.where(row < n - s, pltpu.roll(x, n - s, 0), 0.0)


def _conv_fwd(x, w_ref, kw, row):
    acc = x * w_ref[pl.ds(kw - 1, 1), :]
    for s in range(1, kw):
        acc = acc + _shift_back(x, s, row) * w_ref[pl.ds(kw - 1 - s, 1), :]
    return acc


def _conv_bwd(x, dy, w_ref, kw, row):
    dx = dy * w_ref[pl.ds(kw - 1, 1), :]
    dw = [None] * kw
    dw[kw - 1] = jnp.sum(dy * x, axis=0, keepdims=True)
    for s in range(1, kw):
        dx = dx + _shift_fwd(dy, s, row) * w_ref[pl.ds(kw - 1 - s, 1), :]
        dw[kw - 1 - s] = jnp.sum(dy * _shift_back(x, s, row), axis=0, keepdims=True)
    return dx, dw


def _rows_to_block(rows, nrows, width):
    rid = lax.broadcasted_iota(jnp.int32, (nrows, width), 0)
    out = jnp.zeros((nrows, width), F32)
    for i, r in enumerate(rows):
        out = jnp.where(rid == i, r, out)
    return out


def rmsnorm_fwd(x, g, *, name):
    t, d = x.shape
    tm = _pick(t, (256, 128))

    def body(x_ref, g_ref, o_ref):
        xv = x_ref[...]
        r = lax.rsqrt(jnp.mean(xv * xv, axis=-1, keepdims=True) + EPS)
        o_ref[...] = (xv * r * g_ref[...]).astype(o_ref.dtype)

    return pl.pallas_call(
        body, name=name, grid=(t // tm,),
        in_specs=[pl.BlockSpec((tm, d), lambda i: (i, 0)), pl.BlockSpec((1, d), lambda i: (0, 0))],
        out_specs=pl.BlockSpec((tm, d), lambda i: (i, 0)),
        out_shape=jax.ShapeDtypeStruct((t, d), BF16),
        compiler_params=_cparams(("parallel",)),
    )(x, g.reshape(1, d))


def rmsnorm_bwd(x, g, dh, dres, *, name):
    t, d = x.shape
    tm = _pick(t, (256, 128))

    def body(x_ref, g_ref, dh_ref, dres_ref, dx_ref, dxb_ref, dg_ref):
        xv = x_ref[...]
        r = lax.rsqrt(jnp.mean(xv * xv, axis=-1, keepdims=True) + EPS)
        xh = xv * r
        dy = dh_ref[...]
        dxh = dy * g_ref[...]
        dx = dres_ref[...] + r * (dxh - xh * jnp.mean(dxh * xh, axis=-1, keepdims=True))
        dx_ref[...] = dx
        dxb_ref[...] = dx.astype(BF16)

        @pl.when(pl.program_id(0) == 0)
        def _():
            dg_ref[...] = jnp.zeros_like(dg_ref)

        dg_ref[...] += jnp.sum(dy * xh, axis=0, keepdims=True)

    row = pl.BlockSpec((tm, d), lambda i: (i, 0))
    vec = pl.BlockSpec((1, d), lambda i: (0, 0))
    return pl.pallas_call(
        body, name=name, grid=(t // tm,),
        in_specs=[row, vec, row, row],
        out_specs=[row, row, vec],
        out_shape=[jax.ShapeDtypeStruct((t, d), F32), jax.ShapeDtypeStruct((t, d), BF16),
                   jax.ShapeDtypeStruct((1, d), F32)],
        compiler_params=_cparams(("arbitrary",)),
    )(x, g.reshape(1, d), dh, dres)


def loss_head(x, g, target, *, name):
    t, d = x.shape
    tm = _pick(t, (256, 128))

    def body(x_ref, g_ref, tg_ref, loss_ref, dx_ref, dxb_ref, dg_ref):
        xv = x_ref[...]
        r = lax.rsqrt(jnp.mean(xv * xv, axis=-1, keepdims=True) + EPS)
        xh = xv * r
        err = xh * g_ref[...] - tg_ref[...]
        dy = err * (1.0 / d)
        dxh = dy * g_ref[...]
        dx = r * (dxh - xh * jnp.mean(dxh * xh, axis=-1, keepdims=True))
        dx_ref[...] = dx
        dxb_ref[...] = dx.astype(BF16)

        @pl.when(pl.program_id(0) == 0)
        def _():
            dg_ref[...] = jnp.zeros_like(dg_ref)
            loss_ref[...] = jnp.zeros_like(loss_ref)

        dg_ref[...] += jnp.sum(dy * xh, axis=0, keepdims=True)
        part = jnp.sum(jnp.sum(err * err, axis=-1, keepdims=True), axis=0, keepdims=True) * (0.5 / d)
        loss_ref[...] += jnp.broadcast_to(part, loss_ref.shape)

    row = pl.BlockSpec((tm, d), lambda i: (i, 0))
    vec = pl.BlockSpec((1, d), lambda i: (0, 0))
    return pl.pallas_call(
        body, name=name, grid=(t // tm,),
        in_specs=[row, vec, row],
        out_specs=[pl.BlockSpec((8, 128), lambda i: (0, 0)), row, row, vec],
        out_shape=[jax.ShapeDtypeStruct((8, 128), F32), jax.ShapeDtypeStruct((t, d), F32),
                   jax.ShapeDtypeStruct((t, d), BF16), jax.ShapeDtypeStruct((1, d), F32)],
        compiler_params=_cparams(("arbitrary",)),
    )(x, g.reshape(1, d), target)


def _conf_forward_parts(val, gate, w_ref, b, lg, lb, kw, row, u1=None):
    sg = _sigmoid(gate)
    u0 = val * sg
    if u1 is None:
        u1 = _conv_fwd(u0, w_ref, kw, row) + b
    mu = jnp.mean(u1, axis=-1, keepdims=True)
    xc = u1 - mu
    rs = lax.rsqrt(jnp.mean(xc * xc, axis=-1, keepdims=True) + EPS)
    xh = xc * rs
    u2 = xh * lg + lb
    s2 = _sigmoid(u2)
    return sg, u0, u1, rs, xh, u2, s2


def conf_fwd(p, dw_w, dw_b, ln_g, ln_b, *, nb, s, name, carry=None):
    kw, ch = dw_w.shape
    ng = ch // HEAD

    def body(val_ref, gate_ref, w_ref, b_ref, lg_ref, lb_ref, o_ref, u1_ref):
        row = lax.broadcasted_iota(jnp.int32, (s, HEAD), 0)
        _, _, u1, _, _, u2, s2 = _conf_forward_parts(val_ref[...], gate_ref[...], w_ref, b_ref[...], lg_ref[...],
                                                     lb_ref[...], kw, row)
        o_ref[...] = (u2 * s2).astype(o_ref.dtype)
        u1_ref[...] = u1

    vec = pl.BlockSpec((1, HEAD), lambda b, g: (0, g))
    blk = pl.BlockSpec((s, HEAD), lambda b, g: (b, g))
    outs, got = carried_call(
        body, name=name, grid=(nb, ng),
        in_specs=[blk, pl.BlockSpec((s, HEAD), lambda b, g: (b, ng + g)),
                  pl.BlockSpec((kw, HEAD), lambda b, g: (0, g)), vec, vec, vec],
        out_specs=[blk, blk],
        out_shape=[jax.ShapeDtypeStruct((nb * s, ch), BF16), jax.ShapeDtypeStruct((nb * s, ch), F32)],
        scratch_shapes=[], semantics=("parallel", "parallel"),
        args=(p, p, dw_w, dw_b.reshape(1, ch), ln_g.reshape(1, ch), ln_b.reshape(1, ch)), carry=carry)
    return outs + [got]


def conf_bwd(p, u1, dw_w, dw_b, ln_g, ln_b, du3, dout_a, *, nb, s, name, carry=None):
    kw, ch = dw_w.shape
    ng = ch // HEAD

    def body(val_ref, gate_ref, u1_ref, w_ref, b_ref, lg_ref, lb_ref, du3_ref, doa_ref, dval_ref, dgate_ref, dw_out,
             sm_out):
        row = lax.broadcasted_iota(jnp.int32, (s, HEAD), 0)
        val = val_ref[...]
        sg, u0, _, rs, xh, u2, s2 = _conf_forward_parts(val, gate_ref[...], w_ref, b_ref[...], lg_ref[...],
                                                        lb_ref[...], kw, row, u1=u1_ref[...])
        du2 = du3_ref[...] * (s2 * (1.0 + u2 * (1.0 - s2)))
        dlg = jnp.sum(du2 * xh, axis=0, keepdims=True)
        dlb = jnp.sum(du2, axis=0, keepdims=True)
        dxh = du2 * lg_ref[...]
        du1 = rs * (dxh - jnp.mean(dxh, axis=-1, keepdims=True) - xh * jnp.mean(dxh * xh, axis=-1, keepdims=True))
        ddb = jnp.sum(du1, axis=0, keepdims=True)
        du0, dw = _conv_bwd(u0, du1, w_ref, kw, row)
        dval_ref[...] = (du0 * sg).astype(dval_ref.dtype)
        dgate_ref[...] = (du0 * val * sg * (1.0 - sg)).astype(dgate_ref.dtype)
        for k in range(kw):
            dw_out[0, pl.ds(k, 1), :] = dw[k]
        dpb = jnp.sum(doa_ref[...].astype(F32), axis=0, keepdims=True)
        sm_out[0] = _rows_to_block([ddb, dlg, dlb, dpb], 8, HEAD)

    vec = pl.BlockSpec((1, HEAD), lambda b, g: (0, g))
    blk = pl.BlockSpec((s, HEAD), lambda b, g: (b, g))
    outs, got = carried_call(
        body, name=name, grid=(nb, ng),
        in_specs=[blk, pl.BlockSpec((s, HEAD), lambda b, g: (b, ng + g)), blk,
                  pl.BlockSpec((kw, HEAD), lambda b, g: (0, g)), vec, vec, vec, blk, blk],
        out_specs=[blk, blk, pl.BlockSpec((1, kw, HEAD), lambda b, g: (b, 0, g)),
                   pl.BlockSpec((1, 8, HEAD), lambda b, g: (b, 0, g))],
        out_shape=[jax.ShapeDtypeStruct((nb * s, ch), BF16), jax.ShapeDtypeStruct((nb * s, ch), BF16),
                   jax.ShapeDtypeStruct((nb, kw, ch), F32), jax.ShapeDtypeStruct((nb, 8, ch), F32)],
        scratch_shapes=[], semantics=("parallel", "parallel"),
        args=(p, p, u1, dw_w, dw_b.reshape(1, ch), ln_g.reshape(1, ch), ln_b.reshape(1, ch), du3, dout_a),
        carry=carry)
    return outs + [got]


def qkvconv_fwd(p, w, *, col0, nb, s, name):
    kw, ch = w.shape
    nblk = ch // HEAD
    c0 = col0 // HEAD

    def body(x_ref, w_ref, o_ref):
        row = lax.broadcasted_iota(jnp.int32, (s, HEAD), 0)
        c = _conv_fwd(x_ref[...], w_ref, kw, row)
        o_ref[...] = c * _sigmoid(c)

    return pl.pallas_call(
        body, name=name, grid=(nb, nblk),
        in_specs=[pl.BlockSpec((s, HEAD), lambda b, j: (b, c0 + j)), pl.BlockSpec((kw, HEAD), lambda b, j: (0, j))],
        out_specs=pl.BlockSpec((s, HEAD), lambda b, j: (b, j)),
        out_shape=jax.ShapeDtypeStruct((nb * s, ch), F32),
        compiler_params=_cparams(("parallel", "parallel")),
    )(p, w)


def qkvconv_bwd(p, w, dy, *, col0, nb, s, name):
    kw, ch = w.shape
    nblk = ch // HEAD
    c0 = col0 // HEAD

    def body(x_ref, w_ref, dy_ref, dx_ref, dw_out):
        row = lax.broadcasted_iota(jnp.int32, (s, HEAD), 0)
        xv = x_ref[...]
        c = _conv_fwd(xv, w_ref, kw, row)
        sc = _sigmoid(c)
        dc = dy_ref[...] * (sc * (1.0 + c * (1.0 - sc)))
        dx, dw = _conv_bwd(xv, dc, w_ref, kw, row)
        dx_ref[...] = dx.astype(dx_ref.dtype)
        dw_out[0] = _rows_to_block(dw, 8, HEAD)

    blk = pl.BlockSpec((s, HEAD), lambda b, j: (b, j))
    return pl.pallas_call(
        body, name=name, grid=(nb, nblk),
        in_specs=[pl.BlockSpec((s, HEAD), lambda b, j: (b, c0 + j)), pl.BlockSpec((kw, HEAD), lambda b, j: (0, j)), blk],
        out_specs=[blk, pl.BlockSpec((1, 8, HEAD), lambda b, j: (b, 0, j))],
        out_shape=[jax.ShapeDtypeStruct((nb * s, ch), BF16), jax.ShapeDtypeStruct((nb, 8, ch), F32)],
        compiler_params=_cparams(("parallel", "parallel")),
    )(p, w, dy)


def ffn_act_fwd(up, w, b, *, nb, s, name, carry=None):
    kw, dff = w.shape
    cb = _pick(dff, (256, 128))
    nblk = dff // cb

    def body(g_ref, u_ref, w_ref, b_ref, o_ref):
        row = lax.broadcasted_iota(jnp.int32, (s, cb), 0)
        gc = _conv_fwd(g_ref[...].astype(F32), w_ref, kw, row) + b_ref[...]
        o_ref[...] = (gc * _sigmoid(gc) * u_ref[...].astype(F32)).astype(o_ref.dtype)

    outs, got = carried_call(
        body, name=name, grid=(nb, nblk),
        in_specs=[pl.BlockSpec((s, cb), lambda i, j: (i, j)), pl.BlockSpec((s, cb), lambda i, j: (i, nblk + j)),
                  pl.BlockSpec((kw, cb), lambda i, j: (0, j)), pl.BlockSpec((1, cb), lambda i, j: (0, j))],
        out_specs=[pl.BlockSpec((s, cb), lambda i, j: (i, j))],
        out_shape=[jax.ShapeDtypeStruct((nb * s, dff), BF16)],
        scratch_shapes=[], semantics=("parallel", "parallel"), args=(up, up, w, b.reshape(1, dff)), carry=carry)
    return outs + [got]


def ffn_act_bwd(up, w, b, dact, *, nb, s, name):
    kw, dff = w.shape
    cb = _pick(dff, (256, 128))
    nblk = dff // cb

    def body(g_ref, u_ref, w_ref, b_ref, da_ref, dg_ref, du_ref, sm_out):
        row = lax.broadcasted_iota(jnp.int32, (s, cb), 0)
        gv = g_ref[...].astype(F32)
        gc = _conv_fwd(gv, w_ref, kw, row) + b_ref[...]
        sc = _sigmoid(gc)
        da = da_ref[...].astype(F32)
        du_ref[...] = (da * gc * sc).astype(du_ref.dtype)
        dgc = da * u_ref[...].astype(F32) * (sc * (1.0 + gc * (1.0 - sc)))
        dgate, dw = _conv_bwd(gv, dgc, w_ref, kw, row)
        dg_ref[...] = dgate.astype(dg_ref.dtype)
        sm_out[0] = _rows_to_block(dw + [jnp.sum(dgc, axis=0, keepdims=True)], 8, cb)

    blk = pl.BlockSpec((s, cb), lambda i, j: (i, j))
    return pl.pallas_call(
        body, name=name, grid=(nb, nblk),
        in_specs=[blk, pl.BlockSpec((s, cb), lambda i, j: (i, nblk + j)),
                  pl.BlockSpec((kw, cb), lambda i, j: (0, j)), pl.BlockSpec((1, cb), lambda i, j: (0, j)), blk],
        out_specs=[blk, blk, pl.BlockSpec((1, 8, cb), lambda i, j: (i, 0, j))],
        out_shape=[jax.ShapeDtypeStruct((nb * s, dff), BF16), jax.ShapeDtypeStruct((nb * s, dff), BF16),
                   jax.ShapeDtypeStruct((nb, 8, dff), F32)],
        compiler_params=_cparams(("parallel", "parallel")),
    )(up, up, w, b.reshape(1, dff), dact)


def _dot(a, b, dims):
    return lax.dot_general(a, b, (dims, ((0,), (0,))), preferred_element_type=F32)


def _mm_nn(a, b):
    return _dot(a.astype(BF16), b.astype(BF16), ((2,), (1,)))


def _mm_nt(a, b):
    return _dot(a.astype(BF16), b.astype(BF16), ((2,), (2,)))


def _mm_tn(a, b):
    return _dot(a.astype(BF16), b.astype(BF16), ((1,), (1,)))


def _split3(x):
    hi = x.astype(BF16)
    rest = x - hi.astype(F32)
    mid = rest.astype(BF16)
    return hi, mid, (rest - mid.astype(F32)).astype(BF16)


def _mask_dot(mask, x, dims, mask_first):
    if mask_first:
        return sum(_dot(mask, p, dims) for p in _split3(x))
    return sum(_dot(p, mask, dims) for p in _split3(x))


@jax.custom_vjp
def _mask_nn(mask, x):
    return _mask_dot(mask, x, ((2,), (1,)), True)


def _mask_nn_fwd(mask, x):
    return _mask_nn(mask, x), mask


def _mask_nn_bwd(mask, ct):
    return jnp.zeros_like(mask), _mask_dot(mask, ct, ((1,), (1,)), True)


_mask_nn.defvjp(_mask_nn_fwd, _mask_nn_bwd)


@jax.custom_vjp
def _mask_tn(x, mask):
    return _mask_dot(mask, x, ((1,), (1,)), False)


def _mask_tn_fwd(x, mask):
    return _mask_tn(x, mask), mask


def _mask_tn_bwd(mask, ct):
    return _mask_dot(mask, ct, ((2,), (2,)), True), jnp.zeros_like(mask)


_mask_tn.defvjp(_mask_tn_fwd, _mask_tn_bwd)


GDN_ROWS = 256


def _gdn_chunk(qc, kc, vc, zc, braw, araw, alog, dtb, ng, state):
    nhead, r = qc.shape[0], qc.shape[1]
    ri = lax.broadcasted_iota(jnp.int32, (r, r), 0)
    ci = lax.broadcasted_iota(jnp.int32, (r, r), 1)
    same = (ri // CHUNK) == (ci // CHUNK)
    causal = same & (ri >= ci)
    strict = same & (ri > ci)
    eye = (ri == ci).astype(F32)
    row_chunk = lax.broadcasted_iota(jnp.int32, (r, HEAD), 0) // CHUNK
    per_head = lambda m: jnp.broadcast_to(m.astype(BF16), (nhead, r, r))

    q = qc * lax.rsqrt(jnp.sum(qc * qc, axis=-1, keepdims=True) + EPS) * (HEAD ** -0.5)
    k = kc * lax.rsqrt(jnp.sum(kc * kc, axis=-1, keepdims=True) + EPS)
    beta = _sigmoid(braw)
    g = -jnp.exp(alog) * _softplus(araw + dtb)

    g_w = jnp.broadcast_to(g, (nhead, r, HEAD))
    widen = lambda t: jnp.concatenate([t] * (r // HEAD), axis=2)
    gw = _mask_nn(per_head(causal), g_w)
    gi = widen(gw)
    gj = _mask_tn(widen(g_w), per_head(same & (ri <= ci)))
    decay = jnp.where(causal, jnp.exp(jnp.where(causal, gi - gj, 0.0)), 0.0)

    kb = k * beta
    vb = vc * beta
    lmat = jnp.where(strict, _mm_nt(kb, k) * decay, 0.0)
    x = -lmat
    ainv = eye + x
    p = 1
    while 2 * p < CHUNK:
        x = _mm_nn(x, x)
        ainv = _mm_nn(ainv, eye + x)
        p *= 2
    u = _mm_nn(ainv, vb)
    w = _mm_nn(ainv, kb * jnp.exp(gw))
    qk = jnp.where(causal, _mm_nt(q, k) * decay, 0.0)
    qg = q * jnp.exp(gw)

    o = jnp.zeros((nhead, r, HEAD), F32)
    for c in range(r // CHUNK):
        in_c = row_chunk == c
        glast = jnp.sum(jnp.where(in_c, g_w, 0.0), axis=1, keepdims=True)
        v_new = jnp.where(in_c, u - _mm_nn(w, state), 0.0)
        o = o + jnp.where(in_c, _mm_nn(qg, state), 0.0) + _mm_nn(qk, v_new)
        k_dec = jnp.where(in_c, k * jnp.exp(jnp.where(in_c, glast - gw, 0.0)), 0.0)
        state = state * jnp.exp(glast) + _mm_tn(k_dec, v_new)

    o = o * lax.rsqrt(jnp.mean(o * o, axis=-1, keepdims=True) + EPS) * ng
    o = o * (zc * _sigmoid(zc))
    return o, state


GDN_HEADS_FWD = 4
GDN_HEADS_BWD = 4


def _gdn_specs(s, nh, nqk, zcol, n_chunks, gh, single=False):
    rep = nh // nqk
    qw, vw = (gh // rep) * HEAD, gh * HEAD
    assert gh % rep == 0 and nh % gh == 0 and (nqk * HEAD) % qw == 0 and (2 * nqk * HEAD) % vw == 0 and zcol % vw == 0
    k0, v0, z0 = (nqk * HEAD) // qw, (2 * nqk * HEAD) // vw, zcol // vw
    mode = dict(pipeline_mode=pl.Buffered(1)) if single else {}
    return dict(
        q=pl.BlockSpec((s, qw), lambda b, j: (b, j), **mode),
        k=pl.BlockSpec((s, qw), lambda b, j: (b, k0 + j), **mode),
        v=pl.BlockSpec((s, vw), lambda b, j: (b, v0 + j), **mode),
        z=pl.BlockSpec((s, vw), lambda b, j: (b, z0 + j), **mode),
        ba=pl.BlockSpec((s, HEAD), lambda b, j: (b, 0)),
        gp=pl.BlockSpec((8, HEAD), lambda b, j: (0, 0)),
        qk_out=pl.BlockSpec((s, qw), lambda b, j: (b, j), **mode),
        head=pl.BlockSpec((s, vw), lambda b, j: (b, j), **mode),
        head_in=pl.BlockSpec((s, vw), lambda b, j: (b, j), **mode),
        st=pl.BlockSpec((1, gh, n_chunks, HEAD, HEAD), lambda b, j: (b, j, 0, 0, 0)),
        st_in=pl.BlockSpec((1, gh, n_chunks, HEAD, HEAD), lambda b, j: (b, j, 0, 0, 0), **mode),
    )


def _gdn_scalars(gp_ref, h):
    lane = lax.broadcasted_iota(jnp.int32, (1, HEAD), 1)
    sel = (lane == h).astype(F32)
    alog = jnp.sum(gp_ref[pl.ds(0, 1), :] * sel, axis=-1, keepdims=True)
    dtb = jnp.sum(gp_ref[pl.ds(1, 1), :] * sel, axis=-1, keepdims=True)
    return alog, dtb, sel


def _lanes(i):
    return pl.ds(i * HEAD, HEAD)


def gdn_fwd(qkvc, p, pba, gp, *, nb, s, nh, nqk, zcol, name, carry=None):
    rb = min(GDN_ROWS, s)
    n_chunks = s // rb
    gh = GDN_HEADS_FWD
    rep = nh // nqk
    sp = _gdn_specs(s, nh, nqk, zcol, n_chunks, gh)

    def body(q_ref, k_ref, v_ref, z_ref, ba_ref, gp_ref, o_ref, st_ref):
        h0 = pl.program_id(1) * gh
        ng = gp_ref[pl.ds(2, 1), :]
        lane = lax.broadcasted_iota(jnp.int32, (rb, HEAD), 1)
        heads = []
        for i in range(gh):
            alog, dtb, _ = _gdn_scalars(gp_ref, h0 + i)
            heads.append((alog, dtb, (lane == h0 + i).astype(F32), (lane == nh + h0 + i).astype(F32)))

        alogs = jnp.stack([hd[0] for hd in heads])
        dtbs = jnp.stack([hd[1] for hd in heads])

        def step(n, state):
            rows = pl.ds(pl.multiple_of(n * rb, rb), rb)
            ba = ba_ref[rows, :]
            qs = jnp.stack([q_ref[rows, _lanes(i // rep)] for i in range(gh)])
            ks = jnp.stack([k_ref[rows, _lanes(i // rep)] for i in range(gh)])
            vs = jnp.stack([v_ref[rows, _lanes(i)] for i in range(gh)])
            zs = jnp.stack([z_ref[rows, _lanes(i)] for i in range(gh)])
            braw = jnp.stack([jnp.sum(ba * hd[2], axis=-1, keepdims=True) for hd in heads])
            araw = jnp.stack([jnp.sum(ba * hd[3], axis=-1, keepdims=True) for hd in heads])
            o, new_state = _gdn_chunk(qs, ks, vs, zs, braw, araw, alogs, dtbs, ng, state)
            for i in range(gh):
                st_ref[0, i, n] = state[i]
                o_ref[rows, _lanes(i)] = o[i].astype(o_ref.dtype)
            return new_state

        lax.fori_loop(0, n_chunks, step, jnp.zeros((gh, HEAD, HEAD), F32))

    outs, got = carried_call(
        body, name=name, grid=(nb, nh // gh),
        in_specs=[sp["q"], sp["k"], sp["v"], sp["z"], sp["ba"], sp["gp"]],
        out_specs=[sp["head"], sp["st"]],
        out_shape=[jax.ShapeDtypeStruct((nb * s, nh * HEAD), BF16),
                   jax.ShapeDtypeStruct((nb, nh, n_chunks, HEAD, HEAD), F32)],
        scratch_shapes=[], semantics=("parallel", "parallel"), args=(qkvc, qkvc, qkvc, p, pba, gp), carry=carry)
    return outs + [got]


def gdn_bwd(qkvc, p, pba, gp, states, dout, *, nb, s, nh, nqk, zcol, name, carry=None):
    rb = min(GDN_ROWS, s)
    n_chunks = s // rb
    gh = GDN_HEADS_BWD
    rep = nh // nqk
    sp = _gdn_specs(s, nh, nqk, zcol, n_chunks, gh, single=True)

    def body(q_ref, k_ref, v_ref, z_ref, ba_ref, gp_ref, st_ref, do_ref,
             dq_ref, dk_ref, dv_ref, dz_ref, dba_ref, dgp_ref):
        h0 = pl.program_id(1) * gh
        ng = gp_ref[pl.ds(2, 1), :]
        lane = lax.broadcasted_iota(jnp.int32, (rb, HEAD), 1)
        heads = []
        for i in range(gh):
            alog, dtb, sel_row = _gdn_scalars(gp_ref, h0 + i)
            heads.append((alog, dtb, (lane == h0 + i).astype(F32), (lane == nh + h0 + i).astype(F32), sel_row))

        @pl.when(h0 == 0)
        def _():
            dba_ref[...] = jnp.zeros_like(dba_ref)
            dgp_ref[...] = jnp.zeros_like(dgp_ref)

        alogs = jnp.stack([hd[0] for hd in heads])
        dtbs = jnp.stack([hd[1] for hd in heads])

        def step(it, carry):
            dstate, dalog, ddtb, dng = carry
            n = n_chunks - 1 - it
            rows = pl.ds(pl.multiple_of(n * rb, rb), rb)
            ba = ba_ref[rows, :]
            qs = jnp.stack([q_ref[rows, _lanes(i // rep)] for i in range(gh)])
            ks = jnp.stack([k_ref[rows, _lanes(i // rep)] for i in range(gh)])
            vs = jnp.stack([v_ref[rows, _lanes(i)] for i in range(gh)])
            zs = jnp.stack([z_ref[rows, _lanes(i)] for i in range(gh)])
            dos = jnp.stack([do_ref[rows, _lanes(i)] for i in range(gh)])
            braw = jnp.stack([jnp.sum(ba * hd[2], axis=-1, keepdims=True) for hd in heads])
            araw = jnp.stack([jnp.sum(ba * hd[3], axis=-1, keepdims=True) for hd in heads])
            _, vjp = jax.vjp(_gdn_chunk, qs, ks, vs, zs, braw, araw, alogs, dtbs, ng, st_ref[0, :, n])
            gq, gk, gv, gz, gb, ga, galog, gdtb, gng, gstate = vjp((dos, dstate))
            dba = dba_ref[rows, :]
            for i in range(gh):
                dv_ref[rows, _lanes(i)] = gv[i]
                dz_ref[rows, _lanes(i)] = gz[i].astype(dz_ref.dtype)
                dba = dba + (gb[i] * heads[i][2] + ga[i] * heads[i][3])
            for j in range(gh // rep):
                dq_ref[rows, _lanes(j)] = sum(gq[i] for i in range(j * rep, (j + 1) * rep))
                dk_ref[rows, _lanes(j)] = sum(gk[i] for i in range(j * rep, (j + 1) * rep))
            dba_ref[rows, :] = dba
            return gstate, dalog + galog, ddtb + gdtb, dng + gng

        init = (jnp.zeros((gh, HEAD, HEAD), F32), jnp.zeros((gh, 1, 1), F32), jnp.zeros((gh, 1, 1), F32),
                jnp.zeros((1, HEAD), F32))
        _, dalog, ddtb, dng = lax.fori_loop(0, n_chunks, step, init)
        rows3 = [sum(dalog[i] * heads[i][4] for i in range(gh)), sum(ddtb[i] * heads[i][4] for i in range(gh)), dng]
        dgp_ref[0] += _rows_to_block(rows3, 8, HEAD)

    outs, got = carried_call(
        body, name=name, grid=(nb, nh // gh),
        in_specs=[sp["q"], sp["k"], sp["v"], sp["z"], sp["ba"], sp["gp"], sp["st_in"], sp["head_in"]],
        out_specs=[sp["qk_out"], sp["qk_out"], sp["head"], sp["head"], sp["ba"],
                   pl.BlockSpec((1, 8, HEAD), lambda b, j: (b, 0, 0))],
        out_shape=[jax.ShapeDtypeStruct((nb * s, nqk * HEAD), F32), jax.ShapeDtypeStruct((nb * s, nqk * HEAD), F32),
                   jax.ShapeDtypeStruct((nb * s, nh * HEAD), F32), jax.ShapeDtypeStruct((nb * s, nh * HEAD), BF16),
                   jax.ShapeDtypeStruct((nb * s, HEAD), F32), jax.ShapeDtypeStruct((nb, 8, HEAD), F32)],
        scratch_shapes=[], semantics=("parallel", "arbitrary"),
        args=(qkvc, qkvc, qkvc, p, pba, gp, states, dout), carry=carry)
    return outs + [got]


ADAM_LR = 0.001
ADAM_B1 = 0.9
ADAM_B2 = 0.999
ADAM_EPS = 1e-08
ADAM_WD = 0.01
ADAM_STEP = 10
EW_BLOCK_BYTES = 1 << 20


def _row_tile(rows, cols):
    for tr in (1024, 512, 256, 128, 64, 32, 16, 8):
        if rows % tr == 0 and tr * cols * 4 <= EW_BLOCK_BYTES:
            return tr
    return rows


def sum_slots(rs, *, name):
    nl = len(rs)
    n, rows, cols = rs[0].shape
    tr = _row_tile(rows, cols)
    nblk = rows // tr

    def body(*refs):
        o_ref = refs[nl]
        for l in range(nl):
            @pl.when(pl.program_id(0) == l)
            def _(l=l):
                acc = refs[l][0].astype(F32)
                for i in range(1, n):
                    acc = acc + refs[l][i].astype(F32)
                o_ref[0] = acc

    def in_map(l):
        return lambda li, i: (0, jnp.where(li == l, i, jnp.where(li < l, 0, nblk - 1)), 0)

    return pl.pallas_call(
        body, name=name, grid=(nl, nblk),
        in_specs=[pl.BlockSpec((n, tr, cols), in_map(l)) for l in range(nl)],
        out_specs=pl.BlockSpec((1, tr, cols), lambda li, i: (li, i, 0)),
        out_shape=jax.ShapeDtypeStruct((nl, rows, cols), F32),
        compiler_params=_cparams(("arbitrary", "arbitrary")),
    )(*rs)


def adamw(g_parts, w, m, v, *, name):
    shape = w.shape
    cols = shape[-1]
    lead = shape[0] if w.ndim == 3 else 1
    view = (lambda a: a.reshape(shape)) if w.ndim == 3 else (lambda a: a.reshape(1, -1, cols))
    rows = view(w).shape[1]
    tr = _row_tile(rows, cols)
    npart = len(g_parts)
    c1 = 1.0 - ADAM_B1 ** ADAM_STEP
    c2 = 1.0 - ADAM_B2 ** ADAM_STEP

    def body(*refs):
        w_ref, m_ref, v_ref = refs[npart:npart + 3]
        g_ref, d_ref, nm_ref, nv_ref = refs[npart + 3:]
        g = refs[0][...]
        for i in range(1, npart):
            g = g + refs[i][...]
        nm = ADAM_B1 * m_ref[...] + (1.0 - ADAM_B1) * g
        nv = ADAM_B2 * v_ref[...] + (1.0 - ADAM_B2) * (g * g)
        g_ref[...] = g
        nm_ref[...] = nm
        nv_ref[...] = nv
        d_ref[...] = -ADAM_LR * ((nm / c1) / (jnp.sqrt(nv / c2) + ADAM_EPS) + ADAM_WD * w_ref[...])

    blk = pl.BlockSpec((1, tr, cols), lambda l, i: (l, i, 0))
    outs = pl.pallas_call(
        body, name=name, grid=(lead, rows // tr),
        in_specs=[blk] * (npart + 3),
        out_specs=[blk] * 4,
        out_shape=[jax.ShapeDtypeStruct((lead, rows, cols), F32)] * 4,
        compiler_params=_cparams(("parallel", "parallel")),
    )(*[view(a) for a in g_parts], view(w), view(m), view(v))
    return tuple(o.reshape(shape) for o in outs)


def allreduce_small(vec, *, name):
    r = vec.shape[0]

    def body(v_ref, o_ref, slots, send_sems, recv_sems):
        x, y, c = lax.axis_index("x"), lax.axis_index("y"), lax.axis_index("c")
        me = 4 * x + 2 * y + c
        slots[me] = v_ref[...]
        copies = []
        for j in range(1, 8):
            px = 1 - x if j & 4 else x
            py = 1 - y if j & 2 else y
            pc = 1 - c if j & 1 else c
            rc = pltpu.make_async_remote_copy(src_ref=v_ref, dst_ref=slots.at[me], send_sem=send_sems.at[j - 1],
                                              recv_sem=recv_sems.at[j - 1], device_id=(px, py, pc), device_id_type=MESH)
            rc.start()
            copies.append(rc)
        for cp in copies:
            cp.wait()
        acc = slots[0]
        for i in range(1, 8):
            acc = acc + slots[i]
        o_ref[...] = acc

    vm = pl.BlockSpec(memory_space=pltpu.VMEM)
    return pl.pallas_call(
        body, name=name, in_specs=[vm], out_specs=vm,
        out_shape=jax.ShapeDtypeStruct((r, 128), F32),
        scratch_shapes=[pltpu.VMEM((8, r, 128), F32), pltpu.SemaphoreType.DMA((7,)), pltpu.SemaphoreType.DMA((7,))],
        compiler_params=pltpu.CompilerParams(vmem_limit_bytes=VMEM_LIMIT),
    )(vec)


WEIGHTS = ("mix_norm_g", "w_in", "conv_dw_w", "conv_dw_b", "conv_ln_g", "conv_ln_b", "conv_pw_w", "conv_pw_b",
           "gdn_conv_w", "gdn_a_log", "gdn_dt_bias", "gdn_norm_g", "w_out", "ffn_norm_g", "w_up", "ffn_conv_w",
           "ffn_conv_b", "w_down", "final_norm_g")
COL_SHARDED = ("w_in", "w_up", "conv_dw_w", "gdn_conv_w", "ffn_conv_w")
ROW_SHARDED = ("conv_pw_w", "w_out", "w_down")
BIG = ("w_in", "conv_pw_w", "w_out", "w_up", "w_down")
SMALL_CONV = ("conv_dw_w", "gdn_conv_w", "ffn_conv_w")


def _full_from_slots(name, part):
    if name in COL_SHARDED:
        r, cs = part.shape[1:]
        return jnp.transpose(part, (1, 0, 2)).reshape(r, 4 * cs)
    rs, c = part.shape[1:]
    return part.reshape(4 * rs, c)


def _slots_from_full(name, full):
    if name in COL_SHARDED:
        r, c = full.shape
        return jnp.transpose(full.reshape(r, 4, c // 4), (1, 0, 2))
    r, c = full.shape
    return full.reshape(4, r // 4, c)


def _pack(parts):
    flat = jnp.concatenate([p.reshape(-1).astype(F32) for p in parts])
    pad = (-flat.shape[0]) % 1024
    return jnp.pad(flat, (0, pad)).reshape(-1, 128)


def _unpack(vec, shapes):
    flat = vec.reshape(-1)
    out, pos = [], 0
    for shp in shapes:
        n = 1
        for d in shp:
            n *= d
        out.append(flat[pos:pos + n].reshape(shp))
        pos += n
    return out


def kernel(x, mix_norm_g, w_in, conv_dw_w, conv_dw_b, conv_ln_g, conv_ln_b, conv_pw_w, conv_pw_b, gdn_conv_w, gdn_a_log, gdn_dt_bias, gdn_norm_g, w_out, ffn_norm_g, w_up, ffn_conv_w, ffn_conv_b, w_down, final_norm_g, loss_target, m_mix_norm_g, m_w_in, m_conv_dw_w, m_conv_dw_b, m_conv_ln_g, m_conv_ln_b, m_conv_pw_w, m_conv_pw_b, m_gdn_conv_w, m_gdn_a_log, m_gdn_dt_bias, m_gdn_norm_g, m_w_out, m_ffn_norm_g, m_w_up, m_ffn_conv_w, m_ffn_conv_b, m_w_down, m_final_norm_g, v_mix_norm_g, v_w_in, v_conv_dw_w, v_conv_dw_b, v_conv_ln_g, v_conv_ln_b, v_conv_pw_w, v_conv_pw_b, v_gdn_conv_w, v_gdn_a_log, v_gdn_dt_bias, v_gdn_norm_g, v_w_out, v_ffn_norm_g, v_w_up, v_ffn_conv_w, v_ffn_conv_b, v_w_down, v_final_norm_g):
    wts = dict(zip(WEIGHTS, (mix_norm_g, w_in, conv_dw_w, conv_dw_b, conv_ln_g, conv_ln_b, conv_pw_w, conv_pw_b,
                             gdn_conv_w, gdn_a_log, gdn_dt_bias, gdn_norm_g, w_out, ffn_norm_g, w_up, ffn_conv_w,
                             ffn_conv_b, w_down, final_norm_g)))
    mom = dict(zip(WEIGHTS, (m_mix_norm_g, m_w_in, m_conv_dw_w, m_conv_dw_b, m_conv_ln_g, m_conv_ln_b, m_conv_pw_w,
                             m_conv_pw_b, m_gdn_conv_w, m_gdn_a_log, m_gdn_dt_bias, m_gdn_norm_g, m_w_out,
                             m_ffn_norm_g, m_w_up, m_ffn_conv_w, m_ffn_conv_b, m_w_down, m_final_norm_g)))
    var = dict(zip(WEIGHTS, (v_mix_norm_g, v_w_in, v_conv_dw_w, v_conv_dw_b, v_conv_ln_g, v_conv_ln_b, v_conv_pw_w,
                             v_conv_pw_b, v_gdn_conv_w, v_gdn_a_log, v_gdn_dt_bias, v_gdn_norm_g, v_w_out,
                             v_ffn_norm_g, v_w_up, v_ffn_conv_w, v_ffn_conv_b, v_w_down, v_final_norm_g)))

    nb, s, d = x.shape
    t = nb * s
    depth = mix_norm_g.shape[0]
    ch = conv_dw_b.shape[1]
    nh = gdn_a_log.shape[1]
    nqk = nh // 2
    kwid, vwid = nqk * HEAD, nh * HEAD
    main = 2 * ch + 2 * kwid + 2 * vwid
    qcol, zcol = 2 * ch, 2 * ch + 2 * kwid + vwid
    dff = ffn_conv_b.shape[1]
    my_xy = 2 * lax.axis_index("x") + lax.axis_index("y")

    def shard(key):
        n, l, i, k = key
        rows = wts[n].shape[1] // k
        return wts[n][l, i * rows:(i + 1) * rows].astype(BF16)

    gather = lambda *keys: SplitGather([shard(k) for k in keys])
    pieces, full = {}, {}

    def arrived(keys, got):
        for (n, l, i, k), g in zip(keys, got):
            if n == "w_up":
                full[n, l] = g
                continue
            pieces.setdefault((n, l), {})[i] = _full_from_slots(n, g)
            if len(pieces[n, l]) == k:
                blocks = [pieces[n, l][j] for j in range(k)]
                full[n, l] = blocks[0] if k == 1 else jnp.concatenate(blocks, axis=0)

    def carrying_gather(keys, call):
        res = call(carry=gather(*keys))
        arrived(keys, res[-1])
        return res[0] if len(res) == 2 else res[:-1]

    first, small = run_exchanges([gather(("w_in", 0, 0, 1)), Exchange([wts[n] for n in SMALL_CONV], gather=True)],
                                 name="gather_first")
    arrived([("w_in", 0, 0, 1)], first)
    for n, g in zip(SMALL_CONV, small):
        for l in range(depth):
            full[n, l] = _full_from_slots(n, g[:, l])

    xc = x.reshape(t, d)
    saved, lws = [], []
    for l in range(depth):
        w_in_f = full["w_in", l]
        w_main = w_in_f[:, :main]
        w_ba = jnp.pad(w_in_f[:, main:], ((0, 0), (0, HEAD - 2 * nh)))
        gp = (jnp.zeros((8, HEAD), F32).at[0, :nh].set(gdn_a_log[l]).at[1, :nh].set(gdn_dt_bias[l])
              .at[2].set(gdn_norm_g[l]))
        h = rmsnorm_fwd(xc, mix_norm_g[l], name=f"f{l}_norm1")
        p = carrying_gather([("conv_pw_w", l, 0, 1), ("w_out", l, 0, 1)],
                            functools.partial(matmul, h, w_main, name=f"f{l}_in_main"))
        pba = matmul(h, w_ba, name=f"f{l}_in_ba")
        u3, u1, _ = conf_fwd(p, full["conv_dw_w", l], conv_dw_b[l], conv_ln_g[l], conv_ln_b[l], nb=nb, s=s,
                             name=f"f{l}_conf")
        out_a = matmul(u3, full["conv_pw_w", l], bias=conv_pw_b[l], out_dtype=BF16, name=f"f{l}_pw")
        qkvc = qkvconv_fwd(p, full["gdn_conv_w", l], col0=qcol, nb=nb, s=s, name=f"f{l}_qkvconv")
        out_b, states = carrying_gather([("w_up", l, 0, 1)],
                                        functools.partial(gdn_fwd, qkvc, p, pba, gp, nb=nb, s=s, nh=nh, nqk=nqk,
                                                          zcol=zcol, name=f"f{l}_gdn"))
        wout_a, wout_b = full["w_out", l][:ch], full["w_out", l][ch:]
        x1 = matmul(out_a, wout_a, res=xc, name=f"f{l}_out_a")
        x1 = matmul(out_b, wout_b, res=x1, name=f"f{l}_out_b")
        h2 = rmsnorm_fwd(x1, ffn_norm_g[l], name=f"f{l}_norm2")
        up = carrying_gather([("w_down", l, 0, 1)],
                             functools.partial(matmul, h2, full["w_up", l], out_dtype=BF16, name=f"f{l}_up"))
        act_call = functools.partial(ffn_act_fwd, up, full["ffn_conv_w", l], ffn_conv_b[l], nb=nb, s=s,
                                     name=f"f{l}_act")
        act = act_call()[0]
        if l + 1 < depth:
            x2 = carrying_gather([("w_in", l + 1, 0, 1)],
                                 functools.partial(matmul, act, full["w_down", l], res=x1, name=f"f{l}_down"))
        else:
            x2 = matmul(act, full["w_down", l], res=x1, name=f"f{l}_down")
        saved.append(dict(x=xc, h=h, p=p, pba=pba, u3=u3, u1=u1, out_a=out_a, qkvc=qkvc, out_b=out_b, states=states,
                          x1=x1, h2=h2, up=up, act=act))
        lws.append(dict(w_main=w_main, w_ba=w_ba, pw=full["conv_pw_w", l], wout_a=wout_a, wout_b=wout_b,
                        wup=full["w_up", l], wdown=full["w_down", l], dw_w=full["conv_dw_w", l],
                        gconv_w=full["gdn_conv_w", l], fconv_w=full["ffn_conv_w", l], gp=gp))
        xc = x2

    loss_blk, dx, dxb, dgf = loss_head(xc, final_norm_g, loss_target.reshape(t, d), name="loss_head")

    stacks, received = {}, {}
    scatter = lambda *keys: Exchange([stacks[k] for k in keys], gather=False)

    def produced(n, l, grad, halves=False):
        slots = _slots_from_full(n, grad).astype(BF16)
        if halves:
            half = slots.shape[1] // 2
            stacks[n, l, 0], stacks[n, l, 1] = slots[:, :half], slots[:, half:]
        else:
            stacks[n, l] = slots

    def landed(keys, got):
        for k, g in zip(keys, got):
            received[k] = g

    def carrying(keys, call, **kw):
        res = call(carry=scatter(*keys), **kw)
        landed(keys, res[-1])
        return res[0] if len(res) == 2 else res[:-1]

    small_grads = {n: [None] * depth for n in WEIGHTS if n not in BIG and n != "final_norm_g"}
    for l in reversed(range(depth)):
        lw, sv = lws[l], saved[l]
        dact_call = functools.partial(matmul, dxb, lw["wdown"], tb=True, out_dtype=BF16, name=f"b{l}_dact")
        dact = carrying([("w_in", l + 1, 1)], dact_call) if l + 1 < depth else dact_call()
        produced("w_down", l, matmul(sv["act"], dxb, ta=True, out_dtype=BF16, name=f"b{l}_dwdown"), halves=True)
        dgate, dupv, fpart = ffn_act_bwd(sv["up"], lw["fconv_w"], ffn_conv_b[l], dact, nb=nb, s=s, name=f"b{l}_act")
        dh2 = carrying([("w_down", l, 0)], functools.partial(matmul, dgate, lw["wup"], tb=True, name=f"b{l}_dh2_gate"))
        dh2 = carrying([("w_down", l, 1)], functools.partial(matmul, dupv, lw["wup"], tb=True, b_koff=dff, res=dh2,
                                                             name=f"b{l}_dh2_up"))
        stacks["w_up", l] = jnp.concatenate(
            [matmul(sv["h2"], dgate, ta=True, out_dtype=BF16, out_slots=dff // w_up.shape[2], name=f"b{l}_dwup_gate"),
             matmul(sv["h2"], dupv, ta=True, out_dtype=BF16, out_slots=dff // w_up.shape[2], name=f"b{l}_dwup_up")], axis=0)
        dx1, dx1b, dg2 = rmsnorm_bwd(sv["x1"], ffn_norm_g[l], dh2, dx, name=f"b{l}_norm2")
        fsum = jnp.sum(fpart, axis=0)
        small_grads["ffn_norm_g"][l] = dg2[0]
        small_grads["ffn_conv_w"][l] = fsum[:ffn_conv_w.shape[1]]
        small_grads["ffn_conv_b"][l] = fsum[ffn_conv_w.shape[1]]
        dout_a = matmul(dx1b, lw["wout_a"], tb=True, out_dtype=BF16, name=f"b{l}_dout_a")
        dout_b = matmul(dx1b, lw["wout_b"], tb=True, name=f"b{l}_dout_b")
        produced("w_out", l, jnp.concatenate(
            [matmul(sv["out_a"], dx1b, ta=True, out_dtype=BF16, name=f"b{l}_dwout_a"),
             matmul(sv["out_b"], dx1b, ta=True, out_dtype=BF16, name=f"b{l}_dwout_b")], axis=0))
        du3 = matmul(dout_a, lw["pw"], tb=True, name=f"b{l}_du3")
        produced("conv_pw_w", l, matmul(sv["u3"], dout_a, ta=True, out_dtype=BF16, name=f"b{l}_dwpw"))
        dval, dagate, cw_part, cs_part = carrying(
            [("w_out", l), ("conv_pw_w", l)],
            functools.partial(conf_bwd, sv["p"], sv["u1"], lw["dw_w"], conv_dw_b[l], conv_ln_g[l], conv_ln_b[l], du3,
                              dout_a,
                              nb=nb, s=s, name=f"b{l}_conf"))
        csum = jnp.sum(cs_part, axis=0)
        small_grads["conv_dw_w"][l] = jnp.sum(cw_part, axis=0)
        small_grads["conv_dw_b"][l] = csum[0]
        small_grads["conv_ln_g"][l] = csum[1]
        small_grads["conv_ln_b"][l] = csum[2]
        small_grads["conv_pw_b"][l] = csum[3]
        dq, dk, dv, dz, dpba, dgp = carrying(
            [("w_up", l)],
            functools.partial(gdn_bwd, sv["qkvc"], sv["p"], sv["pba"], lw["gp"], sv["states"], dout_b,
                              nb=nb, s=s, nh=nh, nqk=nqk, zcol=zcol, name=f"b{l}_gdn"))
        dqkv, gw_part = qkvconv_bwd(sv["p"], lw["gconv_w"], jnp.concatenate([dq, dk, dv], axis=1),
                                    col0=qcol, nb=nb, s=s, name=f"b{l}_qkvconv")
        gsum = jnp.sum(dgp, axis=0)
        small_grads["gdn_conv_w"][l] = jnp.sum(gw_part, axis=0)[:gdn_conv_w.shape[1]]
        small_grads["gdn_a_log"][l] = gsum[0, :nh]
        small_grads["gdn_dt_bias"][l] = gsum[1, :nh]
        small_grads["gdn_norm_g"][l] = gsum[2]
        dp = jnp.concatenate([dval, dagate, dqkv, dz], axis=1)
        dw_main = matmul(sv["h"], dp, ta=True, out_dtype=BF16, name=f"b{l}_dwin_main")
        dw_ba = matmul(sv["h"], dpba, ta=True, out_dtype=BF16, name=f"b{l}_dwin_ba")
        produced("w_in", l, jnp.concatenate([dw_main, dw_ba[:, :2 * nh]], axis=1), halves=True)
        dh = carrying([("w_in", l, 0)], functools.partial(matmul, dp, lw["w_main"], tb=True, name=f"b{l}_dh_main"))
        dh = matmul(dpba, lw["w_ba"], tb=True, res=dh, name=f"b{l}_dh_ba")
        dx, dxb, dg1 = rmsnorm_bwd(sv["x"], mix_norm_g[l], dh, dx1, name=f"b{l}_norm1")
        small_grads["mix_norm_g"][l] = dg1[0]

    def summed(n):
        parts = [received[k] for l in range(depth) for k in ([(n, l)] if (n, l) in received else [(n, l, 0), (n, l, 1)])]
        return sum_slots(parts, name=f"sum_{n}").reshape(wts[n].shape)

    early = [n for n in BIG if n != "w_in"]
    partial = {n: summed(n) for n in early}
    swapped, last = run_exchanges([SiblingSwap([partial[n] for n in early]),
                                   Exchange([stacks["w_in", 0, 1]], gather=False)], name="swap_and_scatter_last")
    other = dict(zip(early, swapped))
    received["w_in", 0, 1] = last[0]
    partial["w_in"] = summed("w_in")
    other["w_in"] = run_exchanges([SiblingSwap([partial["w_in"]])], name="swap_w_in")[0][0]

    grads, deltas, new_m, new_v = {}, {}, {}, {}
    for n in BIG:
        grads[n], deltas[n], new_m[n], new_v[n] = adamw([partial[n], other[n]], wts[n], mom[n], var[n],
                                                        name=f"adamw_{n}")

    small_names = [n for n in WEIGHTS if n not in BIG]
    small_full = [jnp.stack(small_grads[n]) if n != "final_norm_g" else dgf[0] for n in small_names]
    packed = _pack(small_full + [loss_blk[0, :1]])
    reduced = allreduce_small(packed, name="allreduce_small")
    parts = _unpack(reduced, [a.shape for a in small_full] + [(1,)])
    loss = parts[-1][0]
    for n, g in zip(small_names, parts[:-1]):
        if n in SMALL_CONV:
            wid = wts[n].shape[-1]
            g = lax.dynamic_slice_in_dim(g, my_xy * wid, wid, axis=g.ndim - 1)
        grads[n], deltas[n], new_m[n], new_v[n] = adamw([g], wts[n], mom[n], var[n], name=f"adamw_{n}")

    return (loss, dx.reshape(nb, s, d), *[grads[n] for n in WEIGHTS], *[deltas[n] for n in WEIGHTS],
            *[new_m[n] for n in WEIGHTS], *[new_v[n] for n in WEIGHTS])
```

```python
import functools

import jax
import jax.numpy as jnp
from jax import lax
from jax.experimental import pallas as pl
from jax.experimental.pallas import tpu as pltpu

F32 = jnp.float32
BF16 = jnp.bfloat16
EPS = 1e-6
CHUNK = 64
HEAD = 128
HIGHEST = lax.Precision.HIGHEST
VMEM_LIMIT = 56 * 1024 * 1024
MM_VMEM_BUDGET = 48 * 1024 * 1024


def _pick(dim, cands):
    for c in cands:
        if dim % c == 0:
            return c
    return dim


def _cparams(sem):
    return pltpu.CompilerParams(dimension_semantics=sem, vmem_limit_bytes=VMEM_LIMIT)


MESH = pl.DeviceIdType.MESH
ANY = pl.BlockSpec(memory_space=pl.ANY)


def _xy_peers():
    x, y = lax.axis_index("x"), lax.axis_index("y")
    peers = []
    for fx, fy in ((0, 1), (1, 0), (1, 1)):
        px = 1 - x if fx else x
        py = 1 - y if fy else y
        peers.append((2 * px + py, px, py))
    return 2 * x + y, peers


class Exchange:
    def __init__(self, arrs, gather):
        self.arrs, self.gather, self.n = list(arrs), gather, len(arrs)
        self.out_shape = [jax.ShapeDtypeStruct((4,) + (a.shape if gather else a.shape[1:]), a.dtype) for a in arrs]
        self.scratch = [pltpu.SemaphoreType.DMA((3 * self.n,)), pltpu.SemaphoreType.DMA((3 * self.n,)),
                        pltpu.SemaphoreType.DMA((self.n,))]

    def copies(self, ins, outs, send_sems, recv_sems, local_sems):
        me, peers = _xy_peers()
        c = lax.axis_index("c")
        out = []
        for k in range(self.n):
            out.append(pltpu.make_async_copy(ins[k] if self.gather else ins[k].at[me], outs[k].at[me],
                                             local_sems.at[k]))
            for j, (slot, px, py) in enumerate(peers):
                out.append(pltpu.make_async_remote_copy(
                    src_ref=ins[k] if self.gather else ins[k].at[slot], dst_ref=outs[k].at[me],
                    send_sem=send_sems.at[3 * k + j], recv_sem=recv_sems.at[3 * k + j],
                    device_id=(px, py, c), device_id_type=MESH))
        return out

    def start(self, ins, outs, sems):
        for cp in self.copies(ins, outs, *sems):
            cp.start()

    def finish(self, ins, outs, sems):
        for cp in self.copies(ins, outs, *sems):
            cp.wait()


class SplitGather:
    def __init__(self, arrs, windows=None, into=None):
        self.arrs, self.n = list(arrs), len(arrs)
        self.windows = list(windows) if windows else [(0, a.shape[0]) for a in arrs]
        self.into = list(into) if into else [None] * self.n
        assert all(a.ndim == 2 and nr % 32 == 0 and r0 % 16 == 0 for a, (r0, nr) in zip(arrs, self.windows))
        self.out_shape = [jax.ShapeDtypeStruct((4,) + a.shape, a.dtype) for a in arrs]
        dma = pltpu.SemaphoreType.DMA
        self.scratch = [dma((3 * self.n,)), dma((3 * self.n,)), dma((3 * self.n,)), dma((3 * self.n,)), dma((self.n,))]

    def _window(self, k):
        return pl.ds(*self.windows[k])

    def _half(self, k, c):
        r0, nr = self.windows[k]
        return pl.ds(pl.multiple_of(r0 + c * (nr // 2), 16), nr // 2)

    def over_ici(self, ins, outs, sems):
        me, peers = _xy_peers()
        c = lax.axis_index("c")
        out = []
        for k in range(self.n):
            out.append(pltpu.make_async_copy(ins[k].at[self._window(k)], outs[k].at[me, self._window(k)],
                                             sems[4].at[k]))
            for j, (slot, px, py) in enumerate(peers):
                out.append(pltpu.make_async_remote_copy(
                    src_ref=ins[k].at[self._half(k, c)], dst_ref=outs[k].at[me, self._half(k, c)],
                    send_sem=sems[0].at[3 * k + j], recv_sem=sems[1].at[3 * k + j],
                    device_id=(px, py, c), device_id_type=MESH))
        return out

    def over_d2d(self, outs, sems):
        _, peers = _xy_peers()
        x, y, c = lax.axis_index("x"), lax.axis_index("y"), lax.axis_index("c")
        out = []
        for k in range(self.n):
            for j, (slot, _, _) in enumerate(peers):
                rows = outs[k].at[slot, self._half(k, c)]
                out.append(pltpu.make_async_remote_copy(
                    src_ref=rows, dst_ref=rows, send_sem=sems[2].at[3 * k + j], recv_sem=sems[3].at[3 * k + j],
                    device_id=(x, y, 1 - c), device_id_type=MESH))
        return out

    def start(self, ins, outs, sems):
        for cp in self.over_ici(ins, outs, sems):
            cp.start()

    def finish(self, ins, outs, sems):
        for cp in self.over_ici(ins, outs, sems):
            cp.wait()
        passed = self.over_d2d(outs, sems)
        for cp in passed:
            cp.start()
        for cp in passed:
            cp.wait()


class SiblingSwap:
    def __init__(self, arrs):
        self.arrs, self.n = list(arrs), len(arrs)
        self.out_shape = [jax.ShapeDtypeStruct(a.shape, a.dtype) for a in arrs]
        self.scratch = [pltpu.SemaphoreType.DMA((self.n,)), pltpu.SemaphoreType.DMA((self.n,))]

    def copies(self, ins, outs, sems):
        peer = (lax.axis_index("x"), lax.axis_index("y"), 1 - lax.axis_index("c"))
        return [pltpu.make_async_remote_copy(src_ref=ins[k], dst_ref=outs[k], send_sem=sems[0].at[k],
                                             recv_sem=sems[1].at[k], device_id=peer, device_id_type=MESH)
                for k in range(self.n)]

    def start(self, ins, outs, sems):
        for cp in self.copies(ins, outs, sems):
            cp.start()

    def finish(self, ins, outs, sems):
        for cp in self.copies(ins, outs, sems):
            cp.wait()


def _split_refs(refs, carries, attr):
    groups, pos = [], 0
    for cr in carries:
        n = len(getattr(cr, attr))
        groups.append(refs[pos:pos + n])
        pos += n
    return groups


def carried_call(body, *, name, grid, in_specs, out_specs, out_shape, scratch_shapes, semantics, args, carry=None):
    n_in, n_out, n_scr = len(in_specs), len(out_specs), len(scratch_shapes)
    if carry is None:
        outs = pl.pallas_call(body, name=name, grid=grid, in_specs=in_specs, out_specs=out_specs, out_shape=out_shape,
                              scratch_shapes=scratch_shapes, compiler_params=_cparams(semantics))(*args)
        return list(outs), []
    carries = list(carry) if isinstance(carry, (list, tuple)) else [carry]
    n = sum(cr.n for cr in carries)
    flat_into = [b for cr in carries for b in getattr(cr, "into", [None] * cr.n)]
    extra = [(i, b) for i, b in enumerate(flat_into) if b is not None]
    aliases = {n_in + n + e: n_out + i for e, (i, _) in enumerate(extra)}
    n_x = len(extra)

    def wrapped(*refs):
        ins, cin = refs[:n_in], _split_refs(refs[n_in:n_in + n], carries, "arrs")
        refs = refs[:n_in + n] + refs[n_in + n + n_x:]
        outs = refs[n_in + n:n_in + n + n_out]
        cout = _split_refs(refs[n_in + n + n_out:n_in + 2 * n + n_out], carries, "arrs")
        scratch = refs[n_in + 2 * n + n_out:n_in + 2 * n + n_out + n_scr]
        sems = _split_refs(refs[n_in + 2 * n + n_out + n_scr:], carries, "scratch")
        ids = [pl.program_id(i) for i in range(len(grid))]
        first = functools.reduce(jnp.logical_and, [i == 0 for i in ids])
        last = functools.reduce(jnp.logical_and, [i == g - 1 for i, g in zip(ids, grid)])

        @pl.when(first)
        def _():
            for cr, i, o, s in zip(carries, cin, cout, sems):
                cr.start(i, o, s)

        body(*ins, *outs, *scratch)

        @pl.when(last)
        def _():
            for cr, i, o, s in zip(carries, cin, cout, sems):
                cr.finish(i, o, s)

    res = pl.pallas_call(
        wrapped, name=name, grid=grid, in_specs=list(in_specs) + [ANY] * (n + n_x),
        out_specs=list(out_specs) + [ANY] * n,
        out_shape=list(out_shape) + [s for cr in carries for s in cr.out_shape],
        scratch_shapes=list(scratch_shapes) + [s for cr in carries for s in cr.scratch],
        input_output_aliases=aliases,
        compiler_params=_cparams(tuple("arbitrary" for _ in grid)),
    )(*args, *[a for cr in carries for a in cr.arrs], *[b for _, b in extra])
    got = _split_refs(list(res[n_out:]), carries, "arrs")
    return list(res[:n_out]), (got if isinstance(carry, (list, tuple)) else got[0])


def run_exchanges(carries, *, name):
    n = sum(cr.n for cr in carries)

    def body(*refs):
        cin = _split_refs(refs[:n], carries, "arrs")
        cout = _split_refs(refs[n:2 * n], carries, "arrs")
        sems = _split_refs(refs[2 * n:], carries, "scratch")
        for cr, i, o, s in zip(carries, cin, cout, sems):
            cr.start(i, o, s)
        for cr, i, o, s in zip(carries, cin, cout, sems):
            cr.finish(i, o, s)

    res = pl.pallas_call(body, name=name, in_specs=[ANY] * n, out_specs=[ANY] * n,
                         out_shape=[s for cr in carries for s in cr.out_shape],
                         scratch_shapes=[s for cr in carries for s in cr.scratch],
                         )(*[a for cr in carries for a in cr.arrs])
    return _split_refs(list(res), carries, "arrs")


def matmul(a, b, *, ta=False, tb=False, out_dtype=F32, res=None, bias=None, b_koff=0, out_slots=0, name, carry=None):
    if ta:
        kdim, m = a.shape
    else:
        m, kdim = a.shape
    slot_w = 0
    if b.ndim == 3 and tb:
        nslot, n, slot_w = b.shape
        kb = nslot * slot_w
    elif b.ndim == 3:
        nslot, kb, slot_w = b.shape
        n = nslot * slot_w
    elif tb:
        n, kb = b.shape
    else:
        kb, n = b.shape
    assert kb >= kdim + b_koff, (a.shape, b.shape, ta, tb)
    out_w = n // out_slots if out_slots else n
    if (slot_w and not tb) or out_slots:
        tn = _pick(min(slot_w, out_w) if (slot_w and not tb) else out_w, (1408, 1024, 512, 256, 128))
        assert out_w % tn == 0 and (tb or not slot_w or slot_w % tn == 0)
    else:
        tn = _pick(n, (1024, 512, 256, 128))
    out_bytes = jnp.dtype(out_dtype).itemsize

    def vmem_bytes(tm_, tk_):
        blocks = a.dtype.itemsize * tm_ * tk_ + b.dtype.itemsize * tk_ * tn + out_bytes * tm_ * tn
        blocks += 4 * tm_ * tn if res is not None else 0
        temps = 4 * tm_ * tn + (2 * tk_ * tn if tb else 0)
        return 2 * blocks + temps + (4 * tm_ * tn if tk_ < kdim else 0)

    def longest_k(tm_):
        if slot_w and tb:
            return slot_w
        return next(c for c in (kdim, 4096, 2816, 2560, 2048, 1024, 512, 256, 128)
                    if kdim % c == 0 and b_koff % c == 0 and c % 128 == 0
                    and (vmem_bytes(tm_, c) <= MM_VMEM_BUDGET or c == 128))

    tall = [c for c in (1024, 512) if m % c == 0] or [_pick(m, (256, 128))]
    tm = max(tall, key=lambda c: (longest_k(c), c))
    tk = longest_k(tm)
    nk = kdim // tk
    ko = b_koff // tk
    dims = (((0 if ta else 1,), (1 if tb else 0,)), ((), ()))

    def body(*refs):
        a_ref, b_ref = refs[0], refs[1]
        pos = 2
        bias_ref = res_ref = None
        if bias is not None:
            bias_ref = refs[pos]
            pos += 1
        if res is not None:
            res_ref = refs[pos]
            pos += 1
        o_ref = refs[pos]
        part = lax.dot_general(a_ref[...].astype(BF16), b_ref[...].astype(BF16), dims, preferred_element_type=F32)

        def finish(r):
            if bias_ref is not None:
                r = r + bias_ref[...]
            if res_ref is not None:
                r = r + res_ref[...]
            o_ref[...] = r.astype(o_ref.dtype)

        if nk == 1:
            finish(part)
            return
        acc_ref = refs[pos + 1]
        k = pl.program_id(2)

        @pl.when(k == 0)
        def _():
            acc_ref[...] = part

        @pl.when((k > 0) & (k < nk - 1))
        def _():
            acc_ref[...] += part

        @pl.when(k == nk - 1)
        def _():
            finish(acc_ref[...] + part)

    assert kdim % tk == 0 and b_koff % tk == 0
    a_spec = pl.BlockSpec((tk, tm), lambda i, j, k: (k, i)) if ta else pl.BlockSpec((tm, tk), lambda i, j, k: (i, k))
    if slot_w and tb:
        b_spec = pl.BlockSpec((None, tn, tk), lambda i, j, k: (k + ko, j, 0))
    elif slot_w:
        per = slot_w // tn
        b_spec = pl.BlockSpec((None, tk, tn), lambda i, j, k: (j // per, k + ko, j % per))
    elif tb:
        b_spec = pl.BlockSpec((tn, tk), lambda i, j, k: (j, k + ko))
    else:
        b_spec = pl.BlockSpec((tk, tn), lambda i, j, k: (k + ko, j))
    if out_slots:
        oper = out_w // tn
        out_spec = pl.BlockSpec((None, tm, tn), lambda i, j, k: (j // oper, i, j % oper))
        out_struct = jax.ShapeDtypeStruct((out_slots, m, out_w), out_dtype)
    else:
        out_spec = pl.BlockSpec((tm, tn), lambda i, j, k: (i, j))
        out_struct = jax.ShapeDtypeStruct((m, n), out_dtype)
    in_specs = [a_spec, b_spec]
    args = [a, b]
    if bias is not None:
        in_specs.append(pl.BlockSpec((1, tn), lambda i, j, k: (0, j)))
        args.append(bias.reshape(1, n).astype(F32))
    if res is not None:
        in_specs.append(pl.BlockSpec((tm, tn), lambda i, j, k: (i, j)))
        args.append(res)
    outs, got = carried_call(
        body, name=name, grid=(m // tm, n // tn, nk), in_specs=in_specs,
        out_specs=[out_spec], out_shape=[out_struct],
        scratch_shapes=[pltpu.VMEM((tm, tn), F32)] if nk > 1 else [],
        semantics=("parallel", "parallel", "arbitrary"), args=args, carry=carry)
    return outs[0] if carry is None else (outs[0], got)


def _sigmoid(x):
    return 1.0 / (1.0 + jnp.exp(-x))


def _softplus(x):
    return jnp.maximum(x, 0.0) + jnp.log(1.0 + jnp.exp(-jnp.abs(x)))


def _shift_back(x, s, row):
    if s == 0:
        return x
    return jnp.where(row >= s, pltpu.roll(x, s, 0), 0.0)


def _shift_fwd(x, s, row):
    if s == 0:
        return x
    n = x.shape[0]
    return jnp.where(row < n - s, pltpu.roll(x, n - s, 0), 0.0)


def _conv_fwd(x, w_ref, kw, row):
    acc = x * w_ref[pl.ds(kw - 1, 1), :]
    for s in range(1, kw):
        acc = acc + _shift_back(x, s, row) * w_ref[pl.ds(kw - 1 - s, 1), :]
    return acc


def _conv_bwd(x, dy, w_ref, kw, row):
    dx = dy * w_ref[pl.ds(kw - 1, 1), :]
    dw = [None] * kw
    dw[kw - 1] = jnp.sum(dy * x, axis=0, keepdims=True)
    for s in range(1, kw):
        dx = dx + _shift_fwd(dy, s, row) * w_ref[pl.ds(kw - 1 - s, 1), :]
        dw[kw - 1 - s] = jnp.sum(dy * _shift_back(x, s, row), axis=0, keepdims=True)
    return dx, dw


def _rows_to_block(rows, nrows, width):
    rid = lax.broadcasted_iota(jnp.int32, (nrows, width), 0)
    out = jnp.zeros((nrows, width), F32)
    for i, r in enumerate(rows):
        out = jnp.where(rid == i, r, out)
    return out


def rmsnorm_fwd(x, g, *, name):
    t, d = x.shape
    tm = _pick(t, (256, 128))

    def body(x_ref, g_ref, o_ref):
        xv = x_ref[...]
        r = lax.rsqrt(jnp.mean(xv * xv, axis=-1, keepdims=True) + EPS)
        o_ref[...] = (xv * r * g_ref[...]).astype(o_ref.dtype)

    return pl.pallas_call(
        body, name=name, grid=(t // tm,),
        in_specs=[pl.BlockSpec((tm, d), lambda i: (i, 0)), pl.BlockSpec((1, d), lambda i: (0, 0))],
        out_specs=pl.BlockSpec((tm, d), lambda i: (i, 0)),
        out_shape=jax.ShapeDtypeStruct((t, d), BF16),
        compiler_params=_cparams(("parallel",)),
    )(x, g.reshape(1, d))


def rmsnorm_bwd(x, g, dh, dres, *, name):
    t, d = x.shape
    tm = _pick(t, (256, 128))

    def body(x_ref, g_ref, dh_ref, dres_ref, dx_ref, dxb_ref, dg_ref):
        xv = x_ref[...]
        r = lax.rsqrt(jnp.mean(xv * xv, axis=-1, keepdims=True) + EPS)
        xh = xv * r
        dy = dh_ref[...]
        dxh = dy * g_ref[...]
        dx = dres_ref[...] + r * (dxh - xh * jnp.mean(dxh * xh, axis=-1, keepdims=True))
        dx_ref[...] = dx
        dxb_ref[...] = dx.astype(BF16)

        @pl.when(pl.program_id(0) == 0)
        def _():
            dg_ref[...] = jnp.zeros_like(dg_ref)

        dg_ref[...] += jnp.sum(dy * xh, axis=0, keepdims=True)

    row = pl.BlockSpec((tm, d), lambda i: (i, 0))
    vec = pl.BlockSpec((1, d), lambda i: (0, 0))
    return pl.pallas_call(
        body, name=name, grid=(t // tm,),
        in_specs=[row, vec, row, row],
        out_specs=[row, row, vec],
        out_shape=[jax.ShapeDtypeStruct((t, d), F32), jax.ShapeDtypeStruct((t, d), BF16),
                   jax.ShapeDtypeStruct((1, d), F32)],
        compiler_params=_cparams(("arbitrary",)),
    )(x, g.reshape(1, d), dh, dres)


def loss_head(x, g, target, *, name):
    t, d = x.shape
    tm = _pick(t, (256, 128))

    def body(x_ref, g_ref, tg_ref, loss_ref, dx_ref, dxb_ref, dg_ref):
        xv = x_ref[...]
        r = lax.rsqrt(jnp.mean(xv * xv, axis=-1, keepdims=True) + EPS)
        xh = xv * r
        err = xh * g_ref[...] - tg_ref[...]
        dy = err * (1.0 / d)
        dxh = dy * g_ref[...]
        dx = r * (dxh - xh * jnp.mean(dxh * xh, axis=-1, keepdims=True))
        dx_ref[...] = dx
        dxb_ref[...] = dx.astype(BF16)

        @pl.when(pl.program_id(0) == 0)
        def _():
            dg_ref[...] = jnp.zeros_like(dg_ref)
            loss_ref[...] = jnp.zeros_like(loss_ref)

        dg_ref[...] += jnp.sum(dy * xh, axis=0, keepdims=True)
        part = jnp.sum(jnp.sum(err * err, axis=-1, keepdims=True), axis=0, keepdims=True) * (0.5 / d)
        loss_ref[...] += jnp.broadcast_to(part, loss_ref.shape)

    row = pl.BlockSpec((tm, d), lambda i: (i, 0))
    vec = pl.BlockSpec((1, d), lambda i: (0, 0))
    return pl.pallas_call(
        body, name=name, grid=(t // tm,),
        in_specs=[row, vec, row],
        out_specs=[pl.BlockSpec((8, 128), lambda i: (0, 0)), row, row, vec],
        out_shape=[jax.ShapeDtypeStruct((8, 128), F32), jax.ShapeDtypeStruct((t, d), F32),
                   jax.ShapeDtypeStruct((t, d), BF16), jax.ShapeDtypeStruct((1, d), F32)],
        compiler_params=_cparams(("arbitrary",)),
    )(x, g.reshape(1, d), target)


def _conf_forward_parts(val, gate, w_ref, b, lg, lb, kw, row, u1=None):
    sg = _sigmoid(gate)
    u0 = val * sg
    if u1 is None:
        u1 = _conv_fwd(u0, w_ref, kw, row) + b
    mu = jnp.mean(u1, axis=-1, keepdims=True)
    xc = u1 - mu
    rs = lax.rsqrt(jnp.mean(xc * xc, axis=-1, keepdims=True) + EPS)
    xh = xc * rs
    u2 = xh * lg + lb
    s2 = _sigmoid(u2)
    return sg, u0, u1, rs, xh, u2, s2


def conf_fwd(p, dw_w, dw_b, ln_g, ln_b, *, nb, s, name, carry=None):
    kw, ch = dw_w.shape
    ng = ch // HEAD

    def body(val_ref, gate_ref, w_ref, b_ref, lg_ref, lb_ref, o_ref, u1_ref):
        row = lax.broadcasted_iota(jnp.int32, (s, HEAD), 0)
        _, _, u1, _, _, u2, s2 = _conf_forward_parts(val_ref[...], gate_ref[...], w_ref, b_ref[...], lg_ref[...],
                                                     lb_ref[...], kw, row)
        o_ref[...] = (u2 * s2).astype(o_ref.dtype)
        u1_ref[...] = u1

    vec = pl.BlockSpec((1, HEAD), lambda b, g: (0, g))
    blk = pl.BlockSpec((s, HEAD), lambda b, g: (b, g))
    outs, got = carried_call(
        body, name=name, grid=(nb, ng),
        in_specs=[blk, pl.BlockSpec((s, HEAD), lambda b, g: (b, ng + g)),
                  pl.BlockSpec((kw, HEAD), lambda b, g: (0, g)), vec, vec, vec],
        out_specs=[blk, blk],
        out_shape=[jax.ShapeDtypeStruct((nb * s, ch), BF16), jax.ShapeDtypeStruct((nb * s, ch), F32)],
        scratch_shapes=[], semantics=("parallel", "parallel"),
        args=(p, p, dw_w, dw_b.reshape(1, ch), ln_g.reshape(1, ch), ln_b.reshape(1, ch)), carry=carry)
    return outs + [got]


def conf_bwd(p, u1, dw_w, dw_b, ln_g, ln_b, du3, dout_a, *, nb, s, name, carry=None):
    kw, ch = dw_w.shape
    ng = ch // HEAD

    def body(val_ref, gate_ref, u1_ref, w_ref, b_ref, lg_ref, lb_ref, du3_ref, doa_ref, dval_ref, dgate_ref, dw_out,
             sm_out):
        row = lax.broadcasted_iota(jnp.int32, (s, HEAD), 0)
        val = val_ref[...]
        sg, u0, _, rs, xh, u2, s2 = _conf_forward_parts(val, gate_ref[...], w_ref, b_ref[...], lg_ref[...],
                                                        lb_ref[...], kw, row, u1=u1_ref[...])
        du2 = du3_ref[...] * (s2 * (1.0 + u2 * (1.0 - s2)))
        dlg = jnp.sum(du2 * xh, axis=0, keepdims=True)
        dlb = jnp.sum(du2, axis=0, keepdims=True)
        dxh = du2 * lg_ref[...]
        du1 = rs * (dxh - jnp.mean(dxh, axis=-1, keepdims=True) - xh * jnp.mean(dxh * xh, axis=-1, keepdims=True))
        ddb = jnp.sum(du1, axis=0, keepdims=True)
        du0, dw = _conv_bwd(u0, du1, w_ref, kw, row)
        dval_ref[...] = (du0 * sg).astype(dval_ref.dtype)
        dgate_ref[...] = (du0 * val * sg * (1.0 - sg)).astype(dgate_ref.dtype)
        for k in range(kw):
            dw_out[0, pl.ds(k, 1), :] = dw[k]
        dpb = jnp.sum(doa_ref[...].astype(F32), axis=0, keepdims=True)
        sm_out[0] = _rows_to_block([ddb, dlg, dlb, dpb], 8, HEAD)

    vec = pl.BlockSpec((1, HEAD), lambda b, g: (0, g))
    blk = pl.BlockSpec((s, HEAD), lambda b, g: (b, g))
    outs, got = carried_call(
        body, name=name, grid=(nb, ng),
        in_specs=[blk, pl.BlockSpec((s, HEAD), lambda b, g: (b, ng + g)), blk,
                  pl.BlockSpec((kw, HEAD), lambda b, g: (0, g)), vec, vec, vec, blk, blk],
        out_specs=[blk, blk, pl.BlockSpec((1, kw, HEAD), lambda b, g: (b, 0, g)),
                   pl.BlockSpec((1, 8, HEAD), lambda b, g: (b, 0, g))],
        out_shape=[jax.ShapeDtypeStruct((nb * s, ch), BF16), jax.ShapeDtypeStruct((nb * s, ch), BF16),
                   jax.ShapeDtypeStruct((nb, kw, ch), F32), jax.ShapeDtypeStruct((nb, 8, ch), F32)],
        scratch_shapes=[], semantics=("parallel", "parallel"),
        args=(p, p, u1, dw_w, dw_b.reshape(1, ch), ln_g.reshape(1, ch), ln_b.reshape(1, ch), du3, dout_a),
        carry=carry)
    return outs + [got]


def qkvconv_fwd(p, w, *, col0, nb, s, name):
    kw, ch = w.shape
    nblk = ch // HEAD
    c0 = col0 // HEAD

    def body(x_ref, w_ref, o_ref):
        row = lax.broadcasted_iota(jnp.int32, (s, HEAD), 0)
        c = _conv_fwd(x_ref[...], w_ref, kw, row)
        o_ref[...] = c * _sigmoid(c)

    return pl.pallas_call(
        body, name=name, grid=(nb, nblk),
        in_specs=[pl.BlockSpec((s, HEAD), lambda b, j: (b, c0 + j)), pl.BlockSpec((kw, HEAD), lambda b, j: (0, j))],
        out_specs=pl.BlockSpec((s, HEAD), lambda b, j: (b, j)),
        out_shape=jax.ShapeDtypeStruct((nb * s, ch), F32),
        compiler_params=_cparams(("parallel", "parallel")),
    )(p, w)


def qkvconv_bwd(p, w, dy, *, col0, nb, s, name):
    kw, ch = w.shape
    nblk = ch // HEAD
    c0 = col0 // HEAD

    def body(x_ref, w_ref, dy_ref, dx_ref, dw_out):
        row = lax.broadcasted_iota(jnp.int32, (s, HEAD), 0)
        xv = x_ref[...]
        c = _conv_fwd(xv, w_ref, kw, row)
        sc = _sigmoid(c)
        dc = dy_ref[...] * (sc * (1.0 + c * (1.0 - sc)))
        dx, dw = _conv_bwd(xv, dc, w_ref, kw, row)
        dx_ref[...] = dx.astype(dx_ref.dtype)
        dw_out[0] = _rows_to_block(dw, 8, HEAD)

    blk = pl.BlockSpec((s, HEAD), lambda b, j: (b, j))
    return pl.pallas_call(
        body, name=name, grid=(nb, nblk),
        in_specs=[pl.BlockSpec((s, HEAD), lambda b, j: (b, c0 + j)), pl.BlockSpec((kw, HEAD), lambda b, j: (0, j)), blk],
        out_specs=[blk, pl.BlockSpec((1, 8, HEAD), lambda b, j: (b, 0, j))],
        out_shape=[jax.ShapeDtypeStruct((nb * s, ch), BF16), jax.ShapeDtypeStruct((nb, 8, ch), F32)],
        compiler_params=_cparams(("parallel", "parallel")),
    )(p, w, dy)


def ffn_act_fwd(up, w, b, *, nb, s, name, carry=None):
    kw, dff = w.shape
    cb = _pick(dff, (256, 128))
    nblk = dff // cb

    def body(g_ref, u_ref, w_ref, b_ref, o_ref):
        row = lax.broadcasted_iota(jnp.int32, (s, cb), 0)
        gc = _conv_fwd(g_ref[...].astype(F32), w_ref, kw, row) + b_ref[...]
        o_ref[...] = (gc * _sigmoid(gc) * u_ref[...].astype(F32)).astype(o_ref.dtype)

    outs, got = carried_call(
        body, name=name, grid=(nb, nblk),
        in_specs=[pl.BlockSpec((s, cb), lambda i, j: (i, j)), pl.BlockSpec((s, cb), lambda i, j: (i, nblk + j)),
                  pl.BlockSpec((kw, cb), lambda i, j: (0, j)), pl.BlockSpec((1, cb), lambda i, j: (0, j))],
        out_specs=[pl.BlockSpec((s, cb), lambda i, j: (i, j))],
        out_shape=[jax.ShapeDtypeStruct((nb * s, dff), BF16)],
        scratch_shapes=[], semantics=("parallel", "parallel"), args=(up, up, w, b.reshape(1, dff)), carry=carry)
    return outs + [got]


def ffn_act_bwd(up, w, b, dact, *, nb, s, name):
    kw, dff = w.shape
    cb = _pick(dff, (256, 128))
    nblk = dff // cb

    def body(g_ref, u_ref, w_ref, b_ref, da_ref, dg_ref, du_ref, sm_out):
        row = lax.broadcasted_iota(jnp.int32, (s, cb), 0)
        gv = g_ref[...].astype(F32)
        gc = _conv_fwd(gv, w_ref, kw, row) + b_ref[...]
        sc = _sigmoid(gc)
        da = da_ref[...].astype(F32)
        du_ref[...] = (da * gc * sc).astype(du_ref.dtype)
        dgc = da * u_ref[...].astype(F32) * (sc * (1.0 + gc * (1.0 - sc)))
        dgate, dw = _conv_bwd(gv, dgc, w_ref, kw, row)
        dg_ref[...] = dgate.astype(dg_ref.dtype)
        sm_out[0] = _rows_to_block(dw + [jnp.sum(dgc, axis=0, keepdims=True)], 8, cb)

    blk = pl.BlockSpec((s, cb), lambda i, j: (i, j))
    return pl.pallas_call(
        body, name=name, grid=(nb, nblk),
        in_specs=[blk, pl.BlockSpec((s, cb), lambda i, j: (i, nblk + j)),
                  pl.BlockSpec((kw, cb), lambda i, j: (0, j)), pl.BlockSpec((1, cb), lambda i, j: (0, j)), blk],
        out_specs=[blk, blk, pl.BlockSpec((1, 8, cb), lambda i, j: (i, 0, j))],
        out_shape=[jax.ShapeDtypeStruct((nb * s, dff), BF16), jax.ShapeDtypeStruct((nb * s, dff), BF16),
                   jax.ShapeDtypeStruct((nb, 8, dff), F32)],
        compiler_params=_cparams(("parallel", "parallel")),
    )(up, up, w, b.reshape(1, dff), dact)


def _dot(a, b, dims):
    return lax.dot_general(a, b, (dims, ((0,), (0,))), preferred_element_type=F32)


def _mm_nn(a, b):
    return _dot(a.astype(BF16), b.astype(BF16), ((2,), (1,)))


def _mm_nt(a, b):
    return _dot(a.astype(BF16), b.astype(BF16), ((2,), (2,)))


def _mm_tn(a, b):
    return _dot(a.astype(BF16), b.astype(BF16), ((1,), (1,)))


def _split3(x):
    hi = x.astype(BF16)
    rest = x - hi.astype(F32)
    mid = rest.astype(BF16)
    return hi, mid, (rest - mid.astype(F32)).astype(BF16)


def _mask_dot(mask, x, dims, mask_first):
    if mask_first:
        return sum(_dot(mask, p, dims) for p in _split3(x))
    return sum(_dot(p, mask, dims) for p in _split3(x))


@jax.custom_vjp
def _mask_nn(mask, x):
    return _mask_dot(mask, x, ((2,), (1,)), True)


def _mask_nn_fwd(mask, x):
    return _mask_nn(mask, x), mask


def _mask_nn_bwd(mask, ct):
    return jnp.zeros_like(mask), _mask_dot(mask, ct, ((1,), (1,)), True)


_mask_nn.defvjp(_mask_nn_fwd, _mask_nn_bwd)


@jax.custom_vjp
def _mask_tn(x, mask):
    return _mask_dot(mask, x, ((1,), (1,)), False)


def _mask_tn_fwd(x, mask):
    return _mask_tn(x, mask), mask


def _mask_tn_bwd(mask, ct):
    return _mask_dot(mask, ct, ((2,), (2,)), True), jnp.zeros_like(mask)


_mask_tn.defvjp(_mask_tn_fwd, _mask_tn_bwd)


GDN_ROWS = 256


def _gdn_chunk(qc, kc, vc, zc, braw, araw, alog, dtb, ng, state):
    nhead, r = qc.shape[0], qc.shape[1]
    ri = lax.broadcasted_iota(jnp.int32, (r, r), 0)
    ci = lax.broadcasted_iota(jnp.int32, (r, r), 1)
    same = (ri // CHUNK) == (ci // CHUNK)
    causal = same & (ri >= ci)
    strict = same & (ri > ci)
    eye = (ri == ci).astype(F32)
    row_chunk = lax.broadcasted_iota(jnp.int32, (r, HEAD), 0) // CHUNK
    per_head = lambda m: jnp.broadcast_to(m.astype(BF16), (nhead, r, r))

    q = qc * lax.rsqrt(jnp.sum(qc * qc, axis=-1, keepdims=True) + EPS) * (HEAD ** -0.5)
    k = kc * lax.rsqrt(jnp.sum(kc * kc, axis=-1, keepdims=True) + EPS)
    beta = _sigmoid(braw)
    g = -jnp.exp(alog) * _softplus(araw + dtb)

    g_w = jnp.broadcast_to(g, (nhead, r, HEAD))
    widen = lambda t: jnp.concatenate([t] * (r // HEAD), axis=2)
    gw = _mask_nn(per_head(causal), g_w)
    gi = widen(gw)
    gj = _mask_tn(widen(g_w), per_head(same & (ri <= ci)))
    decay = jnp.where(causal, jnp.exp(jnp.where(causal, gi - gj, 0.0)), 0.0)

    kb = k * beta
    vb = vc * beta
    lmat = jnp.where(strict, _mm_nt(kb, k) * decay, 0.0)
    x = -lmat
    ainv = eye + x
    p = 1
    while 2 * p < CHUNK:
        x = _mm_nn(x, x)
        ainv = _mm_nn(ainv, eye + x)
        p *= 2
    u = _mm_nn(ainv, vb)
    w = _mm_nn(ainv, kb * jnp.exp(gw))
    qk = jnp.where(causal, _mm_nt(q, k) * decay, 0.0)
    qg = q * jnp.exp(gw)

    o = jnp.zeros((nhead, r, HEAD), F32)
    for c in range(r // CHUNK):
        in_c = row_chunk == c
        glast = jnp.sum(jnp.where(in_c, g_w, 0.0), axis=1, keepdims=True)
        v_new = jnp.where(in_c, u - _mm_nn(w, state), 0.0)
        o = o + jnp.where(in_c, _mm_nn(qg, state), 0.0) + _mm_nn(qk, v_new)
        k_dec = jnp.where(in_c, k * jnp.exp(jnp.where(in_c, glast - gw, 0.0)), 0.0)
        state = state * jnp.exp(glast) + _mm_tn(k_dec, v_new)

    o = o * lax.rsqrt(jnp.mean(o * o, axis=-1, keepdims=True) + EPS) * ng
    o = o * (zc * _sigmoid(zc))
    return o, state


GDN_HEADS_FWD = 4
GDN_HEADS_BWD = 4


def _gdn_specs(s, nh, nqk, zcol, n_chunks, gh, single=False):
    rep = nh // nqk
    qw, vw = (gh // rep) * HEAD, gh * HEAD
    assert gh % rep == 0 and nh % gh == 0 and (nqk * HEAD) % qw == 0 and (2 * nqk * HEAD) % vw == 0 and zcol % vw == 0
    k0, v0, z0 = (nqk * HEAD) // qw, (2 * nqk * HEAD) // vw, zcol // vw
    mode = dict(pipeline_mode=pl.Buffered(1)) if single else {}
    return dict(
        q=pl.BlockSpec((s, qw), lambda b, j: (b, j), **mode),
        k=pl.BlockSpec((s, qw), lambda b, j: (b, k0 + j), **mode),
        v=pl.BlockSpec((s, vw), lambda b, j: (b, v0 + j), **mode),
        z=pl.BlockSpec((s, vw), lambda b, j: (b, z0 + j), **mode),
        ba=pl.BlockSpec((s, HEAD), lambda b, j: (b, 0)),
        gp=pl.BlockSpec((8, HEAD), lambda b, j: (0, 0)),
        qk_out=pl.BlockSpec((s, qw), lambda b, j: (b, j), **mode),
        head=pl.BlockSpec((s, vw), lambda b, j: (b, j), **mode),
        head_in=pl.BlockSpec((s, vw), lambda b, j: (b, j), **mode),
        st=pl.BlockSpec((1, gh, n_chunks, HEAD, HEAD), lambda b, j: (b, j, 0, 0, 0)),
        st_in=pl.BlockSpec((1, gh, n_chunks, HEAD, HEAD), lambda b, j: (b, j, 0, 0, 0), **mode),
    )


def _gdn_scalars(gp_ref, h):
    lane = lax.broadcasted_iota(jnp.int32, (1, HEAD), 1)
    sel = (lane == h).astype(F32)
    alog = jnp.sum(gp_ref[pl.ds(0, 1), :] * sel, axis=-1, keepdims=True)
    dtb = jnp.sum(gp_ref[pl.ds(1, 1), :] * sel, axis=-1, keepdims=True)
    return alog, dtb, sel


def _lanes(i):
    return pl.ds(i * HEAD, HEAD)


def gdn_fwd(qkvc, p, pba, gp, *, nb, s, nh, nqk, zcol, name, carry=None):
    rb = min(GDN_ROWS, s)
    n_chunks = s // rb
    gh = GDN_HEADS_FWD
    rep = nh // nqk
    sp = _gdn_specs(s, nh, nqk, zcol, n_chunks, gh)

    def body(q_ref, k_ref, v_ref, z_ref, ba_ref, gp_ref, o_ref, st_ref):
        h0 = pl.program_id(1) * gh
        ng = gp_ref[pl.ds(2, 1), :]
        lane = lax.broadcasted_iota(jnp.int32, (rb, HEAD), 1)
        heads = []
        for i in range(gh):
            alog, dtb, _ = _gdn_scalars(gp_ref, h0 + i)
            heads.append((alog, dtb, (lane == h0 + i).astype(F32), (lane == nh + h0 + i).astype(F32)))

        alogs = jnp.stack([hd[0] for hd in heads])
        dtbs = jnp.stack([hd[1] for hd in heads])

        def step(n, state):
            rows = pl.ds(pl.multiple_of(n * rb, rb), rb)
            ba = ba_ref[rows, :]
            qs = jnp.stack([q_ref[rows, _lanes(i // rep)] for i in range(gh)])
            ks = jnp.stack([k_ref[rows, _lanes(i // rep)] for i in range(gh)])
            vs = jnp.stack([v_ref[rows, _lanes(i)] for i in range(gh)])
            zs = jnp.stack([z_ref[rows, _lanes(i)] for i in range(gh)])
            braw = jnp.stack([jnp.sum(ba * hd[2], axis=-1, keepdims=True) for hd in heads])
            araw = jnp.stack([jnp.sum(ba * hd[3], axis=-1, keepdims=True) for hd in heads])
            o, new_state = _gdn_chunk(qs, ks, vs, zs, braw, araw, alogs, dtbs, ng, state)
            for i in range(gh):
                st_ref[0, i, n] = state[i]
                o_ref[rows, _lanes(i)] = o[i].astype(o_ref.dtype)
            return new_state

        lax.fori_loop(0, n_chunks, step, jnp.zeros((gh, HEAD, HEAD), F32))

    outs, got = carried_call(
        body, name=name, grid=(nb, nh // gh),
        in_specs=[sp["q"], sp["k"], sp["v"], sp["z"], sp["ba"], sp["gp"]],
        out_specs=[sp["head"], sp["st"]],
        out_shape=[jax.ShapeDtypeStruct((nb * s, nh * HEAD), BF16),
                   jax.ShapeDtypeStruct((nb, nh, n_chunks, HEAD, HEAD), F32)],
        scratch_shapes=[], semantics=("parallel", "parallel"), args=(qkvc, qkvc, qkvc, p, pba, gp), carry=carry)
    return outs + [got]


def gdn_bwd(qkvc, p, pba, gp, states, dout, *, nb, s, nh, nqk, zcol, name, carry=None):
    rb = min(GDN_ROWS, s)
    n_chunks = s // rb
    gh = GDN_HEADS_BWD
    rep = nh // nqk
    sp = _gdn_specs(s, nh, nqk, zcol, n_chunks, gh, single=True)

    def body(q_ref, k_ref, v_ref, z_ref, ba_ref, gp_ref, st_ref, do_ref,
             dq_ref, dk_ref, dv_ref, dz_ref, dba_ref, dgp_ref):
        h0 = pl.program_id(1) * gh
        ng = gp_ref[pl.ds(2, 1), :]
        lane = lax.broadcasted_iota(jnp.int32, (rb, HEAD), 1)
        heads = []
        for i in range(gh):
            alog, dtb, sel_row = _gdn_scalars(gp_ref, h0 + i)
            heads.append((alog, dtb, (lane == h0 + i).astype(F32), (lane == nh + h0 + i).astype(F32), sel_row))

        @pl.when(h0 == 0)
        def _():
            dba_ref[...] = jnp.zeros_like(dba_ref)
            dgp_ref[...] = jnp.zeros_like(dgp_ref)

        alogs = jnp.stack([hd[0] for hd in heads])
        dtbs = jnp.stack([hd[1] for hd in heads])

        def step(it, carry):
            dstate, dalog, ddtb, dng = carry
            n = n_chunks - 1 - it
            rows = pl.ds(pl.multiple_of(n * rb, rb), rb)
            ba = ba_ref[rows, :]
            qs = jnp.stack([q_ref[rows, _lanes(i // rep)] for i in range(gh)])
            ks = jnp.stack([k_ref[rows, _lanes(i // rep)] for i in range(gh)])
            vs = jnp.stack([v_ref[rows, _lanes(i)] for i in range(gh)])
            zs = jnp.stack([z_ref[rows, _lanes(i)] for i in range(gh)])
            dos = jnp.stack([do_ref[rows, _lanes(i)] for i in range(gh)])
            braw = jnp.stack([jnp.sum(ba * hd[2], axis=-1, keepdims=True) for hd in heads])
            araw = jnp.stack([jnp.sum(ba * hd[3], axis=-1, keepdims=True) for hd in heads])
            _, vjp = jax.vjp(_gdn_chunk, qs, ks, vs, zs, braw, araw, alogs, dtbs, ng, st_ref[0, :, n])
            gq, gk, gv, gz, gb, ga, galog, gdtb, gng, gstate = vjp((dos, dstate))
            dba = dba_ref[rows, :]
            for i in range(gh):
                dv_ref[rows, _lanes(i)] = gv[i]
                dz_ref[rows, _lanes(i)] = gz[i].astype(dz_ref.dtype)
                dba = dba + (gb[i] * heads[i][2] + ga[i] * heads[i][3])
            for j in range(gh // rep):
                dq_ref[rows, _lanes(j)] = sum(gq[i] for i in range(j * rep, (j + 1) * rep))
                dk_ref[rows, _lanes(j)] = sum(gk[i] for i in range(j * rep, (j + 1) * rep))
            dba_ref[rows, :] = dba
            return gstate, dalog + galog, ddtb + gdtb, dng + gng

        init = (jnp.zeros((gh, HEAD, HEAD), F32), jnp.zeros((gh, 1, 1), F32), jnp.zeros((gh, 1, 1), F32),
                jnp.zeros((1, HEAD), F32))
        _, dalog, ddtb, dng = lax.fori_loop(0, n_chunks, step, init)
        rows3 = [sum(dalog[i] * heads[i][4] for i in range(gh)), sum(ddtb[i] * heads[i][4] for i in range(gh)), dng]
        dgp_ref[0] += _rows_to_block(rows3, 8, HEAD)

    outs, got = carried_call(
        body, name=name, grid=(nb, nh // gh),
        in_specs=[sp["q"], sp["k"], sp["v"], sp["z"], sp["ba"], sp["gp"], sp["st_in"], sp["head_in"]],
        out_specs=[sp["qk_out"], sp["qk_out"], sp["head"], sp["head"], sp["ba"],
                   pl.BlockSpec((1, 8, HEAD), lambda b, j: (b, 0, 0))],
        out_shape=[jax.ShapeDtypeStruct((nb * s, nqk * HEAD), F32), jax.ShapeDtypeStruct((nb * s, nqk * HEAD), F32),
                   jax.ShapeDtypeStruct((nb * s, nh * HEAD), F32), jax.ShapeDtypeStruct((nb * s, nh * HEAD), BF16),
                   jax.ShapeDtypeStruct((nb * s, HEAD), F32), jax.ShapeDtypeStruct((nb, 8, HEAD), F32)],
        scratch_shapes=[], semantics=("parallel", "arbitrary"),
        args=(qkvc, qkvc, qkvc, p, pba, gp, states, dout), carry=carry)
    return outs + [got]


ADAM_LR = 0.001
ADAM_B1 = 0.9
ADAM_B2 = 0.999
ADAM_EPS = 1e-08
ADAM_WD = 0.01
ADAM_STEP = 10
EW_BLOCK_BYTES = 1 << 20


def _row_tile(rows, cols):
    for tr in (1024, 512, 256, 128, 64, 32, 16, 8):
        if rows % tr == 0 and tr * cols * 4 <= EW_BLOCK_BYTES:
            return tr
    return rows


def sum_slots(rs, *, name):
    nl = len(rs)
    n, rows, cols = rs[0].shape
    tr = _row_tile(rows, cols)
    nblk = rows // tr

    def body(*refs):
        o_ref = refs[nl]
        for l in range(nl):
            @pl.when(pl.program_id(0) == l)
            def _(l=l):
                acc = refs[l][0].astype(F32)
                for i in range(1, n):
                    acc = acc + refs[l][i].astype(F32)
                o_ref[0] = acc

    def in_map(l):
        return lambda li, i: (0, jnp.where(li == l, i, jnp.where(li < l, 0, nblk - 1)), 0)

    return pl.pallas_call(
        body, name=name, grid=(nl, nblk),
        in_specs=[pl.BlockSpec((n, tr, cols), in_map(l)) for l in range(nl)],
        out_specs=pl.BlockSpec((1, tr, cols), lambda li, i: (li, i, 0)),
        out_shape=jax.ShapeDtypeStruct((nl, rows, cols), F32),
        compiler_params=_cparams(("arbitrary", "arbitrary")),
    )(*rs)


def adamw(g_parts, w, m, v, *, name):
    shape = w.shape
    cols = shape[-1]
    lead = shape[0] if w.ndim == 3 else 1
    view = (lambda a: a.reshape(shape)) if w.ndim == 3 else (lambda a: a.reshape(1, -1, cols))
    rows = view(w).shape[1]
    tr = _row_tile(rows, cols)
    npart = len(g_parts)
    c1 = 1.0 - ADAM_B1 ** ADAM_STEP
    c2 = 1.0 - ADAM_B2 ** ADAM_STEP

    def body(*refs):
        w_ref, m_ref, v_ref = refs[npart:npart + 3]
        g_ref, d_ref, nm_ref, nv_ref = refs[npart + 3:]
        g = refs[0][...]
        for i in range(1, npart):
            g = g + refs[i][...]
        nm = ADAM_B1 * m_ref[...] + (1.0 - ADAM_B1) * g
        nv = ADAM_B2 * v_ref[...] + (1.0 - ADAM_B2) * (g * g)
        g_ref[...] = g
        nm_ref[...] = nm
        nv_ref[...] = nv
        d_ref[...] = -ADAM_LR * ((nm / c1) / (jnp.sqrt(nv / c2) + ADAM_EPS) + ADAM_WD * w_ref[...])

    blk = pl.BlockSpec((1, tr, cols), lambda l, i: (l, i, 0))
    outs = pl.pallas_call(
        body, name=name, grid=(lead, rows // tr),
        in_specs=[blk] * (npart + 3),
        out_specs=[blk] * 4,
        out_shape=[jax.ShapeDtypeStruct((lead, rows, cols), F32)] * 4,
        compiler_params=_cparams(("parallel", "parallel")),
    )(*[view(a) for a in g_parts], view(w), view(m), view(v))
    return tuple(o.reshape(shape) for o in outs)


def allreduce_small(vec, *, name):
    r = vec.shape[0]

    def body(v_ref, o_ref, slots, send_sems, recv_sems):
        x, y, c = lax.axis_index("x"), lax.axis_index("y"), lax.axis_index("c")
        me = 4 * x + 2 * y + c
        slots[me] = v_ref[...]
        copies = []
        for j in range(1, 8):
            px = 1 - x if j & 4 else x
            py = 1 - y if j & 2 else y
            pc = 1 - c if j & 1 else c
            rc = pltpu.make_async_remote_copy(src_ref=v_ref, dst_ref=slots.at[me], send_sem=send_sems.at[j - 1],
                                              recv_sem=recv_sems.at[j - 1], device_id=(px, py, pc), device_id_type=MESH)
            rc.start()
            copies.append(rc)
        for cp in copies:
            cp.wait()
        acc = slots[0]
        for i in range(1, 8):
            acc = acc + slots[i]
        o_ref[...] = acc

    vm = pl.BlockSpec(memory_space=pltpu.VMEM)
    return pl.pallas_call(
        body, name=name, in_specs=[vm], out_specs=vm,
        out_shape=jax.ShapeDtypeStruct((r, 128), F32),
        scratch_shapes=[pltpu.VMEM((8, r, 128), F32), pltpu.SemaphoreType.DMA((7,)), pltpu.SemaphoreType.DMA((7,))],
        compiler_params=pltpu.CompilerParams(vmem_limit_bytes=VMEM_LIMIT),
    )(vec)


WEIGHTS = ("mix_norm_g", "w_in", "conv_dw_w", "conv_dw_b", "conv_ln_g", "conv_ln_b", "conv_pw_w", "conv_pw_b",
           "gdn_conv_w", "gdn_a_log", "gdn_dt_bias", "gdn_norm_g", "w_out", "ffn_norm_g", "w_up", "ffn_conv_w",
           "ffn_conv_b", "w_down", "final_norm_g")
COL_SHARDED = ("w_in", "w_up", "conv_dw_w", "gdn_conv_w", "ffn_conv_w")
ROW_SHARDED = ("conv_pw_w", "w_out", "w_down")
BIG = ("w_in", "conv_pw_w", "w_out", "w_up", "w_down")
SMALL_CONV = ("conv_dw_w", "gdn_conv_w", "ffn_conv_w")


def _full_from_slots(name, part):
    if name in COL_SHARDED:
        r, cs = part.shape[1:]
        return jnp.transpose(part, (1, 0, 2)).reshape(r, 4 * cs)
    rs, c = part.shape[1:]
    return part.reshape(4 * rs, c)


def _slots_from_full(name, full):
    if name in COL_SHARDED:
        r, c = full.shape
        return jnp.transpose(full.reshape(r, 4, c // 4), (1, 0, 2))
    r, c = full.shape
    return full.reshape(4, r // 4, c)


def _pack(parts):
    flat = jnp.concatenate([p.reshape(-1).astype(F32) for p in parts])
    pad = (-flat.shape[0]) % 1024
    return jnp.pad(flat, (0, pad)).reshape(-1, 128)


def _unpack(vec, shapes):
    flat = vec.reshape(-1)
    out, pos = [], 0
    for shp in shapes:
        n = 1
        for d in shp:
            n *= d
        out.append(flat[pos:pos + n].reshape(shp))
        pos += n
    return out


def kernel(x, mix_norm_g, w_in, conv_dw_w, conv_dw_b, conv_ln_g, conv_ln_b, conv_pw_w, conv_pw_b, gdn_conv_w, gdn_a_log, gdn_dt_bias, gdn_norm_g, w_out, ffn_norm_g, w_up, ffn_conv_w, ffn_conv_b, w_down, final_norm_g, loss_target, m_mix_norm_g, m_w_in, m_conv_dw_w, m_conv_dw_b, m_conv_ln_g, m_conv_ln_b, m_conv_pw_w, m_conv_pw_b, m_gdn_conv_w, m_gdn_a_log, m_gdn_dt_bias, m_gdn_norm_g, m_w_out, m_ffn_norm_g, m_w_up, m_ffn_conv_w, m_ffn_conv_b, m_w_down, m_final_norm_g, v_mix_norm_g, v_w_in, v_conv_dw_w, v_conv_dw_b, v_conv_ln_g, v_conv_ln_b, v_conv_pw_w, v_conv_pw_b, v_gdn_conv_w, v_gdn_a_log, v_gdn_dt_bias, v_gdn_norm_g, v_w_out, v_ffn_norm_g, v_w_up, v_ffn_conv_w, v_ffn_conv_b, v_w_down, v_final_norm_g):
    wts = dict(zip(WEIGHTS, (mix_norm_g, w_in, conv_dw_w, conv_dw_b, conv_ln_g, conv_ln_b, conv_pw_w, conv_pw_b,
                             gdn_conv_w, gdn_a_log, gdn_dt_bias, gdn_norm_g, w_out, ffn_norm_g, w_up, ffn_conv_w,
                             ffn_conv_b, w_down, final_norm_g)))
    mom = dict(zip(WEIGHTS, (m_mix_norm_g, m_w_in, m_conv_dw_w, m_conv_dw_b, m_conv_ln_g, m_conv_ln_b, m_conv_pw_w,
                             m_conv_pw_b, m_gdn_conv_w, m_gdn_a_log, m_gdn_dt_bias, m_gdn_norm_g, m_w_out,
                             m_ffn_norm_g, m_w_up, m_ffn_conv_w, m_ffn_conv_b, m_w_down, m_final_norm_g)))
    var = dict(zip(WEIGHTS, (v_mix_norm_g, v_w_in, v_conv_dw_w, v_conv_dw_b, v_conv_ln_g, v_conv_ln_b, v_conv_pw_w,
                             v_conv_pw_b, v_gdn_conv_w, v_gdn_a_log, v_gdn_dt_bias, v_gdn_norm_g, v_w_out,
                             v_ffn_norm_g, v_w_up, v_ffn_conv_w, v_ffn_conv_b, v_w_down, v_final_norm_g)))

    nb, s, d = x.shape
    t = nb * s
    depth = mix_norm_g.shape[0]
    ch = conv_dw_b.shape[1]
    nh = gdn_a_log.shape[1]
    nqk = nh // 2
    kwid, vwid = nqk * HEAD, nh * HEAD
    main = 2 * ch + 2 * kwid + 2 * vwid
    qcol, zcol = 2 * ch, 2 * ch + 2 * kwid + vwid
    dff = ffn_conv_b.shape[1]
    my_xy = 2 * lax.axis_index("x") + lax.axis_index("y")

    shards = {(n, l): wts[n][l].astype(BF16) for n in BIG for l in range(depth)}
    buffers, covered, full = {}, {}, {}

    def gather(*keys):
        rows = lambda n, k: wts[n].shape[1] // k
        return SplitGather([shards[n, l] for n, l, _, _ in keys],
                           windows=[(i * rows(n, k), rows(n, k)) for n, _, i, k in keys],
                           into=[buffers.get((n, l)) for n, l, _, _ in keys])

    def arrived(keys, got):
        for (n, l, i, k), g in zip(keys, got):
            buffers[n, l] = g
            covered[n, l] = covered.get((n, l), 0) + wts[n].shape[1] // k
            if covered[n, l] == wts[n].shape[1]:
                full[n, l] = g if n == "w_up" else _full_from_slots(n, g)

    def carrying_gather(keys, call):
        res = call(carry=gather(*keys))
        arrived(keys, res[-1])
        return res[0] if len(res) == 2 else res[:-1]

    first, small = run_exchanges([gather(("w_in", 0, 0, 1)), Exchange([wts[n] for n in SMALL_CONV], gather=True)],
                                 name="gather_first")
    arrived([("w_in", 0, 0, 1)], first)
    for n, g in zip(SMALL_CONV, small):
        for l in range(depth):
            full[n, l] = _full_from_slots(n, g[:, l])

    xc = x.reshape(t, d)
    saved, lws = [], []
    for l in range(depth):
        w_in_f = full["w_in", l]
        w_main = w_in_f[:, :main]
        w_ba = jnp.pad(w_in_f[:, main:], ((0, 0), (0, HEAD - 2 * nh)))
        gp = (jnp.zeros((8, HEAD), F32).at[0, :nh].set(gdn_a_log[l]).at[1, :nh].set(gdn_dt_bias[l])
              .at[2].set(gdn_norm_g[l]))
        h = rmsnorm_fwd(xc, mix_norm_g[l], name=f"f{l}_norm1")
        p = carrying_gather([("conv_pw_w", l, 0, 1), ("w_out", l, 0, 1), ("w_up", l, 0, 4)],
                            functools.partial(matmul, h, w_main, name=f"f{l}_in_main"))
        pba = matmul(h, w_ba, name=f"f{l}_in_ba")
        u3, u1 = carrying_gather([("w_up", l, 1, 4)],
                                 functools.partial(conf_fwd, p, full["conv_dw_w", l], conv_dw_b[l], conv_ln_g[l],
                                                   conv_ln_b[l], nb=nb, s=s, name=f"f{l}_conf"))
        out_a = matmul(u3, full["conv_pw_w", l], bias=conv_pw_b[l], out_dtype=BF16, name=f"f{l}_pw")
        qkvc = qkvconv_fwd(p, full["gdn_conv_w", l], col0=qcol, nb=nb, s=s, name=f"f{l}_qkvconv")
        out_b, states = carrying_gather([("w_up", l, 1, 2)],
                                        functools.partial(gdn_fwd, qkvc, p, pba, gp, nb=nb, s=s, nh=nh, nqk=nqk,
                                                          zcol=zcol, name=f"f{l}_gdn"))
        wout_a, wout_b = full["w_out", l][:ch], full["w_out", l][ch:]
        x1 = matmul(out_a, wout_a, res=xc, name=f"f{l}_out_a")
        x1 = matmul(out_b, wout_b, res=x1, name=f"f{l}_out_b")
        h2 = rmsnorm_fwd(x1, ffn_norm_g[l], name=f"f{l}_norm2")
        up = carrying_gather([("w_down", l, 0, 1)],
                             functools.partial(matmul, h2, full["w_up", l], out_dtype=BF16, name=f"f{l}_up"))
        act_call = functools.partial(ffn_act_fwd, up, full["ffn_conv_w", l], ffn_conv_b[l], nb=nb, s=s,
                                     name=f"f{l}_act")
        if l + 1 < depth:
            act = carrying_gather([("w_in", l + 1, 0, 2)], act_call)
            x2 = carrying_gather([("w_in", l + 1, 1, 2)],
                                 functools.partial(matmul, act, full["w_down", l], res=x1, name=f"f{l}_down"))
        else:
            act = act_call()[0]
            x2 = matmul(act, full["w_down", l], res=x1, name=f"f{l}_down")
        saved.append(dict(x=xc, h=h, p=p, pba=pba, u3=u3, u1=u1, out_a=out_a, qkvc=qkvc, out_b=out_b, states=states,
                          x1=x1, h2=h2, up=up, act=act))
        lws.append(dict(w_main=w_main, w_ba=w_ba, pw=full["conv_pw_w", l], wout_a=wout_a, wout_b=wout_b,
                        wup=full["w_up", l], wdown=full["w_down", l], dw_w=full["conv_dw_w", l],
                        gconv_w=full["gdn_conv_w", l], fconv_w=full["ffn_conv_w", l], gp=gp))
        xc = x2

    loss_blk, dx, dxb, dgf = loss_head(xc, final_norm_g, loss_target.reshape(t, d), name="loss_head")

    stacks, received = {}, {}
    scatter = lambda *keys: Exchange([stacks[k] for k in keys], gather=False)

    def produced(n, l, grad, halves=False):
        slots = _slots_from_full(n, grad).astype(BF16)
        if halves:
            half = slots.shape[1] // 2
            stacks[n, l, 0], stacks[n, l, 1] = slots[:, :half], slots[:, half:]
        else:
            stacks[n, l] = slots

    def landed(keys, got):
        for k, g in zip(keys, got):
            received[k] = g

    def carrying(keys, call, **kw):
        res = call(carry=scatter(*keys), **kw)
        landed(keys, res[-1])
        return res[0] if len(res) == 2 else res[:-1]

    small_grads = {n: [None] * depth for n in WEIGHTS if n not in BIG and n != "final_norm_g"}
    for l in reversed(range(depth)):
        lw, sv = lws[l], saved[l]
        dact_call = functools.partial(matmul, dxb, lw["wdown"], tb=True, out_dtype=BF16, name=f"b{l}_dact")
        dact = carrying([("w_in", l + 1, 1)], dact_call) if l + 1 < depth else dact_call()
        produced("w_down", l, matmul(sv["act"], dxb, ta=True, out_dtype=BF16, name=f"b{l}_dwdown"), halves=True)
        dgate, dupv, fpart = ffn_act_bwd(sv["up"], lw["fconv_w"], ffn_conv_b[l], dact, nb=nb, s=s, name=f"b{l}_act")
        dh2 = carrying([("w_down", l, 0)], functools.partial(matmul, dgate, lw["wup"], tb=True, name=f"b{l}_dh2_gate"))
        dh2 = carrying([("w_down", l, 1)], functools.partial(matmul, dupv, lw["wup"], tb=True, b_koff=dff, res=dh2,
                                                             name=f"b{l}_dh2_up"))
        stacks["w_up", l] = jnp.concatenate(
            [matmul(sv["h2"], dgate, ta=True, out_dtype=BF16, out_slots=dff // w_up.shape[2], name=f"b{l}_dwup_gate"),
             matmul(sv["h2"], dupv, ta=True, out_dtype=BF16, out_slots=dff // w_up.shape[2], name=f"b{l}_dwup_up")], axis=0)
        dx1, dx1b, dg2 = rmsnorm_bwd(sv["x1"], ffn_norm_g[l], dh2, dx, name=f"b{l}_norm2")
        fsum = jnp.sum(fpart, axis=0)
        small_grads["ffn_norm_g"][l] = dg2[0]
        small_grads["ffn_conv_w"][l] = fsum[:ffn_conv_w.shape[1]]
        small_grads["ffn_conv_b"][l] = fsum[ffn_conv_w.shape[1]]
        dout_a = matmul(dx1b, lw["wout_a"], tb=True, out_dtype=BF16, name=f"b{l}_dout_a")
        dout_b = matmul(dx1b, lw["wout_b"], tb=True, name=f"b{l}_dout_b")
        produced("w_out", l, jnp.concatenate(
            [matmul(sv["out_a"], dx1b, ta=True, out_dtype=BF16, name=f"b{l}_dwout_a"),
             matmul(sv["out_b"], dx1b, ta=True, out_dtype=BF16, name=f"b{l}_dwout_b")], axis=0))
        du3 = matmul(dout_a, lw["pw"], tb=True, name=f"b{l}_du3")
        produced("conv_pw_w", l, matmul(sv["u3"], dout_a, ta=True, out_dtype=BF16, name=f"b{l}_dwpw"))
        dval, dagate, cw_part, cs_part = carrying(
            [("w_out", l), ("conv_pw_w", l)],
            functools.partial(conf_bwd, sv["p"], sv["u1"], lw["dw_w"], conv_dw_b[l], conv_ln_g[l], conv_ln_b[l], du3,
                              dout_a,
                              nb=nb, s=s, name=f"b{l}_conf"))
        csum = jnp.sum(cs_part, axis=0)
        small_grads["conv_dw_w"][l] = jnp.sum(cw_part, axis=0)
        small_grads["conv_dw_b"][l] = csum[0]
        small_grads["conv_ln_g"][l] = csum[1]
        small_grads["conv_ln_b"][l] = csum[2]
        small_grads["conv_pw_b"][l] = csum[3]
        dq, dk, dv, dz, dpba, dgp = carrying(
            [("w_up", l)],
            functools.partial(gdn_bwd, sv["qkvc"], sv["p"], sv["pba"], lw["gp"], sv["states"], dout_b,
                              nb=nb, s=s, nh=nh, nqk=nqk, zcol=zcol, name=f"b{l}_gdn"))
        dqkv, gw_part = qkvconv_bwd(sv["p"], lw["gconv_w"], jnp.concatenate([dq, dk, dv], axis=1),
                                    col0=qcol, nb=nb, s=s, name=f"b{l}_qkvconv")
        gsum = jnp.sum(dgp, axis=0)
        small_grads["gdn_conv_w"][l] = jnp.sum(gw_part, axis=0)[:gdn_conv_w.shape[1]]
        small_grads["gdn_a_log"][l] = gsum[0, :nh]
        small_grads["gdn_dt_bias"][l] = gsum[1, :nh]
        small_grads["gdn_norm_g"][l] = gsum[2]
        dp = jnp.concatenate([dval, dagate, dqkv, dz], axis=1)
        dw_main = matmul(sv["h"], dp, ta=True, out_dtype=BF16, name=f"b{l}_dwin_main")
        dw_ba = matmul(sv["h"], dpba, ta=True, out_dtype=BF16, name=f"b{l}_dwin_ba")
        produced("w_in", l, jnp.concatenate([dw_main, dw_ba[:, :2 * nh]], axis=1), halves=True)
        dh = carrying([("w_in", l, 0)], functools.partial(matmul, dp, lw["w_main"], tb=True, name=f"b{l}_dh_main"))
        dh = matmul(dpba, lw["w_ba"], tb=True, res=dh, name=f"b{l}_dh_ba")
        dx, dxb, dg1 = rmsnorm_bwd(sv["x"], mix_norm_g[l], dh, dx1, name=f"b{l}_norm1")
        small_grads["mix_norm_g"][l] = dg1[0]

    def summed(n):
        parts = [received[k] for l in range(depth) for k in ([(n, l)] if (n, l) in received else [(n, l, 0), (n, l, 1)])]
        return sum_slots(parts, name=f"sum_{n}").reshape(wts[n].shape)

    early = [n for n in BIG if n != "w_in"]
    partial = {n: summed(n) for n in early}
    swapped, last = run_exchanges([SiblingSwap([partial[n] for n in early]),
                                   Exchange([stacks["w_in", 0, 1]], gather=False)], name="swap_and_scatter_last")
    other = dict(zip(early, swapped))
    received["w_in", 0, 1] = last[0]
    partial["w_in"] = summed("w_in")
    other["w_in"] = run_exchanges([SiblingSwap([partial["w_in"]])], name="swap_w_in")[0][0]

    grads, deltas, new_m, new_v = {}, {}, {}, {}
    for n in BIG:
        grads[n], deltas[n], new_m[n], new_v[n] = adamw([partial[n], other[n]], wts[n], mom[n], var[n],
                                                        name=f"adamw_{n}")

    small_names = [n for n in WEIGHTS if n not in BIG]
    small_full = [jnp.stack(small_grads[n]) if n != "final_norm_g" else dgf[0] for n in small_names]
    packed = _pack(small_full + [loss_blk[0, :1]])
    reduced = allreduce_small(packed, name="allreduce_small")
    parts = _unpack(reduced, [a.shape for a in small_full] + [(1,)])
    loss = parts[-1][0]
    for n, g in zip(small_names, parts[:-1]):
        if n in SMALL_CONV:
            wid = wts[n].shape[-1]
            g = lax.dynamic_slice_in_dim(g, my_xy * wid, wid, axis=g.ndim - 1)
        grads[n], deltas[n], new_m[n], new_v[n] = adamw([g], wts[n], mom[n], var[n], name=f"adamw_{n}")

    return (loss, dx.reshape(nb, s, d), *[grads[n] for n in WEIGHTS], *[deltas[n] for n in WEIGHTS],
            *[new_m[n] for n in WEIGHTS], *[new_v[n] for n in WEIGHTS])
```

```python
import functools

import jax
import jax.numpy as jnp
from jax import lax
from jax.experimental import pallas as pl
from jax.experimental.pallas import tpu as pltpu

F32 = jnp.float32
BF16 = jnp.bfloat16
EPS = 1e-6
CHUNK = 64
HEAD = 128
HIGHEST = lax.Precision.HIGHEST
VMEM_LIMIT = 56 * 1024 * 1024
MM_VMEM_BUDGET = 48 * 1024 * 1024


def _pick(dim, cands):
    for c in cands:
        if dim % c == 0:
            return c
    return dim


def _cparams(sem):
    return pltpu.CompilerParams(dimension_semantics=sem, vmem_limit_bytes=VMEM_LIMIT)


MESH = pl.DeviceIdType.MESH
ANY = pl.BlockSpec(memory_space=pl.ANY)


def _xy_peers():
    x, y = lax.axis_index("x"), lax.axis_index("y")
    peers = []
    for fx, fy in ((0, 1), (1, 0), (1, 1)):
        px = 1 - x if fx else x
        py = 1 - y if fy else y
        peers.append((2 * px + py, px, py))
    return 2 * x + y, peers


class Exchange:
    def __init__(self, arrs, gather):
        self.arrs, self.gather, self.n = list(arrs), gather, len(arrs)
        self.out_shape = [jax.ShapeDtypeStruct((4,) + (a.shape if gather else a.shape[1:]), a.dtype) for a in arrs]
        self.scratch = [pltpu.SemaphoreType.DMA((3 * self.n,)), pltpu.SemaphoreType.DMA((3 * self.n,)),
                        pltpu.SemaphoreType.DMA((self.n,))]

    def copies(self, ins, outs, send_sems, recv_sems, local_sems):
        me, peers = _xy_peers()
        c = lax.axis_index("c")
        out = []
        for k in range(self.n):
            out.append(pltpu.make_async_copy(ins[k] if self.gather else ins[k].at[me], outs[k].at[me],
                                             local_sems.at[k]))
            for j, (slot, px, py) in enumerate(peers):
                out.append(pltpu.make_async_remote_copy(
                    src_ref=ins[k] if self.gather else ins[k].at[slot], dst_ref=outs[k].at[me],
                    send_sem=send_sems.at[3 * k + j], recv_sem=recv_sems.at[3 * k + j],
                    device_id=(px, py, c), device_id_type=MESH))
        return out

    def start(self, ins, outs, sems):
        for cp in self.copies(ins, outs, *sems):
            cp.start()

    def finish(self, ins, outs, sems):
        for cp in self.copies(ins, outs, *sems):
            cp.wait()


class SplitGather:
    def __init__(self, arrs, windows=None, into=None):
        self.arrs, self.n = list(arrs), len(arrs)
        self.windows = list(windows) if windows else [(0, a.shape[0]) for a in arrs]
        self.into = list(into) if into else [None] * self.n
        assert all(a.ndim == 2 and nr % 32 == 0 and r0 % 16 == 0 for a, (r0, nr) in zip(arrs, self.windows))
        self.out_shape = [jax.ShapeDtypeStruct((4,) + a.shape, a.dtype) for a in arrs]
        dma = pltpu.SemaphoreType.DMA
        self.scratch = [dma((3 * self.n,)), dma((3 * self.n,)), dma((3 * self.n,)), dma((3 * self.n,)), dma((self.n,))]

    def _window(self, k):
        return pl.ds(*self.windows[k])

    def _half(self, k, c):
        r0, nr = self.windows[k]
        return pl.ds(pl.multiple_of(r0 + c * (nr // 2), 16), nr // 2)

    def over_ici(self, ins, outs, sems):
        me, peers = _xy_peers()
        c = lax.axis_index("c")
        out = []
        for k in range(self.n):
            out.append(pltpu.make_async_copy(ins[k].at[self._window(k)], outs[k].at[me, self._window(k)],
                                             sems[4].at[k]))
            for j, (slot, px, py) in enumerate(peers):
                out.append(pltpu.make_async_remote_copy(
                    src_ref=ins[k].at[self._half(k, c)], dst_ref=outs[k].at[me, self._half(k, c)],
                    send_sem=sems[0].at[3 * k + j], recv_sem=sems[1].at[3 * k + j],
                    device_id=(px, py, c), device_id_type=MESH))
        return out

    def over_d2d(self, outs, sems):
        _, peers = _xy_peers()
        x, y, c = lax.axis_index("x"), lax.axis_index("y"), lax.axis_index("c")
        out = []
        for k in range(self.n):
            for j, (slot, _, _) in enumerate(peers):
                rows = outs[k].at[slot, self._half(k, c)]
                out.append(pltpu.make_async_remote_copy(
                    src_ref=rows, dst_ref=rows, send_sem=sems[2].at[3 * k + j], recv_sem=sems[3].at[3 * k + j],
                    device_id=(x, y, 1 - c), device_id_type=MESH))
        return out

    def start(self, ins, outs, sems):
        for cp in self.over_ici(ins, outs, sems):
            cp.start()

    def finish(self, ins, outs, sems):
        for cp in self.over_ici(ins, outs, sems):
            cp.wait()
        passed = self.over_d2d(outs, sems)
        for cp in passed:
            cp.start()
        for cp in passed:
            cp.wait()


class SiblingSwap:
    def __init__(self, arrs):
        self.arrs, self.n = list(arrs), len(arrs)
        self.out_shape = [jax.ShapeDtypeStruct(a.shape, a.dtype) for a in arrs]
        self.scratch = [pltpu.SemaphoreType.DMA((self.n,)), pltpu.SemaphoreType.DMA((self.n,))]

    def copies(self, ins, outs, sems):
        peer = (lax.axis_index("x"), lax.axis_index("y"), 1 - lax.axis_index("c"))
        return [pltpu.make_async_remote_copy(src_ref=ins[k], dst_ref=outs[k], send_sem=sems[0].at[k],
                                             recv_sem=sems[1].at[k], device_id=peer, device_id_type=MESH)
                for k in range(self.n)]

    def start(self, ins, outs, sems):
        for cp in self.copies(ins, outs, sems):
            cp.start()

    def finish(self, ins, outs, sems):
        for cp in self.copies(ins, outs, sems):
            cp.wait()


def _split_refs(refs, carries, attr):
    groups, pos = [], 0
    for cr in carries:
        n = len(getattr(cr, attr))
        groups.append(refs[pos:pos + n])
        pos += n
    return groups


def carried_call(body, *, name, grid, in_specs, out_specs, out_shape, scratch_shapes, semantics, args, carry=None):
    n_in, n_out, n_scr = len(in_specs), len(out_specs), len(scratch_shapes)
    if carry is None:
        outs = pl.pallas_call(body, name=name, grid=grid, in_specs=in_specs, out_specs=out_specs, out_shape=out_shape,
                              scratch_shapes=scratch_shapes, compiler_params=_cparams(semantics))(*args)
        return list(outs), []
    carries = list(carry) if isinstance(carry, (list, tuple)) else [carry]
    n = sum(cr.n for cr in carries)
    flat_into = [b for cr in carries for b in getattr(cr, "into", [None] * cr.n)]
    extra = [(i, b) for i, b in enumerate(flat_into) if b is not None]
    aliases = {n_in + n + e: n_out + i for e, (i, _) in enumerate(extra)}
    n_x = len(extra)

    def wrapped(*refs):
        ins, cin = refs[:n_in], _split_refs(refs[n_in:n_in + n], carries, "arrs")
        refs = refs[:n_in + n] + refs[n_in + n + n_x:]
        outs = refs[n_in + n:n_in + n + n_out]
        cout = _split_refs(refs[n_in + n + n_out:n_in + 2 * n + n_out], carries, "arrs")
        scratch = refs[n_in + 2 * n + n_out:n_in + 2 * n + n_out + n_scr]
        sems = _split_refs(refs[n_in + 2 * n + n_out + n_scr:], carries, "scratch")
        ids = [pl.program_id(i) for i in range(len(grid))]
        first = functools.reduce(jnp.logical_and, [i == 0 for i in ids])
        last = functools.reduce(jnp.logical_and, [i == g - 1 for i, g in zip(ids, grid)])

        @pl.when(first)
        def _():
            for cr, i, o, s in zip(carries, cin, cout, sems):
                cr.start(i, o, s)

        body(*ins, *outs, *scratch)

        @pl.when(last)
        def _():
            for cr, i, o, s in zip(carries, cin, cout, sems):
                cr.finish(i, o, s)

    res = pl.pallas_call(
        wrapped, name=name, grid=grid, in_specs=list(in_specs) + [ANY] * (n + n_x),
        out_specs=list(out_specs) + [ANY] * n,
        out_shape=list(out_shape) + [s for cr in carries for s in cr.out_shape],
        scratch_shapes=list(scratch_shapes) + [s for cr in carries for s in cr.scratch],
        input_output_aliases=aliases,
        compiler_params=_cparams(tuple("arbitrary" for _ in grid)),
    )(*args, *[a for cr in carries for a in cr.arrs], *[b for _, b in extra])
    got = _split_refs(list(res[n_out:]), carries, "arrs")
    return list(res[:n_out]), (got if isinstance(carry, (list, tuple)) else got[0])


def run_exchanges(carries, *, name):
    n = sum(cr.n for cr in carries)

    def body(*refs):
        cin = _split_refs(refs[:n], carries, "arrs")
        cout = _split_refs(refs[n:2 * n], carries, "arrs")
        sems = _split_refs(refs[2 * n:], carries, "scratch")
        for cr, i, o, s in zip(carries, cin, cout, sems):
            cr.start(i, o, s)
        for cr, i, o, s in zip(carries, cin, cout, sems):
            cr.finish(i, o, s)

    res = pl.pallas_call(body, name=name, in_specs=[ANY] * n, out_specs=[ANY] * n,
                         out_shape=[s for cr in carries for s in cr.out_shape],
                         scratch_shapes=[s for cr in carries for s in cr.scratch],
                         )(*[a for cr in carries for a in cr.arrs])
    return _split_refs(list(res), carries, "arrs")


def matmul(a, b, *, ta=False, tb=False, out_dtype=F32, res=None, bias=None, b_koff=0, out_slots=0, name, carry=None):
    if ta:
        kdim, m = a.shape
    else:
        m, kdim = a.shape
    slot_w = 0
    if b.ndim == 3 and tb:
        nslot, n, slot_w = b.shape
        kb = nslot * slot_w
    elif b.ndim == 3:
        nslot, kb, slot_w = b.shape
        n = nslot * slot_w
    elif tb:
        n, kb = b.shape
    else:
        kb, n = b.shape
    assert kb >= kdim + b_koff, (a.shape, b.shape, ta, tb)
    out_w = n // out_slots if out_slots else n
    if (slot_w and not tb) or out_slots:
        tn = _pick(min(slot_w, out_w) if (slot_w and not tb) else out_w, (1408, 1024, 512, 256, 128))
        assert out_w % tn == 0 and (tb or not slot_w or slot_w % tn == 0)
    else:
        tn = _pick(n, (1024, 512, 256, 128))
    out_bytes = jnp.dtype(out_dtype).itemsize

    def vmem_bytes(tm_, tk_):
        blocks = a.dtype.itemsize * tm_ * tk_ + b.dtype.itemsize * tk_ * tn + out_bytes * tm_ * tn
        blocks += 4 * tm_ * tn if res is not None else 0
        temps = 4 * tm_ * tn + (2 * tk_ * tn if tb else 0)
        return 2 * blocks + temps + (4 * tm_ * tn if tk_ < kdim else 0)

    def longest_k(tm_):
        if slot_w and tb:
            return slot_w
        return next(c for c in (kdim, 4096, 2816, 2560, 2048, 1024, 512, 256, 128)
                    if kdim % c == 0 and b_koff % c == 0 and c % 128 == 0
                    and (vmem_bytes(tm_, c) <= MM_VMEM_BUDGET or c == 128))

    tall = [c for c in (1024, 512) if m % c == 0] or [_pick(m, (256, 128))]
    tm = max(tall, key=lambda c: (longest_k(c), c))
    tk = longest_k(tm)
    nk = kdim // tk
    ko = b_koff // tk
    dims = (((0 if ta else 1,), (1 if tb else 0,)), ((), ()))

    def body(*refs):
        a_ref, b_ref = refs[0], refs[1]
        pos = 2
        bias_ref = res_ref = None
        if bias is not None:
            bias_ref = refs[pos]
            pos += 1
        if res is not None:
            res_ref = refs[pos]
            pos += 1
        o_ref = refs[pos]
        part = lax.dot_general(a_ref[...].astype(BF16), b_ref[...].astype(BF16), dims, preferred_element_type=F32)

        def finish(r):
            if bias_ref is not None:
                r = r + bias_ref[...]
            if res_ref is not None:
                r = r + res_ref[...]
            o_ref[...] = r.astype(o_ref.dtype)

        if nk == 1:
            finish(part)
            return
        acc_ref = refs[pos + 1]
        k = pl.program_id(2)

        @pl.when(k == 0)
        def _():
            acc_ref[...] = part

        @pl.when((k > 0) & (k < nk - 1))
        def _():
            acc_ref[...] += part

        @pl.when(k == nk - 1)
        def _():
            finish(acc_ref[...] + part)

    assert kdim % tk == 0 and b_koff % tk == 0
    a_spec = pl.BlockSpec((tk, tm), lambda i, j, k: (k, i)) if ta else pl.BlockSpec((tm, tk), lambda i, j, k: (i, k))
    if slot_w and tb:
        b_spec = pl.BlockSpec((None, tn, tk), lambda i, j, k: (k + ko, j, 0))
    elif slot_w:
        per = slot_w // tn
        b_spec = pl.BlockSpec((None, tk, tn), lambda i, j, k: (j // per, k + ko, j % per))
    elif tb:
        b_spec = pl.BlockSpec((tn, tk), lambda i, j, k: (j, k + ko))
    else:
        b_spec = pl.BlockSpec((tk, tn), lambda i, j, k: (k + ko, j))
    if out_slots:
        oper = out_w // tn
        out_spec = pl.BlockSpec((None, tm, tn), lambda i, j, k: (j // oper, i, j % oper))
        out_struct = jax.ShapeDtypeStruct((out_slots, m, out_w), out_dtype)
    else:
        out_spec = pl.BlockSpec((tm, tn), lambda i, j, k: (i, j))
        out_struct = jax.ShapeDtypeStruct((m, n), out_dtype)
    in_specs = [a_spec, b_spec]
    args = [a, b]
    if bias is not None:
        in_specs.append(pl.BlockSpec((1, tn), lambda i, j, k: (0, j)))
        args.append(bias.reshape(1, n).astype(F32))
    if res is not None:
        in_specs.append(pl.BlockSpec((tm, tn), lambda i, j, k: (i, j)))
        args.append(res)
    outs, got = carried_call(
        body, name=name, grid=(m // tm, n // tn, nk), in_specs=in_specs,
        out_specs=[out_spec], out_shape=[out_struct],
        scratch_shapes=[pltpu.VMEM((tm, tn), F32)] if nk > 1 else [],
        semantics=("parallel", "parallel", "arbitrary"), args=args, carry=carry)
    return outs[0] if carry is None else (outs[0], got)


def _sigmoid(x):
    return 1.0 / (1.0 + jnp.exp(-x))


def _softplus(x):
    return jnp.maximum(x, 0.0) + jnp.log(1.0 + jnp.exp(-jnp.abs(x)))


def _shift_back(x, s, row):
    if s == 0:
        return x
    return jnp.where(row >= s, pltpu.roll(x, s, 0), 0.0)


def _shift_fwd(x, s, row):
    if s == 0:
        return x
    n = x.shape[0]
    return jnp.where(row < n - s, pltpu.roll(x, n - s, 0), 0.0)


def _conv_fwd(x, w_ref, kw, row):
    acc = x * w_ref[pl.ds(kw - 1, 1), :]
    for s in range(1, kw):
        acc = acc + _shift_back(x, s, row) * w_ref[pl.ds(kw - 1 - s, 1), :]
    return acc


def _conv_bwd(x, dy, w_ref, kw, row):
    dx = dy * w_ref[pl.ds(kw - 1, 1), :]
    dw = [None] * kw
    dw[kw - 1] = jnp.sum(dy * x, axis=0, keepdims=True)
    for s in range(1, kw):
        dx = dx + _shift_fwd(dy, s, row) * w_ref[pl.ds(kw - 1 - s, 1), :]
        dw[kw - 1 - s] = jnp.sum(dy * _shift_back(x, s, row), axis=0, keepdims=True)
    return dx, dw


def _rows_to_block(rows, nrows, width):
    rid = lax.broadcasted_iota(jnp.int32, (nrows, width), 0)
    out = jnp.zeros((nrows, width), F32)
    for i, r in enumerate(rows):
        out = jnp.where(rid == i, r, out)
    return out


def rmsnorm_fwd(x, g, *, name):
    t, d = x.shape
    tm = _pick(t, (256, 128))

    def body(x_ref, g_ref, o_ref):
        xv = x_ref[...]
        r = lax.rsqrt(jnp.mean(xv * xv, axis=-1, keepdims=True) + EPS)
        o_ref[...] = (xv * r * g_ref[...]).astype(o_ref.dtype)

    return pl.pallas_call(
        body, name=name, grid=(t // tm,),
        in_specs=[pl.BlockSpec((tm, d), lambda i: (i, 0)), pl.BlockSpec((1, d), lambda i: (0, 0))],
        out_specs=pl.BlockSpec((tm, d), lambda i: (i, 0)),
        out_shape=jax.ShapeDtypeStruct((t, d), BF16),
        compiler_params=_cparams(("parallel",)),
    )(x, g.reshape(1, d))


def rmsnorm_bwd(x, g, dh, dres, *, name):
    t, d = x.shape
    tm = _pick(t, (256, 128))

    def body(x_ref, g_ref, dh_ref, dres_ref, dx_ref, dxb_ref, dg_ref):
        xv = x_ref[...]
        r = lax.rsqrt(jnp.mean(xv * xv, axis=-1, keepdims=True) + EPS)
        xh = xv * r
        dy = dh_ref[...]
        dxh = dy * g_ref[...]
        dx = dres_ref[...] + r * (dxh - xh * jnp.mean(dxh * xh, axis=-1, keepdims=True))
        dx_ref[...] = dx
        dxb_ref[...] = dx.astype(BF16)

        @pl.when(pl.program_id(0) == 0)
        def _():
            dg_ref[...] = jnp.zeros_like(dg_ref)

        dg_ref[...] += jnp.sum(dy * xh, axis=0, keepdims=True)

    row = pl.BlockSpec((tm, d), lambda i: (i, 0))
    vec = pl.BlockSpec((1, d), lambda i: (0, 0))
    return pl.pallas_call(
        body, name=name, grid=(t // tm,),
        in_specs=[row, vec, row, row],
        out_specs=[row, row, vec],
        out_shape=[jax.ShapeDtypeStruct((t, d), F32), jax.ShapeDtypeStruct((t, d), BF16),
                   jax.ShapeDtypeStruct((1, d), F32)],
        compiler_params=_cparams(("arbitrary",)),
    )(x, g.reshape(1, d), dh, dres)


def loss_head(x, g, target, *, name):
    t, d = x.shape
    tm = _pick(t, (256, 128))

    def body(x_ref, g_ref, tg_ref, loss_ref, dx_ref, dxb_ref, dg_ref):
        xv = x_ref[...]
        r = lax.rsqrt(jnp.mean(xv * xv, axis=-1, keepdims=True) + EPS)
        xh = xv * r
        err = xh * g_ref[...] - tg_ref[...]
        dy = err * (1.0 / d)
        dxh = dy * g_ref[...]
        dx = r * (dxh - xh * jnp.mean(dxh * xh, axis=-1, keepdims=True))
        dx_ref[...] = dx
        dxb_ref[...] = dx.astype(BF16)

        @pl.when(pl.program_id(0) == 0)
        def _():
            dg_ref[...] = jnp.zeros_like(dg_ref)
            loss_ref[...] = jnp.zeros_like(loss_ref)

        dg_ref[...] += jnp.sum(dy * xh, axis=0, keepdims=True)
        part = jnp.sum(jnp.sum(err * err, axis=-1, keepdims=True), axis=0, keepdims=True) * (0.5 / d)
        loss_ref[...] += jnp.broadcast_to(part, loss_ref.shape)

    row = pl.BlockSpec((tm, d), lambda i: (i, 0))
    vec = pl.BlockSpec((1, d), lambda i: (0, 0))
    return pl.pallas_call(
        body, name=name, grid=(t // tm,),
        in_specs=[row, vec, row],
        out_specs=[pl.BlockSpec((8, 128), lambda i: (0, 0)), row, row, vec],
        out_shape=[jax.ShapeDtypeStruct((8, 128), F32), jax.ShapeDtypeStruct((t, d), F32),
                   jax.ShapeDtypeStruct((t, d), BF16), jax.ShapeDtypeStruct((1, d), F32)],
        compiler_params=_cparams(("arbitrary",)),
    )(x, g.reshape(1, d), target)


def _conf_forward_parts(val, gate, w_ref, b, lg, lb, kw, row, u1=None):
    sg = _sigmoid(gate)
    u0 = val * sg
    if u1 is None:
        u1 = _conv_fwd(u0, w_ref, kw, row) + b
    mu = jnp.mean(u1, axis=-1, keepdims=True)
    xc = u1 - mu
    rs = lax.rsqrt(jnp.mean(xc * xc, axis=-1, keepdims=True) + EPS)
    xh = xc * rs
    u2 = xh * lg + lb
    s2 = _sigmoid(u2)
    return sg, u0, u1, rs, xh, u2, s2


def conf_fwd(p, dw_w, dw_b, ln_g, ln_b, *, nb, s, name, carry=None):
    kw, ch = dw_w.shape
    ng = ch // HEAD

    def body(val_ref, gate_ref, w_ref, b_ref, lg_ref, lb_ref, o_ref, u1_ref):
        row = lax.broadcasted_iota(jnp.int32, (s, HEAD), 0)
        _, _, u1, _, _, u2, s2 = _conf_forward_parts(val_ref[...], gate_ref[...], w_ref, b_ref[...], lg_ref[...],
                                                     lb_ref[...], kw, row)
        o_ref[...] = (u2 * s2).astype(o_ref.dtype)
        u1_ref[...] = u1

    vec = pl.BlockSpec((1, HEAD), lambda b, g: (0, g))
    blk = pl.BlockSpec((s, HEAD), lambda b, g: (b, g))
    outs, got = carried_call(
        body, name=name, grid=(nb, ng),
        in_specs=[blk, pl.BlockSpec((s, HEAD), lambda b, g: (b, ng + g)),
                  pl.BlockSpec((kw, HEAD), lambda b, g: (0, g)), vec, vec, vec],
        out_specs=[blk, blk],
        out_shape=[jax.ShapeDtypeStruct((nb * s, ch), BF16), jax.ShapeDtypeStruct((nb * s, ch), F32)],
        scratch_shapes=[], semantics=("parallel", "parallel"),
        args=(p, p, dw_w, dw_b.reshape(1, ch), ln_g.reshape(1, ch), ln_b.reshape(1, ch)), carry=carry)
    return outs + [got]


def conf_bwd(p, u1, dw_w, dw_b, ln_g, ln_b, du3, dout_a, *, nb, s, name, carry=None):
    kw, ch = dw_w.shape
    ng = ch // HEAD

    def body(val_ref, gate_ref, u1_ref, w_ref, b_ref, lg_ref, lb_ref, du3_ref, doa_ref, dval_ref, dgate_ref, dw_out,
             sm_out):
        row = lax.broadcasted_iota(jnp.int32, (s, HEAD), 0)
        val = val_ref[...]
        sg, u0, _, rs, xh, u2, s2 = _conf_forward_parts(val, gate_ref[...], w_ref, b_ref[...], lg_ref[...],
                                                        lb_ref[...], kw, row, u1=u1_ref[...])
        du2 = du3_ref[...] * (s2 * (1.0 + u2 * (1.0 - s2)))
        dlg = jnp.sum(du2 * xh, axis=0, keepdims=True)
        dlb = jnp.sum(du2, axis=0, keepdims=True)
        dxh = du2 * lg_ref[...]
        du1 = rs * (dxh - jnp.mean(dxh, axis=-1, keepdims=True) - xh * jnp.mean(dxh * xh, axis=-1, keepdims=True))
        ddb = jnp.sum(du1, axis=0, keepdims=True)
        du0, dw = _conv_bwd(u0, du1, w_ref, kw, row)
        dval_ref[...] = (du0 * sg).astype(dval_ref.dtype)
        dgate_ref[...] = (du0 * val * sg * (1.0 - sg)).astype(dgate_ref.dtype)
        for k in range(kw):
            dw_out[0, pl.ds(k, 1), :] = dw[k]
        dpb = jnp.sum(doa_ref[...].astype(F32), axis=0, keepdims=True)
        sm_out[0] = _rows_to_block([ddb, dlg, dlb, dpb], 8, HEAD)

    vec = pl.BlockSpec((1, HEAD), lambda b, g: (0, g))
    blk = pl.BlockSpec((s, HEAD), lambda b, g: (b, g))
    outs, got = carried_call(
        body, name=name, grid=(nb, ng),
        in_specs=[blk, pl.BlockSpec((s, HEAD), lambda b, g: (b, ng + g)), blk,
                  pl.BlockSpec((kw, HEAD), lambda b, g: (0, g)), vec, vec, vec, blk, blk],
        out_specs=[blk, blk, pl.BlockSpec((1, kw, HEAD), lambda b, g: (b, 0, g)),
                   pl.BlockSpec((1, 8, HEAD), lambda b, g: (b, 0, g))],
        out_shape=[jax.ShapeDtypeStruct((nb * s, ch), BF16), jax.ShapeDtypeStruct((nb * s, ch), BF16),
                   jax.ShapeDtypeStruct((nb, kw, ch), F32), jax.ShapeDtypeStruct((nb, 8, ch), F32)],
        scratch_shapes=[], semantics=("parallel", "parallel"),
        args=(p, p, u1, dw_w, dw_b.reshape(1, ch), ln_g.reshape(1, ch), ln_b.reshape(1, ch), du3, dout_a),
        carry=carry)
    return outs + [got]


def qkvconv_fwd(p, w, *, col0, nb, s, name, carry=None):
    kw, ch = w.shape
    nblk = ch // HEAD
    c0 = col0 // HEAD

    def body(x_ref, w_ref, o_ref):
        row = lax.broadcasted_iota(jnp.int32, (s, HEAD), 0)
        c = _conv_fwd(x_ref[...], w_ref, kw, row)
        o_ref[...] = c * _sigmoid(c)

    outs, got = carried_call(
        body, name=name, grid=(nb, nblk),
        in_specs=[pl.BlockSpec((s, HEAD), lambda b, j: (b, c0 + j)), pl.BlockSpec((kw, HEAD), lambda b, j: (0, j))],
        out_specs=[pl.BlockSpec((s, HEAD), lambda b, j: (b, j))],
        out_shape=[jax.ShapeDtypeStruct((nb * s, ch), F32)],
        scratch_shapes=[], semantics=("parallel", "parallel"), args=(p, w), carry=carry)
    return outs + [got]


def qkvconv_bwd(p, w, dy, *, col0, nb, s, name):
    kw, ch = w.shape
    nblk = ch // HEAD
    c0 = col0 // HEAD

    def body(x_ref, w_ref, dy_ref, dx_ref, dw_out):
        row = lax.broadcasted_iota(jnp.int32, (s, HEAD), 0)
        xv = x_ref[...]
        c = _conv_fwd(xv, w_ref, kw, row)
        sc = _sigmoid(c)
        dc = dy_ref[...] * (sc * (1.0 + c * (1.0 - sc)))
        dx, dw = _conv_bwd(xv, dc, w_ref, kw, row)
        dx_ref[...] = dx.astype(dx_ref.dtype)
        dw_out[0] = _rows_to_block(dw, 8, HEAD)

    blk = pl.BlockSpec((s, HEAD), lambda b, j: (b, j))
    return pl.pallas_call(
        body, name=name, grid=(nb, nblk),
        in_specs=[pl.BlockSpec((s, HEAD), lambda b, j: (b, c0 + j)), pl.BlockSpec((kw, HEAD), lambda b, j: (0, j)), blk],
        out_specs=[blk, pl.BlockSpec((1, 8, HEAD), lambda b, j: (b, 0, j))],
        out_shape=[jax.ShapeDtypeStruct((nb * s, ch), BF16), jax.ShapeDtypeStruct((nb, 8, ch), F32)],
        compiler_params=_cparams(("parallel", "parallel")),
    )(p, w, dy)


def ffn_act_fwd(up, w, b, *, nb, s, name, carry=None):
    kw, dff = w.shape
    cb = _pick(dff, (256, 128))
    nblk = dff // cb

    def body(g_ref, u_ref, w_ref, b_ref, o_ref):
        row = lax.broadcasted_iota(jnp.int32, (s, cb), 0)
        gc = _conv_fwd(g_ref[...].astype(F32), w_ref, kw, row) + b_ref[...]
        o_ref[...] = (gc * _sigmoid(gc) * u_ref[...].astype(F32)).astype(o_ref.dtype)

    outs, got = carried_call(
        body, name=name, grid=(nb, nblk),
        in_specs=[pl.BlockSpec((s, cb), lambda i, j: (i, j)), pl.BlockSpec((s, cb), lambda i, j: (i, nblk + j)),
                  pl.BlockSpec((kw, cb), lambda i, j: (0, j)), pl.BlockSpec((1, cb), lambda i, j: (0, j))],
        out_specs=[pl.BlockSpec((s, cb), lambda i, j: (i, j))],
        out_shape=[jax.ShapeDtypeStruct((nb * s, dff), BF16)],
        scratch_shapes=[], semantics=("parallel", "parallel"), args=(up, up, w, b.reshape(1, dff)), carry=carry)
    return outs + [got]


def ffn_act_bwd(up, w, b, dact, *, nb, s, name):
    kw, dff = w.shape
    cb = _pick(dff, (256, 128))
    nblk = dff // cb

    def body(g_ref, u_ref, w_ref, b_ref, da_ref, dg_ref, du_ref, sm_out):
        row = lax.broadcasted_iota(jnp.int32, (s, cb), 0)
        gv = g_ref[...].astype(F32)
        gc = _conv_fwd(gv, w_ref, kw, row) + b_ref[...]
        sc = _sigmoid(gc)
        da = da_ref[...].astype(F32)
        du_ref[...] = (da * gc * sc).astype(du_ref.dtype)
        dgc = da * u_ref[...].astype(F32) * (sc * (1.0 + gc * (1.0 - sc)))
        dgate, dw = _conv_bwd(gv, dgc, w_ref, kw, row)
        dg_ref[...] = dgate.astype(dg_ref.dtype)
        sm_out[0] = _rows_to_block(dw + [jnp.sum(dgc, axis=0, keepdims=True)], 8, cb)

    blk = pl.BlockSpec((s, cb), lambda i, j: (i, j))
    return pl.pallas_call(
        body, name=name, grid=(nb, nblk),
        in_specs=[blk, pl.BlockSpec((s, cb), lambda i, j: (i, nblk + j)),
                  pl.BlockSpec((kw, cb), lambda i, j: (0, j)), pl.BlockSpec((1, cb), lambda i, j: (0, j)), blk],
        out_specs=[blk, blk, pl.BlockSpec((1, 8, cb), lambda i, j: (i, 0, j))],
        out_shape=[jax.ShapeDtypeStruct((nb * s, dff), BF16), jax.ShapeDtypeStruct((nb * s, dff), BF16),
                   jax.ShapeDtypeStruct((nb, 8, dff), F32)],
        compiler_params=_cparams(("parallel", "parallel")),
    )(up, up, w, b.reshape(1, dff), dact)


def _dot(a, b, dims):
    return lax.dot_general(a, b, (dims, ((0,), (0,))), preferred_element_type=F32)


def _mm_nn(a, b):
    return _dot(a.astype(BF16), b.astype(BF16), ((2,), (1,)))


def _mm_nt(a, b):
    return _dot(a.astype(BF16), b.astype(BF16), ((2,), (2,)))


def _mm_tn(a, b):
    return _dot(a.astype(BF16), b.astype(BF16), ((1,), (1,)))


def _split3(x):
    hi = x.astype(BF16)
    rest = x - hi.astype(F32)
    mid = rest.astype(BF16)
    return hi, mid, (rest - mid.astype(F32)).astype(BF16)


def _mask_dot(mask, x, dims, mask_first):
    if mask_first:
        return sum(_dot(mask, p, dims) for p in _split3(x))
    return sum(_dot(p, mask, dims) for p in _split3(x))


@jax.custom_vjp
def _mask_nn(mask, x):
    return _mask_dot(mask, x, ((2,), (1,)), True)


def _mask_nn_fwd(mask, x):
    return _mask_nn(mask, x), mask


def _mask_nn_bwd(mask, ct):
    return jnp.zeros_like(mask), _mask_dot(mask, ct, ((1,), (1,)), True)


_mask_nn.defvjp(_mask_nn_fwd, _mask_nn_bwd)


@jax.custom_vjp
def _mask_tn(x, mask):
    return _mask_dot(mask, x, ((1,), (1,)), False)


def _mask_tn_fwd(x, mask):
    return _mask_tn(x, mask), mask


def _mask_tn_bwd(mask, ct):
    return _mask_dot(mask, ct, ((2,), (2,)), True), jnp.zeros_like(mask)


_mask_tn.defvjp(_mask_tn_fwd, _mask_tn_bwd)


GDN_ROWS = 256


def _gdn_chunk(qc, kc, vc, zc, braw, araw, alog, dtb, ng, state):
    nhead, r = qc.shape[0], qc.shape[1]
    ri = lax.broadcasted_iota(jnp.int32, (r, r), 0)
    ci = lax.broadcasted_iota(jnp.int32, (r, r), 1)
    same = (ri // CHUNK) == (ci // CHUNK)
    causal = same & (ri >= ci)
    strict = same & (ri > ci)
    eye = (ri == ci).astype(F32)
    row_chunk = lax.broadcasted_iota(jnp.int32, (r, HEAD), 0) // CHUNK
    per_head = lambda m: jnp.broadcast_to(m.astype(BF16), (nhead, r, r))

    q = qc * lax.rsqrt(jnp.sum(qc * qc, axis=-1, keepdims=True) + EPS) * (HEAD ** -0.5)
    k = kc * lax.rsqrt(jnp.sum(kc * kc, axis=-1, keepdims=True) + EPS)
    beta = _sigmoid(braw)
    g = -jnp.exp(alog) * _softplus(araw + dtb)

    g_w = jnp.broadcast_to(g, (nhead, r, HEAD))
    widen = lambda t: jnp.concatenate([t] * (r // HEAD), axis=2)
    gw = _mask_nn(per_head(causal), g_w)
    gi = widen(gw)
    gj = _mask_tn(widen(g_w), per_head(same & (ri <= ci)))
    decay = jnp.where(causal, jnp.exp(jnp.where(causal, gi - gj, 0.0)), 0.0)

    kb = k * beta
    vb = vc * beta
    lmat = jnp.where(strict, _mm_nt(kb, k) * decay, 0.0)
    x = -lmat
    ainv = eye + x
    p = 1
    while 2 * p < CHUNK:
        x = _mm_nn(x, x)
        ainv = _mm_nn(ainv, eye + x)
        p *= 2
    u = _mm_nn(ainv, vb)
    w = _mm_nn(ainv, kb * jnp.exp(gw))
    qk = jnp.where(causal, _mm_nt(q, k) * decay, 0.0)
    qg = q * jnp.exp(gw)

    o = jnp.zeros((nhead, r, HEAD), F32)
    for c in range(r // CHUNK):
        in_c = row_chunk == c
        glast = jnp.sum(jnp.where(in_c, g_w, 0.0), axis=1, keepdims=True)
        v_new = jnp.where(in_c, u - _mm_nn(w, state), 0.0)
        o = o + jnp.where(in_c, _mm_nn(qg, state), 0.0) + _mm_nn(qk, v_new)
        k_dec = jnp.where(in_c, k * jnp.exp(jnp.where(in_c, glast - gw, 0.0)), 0.0)
        state = state * jnp.exp(glast) + _mm_tn(k_dec, v_new)

    o = o * lax.rsqrt(jnp.mean(o * o, axis=-1, keepdims=True) + EPS) * ng
    o = o * (zc * _sigmoid(zc))
    return o, state


GDN_HEADS_FWD = 4
GDN_HEADS_BWD = 4


def _gdn_specs(s, nh, nqk, zcol, n_chunks, gh, single=False):
    rep = nh // nqk
    qw, vw = (gh // rep) * HEAD, gh * HEAD
    assert gh % rep == 0 and nh % gh == 0 and (nqk * HEAD) % qw == 0 and (2 * nqk * HEAD) % vw == 0 and zcol % vw == 0
    k0, v0, z0 = (nqk * HEAD) // qw, (2 * nqk * HEAD) // vw, zcol // vw
    mode = dict(pipeline_mode=pl.Buffered(1)) if single else {}
    return dict(
        q=pl.BlockSpec((s, qw), lambda b, j: (b, j), **mode),
        k=pl.BlockSpec((s, qw), lambda b, j: (b, k0 + j), **mode),
        v=pl.BlockSpec((s, vw), lambda b, j: (b, v0 + j), **mode),
        z=pl.BlockSpec((s, vw), lambda b, j: (b, z0 + j), **mode),
        ba=pl.BlockSpec((s, HEAD), lambda b, j: (b, 0)),
        gp=pl.BlockSpec((8, HEAD), lambda b, j: (0, 0)),
        qk_out=pl.BlockSpec((s, qw), lambda b, j: (b, j), **mode),
        head=pl.BlockSpec((s, vw), lambda b, j: (b, j), **mode),
        head_in=pl.BlockSpec((s, vw), lambda b, j: (b, j), **mode),
        st=pl.BlockSpec((1, gh, n_chunks, HEAD, HEAD), lambda b, j: (b, j, 0, 0, 0)),
        st_in=pl.BlockSpec((1, gh, n_chunks, HEAD, HEAD), lambda b, j: (b, j, 0, 0, 0), **mode),
    )


def _gdn_scalars(gp_ref, h):
    lane = lax.broadcasted_iota(jnp.int32, (1, HEAD), 1)
    sel = (lane == h).astype(F32)
    alog = jnp.sum(gp_ref[pl.ds(0, 1), :] * sel, axis=-1, keepdims=True)
    dtb = jnp.sum(gp_ref[pl.ds(1, 1), :] * sel, axis=-1, keepdims=True)
    return alog, dtb, sel


def _lanes(i):
    return pl.ds(i * HEAD, HEAD)


def gdn_fwd(qkvc, p, pba, gp, *, nb, s, nh, nqk, zcol, name, carry=None):
    rb = min(GDN_ROWS, s)
    n_chunks = s // rb
    gh = GDN_HEADS_FWD
    rep = nh // nqk
    sp = _gdn_specs(s, nh, nqk, zcol, n_chunks, gh)

    def body(q_ref, k_ref, v_ref, z_ref, ba_ref, gp_ref, o_ref, st_ref):
        h0 = pl.program_id(1) * gh
        ng = gp_ref[pl.ds(2, 1), :]
        lane = lax.broadcasted_iota(jnp.int32, (rb, HEAD), 1)
        heads = []
        for i in range(gh):
            alog, dtb, _ = _gdn_scalars(gp_ref, h0 + i)
            heads.append((alog, dtb, (lane == h0 + i).astype(F32), (lane == nh + h0 + i).astype(F32)))

        alogs = jnp.stack([hd[0] for hd in heads])
        dtbs = jnp.stack([hd[1] for hd in heads])

        def step(n, state):
            rows = pl.ds(pl.multiple_of(n * rb, rb), rb)
            ba = ba_ref[rows, :]
            qs = jnp.stack([q_ref[rows, _lanes(i // rep)] for i in range(gh)])
            ks = jnp.stack([k_ref[rows, _lanes(i // rep)] for i in range(gh)])
            vs = jnp.stack([v_ref[rows, _lanes(i)] for i in range(gh)])
            zs = jnp.stack([z_ref[rows, _lanes(i)] for i in range(gh)])
            braw = jnp.stack([jnp.sum(ba * hd[2], axis=-1, keepdims=True) for hd in heads])
            araw = jnp.stack([jnp.sum(ba * hd[3], axis=-1, keepdims=True) for hd in heads])
            o, new_state = _gdn_chunk(qs, ks, vs, zs, braw, araw, alogs, dtbs, ng, state)
            for i in range(gh):
                st_ref[0, i, n] = state[i]
                o_ref[rows, _lanes(i)] = o[i].astype(o_ref.dtype)
            return new_state

        lax.fori_loop(0, n_chunks, step, jnp.zeros((gh, HEAD, HEAD), F32))

    outs, got = carried_call(
        body, name=name, grid=(nb, nh // gh),
        in_specs=[sp["q"], sp["k"], sp["v"], sp["z"], sp["ba"], sp["gp"]],
        out_specs=[sp["head"], sp["st"]],
        out_shape=[jax.ShapeDtypeStruct((nb * s, nh * HEAD), BF16),
                   jax.ShapeDtypeStruct((nb, nh, n_chunks, HEAD, HEAD), F32)],
        scratch_shapes=[], semantics=("parallel", "parallel"), args=(qkvc, qkvc, qkvc, p, pba, gp), carry=carry)
    return outs + [got]


def gdn_bwd(qkvc, p, pba, gp, states, dout, *, nb, s, nh, nqk, zcol, name, carry=None):
    rb = min(GDN_ROWS, s)
    n_chunks = s // rb
    gh = GDN_HEADS_BWD
    rep = nh // nqk
    sp = _gdn_specs(s, nh, nqk, zcol, n_chunks, gh, single=True)

    def body(q_ref, k_ref, v_ref, z_ref, ba_ref, gp_ref, st_ref, do_ref,
             dq_ref, dk_ref, dv_ref, dz_ref, dba_ref, dgp_ref):
        h0 = pl.program_id(1) * gh
        ng = gp_ref[pl.ds(2, 1), :]
        lane = lax.broadcasted_iota(jnp.int32, (rb, HEAD), 1)
        heads = []
        for i in range(gh):
            alog, dtb, sel_row = _gdn_scalars(gp_ref, h0 + i)
            heads.append((alog, dtb, (lane == h0 + i).astype(F32), (lane == nh + h0 + i).astype(F32), sel_row))

        @pl.when(h0 == 0)
        def _():
            dba_ref[...] = jnp.zeros_like(dba_ref)
            dgp_ref[...] = jnp.zeros_like(dgp_ref)

        alogs = jnp.stack([hd[0] for hd in heads])
        dtbs = jnp.stack([hd[1] for hd in heads])

        def step(it, carry):
            dstate, dalog, ddtb, dng = carry
            n = n_chunks - 1 - it
            rows = pl.ds(pl.multiple_of(n * rb, rb), rb)
            ba = ba_ref[rows, :]
            qs = jnp.stack([q_ref[rows, _lanes(i // rep)] for i in range(gh)])
            ks = jnp.stack([k_ref[rows, _lanes(i // rep)] for i in range(gh)])
            vs = jnp.stack([v_ref[rows, _lanes(i)] for i in range(gh)])
            zs = jnp.stack([z_ref[rows, _lanes(i)] for i in range(gh)])
            dos = jnp.stack([do_ref[rows, _lanes(i)] for i in range(gh)])
            braw = jnp.stack([jnp.sum(ba * hd[2], axis=-1, keepdims=True) for hd in heads])
            araw = jnp.stack([jnp.sum(ba * hd[3], axis=-1, keepdims=True) for hd in heads])
            _, vjp = jax.vjp(_gdn_chunk, qs, ks, vs, zs, braw, araw, alogs, dtbs, ng, st_ref[0, :, n])
            gq, gk, gv, gz, gb, ga, galog, gdtb, gng, gstate = vjp((dos, dstate))
            dba = dba_ref[rows, :]
            for i in range(gh):
                dv_ref[rows, _lanes(i)] = gv[i]
                dz_ref[rows, _lanes(i)] = gz[i].astype(dz_ref.dtype)
                dba = dba + (gb[i] * heads[i][2] + ga[i] * heads[i][3])
            for j in range(gh // rep):
                dq_ref[rows, _lanes(j)] = sum(gq[i] for i in range(j * rep, (j + 1) * rep))
                dk_ref[rows, _lanes(j)] = sum(gk[i] for i in range(j * rep, (j + 1) * rep))
            dba_ref[rows, :] = dba
            return gstate, dalog + galog, ddtb + gdtb, dng + gng

        init = (jnp.zeros((gh, HEAD, HEAD), F32), jnp.zeros((gh, 1, 1), F32), jnp.zeros((gh, 1, 1), F32),
                jnp.zeros((1, HEAD), F32))
        _, dalog, ddtb, dng = lax.fori_loop(0, n_chunks, step, init)
        rows3 = [sum(dalog[i] * heads[i][4] for i in range(gh)), sum(ddtb[i] * heads[i][4] for i in range(gh)), dng]
        dgp_ref[0] += _rows_to_block(rows3, 8, HEAD)

    outs, got = carried_call(
        body, name=name, grid=(nb, nh // gh),
        in_specs=[sp["q"], sp["k"], sp["v"], sp["z"], sp["ba"], sp["gp"], sp["st_in"], sp["head_in"]],
        out_specs=[sp["qk_out"], sp["qk_out"], sp["head"], sp["head"], sp["ba"],
                   pl.BlockSpec((1, 8, HEAD), lambda b, j: (b, 0, 0))],
        out_shape=[jax.ShapeDtypeStruct((nb * s, nqk * HEAD), F32), jax.ShapeDtypeStruct((nb * s, nqk * HEAD), F32),
                   jax.ShapeDtypeStruct((nb * s, nh * HEAD), F32), jax.ShapeDtypeStruct((nb * s, nh * HEAD), BF16),
                   jax.ShapeDtypeStruct((nb * s, HEAD), F32), jax.ShapeDtypeStruct((nb, 8, HEAD), F32)],
        scratch_shapes=[], semantics=("parallel", "arbitrary"),
        args=(qkvc, qkvc, qkvc, p, pba, gp, states, dout), carry=carry)
    return outs + [got]


ADAM_LR = 0.001
ADAM_B1 = 0.9
ADAM_B2 = 0.999
ADAM_EPS = 1e-08
ADAM_WD = 0.01
ADAM_STEP = 10
EW_BLOCK_BYTES = 1 << 20


def _row_tile(rows, cols):
    for tr in (1024, 512, 256, 128, 64, 32, 16, 8):
        if rows % tr == 0 and tr * cols * 4 <= EW_BLOCK_BYTES:
            return tr
    return rows


def sum_slots(rs, *, name):
    nl = len(rs)
    n, rows, cols = rs[0].shape
    tr = _row_tile(rows, cols)
    nblk = rows // tr

    def body(*refs):
        o_ref = refs[nl]
        for l in range(nl):
            @pl.when(pl.program_id(0) == l)
            def _(l=l):
                acc = refs[l][0].astype(F32)
                for i in range(1, n):
                    acc = acc + refs[l][i].astype(F32)
                o_ref[0] = acc

    def in_map(l):
        return lambda li, i: (0, jnp.where(li == l, i, jnp.where(li < l, 0, nblk - 1)), 0)

    return pl.pallas_call(
        body, name=name, grid=(nl, nblk),
        in_specs=[pl.BlockSpec((n, tr, cols), in_map(l)) for l in range(nl)],
        out_specs=pl.BlockSpec((1, tr, cols), lambda li, i: (li, i, 0)),
        out_shape=jax.ShapeDtypeStruct((nl, rows, cols), F32),
        compiler_params=_cparams(("arbitrary", "arbitrary")),
    )(*rs)


def adamw(g_parts, w, m, v, *, name):
    shape = w.shape
    cols = shape[-1]
    lead = shape[0] if w.ndim == 3 else 1
    view = (lambda a: a.reshape(shape)) if w.ndim == 3 else (lambda a: a.reshape(1, -1, cols))
    rows = view(w).shape[1]
    tr = _row_tile(rows, cols)
    npart = len(g_parts)
    c1 = 1.0 - ADAM_B1 ** ADAM_STEP
    c2 = 1.0 - ADAM_B2 ** ADAM_STEP

    def body(*refs):
        w_ref, m_ref, v_ref = refs[npart:npart + 3]
        g_ref, d_ref, nm_ref, nv_ref = refs[npart + 3:]
        g = refs[0][...]
        for i in range(1, npart):
            g = g + refs[i][...]
        nm = ADAM_B1 * m_ref[...] + (1.0 - ADAM_B1) * g
        nv = ADAM_B2 * v_ref[...] + (1.0 - ADAM_B2) * (g * g)
        g_ref[...] = g
        nm_ref[...] = nm
        nv_ref[...] = nv
        d_ref[...] = -ADAM_LR * ((nm / c1) / (jnp.sqrt(nv / c2) + ADAM_EPS) + ADAM_WD * w_ref[...])

    blk = pl.BlockSpec((1, tr, cols), lambda l, i: (l, i, 0))
    outs = pl.pallas_call(
        body, name=name, grid=(lead, rows // tr),
        in_specs=[blk] * (npart + 3),
        out_specs=[blk] * 4,
        out_shape=[jax.ShapeDtypeStruct((lead, rows, cols), F32)] * 4,
        compiler_params=_cparams(("parallel", "parallel")),
    )(*[view(a) for a in g_parts], view(w), view(m), view(v))
    return tuple(o.reshape(shape) for o in outs)


def allreduce_small(vec, *, name):
    r = vec.shape[0]

    def body(v_ref, o_ref, slots, send_sems, recv_sems):
        x, y, c = lax.axis_index("x"), lax.axis_index("y"), lax.axis_index("c")
        me = 4 * x + 2 * y + c
        slots[me] = v_ref[...]
        copies = []
        for j in range(1, 8):
            px = 1 - x if j & 4 else x
            py = 1 - y if j & 2 else y
            pc = 1 - c if j & 1 else c
            rc = pltpu.make_async_remote_copy(src_ref=v_ref, dst_ref=slots.at[me], send_sem=send_sems.at[j - 1],
                                              recv_sem=recv_sems.at[j - 1], device_id=(px, py, pc), device_id_type=MESH)
            rc.start()
            copies.append(rc)
        for cp in copies:
            cp.wait()
        acc = slots[0]
        for i in range(1, 8):
            acc = acc + slots[i]
        o_ref[...] = acc

    vm = pl.BlockSpec(memory_space=pltpu.VMEM)
    return pl.pallas_call(
        body, name=name, in_specs=[vm], out_specs=vm,
        out_shape=jax.ShapeDtypeStruct((r, 128), F32),
        scratch_shapes=[pltpu.VMEM((8, r, 128), F32), pltpu.SemaphoreType.DMA((7,)), pltpu.SemaphoreType.DMA((7,))],
        compiler_params=pltpu.CompilerParams(vmem_limit_bytes=VMEM_LIMIT),
    )(vec)


WEIGHTS = ("mix_norm_g", "w_in", "conv_dw_w", "conv_dw_b", "conv_ln_g", "conv_ln_b", "conv_pw_w", "conv_pw_b",
           "gdn_conv_w", "gdn_a_log", "gdn_dt_bias", "gdn_norm_g", "w_out", "ffn_norm_g", "w_up", "ffn_conv_w",
           "ffn_conv_b", "w_down", "final_norm_g")
COL_SHARDED = ("w_in", "w_up", "conv_dw_w", "gdn_conv_w", "ffn_conv_w")
ROW_SHARDED = ("conv_pw_w", "w_out", "w_down")
BIG = ("w_in", "conv_pw_w", "w_out", "w_up", "w_down")
SMALL_CONV = ("conv_dw_w", "gdn_conv_w", "ffn_conv_w")
GATHER_PARTS = {"w_up": (18, 6, 30, 5, 5), "w_down": (38, 6), "w_in": (15, 49), "conv_pw_w": (1,), "w_out": (1,)}


def _full_from_slots(name, part):
    if name in COL_SHARDED:
        r, cs = part.shape[1:]
        return jnp.transpose(part, (1, 0, 2)).reshape(r, 4 * cs)
    rs, c = part.shape[1:]
    return part.reshape(4 * rs, c)


def _slots_from_full(name, full):
    if name in COL_SHARDED:
        r, c = full.shape
        return jnp.transpose(full.reshape(r, 4, c // 4), (1, 0, 2))
    r, c = full.shape
    return full.reshape(4, r // 4, c)


def _pack(parts):
    flat = jnp.concatenate([p.reshape(-1).astype(F32) for p in parts])
    pad = (-flat.shape[0]) % 1024
    return jnp.pad(flat, (0, pad)).reshape(-1, 128)


def _unpack(vec, shapes):
    flat = vec.reshape(-1)
    out, pos = [], 0
    for shp in shapes:
        n = 1
        for d in shp:
            n *= d
        out.append(flat[pos:pos + n].reshape(shp))
        pos += n
    return out


def kernel(x, mix_norm_g, w_in, conv_dw_w, conv_dw_b, conv_ln_g, conv_ln_b, conv_pw_w, conv_pw_b, gdn_conv_w, gdn_a_log, gdn_dt_bias, gdn_norm_g, w_out, ffn_norm_g, w_up, ffn_conv_w, ffn_conv_b, w_down, final_norm_g, loss_target, m_mix_norm_g, m_w_in, m_conv_dw_w, m_conv_dw_b, m_conv_ln_g, m_conv_ln_b, m_conv_pw_w, m_conv_pw_b, m_gdn_conv_w, m_gdn_a_log, m_gdn_dt_bias, m_gdn_norm_g, m_w_out, m_ffn_norm_g, m_w_up, m_ffn_conv_w, m_ffn_conv_b, m_w_down, m_final_norm_g, v_mix_norm_g, v_w_in, v_conv_dw_w, v_conv_dw_b, v_conv_ln_g, v_conv_ln_b, v_conv_pw_w, v_conv_pw_b, v_gdn_conv_w, v_gdn_a_log, v_gdn_dt_bias, v_gdn_norm_g, v_w_out, v_ffn_norm_g, v_w_up, v_ffn_conv_w, v_ffn_conv_b, v_w_down, v_final_norm_g):
    wts = dict(zip(WEIGHTS, (mix_norm_g, w_in, conv_dw_w, conv_dw_b, conv_ln_g, conv_ln_b, conv_pw_w, conv_pw_b,
                             gdn_conv_w, gdn_a_log, gdn_dt_bias, gdn_norm_g, w_out, ffn_norm_g, w_up, ffn_conv_w,
                             ffn_conv_b, w_down, final_norm_g)))
    mom = dict(zip(WEIGHTS, (m_mix_norm_g, m_w_in, m_conv_dw_w, m_conv_dw_b, m_conv_ln_g, m_conv_ln_b, m_conv_pw_w,
                             m_conv_pw_b, m_gdn_conv_w, m_gdn_a_log, m_gdn_dt_bias, m_gdn_norm_g, m_w_out,
                             m_ffn_norm_g, m_w_up, m_ffn_conv_w, m_ffn_conv_b, m_w_down, m_final_norm_g)))
    var = dict(zip(WEIGHTS, (v_mix_norm_g, v_w_in, v_conv_dw_w, v_conv_dw_b, v_conv_ln_g, v_conv_ln_b, v_conv_pw_w,
                             v_conv_pw_b, v_gdn_conv_w, v_gdn_a_log, v_gdn_dt_bias, v_gdn_norm_g, v_w_out,
                             v_ffn_norm_g, v_w_up, v_ffn_conv_w, v_ffn_conv_b, v_w_down, v_final_norm_g)))

    nb, s, d = x.shape
    t = nb * s
    depth = mix_norm_g.shape[0]
    ch = conv_dw_b.shape[1]
    nh = gdn_a_log.shape[1]
    nqk = nh // 2
    kwid, vwid = nqk * HEAD, nh * HEAD
    main = 2 * ch + 2 * kwid + 2 * vwid
    qcol, zcol = 2 * ch, 2 * ch + 2 * kwid + vwid
    dff = ffn_conv_b.shape[1]
    my_xy = 2 * lax.axis_index("x") + lax.axis_index("y")

    shards = {(n, l): wts[n][l].astype(BF16) for n in BIG for l in range(depth)}
    buffers, covered, full = {}, {}, {}

    def window(n, i):
        rows, parts = wts[n].shape[1], GATHER_PARTS[n]
        if i is None:
            return 0, rows
        return rows * sum(parts[:i]) // sum(parts), rows * parts[i] // sum(parts)

    def gather(*keys):
        return SplitGather([shards[n, l] for n, l, _ in keys], windows=[window(n, i) for n, _, i in keys],
                           into=[buffers.get((n, l)) for n, l, _ in keys])

    def arrived(keys, got):
        for (n, l, i), g in zip(keys, got):
            buffers[n, l] = g
            covered[n, l] = covered.get((n, l), 0) + window(n, i)[1]
            if covered[n, l] == wts[n].shape[1]:
                full[n, l] = g if n == "w_up" else _full_from_slots(n, g)

    def carrying_gather(keys, call):
        res = call(carry=gather(*keys))
        arrived(keys, res[-1])
        return res[0] if len(res) == 2 else res[:-1]

    first, small = run_exchanges([gather(("w_in", 0, None)), Exchange([wts[n] for n in SMALL_CONV], gather=True)],
                                 name="gather_first")
    arrived([("w_in", 0, None)], first)
    for n, g in zip(SMALL_CONV, small):
        for l in range(depth):
            full[n, l] = _full_from_slots(n, g[:, l])

    xc = x.reshape(t, d)
    saved, lws = [], []
    for l in range(depth):
        w_in_f = full["w_in", l]
        w_main = w_in_f[:, :main]
        w_ba = jnp.pad(w_in_f[:, main:], ((0, 0), (0, HEAD - 2 * nh)))
        gp = (jnp.zeros((8, HEAD), F32).at[0, :nh].set(gdn_a_log[l]).at[1, :nh].set(gdn_dt_bias[l])
              .at[2].set(gdn_norm_g[l]))
        h = rmsnorm_fwd(xc, mix_norm_g[l], name=f"f{l}_norm1")
        p = carrying_gather([("w_up", l, 0)], functools.partial(matmul, h, w_main, name=f"f{l}_in_main"))
        pba = matmul(h, w_ba, name=f"f{l}_in_ba")
        u3, u1 = carrying_gather([("conv_pw_w", l, None), ("w_out", l, None)],
                                 functools.partial(conf_fwd, p, full["conv_dw_w", l], conv_dw_b[l], conv_ln_g[l],
                                                   conv_ln_b[l], nb=nb, s=s, name=f"f{l}_conf"))
        out_a = matmul(u3, full["conv_pw_w", l], bias=conv_pw_b[l], out_dtype=BF16, name=f"f{l}_pw")
        qkvc = carrying_gather([("w_up", l, 1)], functools.partial(qkvconv_fwd, p, full["gdn_conv_w", l], col0=qcol,
                                                                    nb=nb, s=s, name=f"f{l}_qkvconv"))
        out_b, states = carrying_gather([("w_up", l, 2)],
                                        functools.partial(gdn_fwd, qkvc, p, pba, gp, nb=nb, s=s, nh=nh, nqk=nqk,
                                                          zcol=zcol, name=f"f{l}_gdn"))
        wout_a, wout_b = full["w_out", l][:ch], full["w_out", l][ch:]
        x1 = carrying_gather([("w_up", l, 3)], functools.partial(matmul, out_a, wout_a, res=xc, name=f"f{l}_out_a"))
        x1 = carrying_gather([("w_up", l, 4)], functools.partial(matmul, out_b, wout_b, res=x1, name=f"f{l}_out_b"))
        h2 = rmsnorm_fwd(x1, ffn_norm_g[l], name=f"f{l}_norm2")
        up = carrying_gather([("w_down", l, 0)],
                             functools.partial(matmul, h2, full["w_up", l], out_dtype=BF16, name=f"f{l}_up"))
        nxt = l + 1 < depth
        act = carrying_gather([("w_down", l, 1)] + ([("w_in", l + 1, 0)] if nxt else []),
                              functools.partial(ffn_act_fwd, up, full["ffn_conv_w", l], ffn_conv_b[l], nb=nb, s=s,
                                                name=f"f{l}_act"))
        down_call = functools.partial(matmul, act, full["w_down", l], res=x1, name=f"f{l}_down")
        x2 = carrying_gather([("w_in", l + 1, 1)], down_call) if nxt else down_call()
        saved.append(dict(x=xc, h=h, p=p, pba=pba, u3=u3, u1=u1, out_a=out_a, qkvc=qkvc, out_b=out_b, states=states,
                          x1=x1, h2=h2, up=up, act=act))
        lws.append(dict(w_main=w_main, w_ba=w_ba, pw=full["conv_pw_w", l], wout_a=wout_a, wout_b=wout_b,
                        wup=full["w_up", l], wdown=full["w_down", l], dw_w=full["conv_dw_w", l],
                        gconv_w=full["gdn_conv_w", l], fconv_w=full["ffn_conv_w", l], gp=gp))
        xc = x2

    loss_blk, dx, dxb, dgf = loss_head(xc, final_norm_g, loss_target.reshape(t, d), name="loss_head")

    stacks, received = {}, {}
    scatter = lambda *keys: Exchange([stacks[k] for k in keys], gather=False)

    def produced(n, l, grad, halves=False):
        slots = _slots_from_full(n, grad).astype(BF16)
        if halves:
            half = slots.shape[1] // 2
            stacks[n, l, 0], stacks[n, l, 1] = slots[:, :half], slots[:, half:]
        else:
            stacks[n, l] = slots

    def landed(keys, got):
        for k, g in zip(keys, got):
            received[k] = g

    def carrying(keys, call, **kw):
        res = call(carry=scatter(*keys), **kw)
        landed(keys, res[-1])
        return res[0] if len(res) == 2 else res[:-1]

    small_grads = {n: [None] * depth for n in WEIGHTS if n not in BIG and n != "final_norm_g"}
    for l in reversed(range(depth)):
        lw, sv = lws[l], saved[l]
        dact_call = functools.partial(matmul, dxb, lw["wdown"], tb=True, out_dtype=BF16, name=f"b{l}_dact")
        dact = carrying([("w_in", l + 1, 1)], dact_call) if l + 1 < depth else dact_call()
        produced("w_down", l, matmul(sv["act"], dxb, ta=True, out_dtype=BF16, name=f"b{l}_dwdown"), halves=True)
        dgate, dupv, fpart = ffn_act_bwd(sv["up"], lw["fconv_w"], ffn_conv_b[l], dact, nb=nb, s=s, name=f"b{l}_act")
        dh2 = carrying([("w_down", l, 0)], functools.partial(matmul, dgate, lw["wup"], tb=True, name=f"b{l}_dh2_gate"))
        dh2 = carrying([("w_down", l, 1)], functools.partial(matmul, dupv, lw["wup"], tb=True, b_koff=dff, res=dh2,
                                                             name=f"b{l}_dh2_up"))
        stacks["w_up", l] = jnp.concatenate(
            [matmul(sv["h2"], dgate, ta=True, out_dtype=BF16, out_slots=dff // w_up.shape[2], name=f"b{l}_dwup_gate"),
             matmul(sv["h2"], dupv, ta=True, out_dtype=BF16, out_slots=dff // w_up.shape[2], name=f"b{l}_dwup_up")], axis=0)
        dx1, dx1b, dg2 = rmsnorm_bwd(sv["x1"], ffn_norm_g[l], dh2, dx, name=f"b{l}_norm2")
        fsum = jnp.sum(fpart, axis=0)
        small_grads["ffn_norm_g"][l] = dg2[0]
        small_grads["ffn_conv_w"][l] = fsum[:ffn_conv_w.shape[1]]
        small_grads["ffn_conv_b"][l] = fsum[ffn_conv_w.shape[1]]
        dout_a = matmul(dx1b, lw["wout_a"], tb=True, out_dtype=BF16, name=f"b{l}_dout_a")
        dout_b = matmul(dx1b, lw["wout_b"], tb=True, name=f"b{l}_dout_b")
        produced("w_out", l, jnp.concatenate(
            [matmul(sv["out_a"], dx1b, ta=True, out_dtype=BF16, name=f"b{l}_dwout_a"),
             matmul(sv["out_b"], dx1b, ta=True, out_dtype=BF16, name=f"b{l}_dwout_b")], axis=0))
        du3 = matmul(dout_a, lw["pw"], tb=True, name=f"b{l}_du3")
        produced("conv_pw_w", l, matmul(sv["u3"], dout_a, ta=True, out_dtype=BF16, name=f"b{l}_dwpw"))
        dval, dagate, cw_part, cs_part = carrying(
            [("w_out", l), ("conv_pw_w", l)],
            functools.partial(conf_bwd, sv["p"], sv["u1"], lw["dw_w"], conv_dw_b[l], conv_ln_g[l], conv_ln_b[l], du3,
                              dout_a,
                              nb=nb, s=s, name=f"b{l}_conf"))
        csum = jnp.sum(cs_part, axis=0)
        small_grads["conv_dw_w"][l] = jnp.sum(cw_part, axis=0)
        small_grads["conv_dw_b"][l] = csum[0]
        small_grads["conv_ln_g"][l] = csum[1]
        small_grads["conv_ln_b"][l] = csum[2]
        small_grads["conv_pw_b"][l] = csum[3]
        dq, dk, dv, dz, dpba, dgp = carrying(
            [("w_up", l)],
            functools.partial(gdn_bwd, sv["qkvc"], sv["p"], sv["pba"], lw["gp"], sv["states"], dout_b,
                              nb=nb, s=s, nh=nh, nqk=nqk, zcol=zcol, name=f"b{l}_gdn"))
        dqkv, gw_part = qkvconv_bwd(sv["p"], lw["gconv_w"], jnp.concatenate([dq, dk, dv], axis=1),
                                    col0=qcol, nb=nb, s=s, name=f"b{l}_qkvconv")
        gsum = jnp.sum(dgp, axis=0)
        small_grads["gdn_conv_w"][l] = jnp.sum(gw_part, axis=0)[:gdn_conv_w.shape[1]]
        small_grads["gdn_a_log"][l] = gsum[0, :nh]
        small_grads["gdn_dt_bias"][l] = gsum[1, :nh]
        small_grads["gdn_norm_g"][l] = gsum[2]
        dp = jnp.concatenate([dval, dagate, dqkv, dz], axis=1)
        dw_main = matmul(sv["h"], dp, ta=True, out_dtype=BF16, name=f"b{l}_dwin_main")
        dw_ba = matmul(sv["h"], dpba, ta=True, out_dtype=BF16, name=f"b{l}_dwin_ba")
        produced("w_in", l, jnp.concatenate([dw_main, dw_ba[:, :2 * nh]], axis=1), halves=True)
        dh = carrying([("w_in", l, 0)], functools.partial(matmul, dp, lw["w_main"], tb=True, name=f"b{l}_dh_main"))
        dh = matmul(dpba, lw["w_ba"], tb=True, res=dh, name=f"b{l}_dh_ba")
        dx, dxb, dg1 = rmsnorm_bwd(sv["x"], mix_norm_g[l], dh, dx1, name=f"b{l}_norm1")
        small_grads["mix_norm_g"][l] = dg1[0]

    def summed(n):
        parts = [received[k] for l in range(depth) for k in ([(n, l)] if (n, l) in received else [(n, l, 0), (n, l, 1)])]
        return sum_slots(parts, name=f"sum_{n}").reshape(wts[n].shape)

    early = [n for n in BIG if n != "w_in"]
    partial = {n: summed(n) for n in early}
    swapped, last = run_exchanges([SiblingSwap([partial[n] for n in early]),
                                   Exchange([stacks["w_in", 0, 1]], gather=False)], name="swap_and_scatter_last")
    other = dict(zip(early, swapped))
    received["w_in", 0, 1] = last[0]
    partial["w_in"] = summed("w_in")
    other["w_in"] = run_exchanges([SiblingSwap([partial["w_in"]])], name="swap_w_in")[0][0]

    grads, deltas, new_m, new_v = {}, {}, {}, {}
    for n in BIG:
        grads[n], deltas[n], new_m[n], new_v[n] = adamw([partial[n], other[n]], wts[n], mom[n], var[n],
                                                        name=f"adamw_{n}")

    small_names = [n for n in WEIGHTS if n not in BIG]
    small_full = [jnp.stack(small_grads[n]) if n != "final_norm_g" else dgf[0] for n in small_names]
    packed = _pack(small_full + [loss_blk[0, :1]])
    reduced = allreduce_small(packed, name="allreduce_small")
    parts = _unpack(reduced, [a.shape for a in small_full] + [(1,)])
    loss = parts[-1][0]
    for n, g in zip(small_names, parts[:-1]):
        if n in SMALL_CONV:
            wid = wts[n].shape[-1]
            g = lax.dynamic_slice_in_dim(g, my_xy * wid, wid, axis=g.ndim - 1)
        grads[n], deltas[n], new_m[n], new_v[n] = adamw([g], wts[n], mom[n], var[n], name=f"adamw_{n}")

    return (loss, dx.reshape(nb, s, d), *[grads[n] for n in WEIGHTS], *[deltas[n] for n in WEIGHTS],
            *[new_m[n] for n in WEIGHTS], *[new_v[n] for n in WEIGHTS])
```

```python
import functools

import jax
import jax.numpy as jnp
from jax import lax
from jax.experimental import pallas as pl
from jax.experimental.pallas import tpu as pltpu

F32 = jnp.float32
BF16 = jnp.bfloat16
EPS = 1e-6
CHUNK = 64
HEAD = 128
HIGHEST = lax.Precision.HIGHEST
VMEM_LIMIT = 56 * 1024 * 1024
MM_VMEM_BUDGET = 48 * 1024 * 1024


def _pick(dim, cands):
    for c in cands:
        if dim % c == 0:
            return c
    return dim


def _cparams(sem):
    return pltpu.CompilerParams(dimension_semantics=sem, vmem_limit_bytes=VMEM_LIMIT)


MESH = pl.DeviceIdType.MESH
ANY = pl.BlockSpec(memory_space=pl.ANY)


def _xy_peers():
    x, y = lax.axis_index("x"), lax.axis_index("y")
    peers = []
    for fx, fy in ((0, 1), (1, 0), (1, 1)):
        px = 1 - x if fx else x
        py = 1 - y if fy else y
        peers.append((2 * px + py, px, py))
    return 2 * x + y, peers


class Exchange:
    def __init__(self, arrs, gather):
        self.arrs, self.gather, self.n = list(arrs), gather, len(arrs)
        self.out_shape = [jax.ShapeDtypeStruct((4,) + (a.shape if gather else a.shape[1:]), a.dtype) for a in arrs]
        self.scratch = [pltpu.SemaphoreType.DMA((3 * self.n,)), pltpu.SemaphoreType.DMA((3 * self.n,)),
                        pltpu.SemaphoreType.DMA((self.n,))]

    def copies(self, ins, outs, send_sems, recv_sems, local_sems):
        me, peers = _xy_peers()
        c = lax.axis_index("c")
        out = []
        for k in range(self.n):
            out.append(pltpu.make_async_copy(ins[k] if self.gather else ins[k].at[me], outs[k].at[me],
                                             local_sems.at[k]))
            for j, (slot, px, py) in enumerate(peers):
                out.append(pltpu.make_async_remote_copy(
                    src_ref=ins[k] if self.gather else ins[k].at[slot], dst_ref=outs[k].at[me],
                    send_sem=send_sems.at[3 * k + j], recv_sem=recv_sems.at[3 * k + j],
                    device_id=(px, py, c), device_id_type=MESH))
        return out

    def start(self, ins, outs, sems):
        for cp in self.copies(ins, outs, *sems):
            cp.start()

    def finish(self, ins, outs, sems):
        for cp in self.copies(ins, outs, *sems):
            cp.wait()


class SplitGather:
    def __init__(self, arrs, windows=None, into=None):
        self.arrs, self.n = list(arrs), len(arrs)
        self.windows = list(windows) if windows else [(0, a.shape[0]) for a in arrs]
        self.into = list(into) if into else [None] * self.n
        assert all(a.ndim == 2 and nr % 32 == 0 and r0 % 16 == 0 for a, (r0, nr) in zip(arrs, self.windows))
        self.out_shape = [jax.ShapeDtypeStruct((4,) + a.shape, a.dtype) for a in arrs]
        dma = pltpu.SemaphoreType.DMA
        self.scratch = [dma((3 * self.n,)), dma((3 * self.n,)), dma((3 * self.n,)), dma((3 * self.n,)), dma((self.n,))]

    def _window(self, k):
        return pl.ds(*self.windows[k])

    def _half(self, k, c):
        r0, nr = self.windows[k]
        return pl.ds(pl.multiple_of(r0 + c * (nr // 2), 16), nr // 2)

    def over_ici(self, ins, outs, sems):
        me, peers = _xy_peers()
        c = lax.axis_index("c")
        out = []
        for k in range(self.n):
            out.append(pltpu.make_async_copy(ins[k].at[self._window(k)], outs[k].at[me, self._window(k)],
                                             sems[4].at[k]))
            for j, (slot, px, py) in enumerate(peers):
                out.append(pltpu.make_async_remote_copy(
                    src_ref=ins[k].at[self._half(k, c)], dst_ref=outs[k].at[me, self._half(k, c)],
                    send_sem=sems[0].at[3 * k + j], recv_sem=sems[1].at[3 * k + j],
                    device_id=(px, py, c), device_id_type=MESH))
        return out

    def over_d2d(self, outs, sems):
        _, peers = _xy_peers()
        x, y, c = lax.axis_index("x"), lax.axis_index("y"), lax.axis_index("c")
        out = []
        for k in range(self.n):
            for j, (slot, _, _) in enumerate(peers):
                rows = outs[k].at[slot, self._half(k, c)]
                out.append(pltpu.make_async_remote_copy(
                    src_ref=rows, dst_ref=rows, send_sem=sems[2].at[3 * k + j], recv_sem=sems[3].at[3 * k + j],
                    device_id=(x, y, 1 - c), device_id_type=MESH))
        return out

    def start(self, ins, outs, sems):
        for cp in self.over_ici(ins, outs, sems):
            cp.start()

    def finish(self, ins, outs, sems):
        for cp in self.over_ici(ins, outs, sems):
            cp.wait()
        passed = self.over_d2d(outs, sems)
        for cp in passed:
            cp.start()
        for cp in passed:
            cp.wait()


class SiblingSwap:
    def __init__(self, arrs):
        self.arrs, self.n = list(arrs), len(arrs)
        self.out_shape = [jax.ShapeDtypeStruct(a.shape, a.dtype) for a in arrs]
        self.scratch = [pltpu.SemaphoreType.DMA((self.n,)), pltpu.SemaphoreType.DMA((self.n,))]

    def copies(self, ins, outs, sems):
        peer = (lax.axis_index("x"), lax.axis_index("y"), 1 - lax.axis_index("c"))
        return [pltpu.make_async_remote_copy(src_ref=ins[k], dst_ref=outs[k], send_sem=sems[0].at[k],
                                             recv_sem=sems[1].at[k], device_id=peer, device_id_type=MESH)
                for k in range(self.n)]

    def start(self, ins, outs, sems):
        for cp in self.copies(ins, outs, sems):
            cp.start()

    def finish(self, ins, outs, sems):
        for cp in self.copies(ins, outs, sems):
            cp.wait()


def _split_refs(refs, carries, attr):
    groups, pos = [], 0
    for cr in carries:
        n = len(getattr(cr, attr))
        groups.append(refs[pos:pos + n])
        pos += n
    return groups


def carried_call(body, *, name, grid, in_specs, out_specs, out_shape, scratch_shapes, semantics, args, carry=None):
    n_in, n_out, n_scr = len(in_specs), len(out_specs), len(scratch_shapes)
    if carry is None:
        outs = pl.pallas_call(body, name=name, grid=grid, in_specs=in_specs, out_specs=out_specs, out_shape=out_shape,
                              scratch_shapes=scratch_shapes, compiler_params=_cparams(semantics))(*args)
        return list(outs), []
    carries = list(carry) if isinstance(carry, (list, tuple)) else [carry]
    n = sum(cr.n for cr in carries)
    flat_into = [b for cr in carries for b in getattr(cr, "into", [None] * cr.n)]
    extra = [(i, b) for i, b in enumerate(flat_into) if b is not None]
    aliases = {n_in + n + e: n_out + i for e, (i, _) in enumerate(extra)}
    n_x = len(extra)

    def wrapped(*refs):
        ins, cin = refs[:n_in], _split_refs(refs[n_in:n_in + n], carries, "arrs")
        refs = refs[:n_in + n] + refs[n_in + n + n_x:]
        outs = refs[n_in + n:n_in + n + n_out]
        cout = _split_refs(refs[n_in + n + n_out:n_in + 2 * n + n_out], carries, "arrs")
        scratch = refs[n_in + 2 * n + n_out:n_in + 2 * n + n_out + n_scr]
        sems = _split_refs(refs[n_in + 2 * n + n_out + n_scr:], carries, "scratch")
        ids = [pl.program_id(i) for i in range(len(grid))]
        first = functools.reduce(jnp.logical_and, [i == 0 for i in ids])
        last = functools.reduce(jnp.logical_and, [i == g - 1 for i, g in zip(ids, grid)])

        @pl.when(first)
        def _():
            for cr, i, o, s in zip(carries, cin, cout, sems):
                cr.start(i, o, s)

        body(*ins, *outs, *scratch)

        @pl.when(last)
        def _():
            for cr, i, o, s in zip(carries, cin, cout, sems):
                cr.finish(i, o, s)

    res = pl.pallas_call(
        wrapped, name=name, grid=grid, in_specs=list(in_specs) + [ANY] * (n + n_x),
        out_specs=list(out_specs) + [ANY] * n,
        out_shape=list(out_shape) + [s for cr in carries for s in cr.out_shape],
        scratch_shapes=list(scratch_shapes) + [s for cr in carries for s in cr.scratch],
        input_output_aliases=aliases,
        compiler_params=_cparams(tuple("arbitrary" for _ in grid)),
    )(*args, *[a for cr in carries for a in cr.arrs], *[b for _, b in extra])
    got = _split_refs(list(res[n_out:]), carries, "arrs")
    return list(res[:n_out]), (got if isinstance(carry, (list, tuple)) else got[0])


def run_exchanges(carries, *, name):
    n = sum(cr.n for cr in carries)

    def body(*refs):
        cin = _split_refs(refs[:n], carries, "arrs")
        cout = _split_refs(refs[n:2 * n], carries, "arrs")
        sems = _split_refs(refs[2 * n:], carries, "scratch")
        for cr, i, o, s in zip(carries, cin, cout, sems):
            cr.start(i, o, s)
        for cr, i, o, s in zip(carries, cin, cout, sems):
            cr.finish(i, o, s)

    res = pl.pallas_call(body, name=name, in_specs=[ANY] * n, out_specs=[ANY] * n,
                         out_shape=[s for cr in carries for s in cr.out_shape],
                         scratch_shapes=[s for cr in carries for s in cr.scratch],
                         )(*[a for cr in carries for a in cr.arrs])
    return _split_refs(list(res), carries, "arrs")


def matmul(a, b, *, ta=False, tb=False, out_dtype=F32, res=None, bias=None, b_koff=0, out_slots=0, name, carry=None):
    if ta:
        kdim, m = a.shape
    else:
        m, kdim = a.shape
    slot_w = 0
    if b.ndim == 3 and tb:
        nslot, n, slot_w = b.shape
        kb = nslot * slot_w
    elif b.ndim == 3:
        nslot, kb, slot_w = b.shape
        n = nslot * slot_w
    elif tb:
        n, kb = b.shape
    else:
        kb, n = b.shape
    assert kb >= kdim + b_koff, (a.shape, b.shape, ta, tb)
    out_w = n // out_slots if out_slots else n
    if (slot_w and not tb) or out_slots:
        tn = _pick(min(slot_w, out_w) if (slot_w and not tb) else out_w, (1408, 1024, 512, 256, 128))
        assert out_w % tn == 0 and (tb or not slot_w or slot_w % tn == 0)
    else:
        tn = _pick(n, (1024, 512, 256, 128))
    out_bytes = jnp.dtype(out_dtype).itemsize

    def vmem_bytes(tm_, tk_):
        blocks = a.dtype.itemsize * tm_ * tk_ + b.dtype.itemsize * tk_ * tn + out_bytes * tm_ * tn
        blocks += 4 * tm_ * tn if res is not None else 0
        temps = 4 * tm_ * tn + (2 * tk_ * tn if tb else 0)
        return 2 * blocks + temps + (4 * tm_ * tn if tk_ < kdim else 0)

    def longest_k(tm_):
        if slot_w and tb:
            return slot_w
        return next(c for c in (kdim, 4096, 2816, 2560, 2048, 1024, 512, 256, 128)
                    if kdim % c == 0 and b_koff % c == 0 and c % 128 == 0
                    and (vmem_bytes(tm_, c) <= MM_VMEM_BUDGET or c == 128))

    tall = [c for c in (1024, 512) if m % c == 0] or [_pick(m, (256, 128))]
    tm = max(tall, key=lambda c: (longest_k(c), c))
    tk = longest_k(tm)
    nk = kdim // tk
    ko = b_koff // tk
    dims = (((0 if ta else 1,), (1 if tb else 0,)), ((), ()))

    def body(*refs):
        a_ref, b_ref = refs[0], refs[1]
        pos = 2
        bias_ref = res_ref = None
        if bias is not None:
            bias_ref = refs[pos]
            pos += 1
        if res is not None:
            res_ref = refs[pos]
            pos += 1
        o_ref = refs[pos]
        part = lax.dot_general(a_ref[...].astype(BF16), b_ref[...].astype(BF16), dims, preferred_element_type=F32)

        def finish(r):
            if bias_ref is not None:
                r = r + bias_ref[...]
            if res_ref is not None:
                r = r + res_ref[...]
            o_ref[...] = r.astype(o_ref.dtype)

        if nk == 1:
            finish(part)
            return
        acc_ref = refs[pos + 1]
        k = pl.program_id(2)

        @pl.when(k == 0)
        def _():
            acc_ref[...] = part

        @pl.when((k > 0) & (k < nk - 1))
        def _():
            acc_ref[...] += part

        @pl.when(k == nk - 1)
        def _():
            finish(acc_ref[...] + part)

    assert kdim % tk == 0 and b_koff % tk == 0
    a_spec = pl.BlockSpec((tk, tm), lambda i, j, k: (k, i)) if ta else pl.BlockSpec((tm, tk), lambda i, j, k: (i, k))
    if slot_w and tb:
        b_spec = pl.BlockSpec((None, tn, tk), lambda i, j, k: (k + ko, j, 0))
    elif slot_w:
        per = slot_w // tn
        b_spec = pl.BlockSpec((None, tk, tn), lambda i, j, k: (j // per, k + ko, j % per))
    elif tb:
        b_spec = pl.BlockSpec((tn, tk), lambda i, j, k: (j, k + ko))
    else:
        b_spec = pl.BlockSpec((tk, tn), lambda i, j, k: (k + ko, j))
    if out_slots:
        oper = out_w // tn
        out_spec = pl.BlockSpec((None, tm, tn), lambda i, j, k: (j // oper, i, j % oper))
        out_struct = jax.ShapeDtypeStruct((out_slots, m, out_w), out_dtype)
    else:
        out_spec = pl.BlockSpec((tm, tn), lambda i, j, k: (i, j))
        out_struct = jax.ShapeDtypeStruct((m, n), out_dtype)
    in_specs = [a_spec, b_spec]
    args = [a, b]
    if bias is not None:
        in_specs.append(pl.BlockSpec((1, tn), lambda i, j, k: (0, j)))
        args.append(bias.reshape(1, n).astype(F32))
    if res is not None:
        in_specs.append(pl.BlockSpec((tm, tn), lambda i, j, k: (i, j)))
        args.append(res)
    outs, got = carried_call(
        body, name=name, grid=(m // tm, n // tn, nk), in_specs=in_specs,
        out_specs=[out_spec], out_shape=[out_struct],
        scratch_shapes=[pltpu.VMEM((tm, tn), F32)] if nk > 1 else [],
        semantics=("parallel", "parallel", "arbitrary"), args=args, carry=carry)
    return outs[0] if carry is None else (outs[0], got)


def _sigmoid(x):
    return 1.0 / (1.0 + jnp.exp(-x))


def _softplus(x):
    return jnp.maximum(x, 0.0) + jnp.log(1.0 + jnp.exp(-jnp.abs(x)))


def _shift_back(x, s, row):
    if s == 0:
        return x
    return jnp.where(row >= s, pltpu.roll(x, s, 0), 0.0)


def _shift_fwd(x, s, row):
    if s == 0:
        return x
    n = x.shape[0]
    return jnp.where(row < n - s, pltpu.roll(x, n - s, 0), 0.0)


def _conv_fwd(x, w_ref, kw, row):
    acc = x * w_ref[pl.ds(kw - 1, 1), :]
    for s in range(1, kw):
        acc = acc + _shift_back(x, s, row) * w_ref[pl.ds(kw - 1 - s, 1), :]
    return acc


def _conv_bwd(x, dy, w_ref, kw, row):
    dx = dy * w_ref[pl.ds(kw - 1, 1), :]
    dw = [None] * kw
    dw[kw - 1] = jnp.sum(dy * x, axis=0, keepdims=True)
    for s in range(1, kw):
        dx = dx + _shift_fwd(dy, s, row) * w_ref[pl.ds(kw - 1 - s, 1), :]
        dw[kw - 1 - s] = jnp.sum(dy * _shift_back(x, s, row), axis=0, keepdims=True)
    return dx, dw


def _rows_to_block(rows, nrows, width):
    rid = lax.broadcasted_iota(jnp.int32, (nrows, width), 0)
    out = jnp.zeros((nrows, width), F32)
    for i, r in enumerate(rows):
        out = jnp.where(rid == i, r, out)
    return out


def rmsnorm_fwd(x, g, *, name):
    t, d = x.shape
    tm = _pick(t, (256, 128))

    def body(x_ref, g_ref, o_ref):
        xv = x_ref[...]
        r = lax.rsqrt(jnp.mean(xv * xv, axis=-1, keepdims=True) + EPS)
        o_ref[...] = (xv * r * g_ref[...]).astype(o_ref.dtype)

    return pl.pallas_call(
        body, name=name, grid=(t // tm,),
        in_specs=[pl.BlockSpec((tm, d), lambda i: (i, 0)), pl.BlockSpec((1, d), lambda i: (0, 0))],
        out_specs=pl.BlockSpec((tm, d), lambda i: (i, 0)),
        out_shape=jax.ShapeDtypeStruct((t, d), BF16),
        compiler_params=_cparams(("parallel",)),
    )(x, g.reshape(1, d))


def rmsnorm_bwd(x, g, dh, dres, *, name):
    t, d = x.shape
    tm = _pick(t, (256, 128))

    def body(x_ref, g_ref, dh_ref, dres_ref, dx_ref, dxb_ref, dg_ref):
        xv = x_ref[...]
        r = lax.rsqrt(jnp.mean(xv * xv, axis=-1, keepdims=True) + EPS)
        xh = xv * r
        dy = dh_ref[...]
        dxh = dy * g_ref[...]
        dx = dres_ref[...] + r * (dxh - xh * jnp.mean(dxh * xh, axis=-1, keepdims=True))
        dx_ref[...] = dx
        dxb_ref[...] = dx.astype(BF16)

        @pl.when(pl.program_id(0) == 0)
        def _():
            dg_ref[...] = jnp.zeros_like(dg_ref)

        dg_ref[...] += jnp.sum(dy * xh, axis=0, keepdims=True)

    row = pl.BlockSpec((tm, d), lambda i: (i, 0))
    vec = pl.BlockSpec((1, d), lambda i: (0, 0))
    return pl.pallas_call(
        body, name=name, grid=(t // tm,),
        in_specs=[row, vec, row, row],
        out_specs=[row, row, vec],
        out_shape=[jax.ShapeDtypeStruct((t, d), F32), jax.ShapeDtypeStruct((t, d), BF16),
                   jax.ShapeDtypeStruct((1, d), F32)],
        compiler_params=_cparams(("arbitrary",)),
    )(x, g.reshape(1, d), dh, dres)


def loss_head(x, g, target, *, name):
    t, d = x.shape
    tm = _pick(t, (256, 128))

    def body(x_ref, g_ref, tg_ref, loss_ref, dx_ref, dxb_ref, dg_ref):
        xv = x_ref[...]
        r = lax.rsqrt(jnp.mean(xv * xv, axis=-1, keepdims=True) + EPS)
        xh = xv * r
        err = xh * g_ref[...] - tg_ref[...]
        dy = err * (1.0 / d)
        dxh = dy * g_ref[...]
        dx = r * (dxh - xh * jnp.mean(dxh * xh, axis=-1, keepdims=True))
        dx_ref[...] = dx
        dxb_ref[...] = dx.astype(BF16)

        @pl.when(pl.program_id(0) == 0)
        def _():
            dg_ref[...] = jnp.zeros_like(dg_ref)
            loss_ref[...] = jnp.zeros_like(loss_ref)

        dg_ref[...] += jnp.sum(dy * xh, axis=0, keepdims=True)
        part = jnp.sum(jnp.sum(err * err, axis=-1, keepdims=True), axis=0, keepdims=True) * (0.5 / d)
        loss_ref[...] += jnp.broadcast_to(part, loss_ref.shape)

    row = pl.BlockSpec((tm, d), lambda i: (i, 0))
    vec = pl.BlockSpec((1, d), lambda i: (0, 0))
    return pl.pallas_call(
        body, name=name, grid=(t // tm,),
        in_specs=[row, vec, row],
        out_specs=[pl.BlockSpec((8, 128), lambda i: (0, 0)), row, row, vec],
        out_shape=[jax.ShapeDtypeStruct((8, 128), F32), jax.ShapeDtypeStruct((t, d), F32),
                   jax.ShapeDtypeStruct((t, d), BF16), jax.ShapeDtypeStruct((1, d), F32)],
        compiler_params=_cparams(("arbitrary",)),
    )(x, g.reshape(1, d), target)


def _conf_forward_parts(val, gate, w_ref, b, lg, lb, kw, row, u1=None):
    sg = _sigmoid(gate)
    u0 = val * sg
    if u1 is None:
        u1 = _conv_fwd(u0, w_ref, kw, row) + b
    mu = jnp.mean(u1, axis=-1, keepdims=True)
    xc = u1 - mu
    rs = lax.rsqrt(jnp.mean(xc * xc, axis=-1, keepdims=True) + EPS)
    xh = xc * rs
    u2 = xh * lg + lb
    s2 = _sigmoid(u2)
    return sg, u0, u1, rs, xh, u2, s2


def conf_fwd(p, dw_w, dw_b, ln_g, ln_b, *, nb, s, name, carry=None):
    kw, ch = dw_w.shape
    ng = ch // HEAD

    def body(val_ref, gate_ref, w_ref, b_ref, lg_ref, lb_ref, o_ref, u1_ref):
        row = lax.broadcasted_iota(jnp.int32, (s, HEAD), 0)
        _, _, u1, _, _, u2, s2 = _conf_forward_parts(val_ref[...], gate_ref[...], w_ref, b_ref[...], lg_ref[...],
                                                     lb_ref[...], kw, row)
        o_ref[...] = (u2 * s2).astype(o_ref.dtype)
        u1_ref[...] = u1

    vec = pl.BlockSpec((1, HEAD), lambda b, g: (0, g))
    blk = pl.BlockSpec((s, HEAD), lambda b, g: (b, g))
    outs, got = carried_call(
        body, name=name, grid=(nb, ng),
        in_specs=[blk, pl.BlockSpec((s, HEAD), lambda b, g: (b, ng + g)),
                  pl.BlockSpec((kw, HEAD), lambda b, g: (0, g)), vec, vec, vec],
        out_specs=[blk, blk],
        out_shape=[jax.ShapeDtypeStruct((nb * s, ch), BF16), jax.ShapeDtypeStruct((nb * s, ch), F32)],
        scratch_shapes=[], semantics=("parallel", "parallel"),
        args=(p, p, dw_w, dw_b.reshape(1, ch), ln_g.reshape(1, ch), ln_b.reshape(1, ch)), carry=carry)
    return outs + [got]


def conf_bwd(p, u1, dw_w, dw_b, ln_g, ln_b, du3, dout_a, *, nb, s, name, carry=None):
    kw, ch = dw_w.shape
    ng = ch // HEAD

    def body(val_ref, gate_ref, u1_ref, w_ref, b_ref, lg_ref, lb_ref, du3_ref, doa_ref, dval_ref, dgate_ref, dw_out,
             sm_out):
        row = lax.broadcasted_iota(jnp.int32, (s, HEAD), 0)
        val = val_ref[...]
        sg, u0, _, rs, xh, u2, s2 = _conf_forward_parts(val, gate_ref[...], w_ref, b_ref[...], lg_ref[...],
                                                        lb_ref[...], kw, row, u1=u1_ref[...])
        du2 = du3_ref[...] * (s2 * (1.0 + u2 * (1.0 - s2)))
        dlg = jnp.sum(du2 * xh, axis=0, keepdims=True)
        dlb = jnp.sum(du2, axis=0, keepdims=True)
        dxh = du2 * lg_ref[...]
        du1 = rs * (dxh - jnp.mean(dxh, axis=-1, keepdims=True) - xh * jnp.mean(dxh * xh, axis=-1, keepdims=True))
        ddb = jnp.sum(du1, axis=0, keepdims=True)
        du0, dw = _conv_bwd(u0, du1, w_ref, kw, row)
        dval_ref[...] = (du0 * sg).astype(dval_ref.dtype)
        dgate_ref[...] = (du0 * val * sg * (1.0 - sg)).astype(dgate_ref.dtype)
        for k in range(kw):
            dw_out[0, pl.ds(k, 1), :] = dw[k]
        dpb = jnp.sum(doa_ref[...].astype(F32), axis=0, keepdims=True)
        sm_out[0] = _rows_to_block([ddb, dlg, dlb, dpb], 8, HEAD)

    vec = pl.BlockSpec((1, HEAD), lambda b, g: (0, g))
    blk = pl.BlockSpec((s, HEAD), lambda b, g: (b, g))
    outs, got = carried_call(
        body, name=name, grid=(nb, ng),
        in_specs=[blk, pl.BlockSpec((s, HEAD), lambda b, g: (b, ng + g)), blk,
                  pl.BlockSpec((kw, HEAD), lambda b, g: (0, g)), vec, vec, vec, blk, blk],
        out_specs=[blk, blk, pl.BlockSpec((1, kw, HEAD), lambda b, g: (b, 0, g)),
                   pl.BlockSpec((1, 8, HEAD), lambda b, g: (b, 0, g))],
        out_shape=[jax.ShapeDtypeStruct((nb * s, ch), BF16), jax.ShapeDtypeStruct((nb * s, ch), BF16),
                   jax.ShapeDtypeStruct((nb, kw, ch), F32), jax.ShapeDtypeStruct((nb, 8, ch), F32)],
        scratch_shapes=[], semantics=("parallel", "parallel"),
        args=(p, p, u1, dw_w, dw_b.reshape(1, ch), ln_g.reshape(1, ch), ln_b.reshape(1, ch), du3, dout_a),
        carry=carry)
    return outs + [got]


def qkvconv_fwd(p, w, *, col0, nb, s, name, carry=None):
    kw, ch = w.shape
    nblk = ch // HEAD
    c0 = col0 // HEAD

    def body(x_ref, w_ref, o_ref):
        row = lax.broadcasted_iota(jnp.int32, (s, HEAD), 0)
        c = _conv_fwd(x_ref[...], w_ref, kw, row)
        o_ref[...] = c * _sigmoid(c)

    outs, got = carried_call(
        body, name=name, grid=(nb, nblk),
        in_specs=[pl.BlockSpec((s, HEAD), lambda b, j: (b, c0 + j)), pl.BlockSpec((kw, HEAD), lambda b, j: (0, j))],
        out_specs=[pl.BlockSpec((s, HEAD), lambda b, j: (b, j))],
        out_shape=[jax.ShapeDtypeStruct((nb * s, ch), F32)],
        scratch_shapes=[], semantics=("parallel", "parallel"), args=(p, w), carry=carry)
    return outs + [got]


def qkvconv_bwd(p, w, dy, *, col0, nb, s, name):
    kw, ch = w.shape
    nblk = ch // HEAD
    c0 = col0 // HEAD

    def body(x_ref, w_ref, dy_ref, dx_ref, dw_out):
        row = lax.broadcasted_iota(jnp.int32, (s, HEAD), 0)
        xv = x_ref[...]
        c = _conv_fwd(xv, w_ref, kw, row)
        sc = _sigmoid(c)
        dc = dy_ref[...] * (sc * (1.0 + c * (1.0 - sc)))
        dx, dw = _conv_bwd(xv, dc, w_ref, kw, row)
        dx_ref[...] = dx.astype(dx_ref.dtype)
        dw_out[0] = _rows_to_block(dw, 8, HEAD)

    blk = pl.BlockSpec((s, HEAD), lambda b, j: (b, j))
    return pl.pallas_call(
        body, name=name, grid=(nb, nblk),
        in_specs=[pl.BlockSpec((s, HEAD), lambda b, j: (b, c0 + j)), pl.BlockSpec((kw, HEAD), lambda b, j: (0, j)), blk],
        out_specs=[blk, pl.BlockSpec((1, 8, HEAD), lambda b, j: (b, 0, j))],
        out_shape=[jax.ShapeDtypeStruct((nb * s, ch), BF16), jax.ShapeDtypeStruct((nb, 8, ch), F32)],
        compiler_params=_cparams(("parallel", "parallel")),
    )(p, w, dy)


def ffn_act_fwd(up, w, b, *, nb, s, name, carry=None):
    kw, dff = w.shape
    cb = _pick(dff, (256, 128))
    nblk = dff // cb

    def body(g_ref, u_ref, w_ref, b_ref, o_ref):
        row = lax.broadcasted_iota(jnp.int32, (s, cb), 0)
        gc = _conv_fwd(g_ref[...].astype(F32), w_ref, kw, row) + b_ref[...]
        o_ref[...] = (gc * _sigmoid(gc) * u_ref[...].astype(F32)).astype(o_ref.dtype)

    outs, got = carried_call(
        body, name=name, grid=(nb, nblk),
        in_specs=[pl.BlockSpec((s, cb), lambda i, j: (i, j)), pl.BlockSpec((s, cb), lambda i, j: (i, nblk + j)),
                  pl.BlockSpec((kw, cb), lambda i, j: (0, j)), pl.BlockSpec((1, cb), lambda i, j: (0, j))],
        out_specs=[pl.BlockSpec((s, cb), lambda i, j: (i, j))],
        out_shape=[jax.ShapeDtypeStruct((nb * s, dff), BF16)],
        scratch_shapes=[], semantics=("parallel", "parallel"), args=(up, up, w, b.reshape(1, dff)), carry=carry)
    return outs + [got]


def ffn_act_bwd(up, w, b, dact, *, nb, s, name):
    kw, dff = w.shape
    cb = _pick(dff, (256, 128))
    nblk = dff // cb

    def body(g_ref, u_ref, w_ref, b_ref, da_ref, dg_ref, du_ref, sm_out):
        row = lax.broadcasted_iota(jnp.int32, (s, cb), 0)
        gv = g_ref[...].astype(F32)
        gc = _conv_fwd(gv, w_ref, kw, row) + b_ref[...]
        sc = _sigmoid(gc)
        da = da_ref[...].astype(F32)
        du_ref[...] = (da * gc * sc).astype(du_ref.dtype)
        dgc = da * u_ref[...].astype(F32) * (sc * (1.0 + gc * (1.0 - sc)))
        dgate, dw = _conv_bwd(gv, dgc, w_ref, kw, row)
        dg_ref[...] = dgate.astype(dg_ref.dtype)
        sm_out[0] = _rows_to_block(dw + [jnp.sum(dgc, axis=0, keepdims=True)], 8, cb)

    blk = pl.BlockSpec((s, cb), lambda i, j: (i, j))
    return pl.pallas_call(
        body, name=name, grid=(nb, nblk),
        in_specs=[blk, pl.BlockSpec((s, cb), lambda i, j: (i, nblk + j)),
                  pl.BlockSpec((kw, cb), lambda i, j: (0, j)), pl.BlockSpec((1, cb), lambda i, j: (0, j)), blk],
        out_specs=[blk, blk, pl.BlockSpec((1, 8, cb), lambda i, j: (i, 0, j))],
        out_shape=[jax.ShapeDtypeStruct((nb * s, dff), BF16), jax.ShapeDtypeStruct((nb * s, dff), BF16),
                   jax.ShapeDtypeStruct((nb, 8, dff), F32)],
        compiler_params=_cparams(("parallel", "parallel")),
    )(up, up, w, b.reshape(1, dff), dact)


def _dot(a, b, dims):
    return lax.dot_general(a, b, (dims, ((0,), (0,))), preferred_element_type=F32)


def _mm_nn(a, b):
    return _dot(a.astype(BF16), b.astype(BF16), ((2,), (1,)))


def _mm_nt(a, b):
    return _dot(a.astype(BF16), b.astype(BF16), ((2,), (2,)))


def _mm_tn(a, b):
    return _dot(a.astype(BF16), b.astype(BF16), ((1,), (1,)))


def _split3(x):
    hi = x.astype(BF16)
    rest = x - hi.astype(F32)
    mid = rest.astype(BF16)
    return hi, mid, (rest - mid.astype(F32)).astype(BF16)


def _mask_dot(mask, x, dims, mask_first):
    if mask_first:
        return sum(_dot(mask, p, dims) for p in _split3(x))
    return sum(_dot(p, mask, dims) for p in _split3(x))


@jax.custom_vjp
def _mask_nn(mask, x):
    return _mask_dot(mask, x, ((2,), (1,)), True)


def _mask_nn_fwd(mask, x):
    return _mask_nn(mask, x), mask


def _mask_nn_bwd(mask, ct):
    return jnp.zeros_like(mask), _mask_dot(mask, ct, ((1,), (1,)), True)


_mask_nn.defvjp(_mask_nn_fwd, _mask_nn_bwd)


@jax.custom_vjp
def _mask_tn(x, mask):
    return _mask_dot(mask, x, ((1,), (1,)), False)


def _mask_tn_fwd(x, mask):
    return _mask_tn(x, mask), mask


def _mask_tn_bwd(mask, ct):
    return _mask_dot(mask, ct, ((2,), (2,)), True), jnp.zeros_like(mask)


_mask_tn.defvjp(_mask_tn_fwd, _mask_tn_bwd)


GDN_ROWS = 256


def _gdn_chunk(qc, kc, vc, zc, braw, araw, alog, dtb, ng, state):
    nhead, r = qc.shape[0], qc.shape[1]
    ri = lax.broadcasted_iota(jnp.int32, (r, r), 0)
    ci = lax.broadcasted_iota(jnp.int32, (r, r), 1)
    same = (ri // CHUNK) == (ci // CHUNK)
    causal = same & (ri >= ci)
    strict = same & (ri > ci)
    eye = (ri == ci).astype(F32)
    row_chunk = lax.broadcasted_iota(jnp.int32, (r, HEAD), 0) // CHUNK
    per_head = lambda m: jnp.broadcast_to(m.astype(BF16), (nhead, r, r))

    q = qc * lax.rsqrt(jnp.sum(qc * qc, axis=-1, keepdims=True) + EPS) * (HEAD ** -0.5)
    k = kc * lax.rsqrt(jnp.sum(kc * kc, axis=-1, keepdims=True) + EPS)
    beta = _sigmoid(braw)
    g = -jnp.exp(alog) * _softplus(araw + dtb)

    g_w = jnp.broadcast_to(g, (nhead, r, HEAD))
    widen = lambda t: jnp.concatenate([t] * (r // HEAD), axis=2)
    gw = _mask_nn(per_head(causal), g_w)
    gi = widen(gw)
    gj = _mask_tn(widen(g_w), per_head(same & (ri <= ci)))
    decay = jnp.where(causal, jnp.exp(jnp.where(causal, gi - gj, 0.0)), 0.0)

    kb = k * beta
    vb = vc * beta
    lmat = jnp.where(strict, _mm_nt(kb, k) * decay, 0.0)
    x = -lmat
    ainv = eye + x
    p = 1
    while 2 * p < CHUNK:
        x = _mm_nn(x, x)
        ainv = _mm_nn(ainv, eye + x)
        p *= 2
    u = _mm_nn(ainv, vb)
    w = _mm_nn(ainv, kb * jnp.exp(gw))
    qk = jnp.where(causal, _mm_nt(q, k) * decay, 0.0)
    qg = q * jnp.exp(gw)

    o = jnp.zeros((nhead, r, HEAD), F32)
    for c in range(r // CHUNK):
        in_c = row_chunk == c
        glast = jnp.sum(jnp.where(in_c, g_w, 0.0), axis=1, keepdims=True)
        v_new = jnp.where(in_c, u - _mm_nn(w, state), 0.0)
        o = o + jnp.where(in_c, _mm_nn(qg, state), 0.0) + _mm_nn(qk, v_new)
        k_dec = jnp.where(in_c, k * jnp.exp(jnp.where(in_c, glast - gw, 0.0)), 0.0)
        state = state * jnp.exp(glast) + _mm_tn(k_dec, v_new)

    o = o * lax.rsqrt(jnp.mean(o * o, axis=-1, keepdims=True) + EPS) * ng
    o = o * (zc * _sigmoid(zc))
    return o, state


GDN_HEADS_FWD = 4
GDN_HEADS_BWD = 4


def _gdn_specs(s, nh, nqk, zcol, n_chunks, gh, single=False):
    rep = nh // nqk
    qw, vw = (gh // rep) * HEAD, gh * HEAD
    assert gh % rep == 0 and nh % gh == 0 and (nqk * HEAD) % qw == 0 and (2 * nqk * HEAD) % vw == 0 and zcol % vw == 0
    k0, v0, z0 = (nqk * HEAD) // qw, (2 * nqk * HEAD) // vw, zcol // vw
    mode = dict(pipeline_mode=pl.Buffered(1)) if single else {}
    return dict(
        q=pl.BlockSpec((s, qw), lambda b, j: (b, j), **mode),
        k=pl.BlockSpec((s, qw), lambda b, j: (b, k0 + j), **mode),
        v=pl.BlockSpec((s, vw), lambda b, j: (b, v0 + j), **mode),
        z=pl.BlockSpec((s, vw), lambda b, j: (b, z0 + j), **mode),
        ba=pl.BlockSpec((s, HEAD), lambda b, j: (b, 0)),
        gp=pl.BlockSpec((8, HEAD), lambda b, j: (0, 0)),
        qk_out=pl.BlockSpec((s, qw), lambda b, j: (b, j), **mode),
        head=pl.BlockSpec((s, vw), lambda b, j: (b, j), **mode),
        head_in=pl.BlockSpec((s, vw), lambda b, j: (b, j), **mode),
        st=pl.BlockSpec((1, gh, n_chunks, HEAD, HEAD), lambda b, j: (b, j, 0, 0, 0)),
        st_in=pl.BlockSpec((1, gh, n_chunks, HEAD, HEAD), lambda b, j: (b, j, 0, 0, 0), **mode),
    )


def _gdn_scalars(gp_ref, h):
    lane = lax.broadcasted_iota(jnp.int32, (1, HEAD), 1)
    sel = (lane == h).astype(F32)
    alog = jnp.sum(gp_ref[pl.ds(0, 1), :] * sel, axis=-1, keepdims=True)
    dtb = jnp.sum(gp_ref[pl.ds(1, 1), :] * sel, axis=-1, keepdims=True)
    return alog, dtb, sel


def _lanes(i):
    return pl.ds(i * HEAD, HEAD)


def gdn_fwd(qkvc, p, pba, gp, *, nb, s, nh, nqk, zcol, name, carry=None):
    rb = min(GDN_ROWS, s)
    n_chunks = s // rb
    gh = GDN_HEADS_FWD
    rep = nh // nqk
    sp = _gdn_specs(s, nh, nqk, zcol, n_chunks, gh)

    def body(q_ref, k_ref, v_ref, z_ref, ba_ref, gp_ref, o_ref, st_ref):
        h0 = pl.program_id(1) * gh
        ng = gp_ref[pl.ds(2, 1), :]
        lane = lax.broadcasted_iota(jnp.int32, (rb, HEAD), 1)
        heads = []
        for i in range(gh):
            alog, dtb, _ = _gdn_scalars(gp_ref, h0 + i)
            heads.append((alog, dtb, (lane == h0 + i).astype(F32), (lane == nh + h0 + i).astype(F32)))

        alogs = jnp.stack([hd[0] for hd in heads])
        dtbs = jnp.stack([hd[1] for hd in heads])

        def step(n, state):
            rows = pl.ds(pl.multiple_of(n * rb, rb), rb)
            ba = ba_ref[rows, :]
            qs = jnp.stack([q_ref[rows, _lanes(i // rep)] for i in range(gh)])
            ks = jnp.stack([k_ref[rows, _lanes(i // rep)] for i in range(gh)])
            vs = jnp.stack([v_ref[rows, _lanes(i)] for i in range(gh)])
            zs = jnp.stack([z_ref[rows, _lanes(i)] for i in range(gh)])
            braw = jnp.stack([jnp.sum(ba * hd[2], axis=-1, keepdims=True) for hd in heads])
            araw = jnp.stack([jnp.sum(ba * hd[3], axis=-1, keepdims=True) for hd in heads])
            o, new_state = _gdn_chunk(qs, ks, vs, zs, braw, araw, alogs, dtbs, ng, state)
            for i in range(gh):
                st_ref[0, i, n] = state[i]
                o_ref[rows, _lanes(i)] = o[i].astype(o_ref.dtype)
            return new_state

        lax.fori_loop(0, n_chunks, step, jnp.zeros((gh, HEAD, HEAD), F32))

    outs, got = carried_call(
        body, name=name, grid=(nb, nh // gh),
        in_specs=[sp["q"], sp["k"], sp["v"], sp["z"], sp["ba"], sp["gp"]],
        out_specs=[sp["head"], sp["st"]],
        out_shape=[jax.ShapeDtypeStruct((nb * s, nh * HEAD), BF16),
                   jax.ShapeDtypeStruct((nb, nh, n_chunks, HEAD, HEAD), F32)],
        scratch_shapes=[], semantics=("parallel", "parallel"), args=(qkvc, qkvc, qkvc, p, pba, gp), carry=carry)
    return outs + [got]


def gdn_bwd(qkvc, p, pba, gp, states, dout, *, nb, s, nh, nqk, zcol, name, carry=None):
    rb = min(GDN_ROWS, s)
    n_chunks = s // rb
    gh = GDN_HEADS_BWD
    rep = nh // nqk
    sp = _gdn_specs(s, nh, nqk, zcol, n_chunks, gh, single=True)

    def body(q_ref, k_ref, v_ref, z_ref, ba_ref, gp_ref, st_ref, do_ref,
             dq_ref, dk_ref, dv_ref, dz_ref, dba_ref, dgp_ref):
        h0 = pl.program_id(1) * gh
        ng = gp_ref[pl.ds(2, 1), :]
        lane = lax.broadcasted_iota(jnp.int32, (rb, HEAD), 1)
        heads = []
        for i in range(gh):
            alog, dtb, sel_row = _gdn_scalars(gp_ref, h0 + i)
            heads.append((alog, dtb, (lane == h0 + i).astype(F32), (lane == nh + h0 + i).astype(F32), sel_row))

        @pl.when(h0 == 0)
        def _():
            dba_ref[...] = jnp.zeros_like(dba_ref)
            dgp_ref[...] = jnp.zeros_like(dgp_ref)

        alogs = jnp.stack([hd[0] for hd in heads])
        dtbs = jnp.stack([hd[1] for hd in heads])

        def step(it, carry):
            dstate, dalog, ddtb, dng = carry
            n = n_chunks - 1 - it
            rows = pl.ds(pl.multiple_of(n * rb, rb), rb)
            ba = ba_ref[rows, :]
            qs = jnp.stack([q_ref[rows, _lanes(i // rep)] for i in range(gh)])
            ks = jnp.stack([k_ref[rows, _lanes(i // rep)] for i in range(gh)])
            vs = jnp.stack([v_ref[rows, _lanes(i)] for i in range(gh)])
            zs = jnp.stack([z_ref[rows, _lanes(i)] for i in range(gh)])
            dos = jnp.stack([do_ref[rows, _lanes(i)] for i in range(gh)])
            braw = jnp.stack([jnp.sum(ba * hd[2], axis=-1, keepdims=True) for hd in heads])
            araw = jnp.stack([jnp.sum(ba * hd[3], axis=-1, keepdims=True) for hd in heads])
            _, vjp = jax.vjp(_gdn_chunk, qs, ks, vs, zs, braw, araw, alogs, dtbs, ng, st_ref[0, :, n])
            gq, gk, gv, gz, gb, ga, galog, gdtb, gng, gstate = vjp((dos, dstate))
            dba = dba_ref[rows, :]
            for i in range(gh):
                dv_ref[rows, _lanes(i)] = gv[i]
                dz_ref[rows, _lanes(i)] = gz[i].astype(dz_ref.dtype)
                dba = dba + (gb[i] * heads[i][2] + ga[i] * heads[i][3])
            for j in range(gh // rep):
                dq_ref[rows, _lanes(j)] = sum(gq[i] for i in range(j * rep, (j + 1) * rep))
                dk_ref[rows, _lanes(j)] = sum(gk[i] for i in range(j * rep, (j + 1) * rep))
            dba_ref[rows, :] = dba
            return gstate, dalog + galog, ddtb + gdtb, dng + gng

        init = (jnp.zeros((gh, HEAD, HEAD), F32), jnp.zeros((gh, 1, 1), F32), jnp.zeros((gh, 1, 1), F32),
                jnp.zeros((1, HEAD), F32))
        _, dalog, ddtb, dng = lax.fori_loop(0, n_chunks, step, init)
        rows3 = [sum(dalog[i] * heads[i][4] for i in range(gh)), sum(ddtb[i] * heads[i][4] for i in range(gh)), dng]
        dgp_ref[0] += _rows_to_block(rows3, 8, HEAD)

    outs, got = carried_call(
        body, name=name, grid=(nb, nh // gh),
        in_specs=[sp["q"], sp["k"], sp["v"], sp["z"], sp["ba"], sp["gp"], sp["st_in"], sp["head_in"]],
        out_specs=[sp["qk_out"], sp["qk_out"], sp["head"], sp["head"], sp["ba"],
                   pl.BlockSpec((1, 8, HEAD), lambda b, j: (b, 0, 0))],
        out_shape=[jax.ShapeDtypeStruct((nb * s, nqk * HEAD), F32), jax.ShapeDtypeStruct((nb * s, nqk * HEAD), F32),
                   jax.ShapeDtypeStruct((nb * s, nh * HEAD), F32), jax.ShapeDtypeStruct((nb * s, nh * HEAD), BF16),
                   jax.ShapeDtypeStruct((nb * s, HEAD), F32), jax.ShapeDtypeStruct((nb, 8, HEAD), F32)],
        scratch_shapes=[], semantics=("parallel", "arbitrary"),
        args=(qkvc, qkvc, qkvc, p, pba, gp, states, dout), carry=carry)
    return outs + [got]


ADAM_LR = 0.001
ADAM_B1 = 0.9
ADAM_B2 = 0.999
ADAM_EPS = 1e-08
ADAM_WD = 0.01
ADAM_STEP = 10
EW_BLOCK_BYTES = 1 << 20


def _row_tile(rows, cols):
    for tr in (1024, 512, 256, 128, 64, 32, 16, 8):
        if rows % tr == 0 and tr * cols * 4 <= EW_BLOCK_BYTES:
            return tr
    return rows


def sum_slots(rs, *, name):
    nl = len(rs)
    n, rows, cols = rs[0].shape
    tr = _row_tile(rows, cols)
    nblk = rows // tr

    def body(*refs):
        o_ref = refs[nl]
        for l in range(nl):
            @pl.when(pl.program_id(0) == l)
            def _(l=l):
                acc = refs[l][0].astype(F32)
                for i in range(1, n):
                    acc = acc + refs[l][i].astype(F32)
                o_ref[0] = acc.astype(o_ref.dtype)

    def in_map(l):
        return lambda li, i: (0, jnp.where(li == l, i, jnp.where(li < l, 0, nblk - 1)), 0)

    return pl.pallas_call(
        body, name=name, grid=(nl, nblk),
        in_specs=[pl.BlockSpec((n, tr, cols), in_map(l)) for l in range(nl)],
        out_specs=pl.BlockSpec((1, tr, cols), lambda li, i: (li, i, 0)),
        out_shape=jax.ShapeDtypeStruct((nl, rows, cols), BF16),
        compiler_params=_cparams(("arbitrary", "arbitrary")),
    )(*rs)


def adamw(g_parts, w, m, v, *, name):
    shape = w.shape
    cols = shape[-1]
    lead = shape[0] if w.ndim == 3 else 1
    view = (lambda a: a.reshape(shape)) if w.ndim == 3 else (lambda a: a.reshape(1, -1, cols))
    rows = view(w).shape[1]
    tr = _row_tile(rows, cols)
    npart = len(g_parts)
    c1 = 1.0 - ADAM_B1 ** ADAM_STEP
    c2 = 1.0 - ADAM_B2 ** ADAM_STEP

    def body(*refs):
        w_ref, m_ref, v_ref = refs[npart:npart + 3]
        g_ref, d_ref, nm_ref, nv_ref = refs[npart + 3:]
        g = refs[0][...].astype(F32)
        for i in range(1, npart):
            g = g + refs[i][...].astype(F32)
        nm = ADAM_B1 * m_ref[...] + (1.0 - ADAM_B1) * g
        nv = ADAM_B2 * v_ref[...] + (1.0 - ADAM_B2) * (g * g)
        g_ref[...] = g
        nm_ref[...] = nm
        nv_ref[...] = nv
        d_ref[...] = -ADAM_LR * ((nm / c1) / (jnp.sqrt(nv / c2) + ADAM_EPS) + ADAM_WD * w_ref[...])

    blk = pl.BlockSpec((1, tr, cols), lambda l, i: (l, i, 0))
    outs = pl.pallas_call(
        body, name=name, grid=(lead, rows // tr),
        in_specs=[blk] * (npart + 3),
        out_specs=[blk] * 4,
        out_shape=[jax.ShapeDtypeStruct((lead, rows, cols), F32)] * 4,
        compiler_params=_cparams(("parallel", "parallel")),
    )(*[view(a) for a in g_parts], view(w), view(m), view(v))
    return tuple(o.reshape(shape) for o in outs)


def allreduce_small(vec, *, name):
    r = vec.shape[0]

    def body(v_ref, o_ref, slots, send_sems, recv_sems):
        x, y, c = lax.axis_index("x"), lax.axis_index("y"), lax.axis_index("c")
        me = 4 * x + 2 * y + c
        slots[me] = v_ref[...]
        copies = []
        for j in range(1, 8):
            px = 1 - x if j & 4 else x
            py = 1 - y if j & 2 else y
            pc = 1 - c if j & 1 else c
            rc = pltpu.make_async_remote_copy(src_ref=v_ref, dst_ref=slots.at[me], send_sem=send_sems.at[j - 1],
                                              recv_sem=recv_sems.at[j - 1], device_id=(px, py, pc), device_id_type=MESH)
            rc.start()
            copies.append(rc)
        for cp in copies:
            cp.wait()
        acc = slots[0]
        for i in range(1, 8):
            acc = acc + slots[i]
        o_ref[...] = acc

    vm = pl.BlockSpec(memory_space=pltpu.VMEM)
    return pl.pallas_call(
        body, name=name, in_specs=[vm], out_specs=vm,
        out_shape=jax.ShapeDtypeStruct((r, 128), F32),
        scratch_shapes=[pltpu.VMEM((8, r, 128), F32), pltpu.SemaphoreType.DMA((7,)), pltpu.SemaphoreType.DMA((7,))],
        compiler_params=pltpu.CompilerParams(vmem_limit_bytes=VMEM_LIMIT),
    )(vec)


WEIGHTS = ("mix_norm_g", "w_in", "conv_dw_w", "conv_dw_b", "conv_ln_g", "conv_ln_b", "conv_pw_w", "conv_pw_b",
           "gdn_conv_w", "gdn_a_log", "gdn_dt_bias", "gdn_norm_g", "w_out", "ffn_norm_g", "w_up", "ffn_conv_w",
           "ffn_conv_b", "w_down", "final_norm_g")
COL_SHARDED = ("w_in", "w_up", "conv_dw_w", "gdn_conv_w", "ffn_conv_w")
ROW_SHARDED = ("conv_pw_w", "w_out", "w_down")
BIG = ("w_in", "conv_pw_w", "w_out", "w_up", "w_down")
SMALL_CONV = ("conv_dw_w", "gdn_conv_w", "ffn_conv_w")
GATHER_PARTS = {"w_up": (18, 6, 30, 5, 5), "w_down": (38, 6), "w_in": (15, 49), "conv_pw_w": (1,), "w_out": (1,)}


def _full_from_slots(name, part):
    if name in COL_SHARDED:
        r, cs = part.shape[1:]
        return jnp.transpose(part, (1, 0, 2)).reshape(r, 4 * cs)
    rs, c = part.shape[1:]
    return part.reshape(4 * rs, c)


def _slots_from_full(name, full):
    if name in COL_SHARDED:
        r, c = full.shape
        return jnp.transpose(full.reshape(r, 4, c // 4), (1, 0, 2))
    r, c = full.shape
    return full.reshape(4, r // 4, c)


def _pack(parts):
    flat = jnp.concatenate([p.reshape(-1).astype(F32) for p in parts])
    pad = (-flat.shape[0]) % 1024
    return jnp.pad(flat, (0, pad)).reshape(-1, 128)


def _unpack(vec, shapes):
    flat = vec.reshape(-1)
    out, pos = [], 0
    for shp in shapes:
        n = 1
        for d in shp:
            n *= d
        out.append(flat[pos:pos + n].reshape(shp))
        pos += n
    return out


def kernel(x, mix_norm_g, w_in, conv_dw_w, conv_dw_b, conv_ln_g, conv_ln_b, conv_pw_w, conv_pw_b, gdn_conv_w, gdn_a_log, gdn_dt_bias, gdn_norm_g, w_out, ffn_norm_g, w_up, ffn_conv_w, ffn_conv_b, w_down, final_norm_g, loss_target, m_mix_norm_g, m_w_in, m_conv_dw_w, m_conv_dw_b, m_conv_ln_g, m_conv_ln_b, m_conv_pw_w, m_conv_pw_b, m_gdn_conv_w, m_gdn_a_log, m_gdn_dt_bias, m_gdn_norm_g, m_w_out, m_ffn_norm_g, m_w_up, m_ffn_conv_w, m_ffn_conv_b, m_w_down, m_final_norm_g, v_mix_norm_g, v_w_in, v_conv_dw_w, v_conv_dw_b, v_conv_ln_g, v_conv_ln_b, v_conv_pw_w, v_conv_pw_b, v_gdn_conv_w, v_gdn_a_log, v_gdn_dt_bias, v_gdn_norm_g, v_w_out, v_ffn_norm_g, v_w_up, v_ffn_conv_w, v_ffn_conv_b, v_w_down, v_final_norm_g):
    wts = dict(zip(WEIGHTS, (mix_norm_g, w_in, conv_dw_w, conv_dw_b, conv_ln_g, conv_ln_b, conv_pw_w, conv_pw_b,
                             gdn_conv_w, gdn_a_log, gdn_dt_bias, gdn_norm_g, w_out, ffn_norm_g, w_up, ffn_conv_w,
                             ffn_conv_b, w_down, final_norm_g)))
    mom = dict(zip(WEIGHTS, (m_mix_norm_g, m_w_in, m_conv_dw_w, m_conv_dw_b, m_conv_ln_g, m_conv_ln_b, m_conv_pw_w,
                             m_conv_pw_b, m_gdn_conv_w, m_gdn_a_log, m_gdn_dt_bias, m_gdn_norm_g, m_w_out,
                             m_ffn_norm_g, m_w_up, m_ffn_conv_w, m_ffn_conv_b, m_w_down, m_final_norm_g)))
    var = dict(zip(WEIGHTS, (v_mix_norm_g, v_w_in, v_conv_dw_w, v_conv_dw_b, v_conv_ln_g, v_conv_ln_b, v_conv_pw_w,
                             v_conv_pw_b, v_gdn_conv_w, v_gdn_a_log, v_gdn_dt_bias, v_gdn_norm_g, v_w_out,
                             v_ffn_norm_g, v_w_up, v_ffn_conv_w, v_ffn_conv_b, v_w_down, v_final_norm_g)))

    nb, s, d = x.shape
    t = nb * s
    depth = mix_norm_g.shape[0]
    ch = conv_dw_b.shape[1]
    nh = gdn_a_log.shape[1]
    nqk = nh // 2
    kwid, vwid = nqk * HEAD, nh * HEAD
    main = 2 * ch + 2 * kwid + 2 * vwid
    qcol, zcol = 2 * ch, 2 * ch + 2 * kwid + vwid
    dff = ffn_conv_b.shape[1]
    my_xy = 2 * lax.axis_index("x") + lax.axis_index("y")

    shards = {(n, l): wts[n][l].astype(BF16) for n in BIG for l in range(depth)}
    buffers, covered, full = {}, {}, {}

    def window(n, i):
        rows, parts = wts[n].shape[1], GATHER_PARTS[n]
        if i is None:
            return 0, rows
        return rows * sum(parts[:i]) // sum(parts), rows * parts[i] // sum(parts)

    def gather(*keys):
        return SplitGather([shards[n, l] for n, l, _ in keys], windows=[window(n, i) for n, _, i in keys],
                           into=[buffers.get((n, l)) for n, l, _ in keys])

    def arrived(keys, got):
        for (n, l, i), g in zip(keys, got):
            buffers[n, l] = g
            covered[n, l] = covered.get((n, l), 0) + window(n, i)[1]
            if covered[n, l] == wts[n].shape[1]:
                full[n, l] = g if n == "w_up" else _full_from_slots(n, g)

    def carrying_gather(keys, call):
        res = call(carry=gather(*keys))
        arrived(keys, res[-1])
        return res[0] if len(res) == 2 else res[:-1]

    first, small = run_exchanges([gather(("w_in", 0, None)), Exchange([wts[n] for n in SMALL_CONV], gather=True)],
                                 name="gather_first")
    arrived([("w_in", 0, None)], first)
    for n, g in zip(SMALL_CONV, small):
        for l in range(depth):
            full[n, l] = _full_from_slots(n, g[:, l])

    xc = x.reshape(t, d)
    saved, lws = [], []
    for l in range(depth):
        w_in_f = full["w_in", l]
        w_main = w_in_f[:, :main]
        w_ba = jnp.pad(w_in_f[:, main:], ((0, 0), (0, HEAD - 2 * nh)))
        gp = (jnp.zeros((8, HEAD), F32).at[0, :nh].set(gdn_a_log[l]).at[1, :nh].set(gdn_dt_bias[l])
              .at[2].set(gdn_norm_g[l]))
        h = rmsnorm_fwd(xc, mix_norm_g[l], name=f"f{l}_norm1")
        p = carrying_gather([("w_up", l, 0)], functools.partial(matmul, h, w_main, name=f"f{l}_in_main"))
        pba = matmul(h, w_ba, name=f"f{l}_in_ba")
        u3, u1 = carrying_gather([("conv_pw_w", l, None), ("w_out", l, None)],
                                 functools.partial(conf_fwd, p, full["conv_dw_w", l], conv_dw_b[l], conv_ln_g[l],
                                                   conv_ln_b[l], nb=nb, s=s, name=f"f{l}_conf"))
        out_a = matmul(u3, full["conv_pw_w", l], bias=conv_pw_b[l], out_dtype=BF16, name=f"f{l}_pw")
        qkvc = carrying_gather([("w_up", l, 1)], functools.partial(qkvconv_fwd, p, full["gdn_conv_w", l], col0=qcol,
                                                                    nb=nb, s=s, name=f"f{l}_qkvconv"))
        out_b, states = carrying_gather([("w_up", l, 2)],
                                        functools.partial(gdn_fwd, qkvc, p, pba, gp, nb=nb, s=s, nh=nh, nqk=nqk,
                                                          zcol=zcol, name=f"f{l}_gdn"))
        wout_a, wout_b = full["w_out", l][:ch], full["w_out", l][ch:]
        x1 = carrying_gather([("w_up", l, 3)], functools.partial(matmul, out_a, wout_a, res=xc, name=f"f{l}_out_a"))
        x1 = carrying_gather([("w_up", l, 4)], functools.partial(matmul, out_b, wout_b, res=x1, name=f"f{l}_out_b"))
        h2 = rmsnorm_fwd(x1, ffn_norm_g[l], name=f"f{l}_norm2")
        up = carrying_gather([("w_down", l, 0)],
                             functools.partial(matmul, h2, full["w_up", l], out_dtype=BF16, name=f"f{l}_up"))
        nxt = l + 1 < depth
        act = carrying_gather([("w_down", l, 1)] + ([("w_in", l + 1, 0)] if nxt else []),
                              functools.partial(ffn_act_fwd, up, full["ffn_conv_w", l], ffn_conv_b[l], nb=nb, s=s,
                                                name=f"f{l}_act"))
        down_call = functools.partial(matmul, act, full["w_down", l], res=x1, name=f"f{l}_down")
        x2 = carrying_gather([("w_in", l + 1, 1)], down_call) if nxt else down_call()
        saved.append(dict(x=xc, h=h, p=p, pba=pba, u3=u3, u1=u1, out_a=out_a, qkvc=qkvc, out_b=out_b, states=states,
                          x1=x1, h2=h2, up=up, act=act))
        lws.append(dict(w_main=w_main, w_ba=w_ba, pw=full["conv_pw_w", l], wout_a=wout_a, wout_b=wout_b,
                        wup=full["w_up", l], wdown=full["w_down", l], dw_w=full["conv_dw_w", l],
                        gconv_w=full["gdn_conv_w", l], fconv_w=full["ffn_conv_w", l], gp=gp))
        xc = x2

    loss_blk, dx, dxb, dgf = loss_head(xc, final_norm_g, loss_target.reshape(t, d), name="loss_head")

    stacks, received = {}, {}
    scatter = lambda *keys: Exchange([stacks[k] for k in keys], gather=False)

    def produced(n, l, grad, halves=False):
        slots = _slots_from_full(n, grad).astype(BF16)
        if halves:
            half = slots.shape[1] // 2
            stacks[n, l, 0], stacks[n, l, 1] = slots[:, :half], slots[:, half:]
        else:
            stacks[n, l] = slots

    def landed(keys, got):
        for k, g in zip(keys, got):
            received[k] = g

    def carrying(keys, call, **kw):
        res = call(carry=scatter(*keys), **kw)
        landed(keys, res[-1])
        return res[0] if len(res) == 2 else res[:-1]

    small_grads = {n: [None] * depth for n in WEIGHTS if n not in BIG and n != "final_norm_g"}
    for l in reversed(range(depth)):
        lw, sv = lws[l], saved[l]
        dact_call = functools.partial(matmul, dxb, lw["wdown"], tb=True, out_dtype=BF16, name=f"b{l}_dact")
        dact = carrying([("w_in", l + 1, 1)], dact_call) if l + 1 < depth else dact_call()
        produced("w_down", l, matmul(sv["act"], dxb, ta=True, out_dtype=BF16, name=f"b{l}_dwdown"), halves=True)
        dgate, dupv, fpart = ffn_act_bwd(sv["up"], lw["fconv_w"], ffn_conv_b[l], dact, nb=nb, s=s, name=f"b{l}_act")
        dh2 = carrying([("w_down", l, 0)], functools.partial(matmul, dgate, lw["wup"], tb=True, name=f"b{l}_dh2_gate"))
        dh2 = carrying([("w_down", l, 1)], functools.partial(matmul, dupv, lw["wup"], tb=True, b_koff=dff, res=dh2,
                                                             name=f"b{l}_dh2_up"))
        stacks["w_up", l] = jnp.concatenate(
            [matmul(sv["h2"], dgate, ta=True, out_dtype=BF16, out_slots=dff // w_up.shape[2], name=f"b{l}_dwup_gate"),
             matmul(sv["h2"], dupv, ta=True, out_dtype=BF16, out_slots=dff // w_up.shape[2], name=f"b{l}_dwup_up")], axis=0)
        dx1, dx1b, dg2 = rmsnorm_bwd(sv["x1"], ffn_norm_g[l], dh2, dx, name=f"b{l}_norm2")
        fsum = jnp.sum(fpart, axis=0)
        small_grads["ffn_norm_g"][l] = dg2[0]
        small_grads["ffn_conv_w"][l] = fsum[:ffn_conv_w.shape[1]]
        small_grads["ffn_conv_b"][l] = fsum[ffn_conv_w.shape[1]]
        dout_a = matmul(dx1b, lw["wout_a"], tb=True, out_dtype=BF16, name=f"b{l}_dout_a")
        dout_b = matmul(dx1b, lw["wout_b"], tb=True, name=f"b{l}_dout_b")
        produced("w_out", l, jnp.concatenate(
            [matmul(sv["out_a"], dx1b, ta=True, out_dtype=BF16, name=f"b{l}_dwout_a"),
             matmul(sv["out_b"], dx1b, ta=True, out_dtype=BF16, name=f"b{l}_dwout_b")], axis=0))
        du3 = matmul(dout_a, lw["pw"], tb=True, name=f"b{l}_du3")
        produced("conv_pw_w", l, matmul(sv["u3"], dout_a, ta=True, out_dtype=BF16, name=f"b{l}_dwpw"))
        dval, dagate, cw_part, cs_part = carrying(
            [("w_out", l), ("conv_pw_w", l)],
            functools.partial(conf_bwd, sv["p"], sv["u1"], lw["dw_w"], conv_dw_b[l], conv_ln_g[l], conv_ln_b[l], du3,
                              dout_a,
                              nb=nb, s=s, name=f"b{l}_conf"))
        csum = jnp.sum(cs_part, axis=0)
        small_grads["conv_dw_w"][l] = jnp.sum(cw_part, axis=0)
        small_grads["conv_dw_b"][l] = csum[0]
        small_grads["conv_ln_g"][l] = csum[1]
        small_grads["conv_ln_b"][l] = csum[2]
        small_grads["conv_pw_b"][l] = csum[3]
        dq, dk, dv, dz, dpba, dgp = carrying(
            [("w_up", l)],
            functools.partial(gdn_bwd, sv["qkvc"], sv["p"], sv["pba"], lw["gp"], sv["states"], dout_b,
                              nb=nb, s=s, nh=nh, nqk=nqk, zcol=zcol, name=f"b{l}_gdn"))
        dqkv, gw_part = qkvconv_bwd(sv["p"], lw["gconv_w"], jnp.concatenate([dq, dk, dv], axis=1),
                                    col0=qcol, nb=nb, s=s, name=f"b{l}_qkvconv")
        gsum = jnp.sum(dgp, axis=0)
        small_grads["gdn_conv_w"][l] = jnp.sum(gw_part, axis=0)[:gdn_conv_w.shape[1]]
        small_grads["gdn_a_log"][l] = gsum[0, :nh]
        small_grads["gdn_dt_bias"][l] = gsum[1, :nh]
        small_grads["gdn_norm_g"][l] = gsum[2]
        dp = jnp.concatenate([dval, dagate, dqkv, dz], axis=1)
        dw_main = matmul(sv["h"], dp, ta=True, out_dtype=BF16, name=f"b{l}_dwin_main")
        dw_ba = matmul(sv["h"], dpba, ta=True, out_dtype=BF16, name=f"b{l}_dwin_ba")
        produced("w_in", l, jnp.concatenate([dw_main, dw_ba[:, :2 * nh]], axis=1), halves=True)
        dh = carrying([("w_in", l, 0)], functools.partial(matmul, dp, lw["w_main"], tb=True, name=f"b{l}_dh_main"))
        dh = matmul(dpba, lw["w_ba"], tb=True, res=dh, name=f"b{l}_dh_ba")
        dx, dxb, dg1 = rmsnorm_bwd(sv["x"], mix_norm_g[l], dh, dx1, name=f"b{l}_norm1")
        small_grads["mix_norm_g"][l] = dg1[0]

    def summed(n):
        parts = [received[k] for l in range(depth) for k in ([(n, l)] if (n, l) in received else [(n, l, 0), (n, l, 1)])]
        return sum_slots(parts, name=f"sum_{n}").reshape(wts[n].shape)

    early = [n for n in BIG if n != "w_in"]
    partial = {n: summed(n) for n in early}
    swapped, last = run_exchanges([SiblingSwap([partial[n] for n in early]),
                                   Exchange([stacks["w_in", 0, 1]], gather=False)], name="swap_and_scatter_last")
    other = dict(zip(early, swapped))
    received["w_in", 0, 1] = last[0]
    partial["w_in"] = summed("w_in")
    other["w_in"] = run_exchanges([SiblingSwap([partial["w_in"]])], name="swap_w_in")[0][0]

    grads, deltas, new_m, new_v = {}, {}, {}, {}
    for n in BIG:
        grads[n], deltas[n], new_m[n], new_v[n] = adamw([partial[n], other[n]], wts[n], mom[n], var[n],
                                                        name=f"adamw_{n}")

    small_names = [n for n in WEIGHTS if n not in BIG]
    small_full = [jnp.stack(small_grads[n]) if n != "final_norm_g" else dgf[0] for n in small_names]
    packed = _pack(small_full + [loss_blk[0, :1]])
    reduced = allreduce_small(packed, name="allreduce_small")
    parts = _unpack(reduced, [a.shape for a in small_full] + [(1,)])
    loss = parts[-1][0]
    for n, g in zip(small_names, parts[:-1]):
        if n in SMALL_CONV:
            wid = wts[n].shape[-1]
            g = lax.dynamic_slice_in_dim(g, my_xy * wid, wid, axis=g.ndim - 1)
        grads[n], deltas[n], new_m[n], new_v[n] = adamw([g], wts[n], mom[n], var[n], name=f"adamw_{n}")

    return (loss, dx.reshape(nb, s, d), *[grads[n] for n in WEIGHTS], *[deltas[n] for n in WEIGHTS],
            *[new_m[n] for n in WEIGHTS], *[new_v[n] for n in WEIGHTS])
```

```python
import functools

import jax
import jax.numpy as jnp
from jax import lax
from jax.experimental import pallas as pl
from jax.experimental.pallas import tpu as pltpu

F32 = jnp.float32
BF16 = jnp.bfloat16
EPS = 1e-6
CHUNK = 64
HEAD = 128
HIGHEST = lax.Precision.HIGHEST
VMEM_LIMIT = 56 * 1024 * 1024
MM_VMEM_BUDGET = 48 * 1024 * 1024


def _pick(dim, cands):
    for c in cands:
        if dim % c == 0:
            return c
    return dim


def _cparams(sem):
    return pltpu.CompilerParams(dimension_semantics=sem, vmem_limit_bytes=VMEM_LIMIT)


MESH = pl.DeviceIdType.MESH
ANY = pl.BlockSpec(memory_space=pl.ANY)


def _xy_peers():
    x, y = lax.axis_index("x"), lax.axis_index("y")
    peers = []
    for fx, fy in ((0, 1), (1, 0), (1, 1)):
        px = 1 - x if fx else x
        py = 1 - y if fy else y
        peers.append((2 * px + py, px, py))
    return 2 * x + y, peers


class Exchange:
    def __init__(self, arrs, gather):
        self.arrs, self.gather, self.n = list(arrs), gather, len(arrs)
        self.out_shape = [jax.ShapeDtypeStruct((4,) + (a.shape if gather else a.shape[1:]), a.dtype) for a in arrs]
        self.scratch = [pltpu.SemaphoreType.DMA((3 * self.n,)), pltpu.SemaphoreType.DMA((3 * self.n,)),
                        pltpu.SemaphoreType.DMA((self.n,))]

    def copies(self, ins, outs, send_sems, recv_sems, local_sems):
        me, peers = _xy_peers()
        c = lax.axis_index("c")
        out = []
        for k in range(self.n):
            out.append(pltpu.make_async_copy(ins[k] if self.gather else ins[k].at[me], outs[k].at[me],
                                             local_sems.at[k]))
            for j, (slot, px, py) in enumerate(peers):
                out.append(pltpu.make_async_remote_copy(
                    src_ref=ins[k] if self.gather else ins[k].at[slot], dst_ref=outs[k].at[me],
                    send_sem=send_sems.at[3 * k + j], recv_sem=recv_sems.at[3 * k + j],
                    device_id=(px, py, c), device_id_type=MESH))
        return out

    def start(self, ins, outs, sems):
        for cp in self.copies(ins, outs, *sems):
            cp.start()

    def finish(self, ins, outs, sems):
        for cp in self.copies(ins, outs, *sems):
            cp.wait()


class SplitGather:
    def __init__(self, arrs, windows=None, into=None):
        self.arrs, self.n = list(arrs), len(arrs)
        self.windows = list(windows) if windows else [(0, a.shape[0]) for a in arrs]
        self.into = list(into) if into else [None] * self.n
        assert all(a.ndim == 2 and nr % 32 == 0 and r0 % 16 == 0 for a, (r0, nr) in zip(arrs, self.windows))
        self.out_shape = [jax.ShapeDtypeStruct((4,) + a.shape, a.dtype) for a in arrs]
        dma = pltpu.SemaphoreType.DMA
        self.scratch = [dma((3 * self.n,)), dma((3 * self.n,)), dma((3 * self.n,)), dma((3 * self.n,)), dma((self.n,))]

    def _window(self, k):
        return pl.ds(*self.windows[k])

    def _half(self, k, c):
        r0, nr = self.windows[k]
        return pl.ds(pl.multiple_of(r0 + c * (nr // 2), 16), nr // 2)

    def over_ici(self, ins, outs, sems):
        me, peers = _xy_peers()
        c = lax.axis_index("c")
        out = []
        for k in range(self.n):
            out.append(pltpu.make_async_copy(ins[k].at[self._window(k)], outs[k].at[me, self._window(k)],
                                             sems[4].at[k]))
            for j, (slot, px, py) in enumerate(peers):
                out.append(pltpu.make_async_remote_copy(
                    src_ref=ins[k].at[self._half(k, c)], dst_ref=outs[k].at[me, self._half(k, c)],
                    send_sem=sems[0].at[3 * k + j], recv_sem=sems[1].at[3 * k + j],
                    device_id=(px, py, c), device_id_type=MESH))
        return out

    def over_d2d(self, outs, sems):
        _, peers = _xy_peers()
        x, y, c = lax.axis_index("x"), lax.axis_index("y"), lax.axis_index("c")
        out = []
        for k in range(self.n):
            for j, (slot, _, _) in enumerate(peers):
                rows = outs[k].at[slot, self._half(k, c)]
                out.append(pltpu.make_async_remote_copy(
                    src_ref=rows, dst_ref=rows, send_sem=sems[2].at[3 * k + j], recv_sem=sems[3].at[3 * k + j],
                    device_id=(x, y, 1 - c), device_id_type=MESH))
        return out

    def start(self, ins, outs, sems):
        for cp in self.over_ici(ins, outs, sems):
            cp.start()

    def finish(self, ins, outs, sems):
        for cp in self.over_ici(ins, outs, sems):
            cp.wait()
        passed = self.over_d2d(outs, sems)
        for cp in passed:
            cp.start()
        for cp in passed:
            cp.wait()


class SiblingSwap:
    def __init__(self, arrs):
        self.arrs, self.n = list(arrs), len(arrs)
        self.out_shape = [jax.ShapeDtypeStruct(a.shape, a.dtype) for a in arrs]
        self.scratch = [pltpu.SemaphoreType.DMA((self.n,)), pltpu.SemaphoreType.DMA((self.n,))]

    def copies(self, ins, outs, sems):
        peer = (lax.axis_index("x"), lax.axis_index("y"), 1 - lax.axis_index("c"))
        return [pltpu.make_async_remote_copy(src_ref=ins[k], dst_ref=outs[k], send_sem=sems[0].at[k],
                                             recv_sem=sems[1].at[k], device_id=peer, device_id_type=MESH)
                for k in range(self.n)]

    def start(self, ins, outs, sems):
        for cp in self.copies(ins, outs, sems):
            cp.start()

    def finish(self, ins, outs, sems):
        for cp in self.copies(ins, outs, sems):
            cp.wait()


def _split_refs(refs, carries, attr):
    groups, pos = [], 0
    for cr in carries:
        n = len(getattr(cr, attr))
        groups.append(refs[pos:pos + n])
        pos += n
    return groups


def carried_call(body, *, name, grid, in_specs, out_specs, out_shape, scratch_shapes, semantics, args, carry=None):
    n_in, n_out, n_scr = len(in_specs), len(out_specs), len(scratch_shapes)
    if carry is None:
        outs = pl.pallas_call(body, name=name, grid=grid, in_specs=in_specs, out_specs=out_specs, out_shape=out_shape,
                              scratch_shapes=scratch_shapes, compiler_params=_cparams(semantics))(*args)
        return list(outs), []
    carries = list(carry) if isinstance(carry, (list, tuple)) else [carry]
    n = sum(cr.n for cr in carries)
    flat_into = [b for cr in carries for b in getattr(cr, "into", [None] * cr.n)]
    extra = [(i, b) for i, b in enumerate(flat_into) if b is not None]
    aliases = {n_in + n + e: n_out + i for e, (i, _) in enumerate(extra)}
    n_x = len(extra)

    def wrapped(*refs):
        ins, cin = refs[:n_in], _split_refs(refs[n_in:n_in + n], carries, "arrs")
        refs = refs[:n_in + n] + refs[n_in + n + n_x:]
        outs = refs[n_in + n:n_in + n + n_out]
        cout = _split_refs(refs[n_in + n + n_out:n_in + 2 * n + n_out], carries, "arrs")
        scratch = refs[n_in + 2 * n + n_out:n_in + 2 * n + n_out + n_scr]
        sems = _split_refs(refs[n_in + 2 * n + n_out + n_scr:], carries, "scratch")
        ids = [pl.program_id(i) for i in range(len(grid))]
        first = functools.reduce(jnp.logical_and, [i == 0 for i in ids])
        last = functools.reduce(jnp.logical_and, [i == g - 1 for i, g in zip(ids, grid)])

        @pl.when(first)
        def _():
            for cr, i, o, s in zip(carries, cin, cout, sems):
                cr.start(i, o, s)

        body(*ins, *outs, *scratch)

        @pl.when(last)
        def _():
            for cr, i, o, s in zip(carries, cin, cout, sems):
                cr.finish(i, o, s)

    res = pl.pallas_call(
        wrapped, name=name, grid=grid, in_specs=list(in_specs) + [ANY] * (n + n_x),
        out_specs=list(out_specs) + [ANY] * n,
        out_shape=list(out_shape) + [s for cr in carries for s in cr.out_shape],
        scratch_shapes=list(scratch_shapes) + [s for cr in carries for s in cr.scratch],
        input_output_aliases=aliases,
        compiler_params=_cparams(tuple("arbitrary" for _ in grid)),
    )(*args, *[a for cr in carries for a in cr.arrs], *[b for _, b in extra])
    got = _split_refs(list(res[n_out:]), carries, "arrs")
    return list(res[:n_out]), (got if isinstance(carry, (list, tuple)) else got[0])


def run_exchanges(carries, *, name):
    n = sum(cr.n for cr in carries)

    def body(*refs):
        cin = _split_refs(refs[:n], carries, "arrs")
        cout = _split_refs(refs[n:2 * n], carries, "arrs")
        sems = _split_refs(refs[2 * n:], carries, "scratch")
        for cr, i, o, s in zip(carries, cin, cout, sems):
            cr.start(i, o, s)
        for cr, i, o, s in zip(carries, cin, cout, sems):
            cr.finish(i, o, s)

    res = pl.pallas_call(body, name=name, in_specs=[ANY] * n, out_specs=[ANY] * n,
                         out_shape=[s for cr in carries for s in cr.out_shape],
                         scratch_shapes=[s for cr in carries for s in cr.scratch],
                         )(*[a for cr in carries for a in cr.arrs])
    return _split_refs(list(res), carries, "arrs")


def matmul(a, b, *, ta=False, tb=False, out_dtype=F32, res=None, bias=None, b_koff=0, out_slots=0, name, carry=None):
    if ta:
        kdim, m = a.shape
    else:
        m, kdim = a.shape
    slot_w = 0
    if b.ndim == 3 and tb:
        nslot, n, slot_w = b.shape
        kb = nslot * slot_w
    elif b.ndim == 3:
        nslot, kb, slot_w = b.shape
        n = nslot * slot_w
    elif tb:
        n, kb = b.shape
    else:
        kb, n = b.shape
    assert kb >= kdim + b_koff, (a.shape, b.shape, ta, tb)
    out_w = n // out_slots if out_slots else n
    if (slot_w and not tb) or out_slots:
        tn = _pick(min(slot_w, out_w) if (slot_w and not tb) else out_w, (1408, 1024, 512, 256, 128))
        assert out_w % tn == 0 and (tb or not slot_w or slot_w % tn == 0)
    else:
        tn = _pick(n, (1024, 512, 256, 128))
    out_bytes = jnp.dtype(out_dtype).itemsize

    def vmem_bytes(tm_, tk_):
        blocks = a.dtype.itemsize * tm_ * tk_ + b.dtype.itemsize * tk_ * tn + out_bytes * tm_ * tn
        blocks += 4 * tm_ * tn if res is not None else 0
        temps = 4 * tm_ * tn + (2 * tk_ * tn if tb else 0)
        return 2 * blocks + temps + (4 * tm_ * tn if tk_ < kdim else 0)

    def longest_k(tm_):
        if slot_w and tb:
            return slot_w
        return next(c for c in (kdim, 4096, 2816, 2560, 2048, 1024, 512, 256, 128)
                    if kdim % c == 0 and b_koff % c == 0 and c % 128 == 0
                    and (vmem_bytes(tm_, c) <= MM_VMEM_BUDGET or c == 128))

    tall = [c for c in (1024, 512) if m % c == 0] or [_pick(m, (256, 128))]
    tm = max(tall, key=lambda c: (longest_k(c), c))
    tk = longest_k(tm)
    nk = kdim // tk
    ko = b_koff // tk
    dims = (((0 if ta else 1,), (1 if tb else 0,)), ((), ()))

    def body(*refs):
        a_ref, b_ref = refs[0], refs[1]
        pos = 2
        bias_ref = res_ref = None
        if bias is not None:
            bias_ref = refs[pos]
            pos += 1
        if res is not None:
            res_ref = refs[pos]
            pos += 1
        o_ref = refs[pos]
        part = lax.dot_general(a_ref[...].astype(BF16), b_ref[...].astype(BF16), dims, preferred_element_type=F32)

        def finish(r):
            if bias_ref is not None:
                r = r + bias_ref[...]
            if res_ref is not None:
                r = r + res_ref[...]
            o_ref[...] = r.astype(o_ref.dtype)

        if nk == 1:
            finish(part)
            return
        acc_ref = refs[pos + 1]
        k = pl.program_id(2)

        @pl.when(k == 0)
        def _():
            acc_ref[...] = part

        @pl.when((k > 0) & (k < nk - 1))
        def _():
            acc_ref[...] += part

        @pl.when(k == nk - 1)
        def _():
            finish(acc_ref[...] + part)

    assert kdim % tk == 0 and b_koff % tk == 0
    a_spec = pl.BlockSpec((tk, tm), lambda i, j, k: (k, i)) if ta else pl.BlockSpec((tm, tk), lambda i, j, k: (i, k))
    if slot_w and tb:
        b_spec = pl.BlockSpec((None, tn, tk), lambda i, j, k: (k + ko, j, 0))
    elif slot_w:
        per = slot_w // tn
        b_spec = pl.BlockSpec((None, tk, tn), lambda i, j, k: (j // per, k + ko, j % per))
    elif tb:
        b_spec = pl.BlockSpec((tn, tk), lambda i, j, k: (j, k + ko))
    else:
        b_spec = pl.BlockSpec((tk, tn), lambda i, j, k: (k + ko, j))
    if out_slots:
        oper = out_w // tn
        out_spec = pl.BlockSpec((None, tm, tn), lambda i, j, k: (j // oper, i, j % oper))
        out_struct = jax.ShapeDtypeStruct((out_slots, m, out_w), out_dtype)
    else:
        out_spec = pl.BlockSpec((tm, tn), lambda i, j, k: (i, j))
        out_struct = jax.ShapeDtypeStruct((m, n), out_dtype)
    in_specs = [a_spec, b_spec]
    args = [a, b]
    if bias is not None:
        in_specs.append(pl.BlockSpec((1, tn), lambda i, j, k: (0, j)))
        args.append(bias.reshape(1, n).astype(F32))
    if res is not None:
        in_specs.append(pl.BlockSpec((tm, tn), lambda i, j, k: (i, j)))
        args.append(res)
    outs, got = carried_call(
        body, name=name, grid=(m // tm, n // tn, nk), in_specs=in_specs,
        out_specs=[out_spec], out_shape=[out_struct],
        scratch_shapes=[pltpu.VMEM((tm, tn), F32)] if nk > 1 else [],
        semantics=("parallel", "parallel", "arbitrary"), args=args, carry=carry)
    return outs[0] if carry is None else (outs[0], got)


def _sigmoid(x):
    return 0.5 * jnp.tanh(0.5 * x) + 0.5


def _softplus(x):
    return jnp.maximum(x, 0.0) + jnp.log(1.0 + jnp.exp(-jnp.abs(x)))


def _shift_back(x, s, row):
    if s == 0:
        return x
    return jnp.where(row >= s, pltpu.roll(x, s, 0), 0.0)


def _shift_fwd(x, s, row):
    if s == 0:
        return x
    n = x.shape[0]
    return jnp.where(row < n - s, pltpu.roll(x, n - s, 0), 0.0)


def _conv_fwd(x, w_ref, kw, row):
    acc = x * w_ref[pl.ds(kw - 1, 1), :]
    for s in range(1, kw):
        acc = acc + _shift_back(x, s, row) * w_ref[pl.ds(kw - 1 - s, 1), :]
    return acc


def _conv_bwd(x, dy, w_ref, kw, row):
    dx = dy * w_ref[pl.ds(kw - 1, 1), :]
    dw = [None] * kw
    dw[kw - 1] = jnp.sum(dy * x, axis=0, keepdims=True)
    for s in range(1, kw):
        dx = dx + _shift_fwd(dy, s, row) * w_ref[pl.ds(kw - 1 - s, 1), :]
        dw[kw - 1 - s] = jnp.sum(dy * _shift_back(x, s, row), axis=0, keepdims=True)
    return dx, dw


def _rows_to_block(rows, nrows, width):
    rid = lax.broadcasted_iota(jnp.int32, (nrows, width), 0)
    out = jnp.zeros((nrows, width), F32)
    for i, r in enumerate(rows):
        out = jnp.where(rid == i, r, out)
    return out


def rmsnorm_fwd(x, g, *, name):
    t, d = x.shape
    tm = _pick(t, (256, 128))

    def body(x_ref, g_ref, o_ref):
        xv = x_ref[...]
        r = lax.rsqrt(jnp.mean(xv * xv, axis=-1, keepdims=True) + EPS)
        o_ref[...] = (xv * r * g_ref[...]).astype(o_ref.dtype)

    return pl.pallas_call(
        body, name=name, grid=(t // tm,),
        in_specs=[pl.BlockSpec((tm, d), lambda i: (i, 0)), pl.BlockSpec((1, d), lambda i: (0, 0))],
        out_specs=pl.BlockSpec((tm, d), lambda i: (i, 0)),
        out_shape=jax.ShapeDtypeStruct((t, d), BF16),
        compiler_params=_cparams(("parallel",)),
    )(x, g.reshape(1, d))


def rmsnorm_bwd(x, g, dh, dres, *, name):
    t, d = x.shape
    tm = _pick(t, (256, 128))

    def body(x_ref, g_ref, dh_ref, dres_ref, dx_ref, dxb_ref, dg_ref):
        xv = x_ref[...]
        r = lax.rsqrt(jnp.mean(xv * xv, axis=-1, keepdims=True) + EPS)
        xh = xv * r
        dy = dh_ref[...]
        dxh = dy * g_ref[...]
        dx = dres_ref[...] + r * (dxh - xh * jnp.mean(dxh * xh, axis=-1, keepdims=True))
        dx_ref[...] = dx
        dxb_ref[...] = dx.astype(BF16)

        @pl.when(pl.program_id(0) == 0)
        def _():
            dg_ref[...] = jnp.zeros_like(dg_ref)

        dg_ref[...] += jnp.sum(dy * xh, axis=0, keepdims=True)

    row = pl.BlockSpec((tm, d), lambda i: (i, 0))
    vec = pl.BlockSpec((1, d), lambda i: (0, 0))
    return pl.pallas_call(
        body, name=name, grid=(t // tm,),
        in_specs=[row, vec, row, row],
        out_specs=[row, row, vec],
        out_shape=[jax.ShapeDtypeStruct((t, d), F32), jax.ShapeDtypeStruct((t, d), BF16),
                   jax.ShapeDtypeStruct((1, d), F32)],
        compiler_params=_cparams(("arbitrary",)),
    )(x, g.reshape(1, d), dh, dres)


def loss_head(x, g, target, *, name):
    t, d = x.shape
    tm = _pick(t, (256, 128))

    def body(x_ref, g_ref, tg_ref, loss_ref, dx_ref, dxb_ref, dg_ref):
        xv = x_ref[...]
        r = lax.rsqrt(jnp.mean(xv * xv, axis=-1, keepdims=True) + EPS)
        xh = xv * r
        err = xh * g_ref[...] - tg_ref[...]
        dy = err * (1.0 / d)
        dxh = dy * g_ref[...]
        dx = r * (dxh - xh * jnp.mean(dxh * xh, axis=-1, keepdims=True))
        dx_ref[...] = dx
        dxb_ref[...] = dx.astype(BF16)

        @pl.when(pl.program_id(0) == 0)
        def _():
            dg_ref[...] = jnp.zeros_like(dg_ref)
            loss_ref[...] = jnp.zeros_like(loss_ref)

        dg_ref[...] += jnp.sum(dy * xh, axis=0, keepdims=True)
        part = jnp.sum(jnp.sum(err * err, axis=-1, keepdims=True), axis=0, keepdims=True) * (0.5 / d)
        loss_ref[...] += jnp.broadcast_to(part, loss_ref.shape)

    row = pl.BlockSpec((tm, d), lambda i: (i, 0))
    vec = pl.BlockSpec((1, d), lambda i: (0, 0))
    return pl.pallas_call(
        body, name=name, grid=(t // tm,),
        in_specs=[row, vec, row],
        out_specs=[pl.BlockSpec((8, 128), lambda i: (0, 0)), row, row, vec],
        out_shape=[jax.ShapeDtypeStruct((8, 128), F32), jax.ShapeDtypeStruct((t, d), F32),
                   jax.ShapeDtypeStruct((t, d), BF16), jax.ShapeDtypeStruct((1, d), F32)],
        compiler_params=_cparams(("arbitrary",)),
    )(x, g.reshape(1, d), target)


def _conf_forward_parts(val, gate, w_ref, b, lg, lb, kw, row, u1=None):
    sg = _sigmoid(gate)
    u0 = val * sg
    if u1 is None:
        u1 = _conv_fwd(u0, w_ref, kw, row) + b
    mu = jnp.mean(u1, axis=-1, keepdims=True)
    xc = u1 - mu
    rs = lax.rsqrt(jnp.mean(xc * xc, axis=-1, keepdims=True) + EPS)
    xh = xc * rs
    u2 = xh * lg + lb
    s2 = _sigmoid(u2)
    return sg, u0, u1, rs, xh, u2, s2


def conf_fwd(p, dw_w, dw_b, ln_g, ln_b, *, nb, s, name, carry=None):
    kw, ch = dw_w.shape
    ng = ch // HEAD

    def body(val_ref, gate_ref, w_ref, b_ref, lg_ref, lb_ref, o_ref, u1_ref):
        row = lax.broadcasted_iota(jnp.int32, (s, HEAD), 0)
        _, _, u1, _, _, u2, s2 = _conf_forward_parts(val_ref[...], gate_ref[...], w_ref, b_ref[...], lg_ref[...],
                                                     lb_ref[...], kw, row)
        o_ref[...] = (u2 * s2).astype(o_ref.dtype)
        u1_ref[...] = u1

    vec = pl.BlockSpec((1, HEAD), lambda b, g: (0, g))
    blk = pl.BlockSpec((s, HEAD), lambda b, g: (b, g))
    outs, got = carried_call(
        body, name=name, grid=(nb, ng),
        in_specs=[blk, pl.BlockSpec((s, HEAD), lambda b, g: (b, ng + g)),
                  pl.BlockSpec((kw, HEAD), lambda b, g: (0, g)), vec, vec, vec],
        out_specs=[blk, blk],
        out_shape=[jax.ShapeDtypeStruct((nb * s, ch), BF16), jax.ShapeDtypeStruct((nb * s, ch), F32)],
        scratch_shapes=[], semantics=("parallel", "parallel"),
        args=(p, p, dw_w, dw_b.reshape(1, ch), ln_g.reshape(1, ch), ln_b.reshape(1, ch)), carry=carry)
    return outs + [got]


def conf_bwd(p, u1, dw_w, dw_b, ln_g, ln_b, du3, dout_a, *, nb, s, name, carry=None):
    kw, ch = dw_w.shape
    ng = ch // HEAD

    def body(val_ref, gate_ref, u1_ref, w_ref, b_ref, lg_ref, lb_ref, du3_ref, doa_ref, dval_ref, dgate_ref, dw_out,
             sm_out):
        row = lax.broadcasted_iota(jnp.int32, (s, HEAD), 0)
        val = val_ref[...]
        sg, u0, _, rs, xh, u2, s2 = _conf_forward_parts(val, gate_ref[...], w_ref, b_ref[...], lg_ref[...],
                                                        lb_ref[...], kw, row, u1=u1_ref[...])
        du2 = du3_ref[...] * (s2 * (1.0 + u2 * (1.0 - s2)))
        dlg = jnp.sum(du2 * xh, axis=0, keepdims=True)
        dlb = jnp.sum(du2, axis=0, keepdims=True)
        dxh = du2 * lg_ref[...]
        du1 = rs * (dxh - jnp.mean(dxh, axis=-1, keepdims=True) - xh * jnp.mean(dxh * xh, axis=-1, keepdims=True))
        ddb = jnp.sum(du1, axis=0, keepdims=True)
        du0, dw = _conv_bwd(u0, du1, w_ref, kw, row)
        dval_ref[...] = (du0 * sg).astype(dval_ref.dtype)
        dgate_ref[...] = (du0 * val * sg * (1.0 - sg)).astype(dgate_ref.dtype)
        for k in range(kw):
            dw_out[0, pl.ds(k, 1), :] = dw[k]
        dpb = jnp.sum(doa_ref[...].astype(F32), axis=0, keepdims=True)
        sm_out[0] = _rows_to_block([ddb, dlg, dlb, dpb], 8, HEAD)

    vec = pl.BlockSpec((1, HEAD), lambda b, g: (0, g))
    blk = pl.BlockSpec((s, HEAD), lambda b, g: (b, g))
    outs, got = carried_call(
        body, name=name, grid=(nb, ng),
        in_specs=[blk, pl.BlockSpec((s, HEAD), lambda b, g: (b, ng + g)), blk,
                  pl.BlockSpec((kw, HEAD), lambda b, g: (0, g)), vec, vec, vec, blk, blk],
        out_specs=[blk, blk, pl.BlockSpec((1, kw, HEAD), lambda b, g: (b, 0, g)),
                   pl.BlockSpec((1, 8, HEAD), lambda b, g: (b, 0, g))],
        out_shape=[jax.ShapeDtypeStruct((nb * s, ch), BF16), jax.ShapeDtypeStruct((nb * s, ch), BF16),
                   jax.ShapeDtypeStruct((nb, kw, ch), F32), jax.ShapeDtypeStruct((nb, 8, ch), F32)],
        scratch_shapes=[], semantics=("parallel", "parallel"),
        args=(p, p, u1, dw_w, dw_b.reshape(1, ch), ln_g.reshape(1, ch), ln_b.reshape(1, ch), du3, dout_a),
        carry=carry)
    return outs + [got]


def qkvconv_fwd(p, w, *, col0, nb, s, name, carry=None):
    kw, ch = w.shape
    nblk = ch // HEAD
    c0 = col0 // HEAD

    def body(x_ref, w_ref, o_ref):
        row = lax.broadcasted_iota(jnp.int32, (s, HEAD), 0)
        c = _conv_fwd(x_ref[...], w_ref, kw, row)
        o_ref[...] = c * _sigmoid(c)

    outs, got = carried_call(
        body, name=name, grid=(nb, nblk),
        in_specs=[pl.BlockSpec((s, HEAD), lambda b, j: (b, c0 + j)), pl.BlockSpec((kw, HEAD), lambda b, j: (0, j))],
        out_specs=[pl.BlockSpec((s, HEAD), lambda b, j: (b, j))],
        out_shape=[jax.ShapeDtypeStruct((nb * s, ch), F32)],
        scratch_shapes=[], semantics=("parallel", "parallel"), args=(p, w), carry=carry)
    return outs + [got]


def qkvconv_bwd(p, w, dy, *, col0, nb, s, name):
    kw, ch = w.shape
    nblk = ch // HEAD
    c0 = col0 // HEAD

    def body(x_ref, w_ref, dy_ref, dx_ref, dw_out):
        row = lax.broadcasted_iota(jnp.int32, (s, HEAD), 0)
        xv = x_ref[...]
        c = _conv_fwd(xv, w_ref, kw, row)
        sc = _sigmoid(c)
        dc = dy_ref[...] * (sc * (1.0 + c * (1.0 - sc)))
        dx, dw = _conv_bwd(xv, dc, w_ref, kw, row)
        dx_ref[...] = dx.astype(dx_ref.dtype)
        dw_out[0] = _rows_to_block(dw, 8, HEAD)

    blk = pl.BlockSpec((s, HEAD), lambda b, j: (b, j))
    return pl.pallas_call(
        body, name=name, grid=(nb, nblk),
        in_specs=[pl.BlockSpec((s, HEAD), lambda b, j: (b, c0 + j)), pl.BlockSpec((kw, HEAD), lambda b, j: (0, j)), blk],
        out_specs=[blk, pl.BlockSpec((1, 8, HEAD), lambda b, j: (b, 0, j))],
        out_shape=[jax.ShapeDtypeStruct((nb * s, ch), BF16), jax.ShapeDtypeStruct((nb, 8, ch), F32)],
        compiler_params=_cparams(("parallel", "parallel")),
    )(p, w, dy)


def ffn_act_fwd(up, w, b, *, nb, s, name, carry=None):
    kw, dff = w.shape
    cb = _pick(dff, (256, 128))
    nblk = dff // cb

    def body(g_ref, u_ref, w_ref, b_ref, o_ref):
        row = lax.broadcasted_iota(jnp.int32, (s, cb), 0)
        gc = _conv_fwd(g_ref[...].astype(F32), w_ref, kw, row) + b_ref[...]
        o_ref[...] = (gc * _sigmoid(gc) * u_ref[...].astype(F32)).astype(o_ref.dtype)

    outs, got = carried_call(
        body, name=name, grid=(nb, nblk),
        in_specs=[pl.BlockSpec((s, cb), lambda i, j: (i, j)), pl.BlockSpec((s, cb), lambda i, j: (i, nblk + j)),
                  pl.BlockSpec((kw, cb), lambda i, j: (0, j)), pl.BlockSpec((1, cb), lambda i, j: (0, j))],
        out_specs=[pl.BlockSpec((s, cb), lambda i, j: (i, j))],
        out_shape=[jax.ShapeDtypeStruct((nb * s, dff), BF16)],
        scratch_shapes=[], semantics=("parallel", "parallel"), args=(up, up, w, b.reshape(1, dff)), carry=carry)
    return outs + [got]


def ffn_act_bwd(up, w, b, dact, *, nb, s, name):
    kw, dff = w.shape
    cb = _pick(dff, (256, 128))
    nblk = dff // cb

    def body(g_ref, u_ref, w_ref, b_ref, da_ref, dg_ref, du_ref, sm_out):
        row = lax.broadcasted_iota(jnp.int32, (s, cb), 0)
        gv = g_ref[...].astype(F32)
        gc = _conv_fwd(gv, w_ref, kw, row) + b_ref[...]
        sc = _sigmoid(gc)
        da = da_ref[...].astype(F32)
        du_ref[...] = (da * gc * sc).astype(du_ref.dtype)
        dgc = da * u_ref[...].astype(F32) * (sc * (1.0 + gc * (1.0 - sc)))
        dgate, dw = _conv_bwd(gv, dgc, w_ref, kw, row)
        dg_ref[...] = dgate.astype(dg_ref.dtype)
        sm_out[0] = _rows_to_block(dw + [jnp.sum(dgc, axis=0, keepdims=True)], 8, cb)

    blk = pl.BlockSpec((s, cb), lambda i, j: (i, j))
    return pl.pallas_call(
        body, name=name, grid=(nb, nblk),
        in_specs=[blk, pl.BlockSpec((s, cb), lambda i, j: (i, nblk + j)),
                  pl.BlockSpec((kw, cb), lambda i, j: (0, j)), pl.BlockSpec((1, cb), lambda i, j: (0, j)), blk],
        out_specs=[blk, blk, pl.BlockSpec((1, 8, cb), lambda i, j: (i, 0, j))],
        out_shape=[jax.ShapeDtypeStruct((nb * s, dff), BF16), jax.ShapeDtypeStruct((nb * s, dff), BF16),
                   jax.ShapeDtypeStruct((nb, 8, dff), F32)],
        compiler_params=_cparams(("parallel", "parallel")),
    )(up, up, w, b.reshape(1, dff), dact)


def _dot(a, b, dims):
    return lax.dot_general(a, b, (dims, ((0,), (0,))), preferred_element_type=F32)


def _mm_nn(a, b):
    return _dot(a.astype(BF16), b.astype(BF16), ((2,), (1,)))


def _mm_nt(a, b):
    return _dot(a.astype(BF16), b.astype(BF16), ((2,), (2,)))


def _mm_tn(a, b):
    return _dot(a.astype(BF16), b.astype(BF16), ((1,), (1,)))


def _split3(x):
    hi = x.astype(BF16)
    rest = x - hi.astype(F32)
    mid = rest.astype(BF16)
    return hi, mid, (rest - mid.astype(F32)).astype(BF16)


def _mask_dot(mask, x, dims, mask_first):
    if mask_first:
        return sum(_dot(mask, p, dims) for p in _split3(x))
    return sum(_dot(p, mask, dims) for p in _split3(x))


@jax.custom_vjp
def _mask_nn(mask, x):
    return _mask_dot(mask, x, ((2,), (1,)), True)


def _mask_nn_fwd(mask, x):
    return _mask_nn(mask, x), mask


def _mask_nn_bwd(mask, ct):
    return jnp.zeros_like(mask), _mask_dot(mask, ct, ((1,), (1,)), True)


_mask_nn.defvjp(_mask_nn_fwd, _mask_nn_bwd)


@jax.custom_vjp
def _mask_tn(x, mask):
    return _mask_dot(mask, x, ((1,), (1,)), False)


def _mask_tn_fwd(x, mask):
    return _mask_tn(x, mask), mask


def _mask_tn_bwd(mask, ct):
    return _mask_dot(mask, ct, ((2,), (2,)), True), jnp.zeros_like(mask)


_mask_tn.defvjp(_mask_tn_fwd, _mask_tn_bwd)


GDN_ROWS = 256


def _gdn_chunk(qc, kc, vc, zc, braw, araw, alog, dtb, ng, state):
    nhead, r = qc.shape[0], qc.shape[1]
    ri = lax.broadcasted_iota(jnp.int32, (r, r), 0)
    ci = lax.broadcasted_iota(jnp.int32, (r, r), 1)
    same = (ri // CHUNK) == (ci // CHUNK)
    causal = same & (ri >= ci)
    strict = same & (ri > ci)
    eye = (ri == ci).astype(F32)
    row_chunk = lax.broadcasted_iota(jnp.int32, (r, HEAD), 0) // CHUNK
    per_head = lambda m: jnp.broadcast_to(m.astype(BF16), (nhead, r, r))

    q = qc * lax.rsqrt(jnp.sum(qc * qc, axis=-1, keepdims=True) + EPS) * (HEAD ** -0.5)
    k = kc * lax.rsqrt(jnp.sum(kc * kc, axis=-1, keepdims=True) + EPS)
    beta = _sigmoid(braw)
    g = -jnp.exp(alog) * _softplus(araw + dtb)

    g_w = jnp.broadcast_to(g, (nhead, r, HEAD))
    widen = lambda t: jnp.concatenate([t] * (r // HEAD), axis=2)
    gw = _mask_nn(per_head(causal), g_w)
    gi = widen(gw)
    gj = _mask_tn(widen(g_w), per_head(same & (ri <= ci)))
    decay = jnp.where(causal, jnp.exp(jnp.where(causal, gi - gj, 0.0)), 0.0)

    kb = k * beta
    vb = vc * beta
    lmat = jnp.where(strict, _mm_nt(kb, k) * decay, 0.0)
    x = -lmat
    ainv = eye + x
    p = 1
    while 2 * p < CHUNK:
        x = _mm_nn(x, x)
        ainv = _mm_nn(ainv, eye + x)
        p *= 2
    u = _mm_nn(ainv, vb)
    w = _mm_nn(ainv, kb * jnp.exp(gw))
    qk = jnp.where(causal, _mm_nt(q, k) * decay, 0.0)
    qg = q * jnp.exp(gw)

    o = jnp.zeros((nhead, r, HEAD), F32)
    for c in range(r // CHUNK):
        in_c = row_chunk == c
        glast = jnp.sum(jnp.where(in_c, g_w, 0.0), axis=1, keepdims=True)
        v_new = jnp.where(in_c, u - _mm_nn(w, state), 0.0)
        o = o + jnp.where(in_c, _mm_nn(qg, state), 0.0) + _mm_nn(qk, v_new)
        k_dec = jnp.where(in_c, k * jnp.exp(jnp.where(in_c, glast - gw, 0.0)), 0.0)
        state = state * jnp.exp(glast) + _mm_tn(k_dec, v_new)

    o = o * lax.rsqrt(jnp.mean(o * o, axis=-1, keepdims=True) + EPS) * ng
    o = o * (zc * _sigmoid(zc))
    return o, state


GDN_HEADS_FWD = 4
GDN_HEADS_BWD = 4


def _gdn_specs(s, nh, nqk, zcol, n_chunks, gh, single=False):
    rep = nh // nqk
    qw, vw = (gh // rep) * HEAD, gh * HEAD
    assert gh % rep == 0 and nh % gh == 0 and (nqk * HEAD) % qw == 0 and (2 * nqk * HEAD) % vw == 0 and zcol % vw == 0
    k0, v0, z0 = (nqk * HEAD) // qw, (2 * nqk * HEAD) // vw, zcol // vw
    mode = dict(pipeline_mode=pl.Buffered(1)) if single else {}
    return dict(
        q=pl.BlockSpec((s, qw), lambda b, j: (b, j), **mode),
        k=pl.BlockSpec((s, qw), lambda b, j: (b, k0 + j), **mode),
        v=pl.BlockSpec((s, vw), lambda b, j: (b, v0 + j), **mode),
        z=pl.BlockSpec((s, vw), lambda b, j: (b, z0 + j), **mode),
        ba=pl.BlockSpec((s, HEAD), lambda b, j: (b, 0)),
        gp=pl.BlockSpec((8, HEAD), lambda b, j: (0, 0)),
        qk_out=pl.BlockSpec((s, qw), lambda b, j: (b, j), **mode),
        head=pl.BlockSpec((s, vw), lambda b, j: (b, j), **mode),
        head_in=pl.BlockSpec((s, vw), lambda b, j: (b, j), **mode),
        st=pl.BlockSpec((1, gh, n_chunks, HEAD, HEAD), lambda b, j: (b, j, 0, 0, 0)),
        st_in=pl.BlockSpec((1, gh, n_chunks, HEAD, HEAD), lambda b, j: (b, j, 0, 0, 0), **mode),
    )


def _gdn_scalars(gp_ref, h):
    lane = lax.broadcasted_iota(jnp.int32, (1, HEAD), 1)
    sel = (lane == h).astype(F32)
    alog = jnp.sum(gp_ref[pl.ds(0, 1), :] * sel, axis=-1, keepdims=True)
    dtb = jnp.sum(gp_ref[pl.ds(1, 1), :] * sel, axis=-1, keepdims=True)
    return alog, dtb, sel


def _lanes(i):
    return pl.ds(i * HEAD, HEAD)


def gdn_fwd(qkvc, p, pba, gp, *, nb, s, nh, nqk, zcol, name, carry=None):
    rb = min(GDN_ROWS, s)
    n_chunks = s // rb
    gh = GDN_HEADS_FWD
    rep = nh // nqk
    sp = _gdn_specs(s, nh, nqk, zcol, n_chunks, gh)

    def body(q_ref, k_ref, v_ref, z_ref, ba_ref, gp_ref, o_ref, st_ref):
        h0 = pl.program_id(1) * gh
        ng = gp_ref[pl.ds(2, 1), :]
        lane = lax.broadcasted_iota(jnp.int32, (rb, HEAD), 1)
        heads = []
        for i in range(gh):
            alog, dtb, _ = _gdn_scalars(gp_ref, h0 + i)
            heads.append((alog, dtb, (lane == h0 + i).astype(F32), (lane == nh + h0 + i).astype(F32)))

        alogs = jnp.stack([hd[0] for hd in heads])
        dtbs = jnp.stack([hd[1] for hd in heads])

        def step(n, state):
            rows = pl.ds(pl.multiple_of(n * rb, rb), rb)
            ba = ba_ref[rows, :]
            qs = jnp.stack([q_ref[rows, _lanes(i // rep)] for i in range(gh)])
            ks = jnp.stack([k_ref[rows, _lanes(i // rep)] for i in range(gh)])
            vs = jnp.stack([v_ref[rows, _lanes(i)] for i in range(gh)])
            zs = jnp.stack([z_ref[rows, _lanes(i)] for i in range(gh)])
            braw = jnp.stack([jnp.sum(ba * hd[2], axis=-1, keepdims=True) for hd in heads])
            araw = jnp.stack([jnp.sum(ba * hd[3], axis=-1, keepdims=True) for hd in heads])
            o, new_state = _gdn_chunk(qs, ks, vs, zs, braw, araw, alogs, dtbs, ng, state)
            for i in range(gh):
                st_ref[0, i, n] = state[i]
                o_ref[rows, _lanes(i)] = o[i].astype(o_ref.dtype)
            return new_state

        lax.fori_loop(0, n_chunks, step, jnp.zeros((gh, HEAD, HEAD), F32))

    outs, got = carried_call(
        body, name=name, grid=(nb, nh // gh),
        in_specs=[sp["q"], sp["k"], sp["v"], sp["z"], sp["ba"], sp["gp"]],
        out_specs=[sp["head"], sp["st"]],
        out_shape=[jax.ShapeDtypeStruct((nb * s, nh * HEAD), BF16),
                   jax.ShapeDtypeStruct((nb, nh, n_chunks, HEAD, HEAD), F32)],
        scratch_shapes=[], semantics=("parallel", "parallel"), args=(qkvc, qkvc, qkvc, p, pba, gp), carry=carry)
    return outs + [got]


def gdn_bwd(qkvc, p, pba, gp, states, dout, *, nb, s, nh, nqk, zcol, name, carry=None):
    rb = min(GDN_ROWS, s)
    n_chunks = s // rb
    gh = GDN_HEADS_BWD
    rep = nh // nqk
    sp = _gdn_specs(s, nh, nqk, zcol, n_chunks, gh, single=True)

    def body(q_ref, k_ref, v_ref, z_ref, ba_ref, gp_ref, st_ref, do_ref,
             dq_ref, dk_ref, dv_ref, dz_ref, dba_ref, dgp_ref):
        h0 = pl.program_id(1) * gh
        ng = gp_ref[pl.ds(2, 1), :]
        lane = lax.broadcasted_iota(jnp.int32, (rb, HEAD), 1)
        heads = []
        for i in range(gh):
            alog, dtb, sel_row = _gdn_scalars(gp_ref, h0 + i)
            heads.append((alog, dtb, (lane == h0 + i).astype(F32), (lane == nh + h0 + i).astype(F32), sel_row))

        @pl.when(h0 == 0)
        def _():
            dba_ref[...] = jnp.zeros_like(dba_ref)
            dgp_ref[...] = jnp.zeros_like(dgp_ref)

        alogs = jnp.stack([hd[0] for hd in heads])
        dtbs = jnp.stack([hd[1] for hd in heads])

        def step(it, carry):
            dstate, dalog, ddtb, dng = carry
            n = n_chunks - 1 - it
            rows = pl.ds(pl.multiple_of(n * rb, rb), rb)
            ba = ba_ref[rows, :]
            qs = jnp.stack([q_ref[rows, _lanes(i // rep)] for i in range(gh)])
            ks = jnp.stack([k_ref[rows, _lanes(i // rep)] for i in range(gh)])
            vs = jnp.stack([v_ref[rows, _lanes(i)] for i in range(gh)])
            zs = jnp.stack([z_ref[rows, _lanes(i)] for i in range(gh)])
            dos = jnp.stack([do_ref[rows, _lanes(i)] for i in range(gh)])
            braw = jnp.stack([jnp.sum(ba * hd[2], axis=-1, keepdims=True) for hd in heads])
            araw = jnp.stack([jnp.sum(ba * hd[3], axis=-1, keepdims=True) for hd in heads])
            _, vjp = jax.vjp(_gdn_chunk, qs, ks, vs, zs, braw, araw, alogs, dtbs, ng, st_ref[0, :, n])
            gq, gk, gv, gz, gb, ga, galog, gdtb, gng, gstate = vjp((dos, dstate))
            dba = dba_ref[rows, :]
            for i in range(gh):
                dv_ref[rows, _lanes(i)] = gv[i]
                dz_ref[rows, _lanes(i)] = gz[i].astype(dz_ref.dtype)
                dba = dba + (gb[i] * heads[i][2] + ga[i] * heads[i][3])
            for j in range(gh // rep):
                dq_ref[rows, _lanes(j)] = sum(gq[i] for i in range(j * rep, (j + 1) * rep))
                dk_ref[rows, _lanes(j)] = sum(gk[i] for i in range(j * rep, (j + 1) * rep))
            dba_ref[rows, :] = dba
            return gstate, dalog + galog, ddtb + gdtb, dng + gng

        init = (jnp.zeros((gh, HEAD, HEAD), F32), jnp.zeros((gh, 1, 1), F32), jnp.zeros((gh, 1, 1), F32),
                jnp.zeros((1, HEAD), F32))
        _, dalog, ddtb, dng = lax.fori_loop(0, n_chunks, step, init)
        rows3 = [sum(dalog[i] * heads[i][4] for i in range(gh)), sum(ddtb[i] * heads[i][4] for i in range(gh)), dng]
        dgp_ref[0] += _rows_to_block(rows3, 8, HEAD)

    outs, got = carried_call(
        body, name=name, grid=(nb, nh // gh),
        in_specs=[sp["q"], sp["k"], sp["v"], sp["z"], sp["ba"], sp["gp"], sp["st_in"], sp["head_in"]],
        out_specs=[sp["qk_out"], sp["qk_out"], sp["head"], sp["head"], sp["ba"],
                   pl.BlockSpec((1, 8, HEAD), lambda b, j: (b, 0, 0))],
        out_shape=[jax.ShapeDtypeStruct((nb * s, nqk * HEAD), F32), jax.ShapeDtypeStruct((nb * s, nqk * HEAD), F32),
                   jax.ShapeDtypeStruct((nb * s, nh * HEAD), F32), jax.ShapeDtypeStruct((nb * s, nh * HEAD), BF16),
                   jax.ShapeDtypeStruct((nb * s, HEAD), F32), jax.ShapeDtypeStruct((nb, 8, HEAD), F32)],
        scratch_shapes=[], semantics=("parallel", "arbitrary"),
        args=(qkvc, qkvc, qkvc, p, pba, gp, states, dout), carry=carry)
    return outs + [got]


ADAM_LR = 0.001
ADAM_B1 = 0.9
ADAM_B2 = 0.999
ADAM_EPS = 1e-08
ADAM_WD = 0.01
ADAM_STEP = 10
EW_BLOCK_BYTES = 1 << 20


def _row_tile(rows, cols):
    for tr in (1024, 512, 256, 128, 64, 32, 16, 8):
        if rows % tr == 0 and tr * cols * 4 <= EW_BLOCK_BYTES:
            return tr
    return rows


def sum_slots(rs, *, name):
    nl = len(rs)
    n, rows, cols = rs[0].shape
    tr = _row_tile(rows, cols)
    nblk = rows // tr

    def body(*refs):
        o_ref = refs[nl]
        for l in range(nl):
            @pl.when(pl.program_id(0) == l)
            def _(l=l):
                acc = refs[l][0].astype(F32)
                for i in range(1, n):
                    acc = acc + refs[l][i].astype(F32)
                o_ref[0] = acc.astype(o_ref.dtype)

    def in_map(l):
        return lambda li, i: (0, jnp.where(li == l, i, jnp.where(li < l, 0, nblk - 1)), 0)

    return pl.pallas_call(
        body, name=name, grid=(nl, nblk),
        in_specs=[pl.BlockSpec((n, tr, cols), in_map(l)) for l in range(nl)],
        out_specs=pl.BlockSpec((1, tr, cols), lambda li, i: (li, i, 0)),
        out_shape=jax.ShapeDtypeStruct((nl, rows, cols), BF16),
        compiler_params=_cparams(("arbitrary", "arbitrary")),
    )(*rs)


def adamw(g_parts, w, m, v, *, name):
    shape = w.shape
    cols = shape[-1]
    lead = shape[0] if w.ndim == 3 else 1
    view = (lambda a: a.reshape(shape)) if w.ndim == 3 else (lambda a: a.reshape(1, -1, cols))
    rows = view(w).shape[1]
    tr = _row_tile(rows, cols)
    npart = len(g_parts)
    c1 = 1.0 - ADAM_B1 ** ADAM_STEP
    c2 = 1.0 - ADAM_B2 ** ADAM_STEP

    def body(*refs):
        w_ref, m_ref, v_ref = refs[npart:npart + 3]
        g_ref, d_ref, nm_ref, nv_ref = refs[npart + 3:]
        g = refs[0][...].astype(F32)
        for i in range(1, npart):
            g = g + refs[i][...].astype(F32)
        nm = ADAM_B1 * m_ref[...] + (1.0 - ADAM_B1) * g
        nv = ADAM_B2 * v_ref[...] + (1.0 - ADAM_B2) * (g * g)
        g_ref[...] = g
        nm_ref[...] = nm
        nv_ref[...] = nv
        d_ref[...] = -ADAM_LR * ((nm / c1) / (jnp.sqrt(nv / c2) + ADAM_EPS) + ADAM_WD * w_ref[...])

    blk = pl.BlockSpec((1, tr, cols), lambda l, i: (l, i, 0))
    outs = pl.pallas_call(
        body, name=name, grid=(lead, rows // tr),
        in_specs=[blk] * (npart + 3),
        out_specs=[blk] * 4,
        out_shape=[jax.ShapeDtypeStruct((lead, rows, cols), F32)] * 4,
        compiler_params=_cparams(("parallel", "parallel")),
    )(*[view(a) for a in g_parts], view(w), view(m), view(v))
    return tuple(o.reshape(shape) for o in outs)


def allreduce_small(vec, *, name):
    r = vec.shape[0]

    def body(v_ref, o_ref, slots, send_sems, recv_sems):
        x, y, c = lax.axis_index("x"), lax.axis_index("y"), lax.axis_index("c")
        me = 4 * x + 2 * y + c
        slots[me] = v_ref[...]
        copies = []
        for j in range(1, 8):
            px = 1 - x if j & 4 else x
            py = 1 - y if j & 2 else y
            pc = 1 - c if j & 1 else c
            rc = pltpu.make_async_remote_copy(src_ref=v_ref, dst_ref=slots.at[me], send_sem=send_sems.at[j - 1],
                                              recv_sem=recv_sems.at[j - 1], device_id=(px, py, pc), device_id_type=MESH)
            rc.start()
            copies.append(rc)
        for cp in copies:
            cp.wait()
        acc = slots[0]
        for i in range(1, 8):
            acc = acc + slots[i]
        o_ref[...] = acc

    vm = pl.BlockSpec(memory_space=pltpu.VMEM)
    return pl.pallas_call(
        body, name=name, in_specs=[vm], out_specs=vm,
        out_shape=jax.ShapeDtypeStruct((r, 128), F32),
        scratch_shapes=[pltpu.VMEM((8, r, 128), F32), pltpu.SemaphoreType.DMA((7,)), pltpu.SemaphoreType.DMA((7,))],
        compiler_params=pltpu.CompilerParams(vmem_limit_bytes=VMEM_LIMIT),
    )(vec)


WEIGHTS = ("mix_norm_g", "w_in", "conv_dw_w", "conv_dw_b", "conv_ln_g", "conv_ln_b", "conv_pw_w", "conv_pw_b",
           "gdn_conv_w", "gdn_a_log", "gdn_dt_bias", "gdn_norm_g", "w_out", "ffn_norm_g", "w_up", "ffn_conv_w",
           "ffn_conv_b", "w_down", "final_norm_g")
COL_SHARDED = ("w_in", "w_up", "conv_dw_w", "gdn_conv_w", "ffn_conv_w")
ROW_SHARDED = ("conv_pw_w", "w_out", "w_down")
BIG = ("w_in", "conv_pw_w", "w_out", "w_up", "w_down")
SMALL_CONV = ("conv_dw_w", "gdn_conv_w", "ffn_conv_w")
GATHER_PARTS = {"w_up": (18, 6, 30, 5, 5), "w_down": (38, 6), "w_in": (15, 49), "conv_pw_w": (1,), "w_out": (1,)}


def _full_from_slots(name, part):
    if name in COL_SHARDED:
        r, cs = part.shape[1:]
        return jnp.transpose(part, (1, 0, 2)).reshape(r, 4 * cs)
    rs, c = part.shape[1:]
    return part.reshape(4 * rs, c)


def _slots_from_full(name, full):
    if name in COL_SHARDED:
        r, c = full.shape
        return jnp.transpose(full.reshape(r, 4, c // 4), (1, 0, 2))
    r, c = full.shape
    return full.reshape(4, r // 4, c)


def _pack(parts):
    flat = jnp.concatenate([p.reshape(-1).astype(F32) for p in parts])
    pad = (-flat.shape[0]) % 1024
    return jnp.pad(flat, (0, pad)).reshape(-1, 128)


def _unpack(vec, shapes):
    flat = vec.reshape(-1)
    out, pos = [], 0
    for shp in shapes:
        n = 1
        for d in shp:
            n *= d
        out.append(flat[pos:pos + n].reshape(shp))
        pos += n
    return out


def kernel(x, mix_norm_g, w_in, conv_dw_w, conv_dw_b, conv_ln_g, conv_ln_b, conv_pw_w, conv_pw_b, gdn_conv_w, gdn_a_log, gdn_dt_bias, gdn_norm_g, w_out, ffn_norm_g, w_up, ffn_conv_w, ffn_conv_b, w_down, final_norm_g, loss_target, m_mix_norm_g, m_w_in, m_conv_dw_w, m_conv_dw_b, m_conv_ln_g, m_conv_ln_b, m_conv_pw_w, m_conv_pw_b, m_gdn_conv_w, m_gdn_a_log, m_gdn_dt_bias, m_gdn_norm_g, m_w_out, m_ffn_norm_g, m_w_up, m_ffn_conv_w, m_ffn_conv_b, m_w_down, m_final_norm_g, v_mix_norm_g, v_w_in, v_conv_dw_w, v_conv_dw_b, v_conv_ln_g, v_conv_ln_b, v_conv_pw_w, v_conv_pw_b, v_gdn_conv_w, v_gdn_a_log, v_gdn_dt_bias, v_gdn_norm_g, v_w_out, v_ffn_norm_g, v_w_up, v_ffn_conv_w, v_ffn_conv_b, v_w_down, v_final_norm_g):
    wts = dict(zip(WEIGHTS, (mix_norm_g, w_in, conv_dw_w, conv_dw_b, conv_ln_g, conv_ln_b, conv_pw_w, conv_pw_b,
                             gdn_conv_w, gdn_a_log, gdn_dt_bias, gdn_norm_g, w_out, ffn_norm_g, w_up, ffn_conv_w,
                             ffn_conv_b, w_down, final_norm_g)))
    mom = dict(zip(WEIGHTS, (m_mix_norm_g, m_w_in, m_conv_dw_w, m_conv_dw_b, m_conv_ln_g, m_conv_ln_b, m_conv_pw_w,
                             m_conv_pw_b, m_gdn_conv_w, m_gdn_a_log, m_gdn_dt_bias, m_gdn_norm_g, m_w_out,
                             m_ffn_norm_g, m_w_up, m_ffn_conv_w, m_ffn_conv_b, m_w_down, m_final_norm_g)))
    var = dict(zip(WEIGHTS, (v_mix_norm_g, v_w_in, v_conv_dw_w, v_conv_dw_b, v_conv_ln_g, v_conv_ln_b, v_conv_pw_w,
                             v_conv_pw_b, v_gdn_conv_w, v_gdn_a_log, v_gdn_dt_bias, v_gdn_norm_g, v_w_out,
                             v_ffn_norm_g, v_w_up, v_ffn_conv_w, v_ffn_conv_b, v_w_down, v_final_norm_g)))

    nb, s, d = x.shape
    t = nb * s
    depth = mix_norm_g.shape[0]
    ch = conv_dw_b.shape[1]
    nh = gdn_a_log.shape[1]
    nqk = nh // 2
    kwid, vwid = nqk * HEAD, nh * HEAD
    main = 2 * ch + 2 * kwid + 2 * vwid
    qcol, zcol = 2 * ch, 2 * ch + 2 * kwid + vwid
    dff = ffn_conv_b.shape[1]
    my_xy = 2 * lax.axis_index("x") + lax.axis_index("y")

    shards = {(n, l): wts[n][l].astype(BF16) for n in BIG for l in range(depth)}
    buffers, covered, full = {}, {}, {}

    def window(n, i):
        rows, parts = wts[n].shape[1], GATHER_PARTS[n]
        if i is None:
            return 0, rows
        return rows * sum(parts[:i]) // sum(parts), rows * parts[i] // sum(parts)

    def gather(*keys):
        return SplitGather([shards[n, l] for n, l, _ in keys], windows=[window(n, i) for n, _, i in keys],
                           into=[buffers.get((n, l)) for n, l, _ in keys])

    def arrived(keys, got):
        for (n, l, i), g in zip(keys, got):
            buffers[n, l] = g
            covered[n, l] = covered.get((n, l), 0) + window(n, i)[1]
            if covered[n, l] == wts[n].shape[1]:
                full[n, l] = g if n == "w_up" else _full_from_slots(n, g)

    def carrying_gather(keys, call):
        res = call(carry=gather(*keys))
        arrived(keys, res[-1])
        return res[0] if len(res) == 2 else res[:-1]

    first, small = run_exchanges([gather(("w_in", 0, None)), Exchange([wts[n] for n in SMALL_CONV], gather=True)],
                                 name="gather_first")
    arrived([("w_in", 0, None)], first)
    for n, g in zip(SMALL_CONV, small):
        for l in range(depth):
            full[n, l] = _full_from_slots(n, g[:, l])

    xc = x.reshape(t, d)
    saved, lws = [], []
    for l in range(depth):
        w_in_f = full["w_in", l]
        w_main = w_in_f[:, :main]
        w_ba = jnp.pad(w_in_f[:, main:], ((0, 0), (0, HEAD - 2 * nh)))
        gp = (jnp.zeros((8, HEAD), F32).at[0, :nh].set(gdn_a_log[l]).at[1, :nh].set(gdn_dt_bias[l])
              .at[2].set(gdn_norm_g[l]))
        h = rmsnorm_fwd(xc, mix_norm_g[l], name=f"f{l}_norm1")
        p = carrying_gather([("w_up", l, 0)], functools.partial(matmul, h, w_main, name=f"f{l}_in_main"))
        pba = matmul(h, w_ba, name=f"f{l}_in_ba")
        u3, u1 = carrying_gather([("conv_pw_w", l, None), ("w_out", l, None)],
                                 functools.partial(conf_fwd, p, full["conv_dw_w", l], conv_dw_b[l], conv_ln_g[l],
                                                   conv_ln_b[l], nb=nb, s=s, name=f"f{l}_conf"))
        out_a = matmul(u3, full["conv_pw_w", l], bias=conv_pw_b[l], out_dtype=BF16, name=f"f{l}_pw")
        qkvc = carrying_gather([("w_up", l, 1)], functools.partial(qkvconv_fwd, p, full["gdn_conv_w", l], col0=qcol,
                                                                    nb=nb, s=s, name=f"f{l}_qkvconv"))
        out_b, states = carrying_gather([("w_up", l, 2)],
                                        functools.partial(gdn_fwd, qkvc, p, pba, gp, nb=nb, s=s, nh=nh, nqk=nqk,
                                                          zcol=zcol, name=f"f{l}_gdn"))
        wout_a, wout_b = full["w_out", l][:ch], full["w_out", l][ch:]
        x1 = carrying_gather([("w_up", l, 3)], functools.partial(matmul, out_a, wout_a, res=xc, name=f"f{l}_out_a"))
        x1 = carrying_gather([("w_up", l, 4)], functools.partial(matmul, out_b, wout_b, res=x1, name=f"f{l}_out_b"))
        h2 = rmsnorm_fwd(x1, ffn_norm_g[l], name=f"f{l}_norm2")
        up = carrying_gather([("w_down", l, 0)],
                             functools.partial(matmul, h2, full["w_up", l], out_dtype=BF16, name=f"f{l}_up"))
        nxt = l + 1 < depth
        act = carrying_gather([("w_down", l, 1)] + ([("w_in", l + 1, 0)] if nxt else []),
                              functools.partial(ffn_act_fwd, up, full["ffn_conv_w", l], ffn_conv_b[l], nb=nb, s=s,
                                                name=f"f{l}_act"))
        down_call = functools.partial(matmul, act, full["w_down", l], res=x1, name=f"f{l}_down")
        x2 = carrying_gather([("w_in", l + 1, 1)], down_call) if nxt else down_call()
        saved.append(dict(x=xc, h=h, p=p, pba=pba, u3=u3, u1=u1, out_a=out_a, qkvc=qkvc, out_b=out_b, states=states,
                          x1=x1, h2=h2, up=up, act=act))
        lws.append(dict(w_main=w_main, w_ba=w_ba, pw=full["conv_pw_w", l], wout_a=wout_a, wout_b=wout_b,
                        wup=full["w_up", l], wdown=full["w_down", l], dw_w=full["conv_dw_w", l],
                        gconv_w=full["gdn_conv_w", l], fconv_w=full["ffn_conv_w", l], gp=gp))
        xc = x2

    loss_blk, dx, dxb, dgf = loss_head(xc, final_norm_g, loss_target.reshape(t, d), name="loss_head")

    stacks, received = {}, {}
    scatter = lambda *keys: Exchange([stacks[k] for k in keys], gather=False)

    def produced(n, l, grad, halves=False):
        slots = _slots_from_full(n, grad).astype(BF16)
        if halves:
            half = slots.shape[1] // 2
            stacks[n, l, 0], stacks[n, l, 1] = slots[:, :half], slots[:, half:]
        else:
            stacks[n, l] = slots

    def landed(keys, got):
        for k, g in zip(keys, got):
            received[k] = g

    def carrying(keys, call, **kw):
        res = call(carry=scatter(*keys), **kw)
        landed(keys, res[-1])
        return res[0] if len(res) == 2 else res[:-1]

    small_grads = {n: [None] * depth for n in WEIGHTS if n not in BIG and n != "final_norm_g"}
    for l in reversed(range(depth)):
        lw, sv = lws[l], saved[l]
        dact_call = functools.partial(matmul, dxb, lw["wdown"], tb=True, out_dtype=BF16, name=f"b{l}_dact")
        dact = carrying([("w_in", l + 1, 1)], dact_call) if l + 1 < depth else dact_call()
        produced("w_down", l, matmul(sv["act"], dxb, ta=True, out_dtype=BF16, name=f"b{l}_dwdown"), halves=True)
        dgate, dupv, fpart = ffn_act_bwd(sv["up"], lw["fconv_w"], ffn_conv_b[l], dact, nb=nb, s=s, name=f"b{l}_act")
        dh2 = carrying([("w_down", l, 0)], functools.partial(matmul, dgate, lw["wup"], tb=True, name=f"b{l}_dh2_gate"))
        dh2 = carrying([("w_down", l, 1)], functools.partial(matmul, dupv, lw["wup"], tb=True, b_koff=dff, res=dh2,
                                                             name=f"b{l}_dh2_up"))
        stacks["w_up", l] = jnp.concatenate(
            [matmul(sv["h2"], dgate, ta=True, out_dtype=BF16, out_slots=dff // w_up.shape[2], name=f"b{l}_dwup_gate"),
             matmul(sv["h2"], dupv, ta=True, out_dtype=BF16, out_slots=dff // w_up.shape[2], name=f"b{l}_dwup_up")], axis=0)
        dx1, dx1b, dg2 = rmsnorm_bwd(sv["x1"], ffn_norm_g[l], dh2, dx, name=f"b{l}_norm2")
        fsum = jnp.sum(fpart, axis=0)
        small_grads["ffn_norm_g"][l] = dg2[0]
        small_grads["ffn_conv_w"][l] = fsum[:ffn_conv_w.shape[1]]
        small_grads["ffn_conv_b"][l] = fsum[ffn_conv_w.shape[1]]
        dout_a = matmul(dx1b, lw["wout_a"], tb=True, out_dtype=BF16, name=f"b{l}_dout_a")
        dout_b = matmul(dx1b, lw["wout_b"], tb=True, name=f"b{l}_dout_b")
        produced("w_out", l, jnp.concatenate(
            [matmul(sv["out_a"], dx1b, ta=True, out_dtype=BF16, name=f"b{l}_dwout_a"),
             matmul(sv["out_b"], dx1b, ta=True, out_dtype=BF16, name=f"b{l}_dwout_b")], axis=0))
        du3 = matmul(dout_a, lw["pw"], tb=True, name=f"b{l}_du3")
        produced("conv_pw_w", l, matmul(sv["u3"], dout_a, ta=True, out_dtype=BF16, name=f"b{l}_dwpw"))
        dval, dagate, cw_part, cs_part = carrying(
            [("w_out", l), ("conv_pw_w", l)],
            functools.partial(conf_bwd, sv["p"], sv["u1"], lw["dw_w"], conv_dw_b[l], conv_ln_g[l], conv_ln_b[l], du3,
                              dout_a,
                              nb=nb, s=s, name=f"b{l}_conf"))
        csum = jnp.sum(cs_part, axis=0)
        small_grads["conv_dw_w"][l] = jnp.sum(cw_part, axis=0)
        small_grads["conv_dw_b"][l] = csum[0]
        small_grads["conv_ln_g"][l] = csum[1]
        small_grads["conv_ln_b"][l] = csum[2]
        small_grads["conv_pw_b"][l] = csum[3]
        dq, dk, dv, dz, dpba, dgp = carrying(
            [("w_up", l)],
            functools.partial(gdn_bwd, sv["qkvc"], sv["p"], sv["pba"], lw["gp"], sv["states"], dout_b,
                              nb=nb, s=s, nh=nh, nqk=nqk, zcol=zcol, name=f"b{l}_gdn"))
        dqkv, gw_part = qkvconv_bwd(sv["p"], lw["gconv_w"], jnp.concatenate([dq, dk, dv], axis=1),
                                    col0=qcol, nb=nb, s=s, name=f"b{l}_qkvconv")
        gsum = jnp.sum(dgp, axis=0)
        small_grads["gdn_conv_w"][l] = jnp.sum(gw_part, axis=0)[:gdn_conv_w.shape[1]]
        small_grads["gdn_a_log"][l] = gsum[0, :nh]
        small_grads["gdn_dt_bias"][l] = gsum[1, :nh]
        small_grads["gdn_norm_g"][l] = gsum[2]
        dp = jnp.concatenate([dval, dagate, dqkv, dz], axis=1)
        dw_main = matmul(sv["h"], dp, ta=True, out_dtype=BF16, name=f"b{l}_dwin_main")
        dw_ba = matmul(sv["h"], dpba, ta=True, out_dtype=BF16, name=f"b{l}_dwin_ba")
        produced("w_in", l, jnp.concatenate([dw_main, dw_ba[:, :2 * nh]], axis=1), halves=True)
        dh = carrying([("w_in", l, 0)], functools.partial(matmul, dp, lw["w_main"], tb=True, name=f"b{l}_dh_main"))
        dh = matmul(dpba, lw["w_ba"], tb=True, res=dh, name=f"b{l}_dh_ba")
        dx, dxb, dg1 = rmsnorm_bwd(sv["x"], mix_norm_g[l], dh, dx1, name=f"b{l}_norm1")
        small_grads["mix_norm_g"][l] = dg1[0]

    def summed(n):
        parts = [received[k] for l in range(depth) for k in ([(n, l)] if (n, l) in received else [(n, l, 0), (n, l, 1)])]
        return sum_slots(parts, name=f"sum_{n}").reshape(wts[n].shape)

    early = [n for n in BIG if n != "w_in"]
    partial = {n: summed(n) for n in early}
    swapped, last = run_exchanges([SiblingSwap([partial[n] for n in early]),
                                   Exchange([stacks["w_in", 0, 1]], gather=False)], name="swap_and_scatter_last")
    other = dict(zip(early, swapped))
    received["w_in", 0, 1] = last[0]
    partial["w_in"] = summed("w_in")
    other["w_in"] = run_exchanges([SiblingSwap([partial["w_in"]])], name="swap_w_in")[0][0]

    grads, deltas, new_m, new_v = {}, {}, {}, {}
    for n in BIG:
        grads[n], deltas[n], new_m[n], new_v[n] = adamw([partial[n], other[n]], wts[n], mom[n], var[n],
                                                        name=f"adamw_{n}")

    small_names = [n for n in WEIGHTS if n not in BIG]
    small_full = [jnp.stack(small_grads[n]) if n != "final_norm_g" else dgf[0] for n in small_names]
    packed = _pack(small_full + [loss_blk[0, :1]])
    reduced = allreduce_small(packed, name="allreduce_small")
    parts = _unpack(reduced, [a.shape for a in small_full] + [(1,)])
    loss = parts[-1][0]
    for n, g in zip(small_names, parts[:-1]):
        if n in SMALL_CONV:
            wid = wts[n].shape[-1]
            g = lax.dynamic_slice_in_dim(g, my_xy * wid, wid, axis=g.ndim - 1)
        grads[n], deltas[n], new_m[n], new_v[n] = adamw([g], wts[n], mom[n], var[n], name=f"adamw_{n}")

    return (loss, dx.reshape(nb, s, d), *[grads[n] for n in WEIGHTS], *[deltas[n] for n in WEIGHTS],
            *[new_m[n] for n in WEIGHTS], *[new_v[n] for n in WEIGHTS])
```
